```python
import math
import jax, jax.numpy as jnp
from jax import lax
import numpy as np

D_MODEL = 2048
BATCH = 4
SEQ = 2048
DEPTH = 4
DEC_BATCH = 128
DEC_SEQ = 1
PAST_LEN = 16384
PAGE_SIZE = 128

N_EVEN = (DEPTH + 1) // 2
N_ODD = DEPTH // 2
CHUNK = 128
CONV_K = 4
EPS = 1e-6
NEG = -1e30
F32 = jnp.float32

SSD_WIDTH = D_MODEL
SSD_HEAD_DIM = 64
SSD_HEADS = SSD_WIDTH // SSD_HEAD_DIM
SSD_STATE = 128
SSD_GROUPS = 4
SSD_CONV_DIM = SSD_WIDTH + 2 * SSD_GROUPS * SSD_STATE
GDN_HEADS = 16
GDN_DK = 128
GDN_DV = 128
GDN_QK = GDN_HEADS * GDN_DK
GDN_V = GDN_HEADS * GDN_DV
GDN_CONV_DIM = 2 * GDN_QK + GDN_V
EVEN_IN = SSD_WIDTH + SSD_CONV_DIM + SSD_HEADS + GDN_CONV_DIM + GDN_V + 2 * GDN_HEADS
EVEN_MIX = SSD_WIDTH + GDN_V
ML_HEADS = 8
ML_DK = 128
ML_DV = 256
ML_QK = ML_HEADS * ML_DK
ML_V = ML_HEADS * ML_DV
CM_WIDTH = D_MODEL // 2
CM_GROUPS = 8
CM_GROUP_DIM = CM_WIDTH // CM_GROUPS
CM_CHUNK = 128
ODD_IN = 2 * ML_QK + 3 * ML_V + 2 * ML_HEADS + 3 * CM_WIDTH
ODD_MIX = ML_V + CM_WIDTH

kernel_name = 'hybrid_ssd_gdn_mlstm_chunkmlp_step'


def rmsnorm(x, g):
    xf = x.astype(F32)
    y = xf * lax.rsqrt(jnp.mean(xf * xf, axis=-1, keepdims=True) + EPS)
    return (y * g.astype(F32)).astype(x.dtype)


def _l2norm(t):
    t = t.astype(F32)
    return t * lax.rsqrt(jnp.sum(t * t, axis=-1, keepdims=True) + EPS)


def _split(x, sizes):
    return jnp.split(x, [int(i) for i in np.cumsum(sizes)[:-1]], axis=-1)


def _pad_len(x, lp, value=0.0):
    if lp == x.shape[1]:
        return x
    pad = [(0, 0)] * x.ndim
    pad[1] = (0, lp - x.shape[1])
    return jnp.pad(x, pad, constant_values=value)


def _blocks(L):
    qb = min(CHUNK, L)
    return qb, -(-L // qb)


def _to_chunks(t, qb, nc, value=0.0):
    t = _pad_len(t.astype(F32), qb * nc, value)
    t = t.reshape((t.shape[0], nc, qb) + t.shape[2:])
    return jnp.moveaxis(t, 2, 3)


def causal_conv(u, buf, w):
    full = jnp.concatenate([buf.astype(u.dtype), u], axis=1)
    out = lax.conv_general_dilated(full, w[:, None, :].astype(u.dtype), window_strides=(1,), padding='VALID',
                                   dimension_numbers=('NWC', 'WIO', 'NWC'), feature_group_count=u.shape[-1])
    return out, full[:, -(CONV_K - 1):]


def ssd_scan(x, dt, a, bm, cm, s0):
    bsz, L, H, P = x.shape
    G, N = bm.shape[2], bm.shape[3]
    hg = H // G
    q, nc = _blocks(L)
    lp = q * nc
    x = _pad_len(x.astype(F32), lp).reshape(bsz, nc, q, G, hg, P)
    dt = _pad_len(dt.astype(F32), lp).reshape(bsz, nc, q, G, hg)
    bm = _pad_len(bm.astype(F32), lp).reshape(bsz, nc, q, G, N)
    cm = _pad_len(cm.astype(F32), lp).reshape(bsz, nc, q, G, N)
    la = jnp.cumsum(dt * a.reshape(G, hg), axis=2)
    causal = jnp.tril(jnp.ones((q, q), dtype=bool))[:, :, None, None]
    decay = jnp.exp(jnp.where(causal, la[:, :, :, None] - la[:, :, None, :], -jnp.inf))
    cb = jnp.einsum('bctgn,bcsgn->bctsg', cm, bm)
    y_intra = jnp.einsum('bctsgh,bcsgh,bcsghp->bctghp', cb[..., None] * decay, dt, x)
    la_last = la[:, :, -1]
    w_s = jnp.exp(la_last[:, :, None] - la) * dt
    s_loc = jnp.einsum('bcsgh,bcsghp,bcsgn->bcghpn', w_s, x, bm)

    def step(s, inp):
        dec, sl = inp
        return s * dec[..., None, None] + sl, s

    s_fin, s_prev = lax.scan(step, s0.astype(F32).reshape(bsz, G, hg, P, N),
                             (jnp.moveaxis(jnp.exp(la_last), 1, 0), jnp.moveaxis(s_loc, 1, 0)))
    s_prev = jnp.moveaxis(s_prev, 0, 1)
    y_inter = jnp.einsum('bctgn,bcghpn->bctghp', cm, s_prev) * jnp.exp(la)[..., None]
    y = (y_intra + y_inter).reshape(bsz, lp, H, P)[:, :L]
    return y, s_fin.reshape(bsz, H, P, N)


def gdn_scan(q, k, v, beta, g, s0):
    bsz, L, H, K = k.shape
    V = v.shape[-1]
    qb, nc = _blocks(L)
    q, k, v, beta, g = [_to_chunks(t, qb, nc) for t in (q, k, v, beta, g)]
    gc = jnp.cumsum(g, axis=-1)
    causal = jnp.tril(jnp.ones((qb, qb), dtype=bool))
    strict = jnp.tril(jnp.ones((qb, qb), dtype=bool), -1)
    gam = jnp.exp(jnp.where(causal, gc[..., :, None] - gc[..., None, :], -jnp.inf))
    m = jnp.where(strict, beta[..., :, None] * jnp.einsum('bchtd,bchsd->bchts', k, k) * gam, 0.0)
    eye = jnp.eye(qb, dtype=F32)
    tinv = lax.linalg.triangular_solve(eye + m, jnp.broadcast_to(eye, m.shape), left_side=True, lower=True)
    u = jnp.einsum('bchts,bchsv->bchtv', tinv, beta[..., None] * v)
    w = jnp.einsum('bchts,bchsk->bchtk', tinv, (beta * jnp.exp(gc))[..., None] * k)
    aqk = jnp.einsum('bchtd,bchsd->bchts', q, k) * gam
    q_dec = q * jnp.exp(gc)[..., None]
    k_dec = k * jnp.exp(gc[..., -1:] - gc)[..., None]

    def step(s, inp):
        w_c, u_c, kd_c, gl_c = inp
        vn = u_c - jnp.einsum('bhtk,bhkv->bhtv', w_c, s)
        return s * gl_c[..., None, None] + jnp.einsum('bhtk,bhtv->bhkv', kd_c, vn), (s, vn)

    s_fin, (s_prev, vn) = lax.scan(step, s0.astype(F32),
                                   (jnp.moveaxis(w, 1, 0), jnp.moveaxis(u, 1, 0), jnp.moveaxis(k_dec, 1, 0),
                                    jnp.moveaxis(jnp.exp(gc[..., -1]), 1, 0)))
    s_prev = jnp.moveaxis(s_prev, 0, 1)
    vn = jnp.moveaxis(vn, 0, 1)
    o = jnp.einsum('bchtk,bchkv->bchtv', q_dec, s_prev) + jnp.einsum('bchts,bchsv->bchtv', aqk, vn)
    o = jnp.moveaxis(o, 3, 2).reshape(bsz, nc * qb, H, V)[:, :L]
    return o, s_fin


def mlstm_scan(q, k, v, logi, logf, c0, n0, m0):
    bsz, L, H, K = q.shape
    V = v.shape[-1]
    qb, nc = _blocks(L)
    q, k, v, logf = [_to_chunks(t, qb, nc) for t in (q, k, v, logf)]
    logi = _to_chunks(logi, qb, nc, NEG)
    b = jnp.cumsum(logf, axis=-1)
    causal = jnp.tril(jnp.ones((qb, qb), dtype=bool))
    dmat = jnp.where(causal, b[..., :, None] - b[..., None, :] + logi[..., None, :], -jnp.inf)
    m_intra = jnp.max(dmat, axis=-1)
    qk = jnp.einsum('bchtd,bchsd->bchts', q, k) * jnp.exp(dmat - m_intra[..., None])
    h_intra = jnp.einsum('bchts,bchsv->bchtv', qk, v)
    n_intra = jnp.sum(qk, axis=-1)
    b_last = b[..., -1]
    gk = b_last[..., None] - b + logi
    m_k = jnp.max(gk, axis=-1)
    wk = jnp.exp(gk - m_k[..., None])
    c_loc = jnp.einsum('bcht,bchtk,bchtv->bchkv', wk, k, v)
    n_loc = jnp.einsum('bcht,bchtk->bchk', wk, k)

    def step(carry, inp):
        c, n, m = carry
        bl, mk_c, cl, nl = inp
        m_new = jnp.maximum(bl + m, mk_c)
        sa = jnp.exp(bl + m - m_new)
        sb = jnp.exp(mk_c - m_new)
        return (c * sa[..., None, None] + cl * sb[..., None, None], n * sa[..., None] + nl * sb[..., None], m_new), (c, n, m)

    (c_fin, n_fin, m_fin), (c_prev, n_prev, m_prev) = lax.scan(
        step, (c0.astype(F32), n0.astype(F32), m0.astype(F32)),
        (jnp.moveaxis(b_last, 1, 0), jnp.moveaxis(m_k, 1, 0), jnp.moveaxis(c_loc, 1, 0), jnp.moveaxis(n_loc, 1, 0)))
    c_prev = jnp.moveaxis(c_prev, 0, 1)
    n_prev = jnp.moveaxis(n_prev, 0, 1)
    m_prev = jnp.moveaxis(m_prev, 0, 1)
    mb = b + m_prev[..., None]
    m_t = jnp.maximum(mb, m_intra)
    s_inter = jnp.exp(mb - m_t)
    s_intra = jnp.exp(m_intra - m_t)
    num = s_inter[..., None] * jnp.einsum('bchtk,bchkv->bchtv', q, c_prev) + s_intra[..., None] * h_intra
    den = s_inter * jnp.einsum('bchtk,bchk->bcht', q, n_prev) + s_intra * n_intra
    h = num / jnp.maximum(jnp.abs(den), jnp.exp(-m_t))[..., None]
    h = jnp.moveaxis(h, 3, 2).reshape(bsz, nc * qb, H, V)[:, :L]
    return h, c_fin, n_fin, m_fin


def chunk_mlp(u, v, v_gain, ws, wb):
    bsz, L, _ = u.shape
    nc = -(-L // CM_CHUNK)
    tb = CM_CHUNK if nc > 1 else L
    u = jax.nn.gelu(u.astype(F32))
    v = rmsnorm(jax.nn.gelu(v.astype(F32)).reshape(bsz, L, CM_GROUPS, CM_GROUP_DIM), v_gain)
    vp = _pad_len(v, nc * tb).reshape(bsz, nc, tb, CM_GROUPS, CM_GROUP_DIM)
    w_causal = jnp.where(jnp.tril(jnp.ones((tb, tb), dtype=bool)), ws[:, :tb, :tb].astype(F32), 0.0)
    s = jnp.einsum('gts,bcsgd->bctgd', w_causal, vp) + wb[:, :tb].astype(F32).T[None, None, :, :, None]
    s = s.reshape(bsz, nc * tb, CM_WIDTH)[:, :L]
    start = ((L - 1) // CM_CHUNK) * CM_CHUNK
    return u * s, v.reshape(bsz, L, CM_WIDTH)[:, start:]


def _even_layer(x, conv_a, s_a, conv_b, s_b, norm_g, w_in, a_conv_w, a_conv_b, a_dt_bias, a_log, a_d, a_norm,
                b_conv_w, b_dt_bias, b_a_log, b_norm, w_out):
    bsz, L, _ = x.shape
    h = rmsnorm(x, norm_g)
    z_a, xbc, dt_raw, qkv, z_b, beta_raw, g_raw = _split(
        h @ w_in, [SSD_WIDTH, SSD_CONV_DIM, SSD_HEADS, GDN_CONV_DIM, GDN_V, GDN_HEADS, GDN_HEADS])
    xbc, conv_a_new = causal_conv(xbc, conv_a, a_conv_w)
    xbc = jax.nn.silu(xbc + a_conv_b)
    xs, bm, cm = _split(xbc, [SSD_WIDTH, SSD_GROUPS * SSD_STATE, SSD_GROUPS * SSD_STATE])
    xs = xs.reshape(bsz, L, SSD_HEADS, SSD_HEAD_DIM)
    dt = jax.nn.softplus(dt_raw.astype(F32) + a_dt_bias.astype(F32))
    y_a, s_a_new = ssd_scan(xs, dt, -jnp.exp(a_log.astype(F32)),
                            bm.reshape(bsz, L, SSD_GROUPS, SSD_STATE), cm.reshape(bsz, L, SSD_GROUPS, SSD_STATE), s_a)
    y_a = y_a + a_d.astype(F32)[:, None] * xs.astype(F32)
    y_a = y_a.reshape(bsz, L, SSD_WIDTH) * jax.nn.silu(z_a.astype(F32))
    y_a = rmsnorm(y_a.reshape(bsz, L, SSD_GROUPS, SSD_WIDTH // SSD_GROUPS),
                  a_norm.reshape(SSD_GROUPS, SSD_WIDTH // SSD_GROUPS)).reshape(bsz, L, SSD_WIDTH)
    qkv, conv_b_new = causal_conv(qkv, conv_b, b_conv_w)
    qkv = jax.nn.silu(qkv)
    q, k, v = _split(qkv, [GDN_QK, GDN_QK, GDN_V])
    q = _l2norm(q.reshape(bsz, L, GDN_HEADS, GDN_DK)) * (GDN_DK ** -0.5)
    k = _l2norm(k.reshape(bsz, L, GDN_HEADS, GDN_DK))
    v = v.reshape(bsz, L, GDN_HEADS, GDN_DV)
    beta = jax.nn.sigmoid(beta_raw.astype(F32))
    g = -jnp.exp(b_a_log.astype(F32)) * jax.nn.softplus(g_raw.astype(F32) + b_dt_bias.astype(F32))
    o, s_b_new = gdn_scan(q, k, v, beta, g, s_b)
    o = rmsnorm(o, b_norm) * jax.nn.silu(z_b.astype(F32).reshape(bsz, L, GDN_HEADS, GDN_DV))
    mix = jnp.concatenate([y_a, o.reshape(bsz, L, GDN_V)], axis=-1).astype(x.dtype)
    dt_out = x.dtype
    return (x + mix @ w_out, conv_a_new.astype(dt_out), s_a_new.astype(dt_out),
            conv_b_new.astype(dt_out), s_b_new.astype(dt_out))


def _odd_layer(x, c0, n0, m0, norm_g, w_in, i_bias, f_bias, ml_norm, v_gain, ws, wb, w_out):
    bsz, L, _ = x.shape
    h = rmsnorm(x, norm_g)
    q, k, v, o_raw, z_c, i_raw, f_raw, u_d, v_d, z_d = _split(
        h @ w_in, [ML_QK, ML_QK, ML_V, ML_V, ML_V, ML_HEADS, ML_HEADS, CM_WIDTH, CM_WIDTH, CM_WIDTH])
    q = q.reshape(bsz, L, ML_HEADS, ML_DK)
    k = k.reshape(bsz, L, ML_HEADS, ML_DK) * (ML_DK ** -0.5)
    v = v.reshape(bsz, L, ML_HEADS, ML_DV)
    logi = i_raw.astype(F32) + i_bias.astype(F32)
    logf = jax.nn.log_sigmoid(f_raw.astype(F32) + f_bias.astype(F32))
    hc, c_new, n_new, m_new = mlstm_scan(q, k, v, logi, logf, c0, n0, m0)
    hc = rmsnorm(hc, ml_norm) * jax.nn.sigmoid(o_raw.astype(F32).reshape(bsz, L, ML_HEADS, ML_DV))
    hc = hc.reshape(bsz, L, ML_V) * jax.nn.silu(z_c.astype(F32))
    yd, v_rows = chunk_mlp(u_d, v_d, v_gain, ws, wb)
    yd = yd * jax.nn.silu(z_d.astype(F32))
    mix = jnp.concatenate([hc, yd], axis=-1).astype(x.dtype)
    dt_out = x.dtype
    return (x + mix @ w_out, c_new.astype(dt_out), n_new.astype(dt_out), m_new.astype(dt_out),
            v_rows.astype(dt_out))


def _trunk(x, ssd_conv, ssd_state, gdn_conv, gdn_state, ml_c, ml_n, ml_m, even_w, odd_w, final_norm):
    sc_l, ss_l, gc_l, gs_l, mc_l, mn_l, mm_l, cv_l = [], [], [], [], [], [], [], []
    for layer in range(DEPTH):
        i = layer // 2
        if layer % 2 == 0:
            x, sc, ss, gc, gs = _even_layer(x, ssd_conv[i], ssd_state[i], gdn_conv[i], gdn_state[i],
                                            *[w[i] for w in even_w])
            sc_l.append(sc); ss_l.append(ss); gc_l.append(gc); gs_l.append(gs)
        else:
            x, mc, mn, mm, cv = _odd_layer(x, ml_c[i], ml_n[i], ml_m[i], *[w[i] for w in odd_w])
            mc_l.append(mc); mn_l.append(mn); mm_l.append(mm); cv_l.append(cv)
    y = rmsnorm(x, final_norm)
    return (y, jnp.stack(sc_l), jnp.stack(ss_l), jnp.stack(gc_l), jnp.stack(gs_l),
            jnp.stack(mc_l), jnp.stack(mn_l), jnp.stack(mm_l), jnp.stack(cv_l))


def setup_inputs(seed: int = 0) -> dict:
    key = jax.random.key(seed)
    keys = iter(jax.random.split(key, 48))

    def nrm(shape, scale):
        return jax.random.normal(next(keys), shape, jnp.float32) * scale

    def gain(shape):
        return 1.0 + nrm(shape, 0.01)

    def uni(shape, lo, hi):
        return jax.random.uniform(next(keys), shape, jnp.float32, lo, hi)

    def dt_bias(shape):
        dt = jnp.exp(uni(shape, math.log(1e-3), math.log(1e-1)))
        return dt + jnp.log(-jnp.expm1(-dt))

    return {
        'x_prompt': nrm((BATCH, SEQ, D_MODEL), 1.0),
        'x_sample': nrm((DEC_BATCH, DEC_SEQ, D_MODEL), 1.0),
        'state_ssd_conv': nrm((N_EVEN, DEC_BATCH, CONV_K - 1, SSD_CONV_DIM), 1.0),
        'state_ssd': nrm((N_EVEN, DEC_BATCH, SSD_HEADS, SSD_HEAD_DIM, SSD_STATE), 0.1),
        'state_gdn_conv': nrm((N_EVEN, DEC_BATCH, CONV_K - 1, GDN_CONV_DIM), 1.0),
        'state_gdn': nrm((N_EVEN, DEC_BATCH, GDN_HEADS, GDN_DK, GDN_DV), 0.1),
        'state_mlstm_c': nrm((N_ODD, DEC_BATCH, ML_HEADS, ML_DK, ML_DV), 0.1),
        'state_mlstm_n': nrm((N_ODD, DEC_BATCH, ML_HEADS, ML_DK), 0.1),
        'state_mlstm_m': nrm((N_ODD, DEC_BATCH, ML_HEADS), 1.0),
        'even_norm': gain((N_EVEN, D_MODEL)),
        'even_w_in': nrm((N_EVEN, D_MODEL, EVEN_IN), D_MODEL ** -0.5),
        'ssd_conv_w': nrm((N_EVEN, CONV_K, SSD_CONV_DIM), CONV_K ** -0.5),
        'ssd_conv_b': nrm((N_EVEN, SSD_CONV_DIM), 0.02),
        'ssd_dt_bias': dt_bias((N_EVEN, SSD_HEADS)),
        'ssd_a_log': jnp.log(uni((N_EVEN, SSD_HEADS), 1.0, 16.0)),
        'ssd_d': gain((N_EVEN, SSD_HEADS)),
        'ssd_norm': gain((N_EVEN, SSD_WIDTH)),
        'gdn_conv_w': nrm((N_EVEN, CONV_K, GDN_CONV_DIM), CONV_K ** -0.5),
        'gdn_dt_bias': dt_bias((N_EVEN, GDN_HEADS)),
        'gdn_a_log': jnp.log(uni((N_EVEN, GDN_HEADS), 1.0, 16.0)),
        'gdn_norm': gain((N_EVEN, GDN_DV)),
        'even_w_out': nrm((N_EVEN, EVEN_MIX, D_MODEL), EVEN_MIX ** -0.5),
        'odd_norm': gain((N_ODD, D_MODEL)),
        'odd_w_in': nrm((N_ODD, D_MODEL, ODD_IN), D_MODEL ** -0.5),
        'mlstm_i_bias': nrm((N_ODD, ML_HEADS), 0.1),
        'mlstm_f_bias': jnp.linspace(3.0, 6.0, ML_HEADS, dtype=jnp.float32)[None] + nrm((N_ODD, ML_HEADS), 0.1),
        'mlstm_norm': gain((N_ODD, ML_DV)),
        'cmlp_v_norm': gain((N_ODD, CM_GROUP_DIM)),
        'cmlp_ws': nrm((N_ODD, CM_GROUPS, CM_CHUNK, CM_CHUNK), CM_CHUNK ** -0.5),
        'cmlp_b': gain((N_ODD, CM_GROUPS, CM_CHUNK)),
        'odd_w_out': nrm((N_ODD, ODD_MIX, D_MODEL), ODD_MIX ** -0.5),
        'final_norm': gain((D_MODEL,)),
    }


def reference(x_prompt, x_sample, state_ssd_conv, state_ssd, state_gdn_conv, state_gdn, state_mlstm_c,
              state_mlstm_n, state_mlstm_m, even_norm, even_w_in, ssd_conv_w, ssd_conv_b, ssd_dt_bias, ssd_a_log,
              ssd_d, ssd_norm, gdn_conv_w, gdn_dt_bias, gdn_a_log, gdn_norm, even_w_out, odd_norm, odd_w_in,
              mlstm_i_bias, mlstm_f_bias, mlstm_norm, cmlp_v_norm, cmlp_ws, cmlp_b, odd_w_out, final_norm):
    even_w = (even_norm, even_w_in, ssd_conv_w, ssd_conv_b, ssd_dt_bias, ssd_a_log, ssd_d, ssd_norm,
              gdn_conv_w, gdn_dt_bias, gdn_a_log, gdn_norm, even_w_out)
    odd_w = (odd_norm, odd_w_in, mlstm_i_bias, mlstm_f_bias, mlstm_norm, cmlp_v_norm, cmlp_ws, cmlp_b, odd_w_out)
    bp = x_prompt.shape[0]
    pdt = x_prompt.dtype
    (y_p, sc_p, ss_p, gc_p, gs_p, mc_p, mn_p, mm_p, cv_p) = _trunk(
        x_prompt,
        jnp.zeros((N_EVEN, bp, CONV_K - 1, SSD_CONV_DIM), pdt),
        jnp.zeros((N_EVEN, bp, SSD_HEADS, SSD_HEAD_DIM, SSD_STATE), pdt),
        jnp.zeros((N_EVEN, bp, CONV_K - 1, GDN_CONV_DIM), pdt),
        jnp.zeros((N_EVEN, bp, GDN_HEADS, GDN_DK, GDN_DV), pdt),
        jnp.zeros((N_ODD, bp, ML_HEADS, ML_DK, ML_DV), pdt),
        jnp.zeros((N_ODD, bp, ML_HEADS, ML_DK), pdt),
        jnp.zeros((N_ODD, bp, ML_HEADS), pdt),
        even_w, odd_w, final_norm)
    (y_s, sc_s, ss_s, gc_s, gs_s, mc_s, mn_s, mm_s, cv_s) = _trunk(
        x_sample, state_ssd_conv, state_ssd, state_gdn_conv, state_gdn, state_mlstm_c, state_mlstm_n,
        state_mlstm_m, even_w, odd_w, final_norm)
    return (y_p, y_s, sc_p, sc_s, ss_p, ss_s, gc_p, gc_s, gs_p, gs_s, mc_p, mc_s, mn_p, mn_s, mm_p, mm_s, cv_p, cv_s)
```

```python
import functools
import math

import jax
import jax.numpy as jnp
import numpy as np
from jax import lax
from jax.experimental import pallas as pl
from jax.experimental.pallas import tpu as pltpu

F32 = jnp.float32
BF16 = jnp.bfloat16

D_MODEL = 2048
DEPTH = 4
CHUNK = 128
CONV_K = 4
EPS = 1e-6
NEG = -1e30

SSD_WIDTH = D_MODEL
SSD_HEAD_DIM = 64
SSD_HEADS = SSD_WIDTH // SSD_HEAD_DIM
SSD_STATE = 128
SSD_GROUPS = 4
SSD_CONV_DIM = SSD_WIDTH + 2 * SSD_GROUPS * SSD_STATE
GDN_HEADS = 16
GDN_DK = 128
GDN_DV = 128
GDN_QK = GDN_HEADS * GDN_DK
GDN_V = GDN_HEADS * GDN_DV
GDN_CONV_DIM = 2 * GDN_QK + GDN_V
EVEN_MIX = SSD_WIDTH + GDN_V
ML_HEADS = 8
ML_DK = 128
ML_DV = 256
ML_QK = ML_HEADS * ML_DK
ML_V = ML_HEADS * ML_DV
CM_WIDTH = D_MODEL // 2
CM_GROUPS = 8
CM_GROUP_DIM = CM_WIDTH // CM_GROUPS
CM_CHUNK = 128
ODD_MIX = ML_V + CM_WIDTH

EVEN_MAIN = SSD_WIDTH + SSD_CONV_DIM + GDN_CONV_DIM + GDN_V
ODD_MAIN = 2 * ML_QK + 3 * ML_V + 3 * CM_WIDTH
SMALL_W = 128

VMEM_LIMIT = 56 * 1024 * 1024
ROW_TILE = 832


def _inproj_kernel(x_ref, g_ref, w_ref, ws_ref, z_ref, zs_ref, xn_ref):
    @pl.when(pl.program_id(1) == 0)
    def _():
        x = x_ref[...]
        y = x * lax.rsqrt(jnp.mean(x * x, axis=-1, keepdims=True) + EPS)
        xn = (y * g_ref[...]).astype(BF16)
        xn_ref[...] = xn
        zs_ref[...] = jnp.dot(xn, ws_ref[...], preferred_element_type=F32)

    z_ref[...] = jnp.dot(xn_ref[...], w_ref[...], preferred_element_type=F32)


def _inproj(x, g, w_main, w_small, tn):
    m, d = x.shape
    n = w_main.shape[1]
    tm = ROW_TILE
    return pl.pallas_call(
        _inproj_kernel,
        grid=(m // tm, n // tn),
        in_specs=[
            pl.BlockSpec((tm, d), lambda i, j: (i, 0)),
            pl.BlockSpec((1, d), lambda i, j: (0, 0)),
            pl.BlockSpec((d, tn), lambda i, j: (0, j)),
            pl.BlockSpec((d, SMALL_W), lambda i, j: (0, 0)),
        ],
        out_specs=[
            pl.BlockSpec((tm, tn), lambda i, j: (i, j)),
            pl.BlockSpec((tm, SMALL_W), lambda i, j: (i, 0)),
        ],
        out_shape=[jax.ShapeDtypeStruct((m, n), F32), jax.ShapeDtypeStruct((m, SMALL_W), F32)],
        scratch_shapes=[pltpu.VMEM((tm, d), BF16)],
        compiler_params=pltpu.CompilerParams(
            dimension_semantics=("parallel", "arbitrary"), vmem_limit_bytes=VMEM_LIMIT),
        name="inproj",
    )(x, g.reshape(1, d), w_main, w_small)


def _outproj_kernel(x_ref, mix_ref, w_ref, o_ref):
    o_ref[...] = x_ref[...] + jnp.dot(mix_ref[...], w_ref[...], preferred_element_type=F32)


def _outproj(x, mix, w):
    m, d = x.shape
    k = mix.shape[1]
    tm, tn = ROW_TILE, 512
    return pl.pallas_call(
        _outproj_kernel,
        grid=(m // tm, d // tn),
        in_specs=[
            pl.BlockSpec((tm, tn), lambda i, j: (i, j)),
            pl.BlockSpec((tm, k), lambda i, j: (i, 0)),
            pl.BlockSpec((k, tn), lambda i, j: (0, j)),
        ],
        out_specs=pl.BlockSpec((tm, tn), lambda i, j: (i, j)),
        out_shape=jax.ShapeDtypeStruct((m, d), F32),
        compiler_params=pltpu.CompilerParams(
            dimension_semantics=("parallel", "arbitrary"), vmem_limit_bytes=VMEM_LIMIT),
        name="outproj",
    )(x, mix, w)


def _final_norm_kernel(x_ref, g_ref, o_ref):
    x = x_ref[...]
    o_ref[...] = x * lax.rsqrt(jnp.mean(x * x, axis=-1, keepdims=True) + EPS) * g_ref[...]


def _final_norm(x, g):
    m, d = x.shape
    tm = ROW_TILE
    return pl.pallas_call(
        _final_norm_kernel,
        grid=(m // tm,),
        in_specs=[pl.BlockSpec((tm, d), lambda i: (i, 0)), pl.BlockSpec((1, d), lambda i: (0, 0))],
        out_specs=pl.BlockSpec((tm, d), lambda i: (i, 0)),
        out_shape=jax.ShapeDtypeStruct((m, d), F32),
        compiler_params=pltpu.CompilerParams(dimension_semantics=("parallel",), vmem_limit_bytes=VMEM_LIMIT),
        name="final_norm",
    )(x, g.reshape(1, d))


def _pad_cols(w, width):
    return jnp.pad(w, ((0, 0), (0, width - w.shape[1])))


def _prep_even_w_in(w):
    o1 = SSD_WIDTH + SSD_CONV_DIM
    o2 = o1 + SSD_HEADS
    o3 = o2 + GDN_CONV_DIM + GDN_V
    main = jnp.concatenate([w[:, :o1], w[:, o2:o3]], axis=1).astype(BF16)
    small = _pad_cols(jnp.concatenate([w[:, o1:o2], w[:, o3:]], axis=1), SMALL_W).astype(BF16)
    return main, small


def _prep_odd_w_in(w):
    o1 = 2 * ML_QK + 3 * ML_V
    o2 = o1 + 2 * ML_HEADS
    main = jnp.concatenate([w[:, :o1], w[:, o2:]], axis=1).astype(BF16)
    small = _pad_cols(w[:, o1:o2], SMALL_W).astype(BF16)
    return main, small


def rmsnorm(x, g):
    xf = x.astype(F32)
    y = xf * lax.rsqrt(jnp.mean(xf * xf, axis=-1, keepdims=True) + EPS)
    return (y * g.astype(F32)).astype(x.dtype)


def _l2norm(t):
    t = t.astype(F32)
    return t * lax.rsqrt(jnp.sum(t * t, axis=-1, keepdims=True) + EPS)


def _split(x, sizes):
    return jnp.split(x, [int(i) for i in np.cumsum(sizes)[:-1]], axis=-1)


def _pad_len(x, lp, value=0.0):
    if lp == x.shape[1]:
        return x
    pad = [(0, 0)] * x.ndim
    pad[1] = (0, lp - x.shape[1])
    return jnp.pad(x, pad, constant_values=value)


def _blocks(L):
    qb = min(CHUNK, L)
    return qb, -(-L // qb)


def _to_chunks(t, qb, nc, value=0.0):
    t = _pad_len(t.astype(F32), qb * nc, value)
    t = t.reshape((t.shape[0], nc, qb) + t.shape[2:])
    return jnp.moveaxis(t, 2, 3)


def causal_conv(u, buf, w):
    full = jnp.concatenate([buf.astype(u.dtype), u], axis=1)
    out = lax.conv_general_dilated(full, w[:, None, :].astype(u.dtype), window_strides=(1,), padding='VALID',
                                   dimension_numbers=('NWC', 'WIO', 'NWC'), feature_group_count=u.shape[-1])
    return out, full[:, -(CONV_K - 1):]


def ssd_scan(x, dt, a, bm, cm, s0):
    bsz, L, H, P = x.shape
    G, N = bm.shape[2], bm.shape[3]
    hg = H // G
    q, nc = _blocks(L)
    lp = q * nc
    x = _pad_len(x.astype(F32), lp).reshape(bsz, nc, q, G, hg, P)
    dt = _pad_len(dt.astype(F32), lp).reshape(bsz, nc, q, G, hg)
    bm = _pad_len(bm.astype(F32), lp).reshape(bsz, nc, q, G, N)
    cm = _pad_len(cm.astype(F32), lp).reshape(bsz, nc, q, G, N)
    la = jnp.cumsum(dt * a.reshape(G, hg), axis=2)
    causal = jnp.tril(jnp.ones((q, q), dtype=bool))[:, :, None, None]
    decay = jnp.exp(jnp.where(causal, la[:, :, :, None] - la[:, :, None, :], -jnp.inf))
    cb = jnp.einsum('bctgn,bcsgn->bctsg', cm, bm)
    y_intra = jnp.einsum('bctsgh,bcsgh,bcsghp->bctghp', cb[..., None] * decay, dt, x)
    la_last = la[:, :, -1]
    w_s = jnp.exp(la_last[:, :, None] - la) * dt
    s_loc = jnp.einsum('bcsgh,bcsghp,bcsgn->bcghpn', w_s, x, bm)

    def step(s, inp):
        dec, sl = inp
        return s * dec[..., None, None] + sl, s

    s_fin, s_prev = lax.scan(step, s0.astype(F32).reshape(bsz, G, hg, P, N),
                             (jnp.moveaxis(jnp.exp(la_last), 1, 0), jnp.moveaxis(s_loc, 1, 0)))
    s_prev = jnp.moveaxis(s_prev, 0, 1)
    y_inter = jnp.einsum('bctgn,bcghpn->bctghp', cm, s_prev) * jnp.exp(la)[..., None]
    y = (y_intra + y_inter).reshape(bsz, lp, H, P)[:, :L]
    return y, s_fin.reshape(bsz, H, P, N)


def gdn_scan(q, k, v, beta, g, s0):
    bsz, L, H, K = k.shape
    V = v.shape[-1]
    qb, nc = _blocks(L)
    q, k, v, beta, g = [_to_chunks(t, qb, nc) for t in (q, k, v, beta, g)]
    gc = jnp.cumsum(g, axis=-1)
    causal = jnp.tril(jnp.ones((qb, qb), dtype=bool))
    strict = jnp.tril(jnp.ones((qb, qb), dtype=bool), -1)
    gam = jnp.exp(jnp.where(causal, gc[..., :, None] - gc[..., None, :], -jnp.inf))
    m = jnp.where(strict, beta[..., :, None] * jnp.einsum('bchtd,bchsd->bchts', k, k) * gam, 0.0)
    eye = jnp.eye(qb, dtype=F32)
    tinv = lax.linalg.triangular_solve(eye + m, jnp.broadcast_to(eye, m.shape), left_side=True, lower=True)
    u = jnp.einsum('bchts,bchsv->bchtv', tinv, beta[..., None] * v)
    w = jnp.einsum('bchts,bchsk->bchtk', tinv, (beta * jnp.exp(gc))[..., None] * k)
    aqk = jnp.einsum('bchtd,bchsd->bchts', q, k) * gam
    q_dec = q * jnp.exp(gc)[..., None]
    k_dec = k * jnp.exp(gc[..., -1:] - gc)[..., None]

    def step(s, inp):
        w_c, u_c, kd_c, gl_c = inp
        vn = u_c - jnp.einsum('bhtk,bhkv->bhtv', w_c, s)
        return s * gl_c[..., None, None] + jnp.einsum('bhtk,bhtv->bhkv', kd_c, vn), (s, vn)

    s_fin, (s_prev, vn) = lax.scan(step, s0.astype(F32),
                                   (jnp.moveaxis(w, 1, 0), jnp.moveaxis(u, 1, 0), jnp.moveaxis(k_dec, 1, 0),
                                    jnp.moveaxis(jnp.exp(gc[..., -1]), 1, 0)))
    s_prev = jnp.moveaxis(s_prev, 0, 1)
    vn = jnp.moveaxis(vn, 0, 1)
    o = jnp.einsum('bchtk,bchkv->bchtv', q_dec, s_prev) + jnp.einsum('bchts,bchsv->bchtv', aqk, vn)
    o = jnp.moveaxis(o, 3, 2).reshape(bsz, nc * qb, H, V)[:, :L]
    return o, s_fin


def mlstm_scan(q, k, v, logi, logf, c0, n0, m0):
    bsz, L, H, K = q.shape
    V = v.shape[-1]
    qb, nc = _blocks(L)
    q, k, v, logf = [_to_chunks(t, qb, nc) for t in (q, k, v, logf)]
    logi = _to_chunks(logi, qb, nc, NEG)
    b = jnp.cumsum(logf, axis=-1)
    causal = jnp.tril(jnp.ones((qb, qb), dtype=bool))
    dmat = jnp.where(causal, b[..., :, None] - b[..., None, :] + logi[..., None, :], -jnp.inf)
    m_intra = jnp.max(dmat, axis=-1)
    qk = jnp.einsum('bchtd,bchsd->bchts', q, k) * jnp.exp(dmat - m_intra[..., None])
    h_intra = jnp.einsum('bchts,bchsv->bchtv', qk, v)
    n_intra = jnp.sum(qk, axis=-1)
    b_last = b[..., -1]
    gk = b_last[..., None] - b + logi
    m_k = jnp.max(gk, axis=-1)
    wk = jnp.exp(gk - m_k[..., None])
    c_loc = jnp.einsum('bcht,bchtk,bchtv->bchkv', wk, k, v)
    n_loc = jnp.einsum('bcht,bchtk->bchk', wk, k)

    def step(carry, inp):
        c, n, m = carry
        bl, mk_c, cl, nl = inp
        m_new = jnp.maximum(bl + m, mk_c)
        sa = jnp.exp(bl + m - m_new)
        sb = jnp.exp(mk_c - m_new)
        return (c * sa[..., None, None] + cl * sb[..., None, None], n * sa[..., None] + nl * sb[..., None], m_new), (c, n, m)

    (c_fin, n_fin, m_fin), (c_prev, n_prev, m_prev) = lax.scan(
        step, (c0.astype(F32), n0.astype(F32), m0.astype(F32)),
        (jnp.moveaxis(b_last, 1, 0), jnp.moveaxis(m_k, 1, 0), jnp.moveaxis(c_loc, 1, 0), jnp.moveaxis(n_loc, 1, 0)))
    c_prev = jnp.moveaxis(c_prev, 0, 1)
    n_prev = jnp.moveaxis(n_prev, 0, 1)
    m_prev = jnp.moveaxis(m_prev, 0, 1)
    mb = b + m_prev[..., None]
    m_t = jnp.maximum(mb, m_intra)
    s_inter = jnp.exp(mb - m_t)
    s_intra = jnp.exp(m_intra - m_t)
    num = s_inter[..., None] * jnp.einsum('bchtk,bchkv->bchtv', q, c_prev) + s_intra[..., None] * h_intra
    den = s_inter * jnp.einsum('bchtk,bchk->bcht', q, n_prev) + s_intra * n_intra
    h = num / jnp.maximum(jnp.abs(den), jnp.exp(-m_t))[..., None]
    h = jnp.moveaxis(h, 3, 2).reshape(bsz, nc * qb, H, V)[:, :L]
    return h, c_fin, n_fin, m_fin


def chunk_mlp(u, v, v_gain, ws, wb):
    bsz, L, _ = u.shape
    nc = -(-L // CM_CHUNK)
    tb = CM_CHUNK if nc > 1 else L
    u = jax.nn.gelu(u.astype(F32))
    v = rmsnorm(jax.nn.gelu(v.astype(F32)).reshape(bsz, L, CM_GROUPS, CM_GROUP_DIM), v_gain)
    vp = _pad_len(v, nc * tb).reshape(bsz, nc, tb, CM_GROUPS, CM_GROUP_DIM)
    w_causal = jnp.where(jnp.tril(jnp.ones((tb, tb), dtype=bool)), ws[:, :tb, :tb].astype(F32), 0.0)
    s = jnp.einsum('gts,bcsgd->bctgd', w_causal, vp) + wb[:, :tb].astype(F32).T[None, None, :, :, None]
    s = s.reshape(bsz, nc * tb, CM_WIDTH)[:, :L]
    start = ((L - 1) // CM_CHUNK) * CM_CHUNK
    return u * s, v.reshape(bsz, L, CM_WIDTH)[:, start:]


def _even_mixer(zm, zs, conv_a, s_a, conv_b, s_b, a_conv_w, a_conv_b, a_dt_bias, a_log, a_d, a_norm,
                b_conv_w, b_dt_bias, b_a_log, b_norm):
    bsz, L, _ = zm.shape
    z_a, xbc, qkv, z_b = _split(zm, [SSD_WIDTH, SSD_CONV_DIM, GDN_CONV_DIM, GDN_V])
    dt_raw, beta_raw, g_raw = zs[..., :SSD_HEADS], zs[..., SSD_HEADS:SSD_HEADS + GDN_HEADS], \
        zs[..., SSD_HEADS + GDN_HEADS:SSD_HEADS + 2 * GDN_HEADS]
    xbc, conv_a_new = causal_conv(xbc, conv_a, a_conv_w)
    xbc = jax.nn.silu(xbc + a_conv_b)
    xs, bm, cm = _split(xbc, [SSD_WIDTH, SSD_GROUPS * SSD_STATE, SSD_GROUPS * SSD_STATE])
    xs = xs.reshape(bsz, L, SSD_HEADS, SSD_HEAD_DIM)
    dt = jax.nn.softplus(dt_raw + a_dt_bias)
    y_a, s_a_new = ssd_scan(xs, dt, -jnp.exp(a_log), bm.reshape(bsz, L, SSD_GROUPS, SSD_STATE),
                            cm.reshape(bsz, L, SSD_GROUPS, SSD_STATE), s_a)
    y_a = y_a + a_d[:, None] * xs
    y_a = y_a.reshape(bsz, L, SSD_WIDTH) * jax.nn.silu(z_a)
    y_a = rmsnorm(y_a.reshape(bsz, L, SSD_GROUPS, SSD_WIDTH // SSD_GROUPS),
                  a_norm.reshape(SSD_GROUPS, SSD_WIDTH // SSD_GROUPS)).reshape(bsz, L, SSD_WIDTH)
    qkv, conv_b_new = causal_conv(qkv, conv_b, b_conv_w)
    qkv = jax.nn.silu(qkv)
    q, k, v = _split(qkv, [GDN_QK, GDN_QK, GDN_V])
    q = _l2norm(q.reshape(bsz, L, GDN_HEADS, GDN_DK)) * (GDN_DK ** -0.5)
    k = _l2norm(k.reshape(bsz, L, GDN_HEADS, GDN_DK))
    v = v.reshape(bsz, L, GDN_HEADS, GDN_DV)
    beta = jax.nn.sigmoid(beta_raw)
    g = -jnp.exp(b_a_log) * jax.nn.softplus(g_raw + b_dt_bias)
    o, s_b_new = gdn_scan(q, k, v, beta, g, s_b)
    o = rmsnorm(o, b_norm) * jax.nn.silu(z_b.reshape(bsz, L, GDN_HEADS, GDN_DV))
    mix = jnp.concatenate([y_a, o.reshape(bsz, L, GDN_V)], axis=-1).astype(BF16)
    return mix, conv_a_new, s_a_new, conv_b_new, s_b_new


def _odd_mixer(zm, zs, c0, n0, m0, i_bias, f_bias, ml_norm, v_gain, ws, wb):
    bsz, L, _ = zm.shape
    q, k, v, o_raw, z_c, u_d, v_d, z_d = _split(
        zm, [ML_QK, ML_QK, ML_V, ML_V, ML_V, CM_WIDTH, CM_WIDTH, CM_WIDTH])
    i_raw, f_raw = zs[..., :ML_HEADS], zs[..., ML_HEADS:2 * ML_HEADS]
    q = q.reshape(bsz, L, ML_HEADS, ML_DK)
    k = k.reshape(bsz, L, ML_HEADS, ML_DK) * (ML_DK ** -0.5)
    v = v.reshape(bsz, L, ML_HEADS, ML_DV)
    logi = i_raw + i_bias
    logf = jax.nn.log_sigmoid(f_raw + f_bias)
    hc, c_new, n_new, m_new = mlstm_scan(q, k, v, logi, logf, c0, n0, m0)
    hc = rmsnorm(hc, ml_norm) * jax.nn.sigmoid(o_raw.reshape(bsz, L, ML_HEADS, ML_DV))
    hc = hc.reshape(bsz, L, ML_V) * jax.nn.silu(z_c)
    yd, v_rows = chunk_mlp(u_d, v_d, v_gain, ws, wb)
    yd = yd * jax.nn.silu(z_d)
    mix = jnp.concatenate([hc, yd], axis=-1).astype(BF16)
    return mix, c_new, n_new, m_new, v_rows


def kernel(x_prompt, x_sample, state_ssd_conv, state_ssd, state_gdn_conv, state_gdn, state_mlstm_c,
           state_mlstm_n, state_mlstm_m, even_norm, even_w_in, ssd_conv_w, ssd_conv_b, ssd_dt_bias, ssd_a_log,
           ssd_d, ssd_norm, gdn_conv_w, gdn_dt_bias, gdn_a_log, gdn_norm, even_w_out, odd_norm, odd_w_in,
           mlstm_i_bias, mlstm_f_bias, mlstm_norm, cmlp_v_norm, cmlp_ws, cmlp_b, odd_w_out, final_norm):
    bp, seq, d = x_prompt.shape
    bs = x_sample.shape[0]
    mp = bp * seq
    x = jnp.concatenate([x_prompt.reshape(mp, d), x_sample.reshape(bs, d)], axis=0)

    outs_p = {k: [] for k in ("sc", "ss", "gc", "gs", "mc", "mn", "mm", "cv")}
    outs_s = {k: [] for k in ("sc", "ss", "gc", "gs", "mc", "mn", "mm", "cv")}
    for layer in range(DEPTH):
        i = layer // 2
        if layer % 2 == 0:
            w_main, w_small = _prep_even_w_in(even_w_in[i])
            zm, zs = _inproj(x, even_norm[i], w_main, w_small, tn=1024)
            wts = (ssd_conv_w[i], ssd_conv_b[i], ssd_dt_bias[i], ssd_a_log[i], ssd_d[i], ssd_norm[i],
                   gdn_conv_w[i], gdn_dt_bias[i], gdn_a_log[i], gdn_norm[i])
            rp = _even_mixer(zm[:mp].reshape(bp, seq, -1), zs[:mp].reshape(bp, seq, -1),
                             jnp.zeros((bp, CONV_K - 1, SSD_CONV_DIM), F32),
                             jnp.zeros((bp, SSD_HEADS, SSD_HEAD_DIM, SSD_STATE), F32),
                             jnp.zeros((bp, CONV_K - 1, GDN_CONV_DIM), F32),
                             jnp.zeros((bp, GDN_HEADS, GDN_DK, GDN_DV), F32), *wts)
            rs = _even_mixer(zm[mp:].reshape(bs, 1, -1), zs[mp:].reshape(bs, 1, -1),
                             state_ssd_conv[i], state_ssd[i], state_gdn_conv[i], state_gdn[i], *wts)
            for o, r in ((outs_p, rp), (outs_s, rs)):
                o["sc"].append(r[1]); o["ss"].append(r[2]); o["gc"].append(r[3]); o["gs"].append(r[4])
            w_out = even_w_out[i].astype(BF16)
        else:
            w_main, w_small = _prep_odd_w_in(odd_w_in[i])
            zm, zs = _inproj(x, odd_norm[i], w_main, w_small, tn=1024)
            wts = (mlstm_i_bias[i], mlstm_f_bias[i], mlstm_norm[i], cmlp_v_norm[i], cmlp_ws[i], cmlp_b[i])
            rp = _odd_mixer(zm[:mp].reshape(bp, seq, -1), zs[:mp].reshape(bp, seq, -1),
                            jnp.zeros((bp, ML_HEADS, ML_DK, ML_DV), F32), jnp.zeros((bp, ML_HEADS, ML_DK), F32),
                            jnp.zeros((bp, ML_HEADS), F32), *wts)
            rs = _odd_mixer(zm[mp:].reshape(bs, 1, -1), zs[mp:].reshape(bs, 1, -1),
                            state_mlstm_c[i], state_mlstm_n[i], state_mlstm_m[i], *wts)
            for o, r in ((outs_p, rp), (outs_s, rs)):
                o["mc"].append(r[1]); o["mn"].append(r[2]); o["mm"].append(r[3]); o["cv"].append(r[4])
            w_out = odd_w_out[i].astype(BF16)
        mix = jnp.concatenate([rp[0].reshape(mp, -1), rs[0].reshape(bs, -1)], axis=0)
        x = _outproj(x, mix, w_out)

    y = _final_norm(x, final_norm)
    y_p = y[:mp].reshape(bp, seq, d)
    y_s = y[mp:].reshape(bs, 1, d)
    st = lambda o, k: jnp.stack(o[k])
    return (y_p, y_s, st(outs_p, "sc"), st(outs_s, "sc"), st(outs_p, "ss"), st(outs_s, "ss"),
            st(outs_p, "gc"), st(outs_s, "gc"), st(outs_p, "gs"), st(outs_s, "gs"),
            st(outs_p, "mc"), st(outs_s, "mc"), st(outs_p, "mn"), st(outs_s, "mn"),
            st(outs_p, "mm"), st(outs_s, "mm"), st(outs_p, "cv"), st(outs_s, "cv"))
```

```python
import jax
import jax.numpy as jnp
import numpy as np
from jax import lax
from jax.experimental import pallas as pl
from jax.experimental.pallas import tpu as pltpu

F32 = jnp.float32
BF16 = jnp.bfloat16
HI = lax.Precision.HIGHEST

D_MODEL = 2048
DEPTH = 4
CHUNK = 128
CONV_K = 4
EPS = 1e-6
NEG = -1e30

SSD_WIDTH = D_MODEL
SSD_HEAD_DIM = 64
SSD_HEADS = SSD_WIDTH // SSD_HEAD_DIM
SSD_STATE = 128
SSD_GROUPS = 4
SSD_GW = SSD_WIDTH // SSD_GROUPS
SSD_BC = 2 * SSD_GROUPS * SSD_STATE
SSD_CONV_DIM = SSD_WIDTH + SSD_BC
GDN_HEADS = 16
GDN_DK = 128
GDN_DV = 128
GDN_QK = GDN_HEADS * GDN_DK
GDN_V = GDN_HEADS * GDN_DV
GDN_CONV_DIM = 2 * GDN_QK + GDN_V
GDN_HG = 8
ML_HEADS = 8
ML_DK = 128
ML_DV = 256
ML_QK = ML_HEADS * ML_DK
ML_V = ML_HEADS * ML_DV
CM_WIDTH = D_MODEL // 2
CM_GROUPS = 8
CM_GROUP_DIM = CM_WIDTH // CM_GROUPS
CM_CHUNK = 128

LANES = 128
SUBLANES = 8

EVEN_MAIN = 6 * D_MODEL + SSD_BC
EVEN_SMALL = 3 * LANES
ODD_MAIN = 2 * ML_QK + 3 * ML_V + 3 * CM_WIDTH
ODD_SMALL = LANES

VMEM_LIMIT = 56 * 1024 * 1024
ROW_TILE = 832


def _silu(x):
    return x * jax.nn.sigmoid(x)


def _softplus(x):
    return jnp.maximum(x, 0.0) + jnp.log1p(jnp.exp(-jnp.abs(x)))


def _gelu(x):
    return 0.5 * x * (1.0 + jnp.tanh(np.sqrt(2.0 / np.pi).astype(np.float32) * (x + 0.044715 * (x * x * x))))


def _mm(a, b):
    return jnp.dot(a.astype(BF16), b.astype(BF16), preferred_element_type=F32)


def _mm_nt(a, b):
    return lax.dot_general(a.astype(BF16), b.astype(BF16), (((1,), (1,)), ((), ())), preferred_element_type=F32)


def _mm_tn(a, b):
    return lax.dot_general(a.astype(BF16), b.astype(BF16), (((0,), (0,)), ((), ())), preferred_element_type=F32)


def _mm_hi(a, b):
    return jnp.dot(a, b, precision=HI, preferred_element_type=F32)


def _iota(shape, axis):
    return lax.broadcasted_iota(jnp.int32, shape, axis)


def _params(*sem):
    return pltpu.CompilerParams(dimension_semantics=sem, vmem_limit_bytes=VMEM_LIMIT)


def _inproj_kernel(x_ref, g_ref, w_ref, ws_ref, z_ref, zs_ref, xn_ref):
    @pl.when(pl.program_id(1) == 0)
    def _():
        x = x_ref[...]
        y = x * lax.rsqrt(jnp.mean(x * x, axis=-1, keepdims=True) + EPS)
        xn = (y * g_ref[...]).astype(BF16)
        xn_ref[...] = xn
        zs_ref[...] = jnp.dot(xn, ws_ref[...], preferred_element_type=F32)

    z_ref[...] = jnp.dot(xn_ref[...], w_ref[...], preferred_element_type=F32)


def _inproj(x, g, w_main, w_small, tn):
    m, d = x.shape
    n = w_main.shape[1]
    ns = w_small.shape[1]
    tm = ROW_TILE
    return pl.pallas_call(
        _inproj_kernel,
        grid=(m // tm, n // tn),
        in_specs=[
            pl.BlockSpec((tm, d), lambda i, j: (i, 0)),
            pl.BlockSpec((1, d), lambda i, j: (0, 0)),
            pl.BlockSpec((d, tn), lambda i, j: (0, j)),
            pl.BlockSpec((d, ns), lambda i, j: (0, 0)),
        ],
        out_specs=[
            pl.BlockSpec((tm, tn), lambda i, j: (i, j)),
            pl.BlockSpec((tm, ns), lambda i, j: (i, 0)),
        ],
        out_shape=[jax.ShapeDtypeStruct((m, n), F32), jax.ShapeDtypeStruct((m, ns), F32)],
        scratch_shapes=[pltpu.VMEM((tm, d), BF16)],
        compiler_params=_params("parallel", "arbitrary"),
        name="inproj",
    )(x, g.reshape(1, d), w_main, w_small)


def _outproj_kernel(x_ref, ma_ref, mb_ref, wa_ref, wb_ref, o_ref):
    o_ref[...] = (x_ref[...] + jnp.dot(ma_ref[...], wa_ref[...], preferred_element_type=F32)
                  + jnp.dot(mb_ref[...], wb_ref[...], preferred_element_type=F32))


def _outproj(x, mix_a, mix_b, w):
    m, d = x.shape
    ka, kb = mix_a.shape[1], mix_b.shape[1]
    tm, tn = ROW_TILE, 512
    return pl.pallas_call(
        _outproj_kernel,
        grid=(m // tm, d // tn),
        in_specs=[
            pl.BlockSpec((tm, tn), lambda i, j: (i, j)),
            pl.BlockSpec((tm, ka), lambda i, j: (i, 0)),
            pl.BlockSpec((tm, kb), lambda i, j: (i, 0)),
            pl.BlockSpec((ka, tn), lambda i, j: (0, j)),
            pl.BlockSpec((kb, tn), lambda i, j: (ka // kb, j)),
        ],
        out_specs=pl.BlockSpec((tm, tn), lambda i, j: (i, j)),
        out_shape=jax.ShapeDtypeStruct((m, d), F32),
        compiler_params=_params("parallel", "arbitrary"),
        name="outproj",
    )(x, mix_a, mix_b, w, w)


def _final_norm_kernel(x_ref, g_ref, o_ref):
    x = x_ref[...]
    o_ref[...] = x * lax.rsqrt(jnp.mean(x * x, axis=-1, keepdims=True) + EPS) * g_ref[...]


def _final_norm(x, g):
    m, d = x.shape
    tm = ROW_TILE
    return pl.pallas_call(
        _final_norm_kernel,
        grid=(m // tm,),
        in_specs=[pl.BlockSpec((tm, d), lambda i: (i, 0)), pl.BlockSpec((1, d), lambda i: (0, 0))],
        out_specs=pl.BlockSpec((tm, d), lambda i: (i, 0)),
        out_shape=jax.ShapeDtypeStruct((m, d), F32),
        compiler_params=_params("parallel"),
        name="final_norm",
    )(x, g.reshape(1, d))


def _pad_cols(w, width):
    return jnp.pad(w, ((0, 0), (0, width - w.shape[1])))


def _prep_even_w_in(w):
    o_xs = SSD_WIDTH
    o_bc = o_xs + SSD_WIDTH
    o_dt = o_bc + SSD_BC
    o_q = o_dt + SSD_HEADS
    o_zb = o_q + GDN_CONV_DIM
    o_beta = o_zb + GDN_V
    o_g = o_beta + GDN_HEADS
    main = jnp.concatenate([w[:, :o_bc], w[:, o_q:o_beta], w[:, o_bc:o_dt]], axis=1).astype(BF16)
    small = [_pad_cols(w[:, o_dt:o_q], LANES)]
    for hg in range(GDN_HEADS // GDN_HG):
        sl = slice(hg * GDN_HG, (hg + 1) * GDN_HG)
        small.append(_pad_cols(jnp.concatenate([w[:, o_beta:o_g][:, sl], w[:, o_g:][:, sl]], axis=1), LANES))
    return main, jnp.concatenate(small, axis=1).astype(BF16)


def _prep_odd_w_in(w):
    o1 = 2 * ML_QK + 3 * ML_V
    o2 = o1 + 2 * ML_HEADS
    main = jnp.concatenate([w[:, :o1], w[:, o2:]], axis=1).astype(BF16)
    small = _pad_cols(w[:, o1:o2], LANES).astype(BF16)
    return main, small


def _conv_chunk(ext_ref, u, w_ref, first):
    @pl.when(first)
    def _():
        ext_ref[0:SUBLANES, :] = jnp.zeros((SUBLANES, ext_ref.shape[1]), F32)

    ext_ref[SUBLANES:SUBLANES + CHUNK, :] = u
    out = w_ref[CONV_K - 1:CONV_K, :] * u
    for k in range(1, CONV_K):
        out = out + w_ref[CONV_K - 1 - k:CONV_K - k, :] * ext_ref[SUBLANES - k:SUBLANES - k + CHUNK, :]
    ext_ref[0:SUBLANES, :] = ext_ref[CHUNK:CHUNK + SUBLANES, :]
    return out


def _conv_tail(ext_ref):
    return ext_ref[SUBLANES - (CONV_K - 1):SUBLANES, :]


def _causal_masks():
    r = _iota((CHUNK, CHUNK), 0)
    c = _iota((CHUNK, CHUNK), 1)
    return r, c


def _ssd_prompt_kernel(za_ref, xs_ref, bc_ref, zc_ref, zr_ref, wx_ref, bx_ref, wbc_ref, bbc_ref, pc_ref, pr_ref,
                       d_ref, nrm_ref, mix_ref, cx_ref, cbc_ref, st_ref, extx, extbc, s_ref):
    c_id = pl.program_id(1)
    first = c_id == 0
    last = c_id == pl.num_programs(1) - 1

    @pl.when(first)
    def _():
        s_ref[...] = jnp.zeros(s_ref.shape, F32)

    xs = _silu(_conv_chunk(extx, xs_ref[...], wx_ref, first) + bx_ref[...])
    bc = _silu(_conv_chunk(extbc, bc_ref[...], wbc_ref, first) + bbc_ref[...])

    r, c = _causal_masks()
    causal = r >= c
    tril = jnp.where(causal, 1.0, 0.0)
    triu = jnp.where(r <= c, 1.0, 0.0)
    dt = _softplus(zc_ref[...] + pc_ref[0:1, :])
    la = _mm_hi(tril, dt * (-jnp.exp(pc_ref[1:2, :])))
    dtr = _softplus(zr_ref[...] + pr_ref[0])
    lar = _mm_hi(dtr * (-jnp.exp(pr_ref[1])), triu)
    la_last = la[CHUNK - 1:CHUNK, :]
    e_mat = jnp.where((_iota((LANES, SSD_WIDTH), 1) >> 6) == _iota((LANES, SSD_WIDTH), 0), 1.0, 0.0)
    ela_x = _mm_hi(jnp.exp(la), e_mat)
    wsx = _mm_hi(jnp.exp(la_last - la) * dt, e_mat)
    dec_x = _mm_hi(jnp.broadcast_to(jnp.exp(la_last), (SUBLANES, LANES)), e_mat)[0:1, :]
    lane_lo = _iota((CHUNK, LANES), 1) < SSD_HEAD_DIM

    hpg = SSD_HEADS // SSD_GROUPS
    for g in range(SSD_GROUPS):
        gs = slice(g * SSD_GW, (g + 1) * SSD_GW)
        bg = bc[:, g * SSD_STATE:(g + 1) * SSD_STATE]
        cg = bc[:, SSD_GROUPS * SSD_STATE + g * SSD_STATE:SSD_GROUPS * SSD_STATE + (g + 1) * SSD_STATE]
        cb = _mm_nt(cg, bg)
        ys = []
        for pair in range(hpg // 2):
            h0 = g * hpg + 2 * pair
            xpair = xs[:, h0 * SSD_HEAD_DIM:(h0 + 2) * SSD_HEAD_DIM]
            halves = []
            for hh in (h0, h0 + 1):
                seg = jnp.where(causal, la[:, hh:hh + 1] - lar[hh:hh + 1, :], NEG)
                lmat = jnp.exp(seg) * cb * dtr[hh:hh + 1, :]
                halves.append(_mm(lmat, xpair))
            ys.append(jnp.where(lane_lo, halves[0], halves[1]))
        y = jnp.concatenate(ys, axis=1)
        s_prev = s_ref[g]
        y = y + _mm(cg, s_prev) * ela_x[:, gs] + d_ref[:, gs] * xs[:, gs]
        y = y * _silu(za_ref[:, gs])
        y = y * lax.rsqrt(jnp.mean(y * y, axis=-1, keepdims=True) + EPS) * nrm_ref[:, gs]
        mix_ref[:, gs] = y.astype(BF16)
        s_ref[g] = s_prev * dec_x[:, gs] + _mm_tn(bg, xs[:, gs] * wsx[:, gs])

    @pl.when(last)
    def _():
        st_ref[0] = s_ref[...]
        cx_ref[0] = _conv_tail(extx)
        cbc_ref[0] = _conv_tail(extbc)


def _ssd_prompt(zm, zs, zs_t, m_total, bsz, seq, conv_w, conv_b, dt_bias, a_log, d_skip, norm):
    nc = seq // CHUNK
    rb = lambda b, c: b * nc + c
    pc = jnp.zeros((SUBLANES, LANES), F32).at[0, :SSD_HEADS].set(dt_bias).at[1, :SSD_HEADS].set(a_log)
    pr = jnp.stack([jnp.broadcast_to(dt_bias[:, None], (SSD_HEADS, CHUNK)),
                    jnp.broadcast_to(a_log[:, None], (SSD_HEADS, CHUNK))])
    full = lambda *shape: pl.BlockSpec(shape, lambda b, c: (0,) * len(shape))
    return pl.pallas_call(
        _ssd_prompt_kernel,
        grid=(bsz, nc),
        in_specs=[
            pl.BlockSpec((CHUNK, SSD_WIDTH), lambda b, c: (rb(b, c), 0)),
            pl.BlockSpec((CHUNK, SSD_WIDTH), lambda b, c: (rb(b, c), 1)),
            pl.BlockSpec((CHUNK, SSD_BC), lambda b, c: (rb(b, c), 6 * D_MODEL // SSD_BC)),
            pl.BlockSpec((CHUNK, LANES), lambda b, c: (rb(b, c), 0)),
            pl.BlockSpec((SSD_HEADS, CHUNK), lambda b, c: (0, rb(b, c))),
            full(CONV_K, SSD_WIDTH), full(1, SSD_WIDTH), full(CONV_K, SSD_BC), full(1, SSD_BC),
            full(SUBLANES, LANES), full(2, SSD_HEADS, CHUNK), full(1, SSD_WIDTH), full(1, SSD_WIDTH),
        ],
        out_specs=[
            pl.BlockSpec((CHUNK, SSD_WIDTH), lambda b, c: (rb(b, c), 0)),
            pl.BlockSpec((1, CONV_K - 1, SSD_WIDTH), lambda b, c: (b, 0, 0)),
            pl.BlockSpec((1, CONV_K - 1, SSD_BC), lambda b, c: (b, 0, 0)),
            pl.BlockSpec((1, SSD_GROUPS, SSD_STATE, SSD_GW), lambda b, c: (b, 0, 0, 0)),
        ],
        out_shape=[
            jax.ShapeDtypeStruct((m_total, SSD_WIDTH), BF16),
            jax.ShapeDtypeStruct((bsz, CONV_K - 1, SSD_WIDTH), F32),
            jax.ShapeDtypeStruct((bsz, CONV_K - 1, SSD_BC), F32),
            jax.ShapeDtypeStruct((bsz, SSD_GROUPS, SSD_STATE, SSD_GW), F32),
        ],
        scratch_shapes=[pltpu.VMEM((SUBLANES + CHUNK, SSD_WIDTH), F32), pltpu.VMEM((SUBLANES + CHUNK, SSD_BC), F32),
                        pltpu.VMEM((SSD_GROUPS, SSD_STATE, SSD_GW), F32)],
        compiler_params=_params("parallel", "arbitrary"),
        name="ssd_prompt",
    )(zm, zm, zm, zs, zs_t, conv_w[:, :SSD_WIDTH], conv_b[None, :SSD_WIDTH], conv_w[:, SSD_WIDTH:],
      conv_b[None, SSD_WIDTH:], pc, pr, jnp.repeat(d_skip, SSD_HEAD_DIM)[None, :], norm[None, :])


def _tri_inverse(mats, r, c):
    def corner(level):
        return ((r >> (level + 1)) == (c >> (level + 1))) & (((r >> level) & 1) == 1) & (((c >> level) & 1) == 0)

    eye = jnp.where(r == c, 1.0, 0.0)
    ts = [eye - jnp.where(corner(0), a, 0.0) for a in mats]
    for level in range(1, 7):
        cm = corner(level)
        xs = [_mm(t, jnp.where(cm, a, 0.0)) for t, a in zip(ts, mats)]
        ts = [t - _mm(x, t) for t, x in zip(ts, xs)]
    return ts


def _gdn_prompt_kernel(q_ref, k_ref, v_ref, zb_ref, zc_ref, zr_ref, wq_ref, wk_ref, wv_ref, pc_ref, pr_ref, nrm_ref,
                       mix_ref, cq_ref, ck_ref, cv_ref, st_ref, extq, extk, extv, s_ref):
    c_id = pl.program_id(2)
    first = c_id == 0
    last = c_id == pl.num_programs(2) - 1

    @pl.when(first)
    def _():
        s_ref[...] = jnp.zeros(s_ref.shape, F32)

    q_all = _silu(_conv_chunk(extq, q_ref[...], wq_ref, first))
    k_all = _silu(_conv_chunk(extk, k_ref[...], wk_ref, first))
    v_all = _silu(_conv_chunk(extv, v_ref[...], wv_ref, first))

    r, c = _causal_masks()
    causal = r >= c
    strict = r > c
    tril = jnp.where(causal, 1.0, 0.0)
    triu = jnp.where(r <= c, 1.0, 0.0)
    zc = zc_ref[...]
    beta_c = jax.nn.sigmoid(zc)
    gc_c = _mm_hi(tril, -jnp.exp(pc_ref[0:1, :]) * _softplus(zc + pc_ref[1:2, :]))
    gc_r = _mm_hi(-jnp.exp(pr_ref[0]) * _softplus(zr_ref[GDN_HG:2 * GDN_HG, :] + pr_ref[1]), triu)

    heads = range(GDN_HG)
    hs = [slice(j * GDN_DK, (j + 1) * GDN_DK) for j in heads]
    qh = [q_all[:, s] for s in hs]
    kh = [k_all[:, s] for s in hs]
    qh = [x * (lax.rsqrt(jnp.sum(x * x, axis=-1, keepdims=True) + EPS) * (GDN_DK ** -0.5)) for x in qh]
    kh = [x * lax.rsqrt(jnp.sum(x * x, axis=-1, keepdims=True) + EPS) for x in kh]
    gcc = [gc_c[:, GDN_HG + j:GDN_HG + j + 1] for j in heads]
    beta = [beta_c[:, j:j + 1] for j in heads]
    gam = [jnp.exp(jnp.where(causal, gcc[j] - gc_r[j:j + 1, :], NEG)) for j in heads]
    qkk = [_mm_nt(jnp.concatenate([qh[j], kh[j]], axis=0), kh[j]) for j in heads]
    aqk = [qkk[j][:CHUNK] * gam[j] for j in heads]
    tinv = _tri_inverse([jnp.where(strict, beta[j] * qkk[j][CHUNK:] * gam[j], 0.0) for j in heads], r, c)
    egc = [jnp.exp(g) for g in gcc]
    uw = [_mm(tinv[j], jnp.concatenate([beta[j] * v_all[:, hs[j]], (beta[j] * egc[j]) * kh[j]], axis=1))
          for j in heads]
    s_prev = [s_ref[j] for j in heads]
    ws_qs = [_mm(jnp.concatenate([uw[j][:, GDN_DV:], qh[j] * egc[j]], axis=0), s_prev[j]) for j in heads]
    vn = [uw[j][:, :GDN_DV] - ws_qs[j][:CHUNK] for j in heads]
    o = [ws_qs[j][CHUNK:] + _mm(aqk[j], vn[j]) for j in heads]
    for j in heads:
        gc_last = gcc[j][CHUNK - 1:CHUNK, :]
        s_ref[j] = s_prev[j] * jnp.exp(gc_last) + _mm_tn(kh[j] * jnp.exp(gc_last - gcc[j]), vn[j])
    for j in heads:
        on = o[j] * lax.rsqrt(jnp.mean(o[j] * o[j], axis=-1, keepdims=True) + EPS) * nrm_ref[...]
        mix_ref[:, hs[j]] = (on * _silu(zb_ref[:, hs[j]])).astype(BF16)

    @pl.when(last)
    def _():
        st_ref[0] = s_ref[...]
        cq_ref[0] = _conv_tail(extq)
        ck_ref[0] = _conv_tail(extk)
        cv_ref[0] = _conv_tail(extv)


def _gdn_prompt(zm, zs, zs_t, m_total, bsz, seq, conv_w, dt_bias, a_log, norm):
    nc = seq // CHUNK
    nhg = GDN_HEADS // GDN_HG
    w = GDN_HG * GDN_DK
    rb = lambda b, c: b * nc + c
    col0 = 2 * D_MODEL // w
    pc = jnp.zeros((nhg, SUBLANES, LANES), F32)
    pc = pc.at[:, 0, GDN_HG:2 * GDN_HG].set(a_log.reshape(nhg, GDN_HG))
    pc = pc.at[:, 1, GDN_HG:2 * GDN_HG].set(dt_bias.reshape(nhg, GDN_HG))
    pr = jnp.stack([jnp.broadcast_to(a_log.reshape(nhg, GDN_HG, 1), (nhg, GDN_HG, CHUNK)),
                    jnp.broadcast_to(dt_bias.reshape(nhg, GDN_HG, 1), (nhg, GDN_HG, CHUNK))], axis=1)
    seg = lambda s: pl.BlockSpec((CHUNK, w), lambda b, h, c: (rb(b, c), col0 + s * nhg + h))
    wseg = lambda s: pl.BlockSpec((CONV_K, w), lambda b, h, c: (0, s * nhg + h))
    cout = pl.BlockSpec((1, CONV_K - 1, w), lambda b, h, c: (b, 0, h))
    return pl.pallas_call(
        _gdn_prompt_kernel,
        grid=(bsz, nhg, nc),
        in_specs=[
            seg(0), seg(1), seg(2), seg(3),
            pl.BlockSpec((CHUNK, LANES), lambda b, h, c: (rb(b, c), 1 + h)),
            pl.BlockSpec((2 * GDN_HG, CHUNK), lambda b, h, c: ((1 + h) * LANES // (2 * GDN_HG), rb(b, c))),
            wseg(0), wseg(1), wseg(2),
            pl.BlockSpec((None, SUBLANES, LANES), lambda b, h, c: (h, 0, 0)),
            pl.BlockSpec((None, 2, GDN_HG, CHUNK), lambda b, h, c: (h, 0, 0, 0)),
            pl.BlockSpec((1, GDN_DV), lambda b, h, c: (0, 0)),
        ],
        out_specs=[
            pl.BlockSpec((CHUNK, w), lambda b, h, c: (rb(b, c), h)),
            cout, cout, cout,
            pl.BlockSpec((1, GDN_HG, GDN_DK, GDN_DV), lambda b, h, c: (b, h, 0, 0)),
        ],
        out_shape=[
            jax.ShapeDtypeStruct((m_total, GDN_V), BF16),
            jax.ShapeDtypeStruct((bsz, CONV_K - 1, GDN_QK), F32),
            jax.ShapeDtypeStruct((bsz, CONV_K - 1, GDN_QK), F32),
            jax.ShapeDtypeStruct((bsz, CONV_K - 1, GDN_V), F32),
            jax.ShapeDtypeStruct((bsz, GDN_HEADS, GDN_DK, GDN_DV), F32),
        ],
        scratch_shapes=[pltpu.VMEM((SUBLANES + CHUNK, w), F32)] * 3 + [pltpu.VMEM((GDN_HG, GDN_DK, GDN_DV), F32)],
        compiler_params=_params("parallel", "parallel", "arbitrary"),
        name="gdn_prompt",
    )(zm, zm, zm, zm, zs, zs_t, conv_w, conv_w, conv_w, pc, pr, norm[None, :])


def _mlstm_prompt_kernel(q_ref, k_ref, v_ref, o_ref, zc_ref, gc_ref, gr_ref, pc_ref, pr_ref, nrm_ref,
                         mix_ref, c_out, n_out, m_out, c_ref, n_ref, m_ref):
    c_id = pl.program_id(1)
    first = c_id == 0
    last = c_id == pl.num_programs(1) - 1

    @pl.when(first)
    def _():
        c_ref[...] = jnp.zeros(c_ref.shape, F32)
        n_ref[...] = jnp.zeros(n_ref.shape, F32)
        m_ref[...] = jnp.zeros(m_ref.shape, F32)

    r, c = _causal_masks()
    causal = r >= c
    tril = jnp.where(causal, 1.0, 0.0)
    triu = jnp.where(r <= c, 1.0, 0.0)
    gc = gc_ref[...]
    logi_c = gc + pc_ref[0:1, :]
    b_c = _mm_hi(tril, -_softplus(-(gc + pc_ref[1:2, :])))
    logi_r = gr_ref[0:ML_HEADS, :] + pr_ref[0]
    b_r = _mm_hi(-_softplus(-(gr_ref[ML_HEADS:2 * ML_HEADS, :] + pr_ref[1])), triu)

    heads = range(ML_HEADS)
    ks = [slice(j * ML_DK, (j + 1) * ML_DK) for j in heads]
    vs = [slice(j * ML_DV, (j + 1) * ML_DV) for j in heads]
    q = [q_ref[:, s] for s in ks]
    k = [k_ref[:, s] * (ML_DK ** -0.5) for s in ks]
    bc = [b_c[:, ML_HEADS + j:ML_HEADS + j + 1] for j in heads]
    dmat = [jnp.where(causal, bc[j] - b_r[j:j + 1, :] + logi_r[j:j + 1, :], NEG) for j in heads]
    m_intra = [jnp.max(x, axis=-1, keepdims=True) for x in dmat]
    p = [_mm_nt(q[j], k[j]) * jnp.exp(dmat[j] - m_intra[j]) for j in heads]
    h_intra = [_mm(p[j], v_ref[:, vs[j]]) for j in heads]
    c_prev = [c_ref[j] for j in heads]
    qc = [_mm(q[j], c_prev[j]) for j in heads]
    b_last = [x[CHUNK - 1:CHUNK, :] for x in bc]
    gk = [b_last[j] - bc[j] + logi_c[:, j:j + 1] for j in heads]
    m_k = [jnp.max(x, axis=0, keepdims=True) for x in gk]
    kw = [k[j] * jnp.exp(gk[j] - m_k[j]) for j in heads]
    c_loc = [_mm_tn(kw[j], v_ref[:, vs[j]]) for j in heads]
    for j in heads:
        n_intra = jnp.sum(p[j], axis=-1, keepdims=True)
        m_prev = m_ref[j:j + 1, 0:1]
        n_prev = n_ref[j:j + 1, :]
        mb = bc[j] + m_prev
        m_t = jnp.maximum(mb, m_intra[j])
        s_inter = jnp.exp(mb - m_t)
        s_intra = jnp.exp(m_intra[j] - m_t)
        num = s_inter * qc[j] + s_intra * h_intra[j]
        den = s_inter * jnp.sum(q[j] * n_prev, axis=-1, keepdims=True) + s_intra * n_intra
        h = num / jnp.maximum(jnp.abs(den), jnp.exp(-m_t))
        m_new = jnp.maximum(b_last[j] + m_prev, m_k[j])
        sa = jnp.exp(b_last[j] + m_prev - m_new)
        sb = jnp.exp(m_k[j] - m_new)
        c_ref[j] = c_prev[j] * sa + c_loc[j] * sb
        n_ref[j:j + 1, :] = n_prev * sa + jnp.sum(kw[j], axis=0, keepdims=True) * sb
        m_ref[j:j + 1, :] = jnp.broadcast_to(m_new, (1, LANES))
        h = h * lax.rsqrt(jnp.mean(h * h, axis=-1, keepdims=True) + EPS) * nrm_ref[...]
        mix_ref[:, vs[j]] = (h * jax.nn.sigmoid(o_ref[:, vs[j]]) * _silu(zc_ref[:, vs[j]])).astype(BF16)

    @pl.when(last)
    def _():
        c_out[0] = c_ref[...]
        n_out[0] = n_ref[...]
        m_out[0] = m_ref[...]


def _mlstm_prompt(zm, zs, zs_t, m_total, bsz, seq, i_bias, f_bias, norm):
    nc = seq // CHUNK
    rb = lambda b, c: b * nc + c
    pc = jnp.zeros((SUBLANES, LANES), F32).at[0, :ML_HEADS].set(i_bias).at[1, ML_HEADS:2 * ML_HEADS].set(f_bias)
    pr = jnp.stack([jnp.broadcast_to(i_bias[:, None], (ML_HEADS, CHUNK)),
                    jnp.broadcast_to(f_bias[:, None], (ML_HEADS, CHUNK))])
    full = lambda *shape: pl.BlockSpec(shape, lambda b, c: (0,) * len(shape))
    return pl.pallas_call(
        _mlstm_prompt_kernel,
        grid=(bsz, nc),
        in_specs=[
            pl.BlockSpec((CHUNK, ML_QK), lambda b, c: (rb(b, c), 0)),
            pl.BlockSpec((CHUNK, ML_QK), lambda b, c: (rb(b, c), 1)),
            pl.BlockSpec((CHUNK, ML_V), lambda b, c: (rb(b, c), 1)),
            pl.BlockSpec((CHUNK, ML_V), lambda b, c: (rb(b, c), 2)),
            pl.BlockSpec((CHUNK, ML_V), lambda b, c: (rb(b, c), 3)),
            pl.BlockSpec((CHUNK, LANES), lambda b, c: (rb(b, c), 0)),
            pl.BlockSpec((2 * ML_HEADS, CHUNK), lambda b, c: (0, rb(b, c))),
            full(SUBLANES, LANES), full(2, ML_HEADS, CHUNK), full(1, ML_DV),
        ],
        out_specs=[
            pl.BlockSpec((CHUNK, ML_V), lambda b, c: (rb(b, c), 0)),
            pl.BlockSpec((1, ML_HEADS, ML_DK, ML_DV), lambda b, c: (b, 0, 0, 0)),
            pl.BlockSpec((1, ML_HEADS, ML_DK), lambda b, c: (b, 0, 0)),
            pl.BlockSpec((1, ML_HEADS, LANES), lambda b, c: (b, 0, 0)),
        ],
        out_shape=[
            jax.ShapeDtypeStruct((m_total, ML_V), BF16),
            jax.ShapeDtypeStruct((bsz, ML_HEADS, ML_DK, ML_DV), F32),
            jax.ShapeDtypeStruct((bsz, ML_HEADS, ML_DK), F32),
            jax.ShapeDtypeStruct((bsz, ML_HEADS, LANES), F32),
        ],
        scratch_shapes=[pltpu.VMEM((ML_HEADS, ML_DK, ML_DV), F32), pltpu.VMEM((ML_HEADS, ML_DK), F32),
                        pltpu.VMEM((ML_HEADS, LANES), F32)],
        compiler_params=_params("parallel", "arbitrary"),
        name="mlstm_prompt",
    )(zm, zm, zm, zm, zm, zs, zs_t, pc, pr, norm[None, :])


def _cmlp_prompt_kernel(u_ref, v_ref, z_ref, ws_ref, wb_ref, gain_ref, mix_ref, vrows_ref):
    r, c = _causal_masks()
    causal = r >= c
    for g in range(CM_GROUPS):
        gs = slice(g * CM_GROUP_DIM, (g + 1) * CM_GROUP_DIM)
        v = _gelu(v_ref[:, gs])
        v = v * lax.rsqrt(jnp.mean(v * v, axis=-1, keepdims=True) + EPS) * gain_ref[...]
        s = _mm(jnp.where(causal, ws_ref[g], 0.0), v) + wb_ref[:, g:g + 1]
        mix_ref[:, gs] = (_gelu(u_ref[:, gs]) * s * _silu(z_ref[:, gs])).astype(BF16)
        vrows_ref[0, :, gs] = v


def _cmlp_prompt(zm, m_total, bsz, seq, v_gain, ws, wb):
    nc = seq // CM_CHUNK
    rb = lambda b, c: b * nc + c
    col0 = (2 * ML_QK + 3 * ML_V) // CM_WIDTH
    return pl.pallas_call(
        _cmlp_prompt_kernel,
        grid=(bsz, nc),
        in_specs=[
            pl.BlockSpec((CM_CHUNK, CM_WIDTH), lambda b, c: (rb(b, c), col0)),
            pl.BlockSpec((CM_CHUNK, CM_WIDTH), lambda b, c: (rb(b, c), col0 + 1)),
            pl.BlockSpec((CM_CHUNK, CM_WIDTH), lambda b, c: (rb(b, c), col0 + 2)),
            pl.BlockSpec((CM_GROUPS, CM_CHUNK, CM_CHUNK), lambda b, c: (0, 0, 0)),
            pl.BlockSpec((CM_CHUNK, CM_GROUPS), lambda b, c: (0, 0)),
            pl.BlockSpec((1, CM_GROUP_DIM), lambda b, c: (0, 0)),
        ],
        out_specs=[
            pl.BlockSpec((CM_CHUNK, CM_WIDTH), lambda b, c: (rb(b, c), 0)),
            pl.BlockSpec((1, CM_CHUNK, CM_WIDTH), lambda b, c: (b, 0, 0)),
        ],
        out_shape=[jax.ShapeDtypeStruct((m_total, CM_WIDTH), BF16),
                   jax.ShapeDtypeStruct((bsz, CM_CHUNK, CM_WIDTH), F32)],
        compiler_params=_params("parallel", "arbitrary"),
        name="cmlp_prompt",
    )(zm, zm, zm, ws, wb.T, v_gain[None, :])


def rmsnorm(x, g):
    y = x * lax.rsqrt(jnp.mean(x * x, axis=-1, keepdims=True) + EPS)
    return y * g


def _l2norm(t):
    return t * lax.rsqrt(jnp.sum(t * t, axis=-1, keepdims=True) + EPS)


def _conv_step(u, buf, w):
    full = jnp.concatenate([buf, u[:, None, :]], axis=1)
    return jnp.einsum('bkc,kc->bc', full, w), full[:, 1:]


def _even_decode(zm, zs, conv_a, s_a, conv_b, s_b, a_conv_w, a_conv_b, a_dt_bias, a_log, a_d, a_norm,
                 b_conv_w, b_dt_bias, b_a_log, b_norm):
    bsz = zm.shape[0]
    z_a, xs_raw, q_raw, k_raw, v_raw, z_b, bc_raw = jnp.split(zm, [D_MODEL * i for i in range(1, 7)], axis=-1)
    dt_raw = zs[:, :SSD_HEADS]
    gb = zs[:, LANES:].reshape(bsz, GDN_HEADS // GDN_HG, LANES)
    beta_raw = gb[:, :, :GDN_HG].reshape(bsz, GDN_HEADS)
    g_raw = gb[:, :, GDN_HG:2 * GDN_HG].reshape(bsz, GDN_HEADS)
    xbc, conv_a_new = _conv_step(jnp.concatenate([xs_raw, bc_raw], axis=-1), conv_a, a_conv_w)
    xbc = jax.nn.silu(xbc + a_conv_b)
    xs = xbc[:, :SSD_WIDTH].reshape(bsz, SSD_HEADS, SSD_HEAD_DIM)
    bm = xbc[:, SSD_WIDTH:SSD_WIDTH + SSD_GROUPS * SSD_STATE].reshape(bsz, SSD_GROUPS, SSD_STATE)
    cm = xbc[:, SSD_WIDTH + SSD_GROUPS * SSD_STATE:].reshape(bsz, SSD_GROUPS, SSD_STATE)
    dt = jax.nn.softplus(dt_raw + a_dt_bias)
    dec = jnp.exp(dt * -jnp.exp(a_log))
    hpg = SSD_HEADS // SSD_GROUPS
    bm_h = jnp.repeat(bm, hpg, axis=1)
    cm_h = jnp.repeat(cm, hpg, axis=1)
    s_a_new = s_a * dec[:, :, None, None] + (dt[:, :, None] * xs)[..., None] * bm_h[:, :, None, :]
    y_a = jnp.einsum('bhpn,bhn->bhp', s_a_new, cm_h) + a_d[:, None] * xs
    y_a = y_a.reshape(bsz, SSD_WIDTH) * jax.nn.silu(z_a)
    y_a = rmsnorm(y_a.reshape(bsz, SSD_GROUPS, SSD_GW), a_norm.reshape(SSD_GROUPS, SSD_GW)).reshape(bsz, SSD_WIDTH)
    qkv, conv_b_new = _conv_step(jnp.concatenate([q_raw, k_raw, v_raw], axis=-1), conv_b, b_conv_w)
    qkv = jax.nn.silu(qkv)
    q = _l2norm(qkv[:, :GDN_QK].reshape(bsz, GDN_HEADS, GDN_DK)) * (GDN_DK ** -0.5)
    k = _l2norm(qkv[:, GDN_QK:2 * GDN_QK].reshape(bsz, GDN_HEADS, GDN_DK))
    v = qkv[:, 2 * GDN_QK:].reshape(bsz, GDN_HEADS, GDN_DV)
    beta = jax.nn.sigmoid(beta_raw)
    eg = jnp.exp(-jnp.exp(b_a_log) * jax.nn.softplus(g_raw + b_dt_bias))
    vn = beta[..., None] * (v - eg[..., None] * jnp.einsum('bhk,bhkv->bhv', k, s_b))
    o = eg[..., None] * jnp.einsum('bhk,bhkv->bhv', q, s_b) + jnp.sum(q * k, axis=-1, keepdims=True) * vn
    s_b_new = s_b * eg[..., None, None] + k[..., :, None] * vn[..., None, :]
    o = rmsnorm(o, b_norm) * jax.nn.silu(z_b.reshape(bsz, GDN_HEADS, GDN_DV))
    return (y_a.astype(BF16), o.reshape(bsz, GDN_V).astype(BF16), conv_a_new, s_a_new, conv_b_new, s_b_new)


def _odd_decode(zm, zs, c0, n0, m0, i_bias, f_bias, ml_norm, v_gain, ws, wb):
    bsz = zm.shape[0]
    q, k, v, o_raw, z_c, u_d, v_d, z_d = jnp.split(
        zm, list(np.cumsum([ML_QK, ML_QK, ML_V, ML_V, ML_V, CM_WIDTH, CM_WIDTH])), axis=-1)
    q = q.reshape(bsz, ML_HEADS, ML_DK)
    k = k.reshape(bsz, ML_HEADS, ML_DK) * (ML_DK ** -0.5)
    v = v.reshape(bsz, ML_HEADS, ML_DV)
    logi = zs[:, :ML_HEADS] + i_bias
    logf = jax.nn.log_sigmoid(zs[:, ML_HEADS:2 * ML_HEADS] + f_bias)
    m_new = jnp.maximum(logf + m0, logi)
    sa = jnp.exp(logf + m0 - m_new)
    sb = jnp.exp(logi - m_new)
    qk = jnp.sum(q * k, axis=-1)
    num = sa[..., None] * jnp.einsum('bhk,bhkv->bhv', q, c0) + (sb * qk)[..., None] * v
    den = sa * jnp.sum(q * n0, axis=-1) + sb * qk
    h = num / jnp.maximum(jnp.abs(den), jnp.exp(-m_new))[..., None]
    c_new = c0 * sa[..., None, None] + sb[..., None, None] * k[..., :, None] * v[..., None, :]
    n_new = n0 * sa[..., None] + sb[..., None] * k
    hc = rmsnorm(h, ml_norm) * jax.nn.sigmoid(o_raw.reshape(bsz, ML_HEADS, ML_DV))
    hc = hc.reshape(bsz, ML_V) * jax.nn.silu(z_c)
    vg = rmsnorm(jax.nn.gelu(v_d).reshape(bsz, CM_GROUPS, CM_GROUP_DIM), v_gain)
    s = ws[:, 0, 0][None, :, None] * vg + wb[:, 0][None, :, None]
    yd = jax.nn.gelu(u_d) * s.reshape(bsz, CM_WIDTH) * jax.nn.silu(z_d)
    return (hc.astype(BF16), yd.astype(BF16), c_new, n_new, m_new, vg.reshape(bsz, 1, CM_WIDTH))


def kernel(x_prompt, x_sample, state_ssd_conv, state_ssd, state_gdn_conv, state_gdn, state_mlstm_c,
           state_mlstm_n, state_mlstm_m, even_norm, even_w_in, ssd_conv_w, ssd_conv_b, ssd_dt_bias, ssd_a_log,
           ssd_d, ssd_norm, gdn_conv_w, gdn_dt_bias, gdn_a_log, gdn_norm, even_w_out, odd_norm, odd_w_in,
           mlstm_i_bias, mlstm_f_bias, mlstm_norm, cmlp_v_norm, cmlp_ws, cmlp_b, odd_w_out, final_norm):
    bp, seq, d = x_prompt.shape
    bs = x_sample.shape[0]
    mp = bp * seq
    mt = mp + bs
    x = jnp.concatenate([x_prompt.reshape(mp, d), x_sample.reshape(bs, d)], axis=0)

    keys = ("sc", "ss", "gc", "gs", "mc", "mn", "mm", "cv")
    outs_p = {k: [] for k in keys}
    outs_s = {k: [] for k in keys}
    for layer in range(DEPTH):
        i = layer // 2
        if layer % 2 == 0:
            w_main, w_small = _prep_even_w_in(even_w_in[i])
            zm, zs = _inproj(x, even_norm[i], w_main, w_small, tn=1024)
            zs_t = zs[:mp].T
            mix_a, cx, cbc, st = _ssd_prompt(zm, zs, zs_t, mt, bp, seq, ssd_conv_w[i], ssd_conv_b[i],
                                             ssd_dt_bias[i], ssd_a_log[i], ssd_d[i], ssd_norm[i])
            mix_b, cq, ck, cv, gst = _gdn_prompt(zm, zs, zs_t, mt, bp, seq, gdn_conv_w[i], gdn_dt_bias[i],
                                                 gdn_a_log[i], gdn_norm[i])
            hpg = SSD_HEADS // SSD_GROUPS
            outs_p["sc"].append(jnp.concatenate([cx, cbc], axis=-1))
            outs_p["ss"].append(st.reshape(bp, SSD_GROUPS, SSD_STATE, hpg, SSD_HEAD_DIM).transpose(0, 1, 3, 4, 2)
                                .reshape(bp, SSD_HEADS, SSD_HEAD_DIM, SSD_STATE))
            outs_p["gc"].append(jnp.concatenate([cq, ck, cv], axis=-1))
            outs_p["gs"].append(gst)
            rs = _even_decode(zm[mp:], zs[mp:], state_ssd_conv[i], state_ssd[i], state_gdn_conv[i], state_gdn[i],
                              ssd_conv_w[i], ssd_conv_b[i], ssd_dt_bias[i], ssd_a_log[i], ssd_d[i], ssd_norm[i],
                              gdn_conv_w[i], gdn_dt_bias[i], gdn_a_log[i], gdn_norm[i])
            outs_s["sc"].append(rs[2]); outs_s["ss"].append(rs[3]); outs_s["gc"].append(rs[4]); outs_s["gs"].append(rs[5])
            mix_a = mix_a.at[mp:].set(rs[0])
            mix_b = mix_b.at[mp:].set(rs[1])
            w_out = even_w_out[i].astype(BF16)
        else:
            w_main, w_small = _prep_odd_w_in(odd_w_in[i])
            zm, zs = _inproj(x, odd_norm[i], w_main, w_small, tn=1024)
            zs_t = zs[:mp].T
            mix_a, c_p, n_p, m_p = _mlstm_prompt(zm, zs, zs_t, mt, bp, seq, mlstm_i_bias[i], mlstm_f_bias[i],
                                                 mlstm_norm[i])
            mix_b, v_rows = _cmlp_prompt(zm, mt, bp, seq, cmlp_v_norm[i], cmlp_ws[i], cmlp_b[i])
            outs_p["mc"].append(c_p); outs_p["mn"].append(n_p); outs_p["mm"].append(m_p[:, :, 0])
            outs_p["cv"].append(v_rows)
            rs = _odd_decode(zm[mp:], zs[mp:], state_mlstm_c[i], state_mlstm_n[i], state_mlstm_m[i],
                             mlstm_i_bias[i], mlstm_f_bias[i], mlstm_norm[i], cmlp_v_norm[i], cmlp_ws[i], cmlp_b[i])
            outs_s["mc"].append(rs[2]); outs_s["mn"].append(rs[3]); outs_s["mm"].append(rs[4]); outs_s["cv"].append(rs[5])
            mix_a = mix_a.at[mp:].set(rs[0])
            mix_b = mix_b.at[mp:].set(rs[1])
            w_out = odd_w_out[i].astype(BF16)
        x = _outproj(x, mix_a, mix_b, w_out)

    y = _final_norm(x, final_norm)
    y_p = y[:mp].reshape(bp, seq, d)
    y_s = y[mp:].reshape(bs, 1, d)
    st = lambda o, k: jnp.stack(o[k])
    return (y_p, y_s, st(outs_p, "sc"), st(outs_s, "sc"), st(outs_p, "ss"), st(outs_s, "ss"),
            st(outs_p, "gc"), st(outs_s, "gc"), st(outs_p, "gs"), st(outs_s, "gs"),
            st(outs_p, "mc"), st(outs_s, "mc"), st(outs_p, "mn"), st(outs_s, "mn"),
            st(outs_p, "mm"), st(outs_s, "mm"), st(outs_p, "cv"), st(outs_s, "cv"))
```

```python
import jax
import jax.numpy as jnp
import numpy as np
from jax import lax
from jax.experimental import pallas as pl
from jax.experimental.pallas import tpu as pltpu

F32 = jnp.float32
BF16 = jnp.bfloat16
HI = lax.Precision.HIGHEST

D_MODEL = 2048
DEPTH = 4
CHUNK = 128
CONV_K = 4
EPS = 1e-6
NEG = -1e30

SSD_WIDTH = D_MODEL
SSD_HEAD_DIM = 64
SSD_HEADS = SSD_WIDTH // SSD_HEAD_DIM
SSD_STATE = 128
SSD_GROUPS = 4
SSD_GW = SSD_WIDTH // SSD_GROUPS
SSD_BC = 2 * SSD_GROUPS * SSD_STATE
SSD_CONV_DIM = SSD_WIDTH + SSD_BC
GDN_HEADS = 16
GDN_DK = 128
GDN_DV = 128
GDN_QK = GDN_HEADS * GDN_DK
GDN_V = GDN_HEADS * GDN_DV
GDN_CONV_DIM = 2 * GDN_QK + GDN_V
GDN_HG = 8
ML_HEADS = 8
ML_DK = 128
ML_DV = 256
ML_QK = ML_HEADS * ML_DK
ML_V = ML_HEADS * ML_DV
CM_WIDTH = D_MODEL // 2
CM_GROUPS = 8
CM_GROUP_DIM = CM_WIDTH // CM_GROUPS
CM_CHUNK = 128

LANES = 128
SUBLANES = 8

EVEN_MAIN = 6 * D_MODEL + SSD_BC
EVEN_SMALL = 3 * LANES
ODD_MAIN = 2 * ML_QK + 3 * ML_V + 3 * CM_WIDTH
ODD_SMALL = LANES

VMEM_LIMIT = 56 * 1024 * 1024
ROW_TILE = 832


def _silu(x):
    return x * jax.nn.sigmoid(x)


def _softplus(x):
    return jnp.maximum(x, 0.0) + jnp.log1p(jnp.exp(-jnp.abs(x)))


def _gelu(x):
    return 0.5 * x * (1.0 + jnp.tanh(np.sqrt(2.0 / np.pi).astype(np.float32) * (x + 0.044715 * (x * x * x))))


def _mm(a, b):
    return jnp.dot(a.astype(BF16), b.astype(BF16), preferred_element_type=F32)


def _mm_nt(a, b):
    return lax.dot_general(a.astype(BF16), b.astype(BF16), (((1,), (1,)), ((), ())), preferred_element_type=F32)


def _mm_tn(a, b):
    return lax.dot_general(a.astype(BF16), b.astype(BF16), (((0,), (0,)), ((), ())), preferred_element_type=F32)


def _mm_hi(a, b):
    return jnp.dot(a, b, precision=HI, preferred_element_type=F32)


def _iota(shape, axis):
    return lax.broadcasted_iota(jnp.int32, shape, axis)


def _params(*sem):
    return pltpu.CompilerParams(dimension_semantics=sem, vmem_limit_bytes=VMEM_LIMIT)


def _inproj_kernel(x_ref, g_ref, w_ref, ws_ref, z_ref, zs_ref, xn_ref):
    @pl.when(pl.program_id(1) == 0)
    def _():
        x = x_ref[...]
        y = x * lax.rsqrt(jnp.mean(x * x, axis=-1, keepdims=True) + EPS)
        xn = (y * g_ref[...]).astype(BF16)
        xn_ref[...] = xn
        zs_ref[...] = jnp.dot(xn, ws_ref[...], preferred_element_type=F32)

    z_ref[...] = jnp.dot(xn_ref[...], w_ref[...], preferred_element_type=F32)


def _inproj(x, g, w_main, w_small, tn):
    m, d = x.shape
    n = w_main.shape[1]
    ns = w_small.shape[1]
    tm = ROW_TILE
    return pl.pallas_call(
        _inproj_kernel,
        grid=(m // tm, n // tn),
        in_specs=[
            pl.BlockSpec((tm, d), lambda i, j: (i, 0)),
            pl.BlockSpec((1, d), lambda i, j: (0, 0)),
            pl.BlockSpec((d, tn), lambda i, j: (0, j)),
            pl.BlockSpec((d, ns), lambda i, j: (0, 0)),
        ],
        out_specs=[
            pl.BlockSpec((tm, tn), lambda i, j: (i, j)),
            pl.BlockSpec((tm, ns), lambda i, j: (i, 0)),
        ],
        out_shape=[jax.ShapeDtypeStruct((m, n), F32), jax.ShapeDtypeStruct((m, ns), F32)],
        scratch_shapes=[pltpu.VMEM((tm, d), BF16)],
        compiler_params=_params("parallel", "arbitrary"),
        name="inproj",
    )(x, g.reshape(1, d), w_main, w_small)


def _outproj_kernel(x_ref, ma_ref, mb_ref, wa_ref, wb_ref, o_ref):
    o_ref[...] = (x_ref[...] + jnp.dot(ma_ref[...], wa_ref[...], preferred_element_type=F32)
                  + jnp.dot(mb_ref[...], wb_ref[...], preferred_element_type=F32))


def _outproj(x, mix_a, mix_b, w):
    m, d = x.shape
    ka, kb = mix_a.shape[1], mix_b.shape[1]
    tm, tn = ROW_TILE, 512
    return pl.pallas_call(
        _outproj_kernel,
        grid=(m // tm, d // tn),
        in_specs=[
            pl.BlockSpec((tm, tn), lambda i, j: (i, j)),
            pl.BlockSpec((tm, ka), lambda i, j: (i, 0)),
            pl.BlockSpec((tm, kb), lambda i, j: (i, 0)),
            pl.BlockSpec((ka, tn), lambda i, j: (0, j)),
            pl.BlockSpec((kb, tn), lambda i, j: (ka // kb, j)),
        ],
        out_specs=pl.BlockSpec((tm, tn), lambda i, j: (i, j)),
        out_shape=jax.ShapeDtypeStruct((m, d), F32),
        compiler_params=_params("parallel", "arbitrary"),
        name="outproj",
    )(x, mix_a, mix_b, w, w)


def _final_norm_kernel(x_ref, g_ref, o_ref):
    x = x_ref[...]
    o_ref[...] = x * lax.rsqrt(jnp.mean(x * x, axis=-1, keepdims=True) + EPS) * g_ref[...]


def _final_norm(x, g):
    m, d = x.shape
    tm = ROW_TILE
    return pl.pallas_call(
        _final_norm_kernel,
        grid=(m // tm,),
        in_specs=[pl.BlockSpec((tm, d), lambda i: (i, 0)), pl.BlockSpec((1, d), lambda i: (0, 0))],
        out_specs=pl.BlockSpec((tm, d), lambda i: (i, 0)),
        out_shape=jax.ShapeDtypeStruct((m, d), F32),
        compiler_params=_params("parallel"),
        name="final_norm",
    )(x, g.reshape(1, d))


def _pad_cols(w, width):
    return jnp.pad(w, ((0, 0), (0, width - w.shape[1])))


def _prep_even_w_in(w):
    o_xs = SSD_WIDTH
    o_bc = o_xs + SSD_WIDTH
    o_dt = o_bc + SSD_BC
    o_q = o_dt + SSD_HEADS
    o_zb = o_q + GDN_CONV_DIM
    o_beta = o_zb + GDN_V
    o_g = o_beta + GDN_HEADS
    main = jnp.concatenate([w[:, :o_bc], w[:, o_q:o_beta], w[:, o_bc:o_dt]], axis=1).astype(BF16)
    small = [_pad_cols(w[:, o_dt:o_q], LANES)]
    for hg in range(GDN_HEADS // GDN_HG):
        sl = slice(hg * GDN_HG, (hg + 1) * GDN_HG)
        small.append(_pad_cols(jnp.concatenate([w[:, o_beta:o_g][:, sl], w[:, o_g:][:, sl]], axis=1), LANES))
    return main, jnp.concatenate(small, axis=1).astype(BF16)


def _prep_odd_w_in(w):
    o1 = 2 * ML_QK + 3 * ML_V
    o2 = o1 + 2 * ML_HEADS
    main = jnp.concatenate([w[:, :o1], w[:, o2:]], axis=1).astype(BF16)
    small = _pad_cols(w[:, o1:o2], LANES).astype(BF16)
    return main, small


def _conv_chunk(ext_ref, u, w_ref, first):
    @pl.when(first)
    def _():
        ext_ref[0:SUBLANES, :] = jnp.zeros((SUBLANES, ext_ref.shape[1]), F32)

    ext_ref[SUBLANES:SUBLANES + CHUNK, :] = u
    out = w_ref[CONV_K - 1:CONV_K, :] * u
    for k in range(1, CONV_K):
        out = out + w_ref[CONV_K - 1 - k:CONV_K - k, :] * ext_ref[SUBLANES - k:SUBLANES - k + CHUNK, :]
    ext_ref[0:SUBLANES, :] = ext_ref[CHUNK:CHUNK + SUBLANES, :]
    return out


def _conv_tail(ext_ref):
    return ext_ref[SUBLANES - (CONV_K - 1):SUBLANES, :]


def _causal_masks():
    r = _iota((CHUNK, CHUNK), 0)
    c = _iota((CHUNK, CHUNK), 1)
    return r, c


def _ssd_prompt_kernel(za_ref, xs_ref, bc_ref, zc_ref, zr_ref, wx_ref, bx_ref, wbc_ref, bbc_ref, pc_ref, pr_ref,
                       d_ref, nrm_ref, mix_ref, cx_ref, cbc_ref, st_ref, extx, extbc, s_ref):
    c_id = pl.program_id(1)
    first = c_id == 0
    last = c_id == pl.num_programs(1) - 1

    @pl.when(first)
    def _():
        s_ref[...] = jnp.zeros(s_ref.shape, F32)

    xs = _silu(_conv_chunk(extx, xs_ref[...], wx_ref, first) + bx_ref[...])
    bc = _silu(_conv_chunk(extbc, bc_ref[...], wbc_ref, first) + bbc_ref[...])

    r, c = _causal_masks()
    causal = r >= c
    tril = jnp.where(causal, 1.0, 0.0)
    triu = jnp.where(r <= c, 1.0, 0.0)
    dt = _softplus(zc_ref[...] + pc_ref[0:1, :])
    la = _mm_hi(tril, dt * (-jnp.exp(pc_ref[1:2, :])))
    dtr = _softplus(zr_ref[...] + pr_ref[0])
    lar = _mm_hi(dtr * (-jnp.exp(pr_ref[1])), triu)
    la_last = la[CHUNK - 1:CHUNK, :]
    e_mat = jnp.where((_iota((LANES, SSD_WIDTH), 1) >> 6) == _iota((LANES, SSD_WIDTH), 0), 1.0, 0.0)
    ela_x = _mm_hi(jnp.exp(la), e_mat)
    wsx = _mm_hi(jnp.exp(la_last - la) * dt, e_mat)
    dec_x = _mm_hi(jnp.broadcast_to(jnp.exp(la_last), (SUBLANES, LANES)), e_mat)[0:1, :]
    lane_lo = _iota((CHUNK, LANES), 1) < SSD_HEAD_DIM

    hpg = SSD_HEADS // SSD_GROUPS
    for g in range(SSD_GROUPS):
        gs = slice(g * SSD_GW, (g + 1) * SSD_GW)
        bg = bc[:, g * SSD_STATE:(g + 1) * SSD_STATE]
        cg = bc[:, SSD_GROUPS * SSD_STATE + g * SSD_STATE:SSD_GROUPS * SSD_STATE + (g + 1) * SSD_STATE]
        cb = _mm_nt(cg, bg)
        ys = []
        for pair in range(hpg // 2):
            h0 = g * hpg + 2 * pair
            xpair = xs[:, h0 * SSD_HEAD_DIM:(h0 + 2) * SSD_HEAD_DIM]
            halves = []
            for hh in (h0, h0 + 1):
                seg = jnp.where(causal, la[:, hh:hh + 1] - lar[hh:hh + 1, :], NEG)
                lmat = jnp.exp(seg) * cb * dtr[hh:hh + 1, :]
                halves.append(_mm(lmat, xpair))
            ys.append(jnp.where(lane_lo, halves[0], halves[1]))
        y = jnp.concatenate(ys, axis=1)
        s_prev = s_ref[g]
        y = y + _mm(cg, s_prev) * ela_x[:, gs] + d_ref[:, gs] * xs[:, gs]
        y = y * _silu(za_ref[:, gs])
        y = y * lax.rsqrt(jnp.mean(y * y, axis=-1, keepdims=True) + EPS) * nrm_ref[:, gs]
        mix_ref[:, gs] = y.astype(BF16)
        s_ref[g] = s_prev * dec_x[:, gs] + _mm_tn(bg, xs[:, gs] * wsx[:, gs])

    @pl.when(last)
    def _():
        st_ref[0] = s_ref[...]
        cx_ref[0] = _conv_tail(extx)
        cbc_ref[0] = _conv_tail(extbc)


def _ssd_prompt(zm, zs, zs_t, m_total, bsz, seq, conv_w, conv_b, dt_bias, a_log, d_skip, norm):
    nc = seq // CHUNK
    rb = lambda b, c: b * nc + c
    pc = jnp.zeros((SUBLANES, LANES), F32).at[0, :SSD_HEADS].set(dt_bias).at[1, :SSD_HEADS].set(a_log)
    pr = jnp.stack([jnp.broadcast_to(dt_bias[:, None], (SSD_HEADS, CHUNK)),
                    jnp.broadcast_to(a_log[:, None], (SSD_HEADS, CHUNK))])
    full = lambda *shape: pl.BlockSpec(shape, lambda b, c: (0,) * len(shape))
    return pl.pallas_call(
        _ssd_prompt_kernel,
        grid=(bsz, nc),
        in_specs=[
            pl.BlockSpec((CHUNK, SSD_WIDTH), lambda b, c: (rb(b, c), 0)),
            pl.BlockSpec((CHUNK, SSD_WIDTH), lambda b, c: (rb(b, c), 1)),
            pl.BlockSpec((CHUNK, SSD_BC), lambda b, c: (rb(b, c), 6 * D_MODEL // SSD_BC)),
            pl.BlockSpec((CHUNK, LANES), lambda b, c: (rb(b, c), 0)),
            pl.BlockSpec((SSD_HEADS, CHUNK), lambda b, c: (0, rb(b, c))),
            full(CONV_K, SSD_WIDTH), full(1, SSD_WIDTH), full(CONV_K, SSD_BC), full(1, SSD_BC),
            full(SUBLANES, LANES), full(2, SSD_HEADS, CHUNK), full(1, SSD_WIDTH), full(1, SSD_WIDTH),
        ],
        out_specs=[
            pl.BlockSpec((CHUNK, SSD_WIDTH), lambda b, c: (rb(b, c), 0)),
            pl.BlockSpec((1, CONV_K - 1, SSD_WIDTH), lambda b, c: (b, 0, 0)),
            pl.BlockSpec((1, CONV_K - 1, SSD_BC), lambda b, c: (b, 0, 0)),
            pl.BlockSpec((1, SSD_GROUPS, SSD_STATE, SSD_GW), lambda b, c: (b, 0, 0, 0)),
        ],
        out_shape=[
            jax.ShapeDtypeStruct((m_total, SSD_WIDTH), BF16),
            jax.ShapeDtypeStruct((bsz, CONV_K - 1, SSD_WIDTH), F32),
            jax.ShapeDtypeStruct((bsz, CONV_K - 1, SSD_BC), F32),
            jax.ShapeDtypeStruct((bsz, SSD_GROUPS, SSD_STATE, SSD_GW), F32),
        ],
        scratch_shapes=[pltpu.VMEM((SUBLANES + CHUNK, SSD_WIDTH), F32), pltpu.VMEM((SUBLANES + CHUNK, SSD_BC), F32),
                        pltpu.VMEM((SSD_GROUPS, SSD_STATE, SSD_GW), F32)],
        compiler_params=_params("parallel", "arbitrary"),
        name="ssd_prompt",
    )(zm, zm, zm, zs, zs_t, conv_w[:, :SSD_WIDTH], conv_b[None, :SSD_WIDTH], conv_w[:, SSD_WIDTH:],
      conv_b[None, SSD_WIDTH:], pc, pr, jnp.repeat(d_skip, SSD_HEAD_DIM)[None, :], norm[None, :])


def _tri_inverse(mats, r, c):
    def corner(level):
        return ((r >> (level + 1)) == (c >> (level + 1))) & (((r >> level) & 1) == 1) & (((c >> level) & 1) == 0)

    eye = jnp.where(r == c, 1.0, 0.0)
    ts = [eye - jnp.where(corner(0), a, 0.0) for a in mats]
    for level in range(1, 7):
        cm = corner(level)
        xs = [_mm(t, jnp.where(cm, a, 0.0)) for t, a in zip(ts, mats)]
        ts = [t - _mm(x, t) for t, x in zip(ts, xs)]
    return ts


def _gdn_prompt_kernel(q_ref, k_ref, v_ref, zb_ref, zc_ref, zr_ref, wq_ref, wk_ref, wv_ref, pc_ref, pr_ref, nrm_ref,
                       mix_ref, cq_ref, ck_ref, cv_ref, st_ref, extq, extk, extv, s_ref):
    c_id = pl.program_id(2)
    first = c_id == 0
    last = c_id == pl.num_programs(2) - 1

    @pl.when(first)
    def _():
        s_ref[...] = jnp.zeros(s_ref.shape, F32)

    q_all = _silu(_conv_chunk(extq, q_ref[...], wq_ref, first))
    k_all = _silu(_conv_chunk(extk, k_ref[...], wk_ref, first))
    v_all = _silu(_conv_chunk(extv, v_ref[...], wv_ref, first))

    r, c = _causal_masks()
    causal = r >= c
    strict = r > c
    tril = jnp.where(causal, 1.0, 0.0)
    triu = jnp.where(r <= c, 1.0, 0.0)
    zc = zc_ref[...]
    beta_c = jax.nn.sigmoid(zc)
    gc_c = _mm_hi(tril, -jnp.exp(pc_ref[0:1, :]) * _softplus(zc + pc_ref[1:2, :]))
    gc_r = _mm_hi(-jnp.exp(pr_ref[0]) * _softplus(zr_ref[GDN_HG:2 * GDN_HG, :] + pr_ref[1]), triu)

    heads = range(GDN_HG)
    hs = [slice(j * GDN_DK, (j + 1) * GDN_DK) for j in heads]
    qh = [q_all[:, s] for s in hs]
    kh = [k_all[:, s] for s in hs]
    qh = [x * (lax.rsqrt(jnp.sum(x * x, axis=-1, keepdims=True) + EPS) * (GDN_DK ** -0.5)) for x in qh]
    kh = [x * lax.rsqrt(jnp.sum(x * x, axis=-1, keepdims=True) + EPS) for x in kh]
    gcc = [gc_c[:, GDN_HG + j:GDN_HG + j + 1] for j in heads]
    beta = [beta_c[:, j:j + 1] for j in heads]
    gam = [jnp.exp(jnp.where(causal, gcc[j] - gc_r[j:j + 1, :], NEG)) for j in heads]
    qkk = [_mm_nt(jnp.concatenate([qh[j], kh[j]], axis=0), kh[j]) for j in heads]
    aqk = [qkk[j][:CHUNK] * gam[j] for j in heads]
    tinv = _tri_inverse([jnp.where(strict, beta[j] * qkk[j][CHUNK:] * gam[j], 0.0) for j in heads], r, c)
    egc = [jnp.exp(g) for g in gcc]
    uw = [_mm(tinv[j], jnp.concatenate([beta[j] * v_all[:, hs[j]], (beta[j] * egc[j]) * kh[j]], axis=1))
          for j in heads]
    s_prev = [s_ref[j] for j in heads]
    ws_qs = [_mm(jnp.concatenate([uw[j][:, GDN_DV:], qh[j] * egc[j]], axis=0), s_prev[j]) for j in heads]
    vn = [uw[j][:, :GDN_DV] - ws_qs[j][:CHUNK] for j in heads]
    o = [ws_qs[j][CHUNK:] + _mm(aqk[j], vn[j]) for j in heads]
    for j in heads:
        gc_last = gcc[j][CHUNK - 1:CHUNK, :]
        s_ref[j] = s_prev[j] * jnp.exp(gc_last) + _mm_tn(kh[j] * jnp.exp(gc_last - gcc[j]), vn[j])
    for j in heads:
        on = o[j] * lax.rsqrt(jnp.mean(o[j] * o[j], axis=-1, keepdims=True) + EPS) * nrm_ref[...]
        mix_ref[:, hs[j]] = (on * _silu(zb_ref[:, hs[j]])).astype(BF16)

    @pl.when(last)
    def _():
        st_ref[0] = s_ref[...]
        cq_ref[0] = _conv_tail(extq)
        ck_ref[0] = _conv_tail(extk)
        cv_ref[0] = _conv_tail(extv)


def _gdn_prompt(zm, zs, zs_t, m_total, bsz, seq, conv_w, dt_bias, a_log, norm):
    nc = seq // CHUNK
    nhg = GDN_HEADS // GDN_HG
    w = GDN_HG * GDN_DK
    rb = lambda b, c: b * nc + c
    col0 = 2 * D_MODEL // w
    pc = jnp.zeros((nhg, SUBLANES, LANES), F32)
    pc = pc.at[:, 0, GDN_HG:2 * GDN_HG].set(a_log.reshape(nhg, GDN_HG))
    pc = pc.at[:, 1, GDN_HG:2 * GDN_HG].set(dt_bias.reshape(nhg, GDN_HG))
    pr = jnp.stack([jnp.broadcast_to(a_log.reshape(nhg, GDN_HG, 1), (nhg, GDN_HG, CHUNK)),
                    jnp.broadcast_to(dt_bias.reshape(nhg, GDN_HG, 1), (nhg, GDN_HG, CHUNK))], axis=1)
    seg = lambda s: pl.BlockSpec((CHUNK, w), lambda b, h, c: (rb(b, c), col0 + s * nhg + h))
    wseg = lambda s: pl.BlockSpec((CONV_K, w), lambda b, h, c: (0, s * nhg + h))
    cout = pl.BlockSpec((1, CONV_K - 1, w), lambda b, h, c: (b, 0, h))
    return pl.pallas_call(
        _gdn_prompt_kernel,
        grid=(bsz, nhg, nc),
        in_specs=[
            seg(0), seg(1), seg(2), seg(3),
            pl.BlockSpec((CHUNK, LANES), lambda b, h, c: (rb(b, c), 1 + h)),
            pl.BlockSpec((2 * GDN_HG, CHUNK), lambda b, h, c: ((1 + h) * LANES // (2 * GDN_HG), rb(b, c))),
            wseg(0), wseg(1), wseg(2),
            pl.BlockSpec((None, SUBLANES, LANES), lambda b, h, c: (h, 0, 0)),
            pl.BlockSpec((None, 2, GDN_HG, CHUNK), lambda b, h, c: (h, 0, 0, 0)),
            pl.BlockSpec((1, GDN_DV), lambda b, h, c: (0, 0)),
        ],
        out_specs=[
            pl.BlockSpec((CHUNK, w), lambda b, h, c: (rb(b, c), h)),
            cout, cout, cout,
            pl.BlockSpec((1, GDN_HG, GDN_DK, GDN_DV), lambda b, h, c: (b, h, 0, 0)),
        ],
        out_shape=[
            jax.ShapeDtypeStruct((m_total, GDN_V), BF16),
            jax.ShapeDtypeStruct((bsz, CONV_K - 1, GDN_QK), F32),
            jax.ShapeDtypeStruct((bsz, CONV_K - 1, GDN_QK), F32),
            jax.ShapeDtypeStruct((bsz, CONV_K - 1, GDN_V), F32),
            jax.ShapeDtypeStruct((bsz, GDN_HEADS, GDN_DK, GDN_DV), F32),
        ],
        scratch_shapes=[pltpu.VMEM((SUBLANES + CHUNK, w), F32)] * 3 + [pltpu.VMEM((GDN_HG, GDN_DK, GDN_DV), F32)],
        compiler_params=_params("parallel", "parallel", "arbitrary"),
        name="gdn_prompt",
    )(zm, zm, zm, zm, zs, zs_t, conv_w, conv_w, conv_w, pc, pr, norm[None, :])


def _mlstm_prompt_kernel(q_ref, k_ref, v_ref, o_ref, zc_ref, gc_ref, gr_ref, pc_ref, pr_ref, nrm_ref,
                         mix_ref, c_out, n_out, m_out, c_ref, n_ref, m_ref):
    c_id = pl.program_id(1)
    first = c_id == 0
    last = c_id == pl.num_programs(1) - 1

    @pl.when(first)
    def _():
        c_ref[...] = jnp.zeros(c_ref.shape, F32)
        n_ref[...] = jnp.zeros(n_ref.shape, F32)
        m_ref[...] = jnp.zeros(m_ref.shape, F32)

    r, c = _causal_masks()
    causal = r >= c
    tril = jnp.where(causal, 1.0, 0.0)
    triu = jnp.where(r <= c, 1.0, 0.0)
    gc = gc_ref[...]
    logi_c = gc + pc_ref[0:1, :]
    b_c = _mm_hi(tril, -_softplus(-(gc + pc_ref[1:2, :])))
    logi_r = gr_ref[0:ML_HEADS, :] + pr_ref[0]
    b_r = _mm_hi(-_softplus(-(gr_ref[ML_HEADS:2 * ML_HEADS, :] + pr_ref[1])), triu)

    heads = range(ML_HEADS)
    ks = [slice(j * ML_DK, (j + 1) * ML_DK) for j in heads]
    vs = [slice(j * ML_DV, (j + 1) * ML_DV) for j in heads]
    q = [q_ref[:, s] for s in ks]
    k = [k_ref[:, s] * (ML_DK ** -0.5) for s in ks]
    bc = [b_c[:, ML_HEADS + j:ML_HEADS + j + 1] for j in heads]
    dmat = [jnp.where(causal, bc[j] - b_r[j:j + 1, :] + logi_r[j:j + 1, :], NEG) for j in heads]
    m_intra = [jnp.max(x, axis=-1, keepdims=True) for x in dmat]
    p = [_mm_nt(q[j], k[j]) * jnp.exp(dmat[j] - m_intra[j]) for j in heads]
    h_intra = [_mm(p[j], v_ref[:, vs[j]]) for j in heads]
    c_prev = [c_ref[j] for j in heads]
    qc = [_mm(q[j], c_prev[j]) for j in heads]
    b_last = [x[CHUNK - 1:CHUNK, :] for x in bc]
    gk = [b_last[j] - bc[j] + logi_c[:, j:j + 1] for j in heads]
    m_k = [jnp.max(x, axis=0, keepdims=True) for x in gk]
    kw = [k[j] * jnp.exp(gk[j] - m_k[j]) for j in heads]
    c_loc = [_mm_tn(kw[j], v_ref[:, vs[j]]) for j in heads]
    for j in heads:
        n_intra = jnp.sum(p[j], axis=-1, keepdims=True)
        m_prev = m_ref[j:j + 1, 0:1]
        n_prev = n_ref[j:j + 1, :]
        mb = bc[j] + m_prev
        m_t = jnp.maximum(mb, m_intra[j])
        s_inter = jnp.exp(mb - m_t)
        s_intra = jnp.exp(m_intra[j] - m_t)
        num = s_inter * qc[j] + s_intra * h_intra[j]
        den = s_inter * jnp.sum(q[j] * n_prev, axis=-1, keepdims=True) + s_intra * n_intra
        h = num / jnp.maximum(jnp.abs(den), jnp.exp(-m_t))
        m_new = jnp.maximum(b_last[j] + m_prev, m_k[j])
        sa = jnp.exp(b_last[j] + m_prev - m_new)
        sb = jnp.exp(m_k[j] - m_new)
        c_ref[j] = c_prev[j] * sa + c_loc[j] * sb
        n_ref[j:j + 1, :] = n_prev * sa + jnp.sum(kw[j], axis=0, keepdims=True) * sb
        m_ref[j:j + 1, :] = jnp.broadcast_to(m_new, (1, LANES))
        h = h * lax.rsqrt(jnp.mean(h * h, axis=-1, keepdims=True) + EPS) * nrm_ref[...]
        mix_ref[:, vs[j]] = (h * jax.nn.sigmoid(o_ref[:, vs[j]]) * _silu(zc_ref[:, vs[j]])).astype(BF16)

    @pl.when(last)
    def _():
        c_out[0] = c_ref[...]
        n_out[0] = n_ref[...]
        m_out[0] = m_ref[...]


def _mlstm_prompt(zm, zs, zs_t, m_total, bsz, seq, i_bias, f_bias, norm):
    nc = seq // CHUNK
    rb = lambda b, c: b * nc + c
    pc = jnp.zeros((SUBLANES, LANES), F32).at[0, :ML_HEADS].set(i_bias).at[1, ML_HEADS:2 * ML_HEADS].set(f_bias)
    pr = jnp.stack([jnp.broadcast_to(i_bias[:, None], (ML_HEADS, CHUNK)),
                    jnp.broadcast_to(f_bias[:, None], (ML_HEADS, CHUNK))])
    full = lambda *shape: pl.BlockSpec(shape, lambda b, c: (0,) * len(shape))
    return pl.pallas_call(
        _mlstm_prompt_kernel,
        grid=(bsz, nc),
        in_specs=[
            pl.BlockSpec((CHUNK, ML_QK), lambda b, c: (rb(b, c), 0)),
            pl.BlockSpec((CHUNK, ML_QK), lambda b, c: (rb(b, c), 1)),
            pl.BlockSpec((CHUNK, ML_V), lambda b, c: (rb(b, c), 1)),
            pl.BlockSpec((CHUNK, ML_V), lambda b, c: (rb(b, c), 2)),
            pl.BlockSpec((CHUNK, ML_V), lambda b, c: (rb(b, c), 3)),
            pl.BlockSpec((CHUNK, LANES), lambda b, c: (rb(b, c), 0)),
            pl.BlockSpec((2 * ML_HEADS, CHUNK), lambda b, c: (0, rb(b, c))),
            full(SUBLANES, LANES), full(2, ML_HEADS, CHUNK), full(1, ML_DV),
        ],
        out_specs=[
            pl.BlockSpec((CHUNK, ML_V), lambda b, c: (rb(b, c), 0)),
            pl.BlockSpec((1, ML_HEADS, ML_DK, ML_DV), lambda b, c: (b, 0, 0, 0)),
            pl.BlockSpec((1, ML_HEADS, ML_DK), lambda b, c: (b, 0, 0)),
            pl.BlockSpec((1, ML_HEADS, LANES), lambda b, c: (b, 0, 0)),
        ],
        out_shape=[
            jax.ShapeDtypeStruct((m_total, ML_V), BF16),
            jax.ShapeDtypeStruct((bsz, ML_HEADS, ML_DK, ML_DV), F32),
            jax.ShapeDtypeStruct((bsz, ML_HEADS, ML_DK), F32),
            jax.ShapeDtypeStruct((bsz, ML_HEADS, LANES), F32),
        ],
        scratch_shapes=[pltpu.VMEM((ML_HEADS, ML_DK, ML_DV), F32), pltpu.VMEM((ML_HEADS, ML_DK), F32),
                        pltpu.VMEM((ML_HEADS, LANES), F32)],
        compiler_params=_params("parallel", "arbitrary"),
        name="mlstm_prompt",
    )(zm, zm, zm, zm, zm, zs, zs_t, pc, pr, norm[None, :])


def _cmlp_prompt_kernel(u_ref, v_ref, z_ref, ws_ref, wb_ref, gain_ref, mix_ref, vrows_ref):
    r, c = _causal_masks()
    causal = r >= c
    for g in range(CM_GROUPS):
        gs = slice(g * CM_GROUP_DIM, (g + 1) * CM_GROUP_DIM)
        v = _gelu(v_ref[:, gs])
        v = v * lax.rsqrt(jnp.mean(v * v, axis=-1, keepdims=True) + EPS) * gain_ref[...]
        s = _mm(jnp.where(causal, ws_ref[g], 0.0), v) + wb_ref[:, g:g + 1]
        mix_ref[:, gs] = (_gelu(u_ref[:, gs]) * s * _silu(z_ref[:, gs])).astype(BF16)
        vrows_ref[0, :, gs] = v


def _cmlp_prompt(zm, m_total, bsz, seq, v_gain, ws, wb):
    nc = seq // CM_CHUNK
    rb = lambda b, c: b * nc + c
    col0 = (2 * ML_QK + 3 * ML_V) // CM_WIDTH
    return pl.pallas_call(
        _cmlp_prompt_kernel,
        grid=(bsz, nc),
        in_specs=[
            pl.BlockSpec((CM_CHUNK, CM_WIDTH), lambda b, c: (rb(b, c), col0)),
            pl.BlockSpec((CM_CHUNK, CM_WIDTH), lambda b, c: (rb(b, c), col0 + 1)),
            pl.BlockSpec((CM_CHUNK, CM_WIDTH), lambda b, c: (rb(b, c), col0 + 2)),
            pl.BlockSpec((CM_GROUPS, CM_CHUNK, CM_CHUNK), lambda b, c: (0, 0, 0)),
            pl.BlockSpec((CM_CHUNK, CM_GROUPS), lambda b, c: (0, 0)),
            pl.BlockSpec((1, CM_GROUP_DIM), lambda b, c: (0, 0)),
        ],
        out_specs=[
            pl.BlockSpec((CM_CHUNK, CM_WIDTH), lambda b, c: (rb(b, c), 0)),
            pl.BlockSpec((1, CM_CHUNK, CM_WIDTH), lambda b, c: (b, 0, 0)),
        ],
        out_shape=[jax.ShapeDtypeStruct((m_total, CM_WIDTH), BF16),
                   jax.ShapeDtypeStruct((bsz, CM_CHUNK, CM_WIDTH), F32)],
        compiler_params=_params("parallel", "arbitrary"),
        name="cmlp_prompt",
    )(zm, zm, zm, ws, wb.T, v_gain[None, :])


SEQ_BLOCK = SUBLANES


def _conv_step(raw_ref, cin_ref, w_ref, cout_ref):
    u = raw_ref[...]
    out = w_ref[CONV_K - 1:CONV_K, :] * u
    for k in range(CONV_K - 1):
        out = out + w_ref[k:k + 1, :] * cin_ref[k]
    for k in range(CONV_K - 2):
        cout_ref[k] = cin_ref[k + 1]
    cout_ref[CONV_K - 2] = u
    return out


def _row0(row, fill=0.0):
    return jnp.where(_iota((SUBLANES, row.shape[1]), 0) == 0, row, fill)


def _stash_rows(dst_ref, val):
    for i in range(SEQ_BLOCK):
        dst_ref[i] = jnp.broadcast_to(val[i:i + 1, :], (SUBLANES, val.shape[1]))


def _gather_rows(src_ref):
    rid = _iota(src_ref.shape[1:], 0)
    acc = src_ref[0]
    for i in range(1, SEQ_BLOCK):
        acc = jnp.where(rid == i, src_ref[i], acc)
    return acc


def _split3(x):
    hi = x.astype(BF16).astype(F32)
    r1 = x - hi
    mid = r1.astype(BF16).astype(F32)
    lo = (r1 - mid).astype(BF16).astype(F32)
    return hi, mid, lo


def _ssd_decode_kernel(za_ref, xs_ref, b_ref, c_ref, zc_ref, cx_ref, cb_ref, cc_ref, wx_ref, bx_ref, wb_ref, bb_ref,
                       wc_ref, bc_ref, pc_ref, e_ref, a_ref, d_ref, nrm_ref, s_ref, mixin_ref,
                       mix_ref, ncx_ref, ncb_ref, ncc_ref, so_ref, l_s, b_s, c_s, y_s):
    del mixin_ref
    xs = _silu(_conv_step(xs_ref, cx_ref, wx_ref, ncx_ref) + bx_ref[...])
    _stash_rows(b_s, _silu(_conv_step(b_ref, cb_ref, wb_ref, ncb_ref) + bb_ref[...]))
    _stash_rows(c_s, _silu(_conv_step(c_ref, cc_ref, wc_ref, ncc_ref) + bc_ref[...]))
    dtx = _mm_hi(_softplus(zc_ref[...] + pc_ref[0:1, :]), e_ref[...])
    hi, mid, lo = _split3(jnp.exp(dtx * a_ref[...]))
    dx = dtx * xs
    rid = _iota((SUBLANES, SSD_GW), 0)
    for i in range(SEQ_BLOCK):
        l_s[i] = jnp.where(rid == 0, dx[i:i + 1, :], jnp.where(rid == 1, hi[i:i + 1, :],
                           jnp.where(rid == 2, mid[i:i + 1, :], jnp.where(rid == 3, lo[i:i + 1, :], 0.0))))
    ones_rows = jnp.where((_iota((SUBLANES, SSD_STATE), 0) >= 1) & (_iota((SUBLANES, SSD_STATE), 0) <= 3), 1.0, 0.0)
    hpg = SSD_HEADS // SSD_GROUPS

    def body(i, carry):
        rmat = jnp.concatenate([_row0(b_s[i]), ones_rows], axis=1)
        upd = _mm_tn(l_s[i], rmat)
        s_new = s_ref[i].reshape(SSD_GW, SSD_STATE) * upd[:, SSD_STATE:] + upd[:, :SSD_STATE]
        so_ref[i] = s_new.reshape(hpg, SSD_HEAD_DIM, SSD_STATE)
        y_s[i] = _mm_nt(c_s[i], s_new)
        return carry

    lax.fori_loop(0, SEQ_BLOCK, body, 0)
    y = _gather_rows(y_s) + d_ref[...] * xs
    y = y * _silu(za_ref[...])
    y = y * lax.rsqrt(jnp.mean(y * y, axis=-1, keepdims=True) + EPS) * nrm_ref[...]
    mix_ref[...] = y.astype(BF16)


def _ssd_decode(zm, zs, mix, conv_t, state, mp, conv_w, conv_b, dt_bias, a_log, d_skip, norm):
    bsz = state.shape[0]
    hpg = SSD_HEADS // SSD_GROUPS
    r0 = mp // SEQ_BLOCK
    pc = jnp.zeros((SUBLANES, LANES), F32).at[0, :SSD_HEADS].set(dt_bias)
    e_mat = (jnp.arange(SSD_WIDTH)[None, :] // SSD_HEAD_DIM == jnp.arange(LANES)[:, None]).astype(F32)
    a_x = jnp.repeat(-jnp.exp(a_log), SSD_HEAD_DIM)[None, :]
    nb = SSD_WIDTH // SSD_STATE
    zrow = lambda w, col: pl.BlockSpec((SEQ_BLOCK, w), lambda s, g: (r0 + s, col(g)))
    cst = lambda w, col: pl.BlockSpec((CONV_K - 1, SEQ_BLOCK, w), lambda s, g: (0, s, col(g)))
    par = lambda rows, w, col: pl.BlockSpec((rows, w), lambda s, g: (0, col(g)))
    mainb = 6 * D_MODEL // SSD_STATE
    return pl.pallas_call(
        _ssd_decode_kernel,
        grid=(bsz // SEQ_BLOCK, SSD_GROUPS),
        in_specs=[
            zrow(SSD_GW, lambda g: g), zrow(SSD_GW, lambda g: SSD_GROUPS + g),
            zrow(SSD_STATE, lambda g: mainb + g), zrow(SSD_STATE, lambda g: mainb + SSD_GROUPS + g),
            zrow(LANES, lambda g: 0),
            cst(SSD_GW, lambda g: g), cst(SSD_STATE, lambda g: nb + g), cst(SSD_STATE, lambda g: nb + SSD_GROUPS + g),
            par(CONV_K, SSD_GW, lambda g: g), par(1, SSD_GW, lambda g: g),
            par(CONV_K, SSD_STATE, lambda g: nb + g), par(1, SSD_STATE, lambda g: nb + g),
            par(CONV_K, SSD_STATE, lambda g: nb + SSD_GROUPS + g), par(1, SSD_STATE, lambda g: nb + SSD_GROUPS + g),
            par(SUBLANES, LANES, lambda g: 0), par(LANES, SSD_GW, lambda g: g),
            par(1, SSD_GW, lambda g: g), par(1, SSD_GW, lambda g: g), par(1, SSD_GW, lambda g: g),
            pl.BlockSpec((SEQ_BLOCK, hpg, SSD_HEAD_DIM, SSD_STATE), lambda s, g: (s, g, 0, 0)),
            pl.BlockSpec(memory_space=pl.ANY),
        ],
        out_specs=[
            pl.BlockSpec((SEQ_BLOCK, SSD_GW), lambda s, g: (r0 + s, g)),
            cst(SSD_GW, lambda g: g), cst(SSD_STATE, lambda g: g), cst(SSD_STATE, lambda g: g),
            pl.BlockSpec((SEQ_BLOCK, hpg, SSD_HEAD_DIM, SSD_STATE), lambda s, g: (s, g, 0, 0)),
        ],
        out_shape=[
            jax.ShapeDtypeStruct(mix.shape, mix.dtype),
            jax.ShapeDtypeStruct((CONV_K - 1, bsz, SSD_WIDTH), F32),
            jax.ShapeDtypeStruct((CONV_K - 1, bsz, SSD_GROUPS * SSD_STATE), F32),
            jax.ShapeDtypeStruct((CONV_K - 1, bsz, SSD_GROUPS * SSD_STATE), F32),
            jax.ShapeDtypeStruct(state.shape, F32),
        ],
        scratch_shapes=[pltpu.VMEM((SEQ_BLOCK, SUBLANES, SSD_GW), F32), pltpu.VMEM((SEQ_BLOCK, SUBLANES, SSD_STATE), F32),
                        pltpu.VMEM((SEQ_BLOCK, SUBLANES, SSD_STATE), F32), pltpu.VMEM((SEQ_BLOCK, SUBLANES, SSD_GW), F32)],
        input_output_aliases={20: 0},
        compiler_params=_params("parallel", "arbitrary"),
        name="ssd_decode",
    )(zm, zm, zm, zm, zs, conv_t, conv_t, conv_t, conv_w, conv_b[None, :], conv_w, conv_b[None, :], conv_w,
      conv_b[None, :], pc, e_mat, a_x, jnp.repeat(d_skip, SSD_HEAD_DIM)[None, :], norm[None, :], state, mix)


def _gdn_decode_kernel(q_ref, k_ref, v_ref, zb_ref, zc_ref, cq_ref, ck_ref, cv_ref, wq_ref, wk_ref, wv_ref, pc_ref,
                       nrm_ref, s_ref, mixin_ref, mix_ref, ncq_ref, nck_ref, ncv_ref, so_ref,
                       q_s, k_s, v_s, beta_s, eg_s, qk_s, o_s):
    del mixin_ref
    q_all = _silu(_conv_step(q_ref, cq_ref, wq_ref, ncq_ref))
    k_all = _silu(_conv_step(k_ref, ck_ref, wk_ref, nck_ref))
    _stash_rows(v_s, _silu(_conv_step(v_ref, cv_ref, wv_ref, ncv_ref)))
    zc = zc_ref[...]
    beta = jax.nn.sigmoid(zc)
    eg = jnp.exp(-jnp.exp(pc_ref[0:1, :]) * _softplus(zc + pc_ref[1:2, :]))
    heads = range(GDN_HG)
    hs = [slice(j * GDN_DK, (j + 1) * GDN_DK) for j in heads]
    qn, kn, beta_x, eg_x, qk_x = [], [], [], [], []
    shape = (SEQ_BLOCK, GDN_DK)
    for j in heads:
        qh, kh = q_all[:, hs[j]], k_all[:, hs[j]]
        qh = qh * (lax.rsqrt(jnp.sum(qh * qh, axis=-1, keepdims=True) + EPS) * (GDN_DK ** -0.5))
        kh = kh * lax.rsqrt(jnp.sum(kh * kh, axis=-1, keepdims=True) + EPS)
        qn.append(qh)
        kn.append(kh)
        beta_x.append(jnp.broadcast_to(beta[:, j:j + 1], shape))
        eg_x.append(jnp.broadcast_to(eg[:, GDN_HG + j:GDN_HG + j + 1], shape))
        qk_x.append(jnp.broadcast_to(jnp.sum(qh * kh, axis=-1, keepdims=True), shape))
    for ref, parts in ((q_s, qn), (k_s, kn), (beta_s, beta_x), (eg_s, eg_x), (qk_s, qk_x)):
        _stash_rows(ref, jnp.concatenate(parts, axis=1))

    rid = _iota((SUBLANES, GDN_DK), 0)

    def body(i, carry):
        k_b, q_b, v_b, beta_b, eg_b, qk_b = k_s[i], q_s[i], v_s[i], beta_s[i], eg_s[i], qk_s[i]
        s_prev = [s_ref[i, j] for j in heads]
        ks_qs = [_mm(jnp.where(rid == 0, k_b[:, hs[j]], q_b[:, hs[j]]), s_prev[j]) for j in heads]
        vn = [beta_b[0:1, hs[j]] * (v_b[0:1, hs[j]] - eg_b[0:1, hs[j]] * ks_qs[j][0:1, :]) for j in heads]
        outer = [_mm_tn(_row0(k_b[:, hs[j]]), _row0(vn[j])) for j in heads]
        for j in heads:
            so_ref[i, j] = s_prev[j] * eg_b[0:1, j * GDN_DK:j * GDN_DK + 1] + outer[j]
            o_row = eg_b[0:1, hs[j]] * ks_qs[j][1:2, :] + qk_b[0:1, hs[j]] * vn[j]
            o_s[i, :, hs[j]] = jnp.broadcast_to(o_row, (SUBLANES, GDN_DV))
        return carry

    lax.fori_loop(0, SEQ_BLOCK, body, 0)
    o_all = _gather_rows(o_s)
    for j in heads:
        o = o_all[:, hs[j]]
        o = o * lax.rsqrt(jnp.mean(o * o, axis=-1, keepdims=True) + EPS) * nrm_ref[...]
        mix_ref[:, hs[j]] = (o * _silu(zb_ref[:, hs[j]])).astype(BF16)


def _gdn_decode(zm, zs, mix, conv_t, state, mp, conv_w, dt_bias, a_log, norm):
    bsz = state.shape[0]
    nhg = GDN_HEADS // GDN_HG
    w = GDN_HG * GDN_DK
    r0 = mp // SEQ_BLOCK
    col0 = 2 * D_MODEL // w
    pc = jnp.zeros((nhg, SUBLANES, LANES), F32)
    pc = pc.at[:, 0, GDN_HG:2 * GDN_HG].set(a_log.reshape(nhg, GDN_HG))
    pc = pc.at[:, 1, GDN_HG:2 * GDN_HG].set(dt_bias.reshape(nhg, GDN_HG))
    seg = lambda p: pl.BlockSpec((SEQ_BLOCK, w), lambda s, h: (r0 + s, col0 + p * nhg + h))
    cst = lambda p: pl.BlockSpec((CONV_K - 1, SEQ_BLOCK, w), lambda s, h: (0, s, p * nhg + h))
    wseg = lambda p: pl.BlockSpec((CONV_K, w), lambda s, h: (0, p * nhg + h))
    cout = pl.BlockSpec((CONV_K - 1, SEQ_BLOCK, w), lambda s, h: (0, s, h))
    row_scratch = pltpu.VMEM((SEQ_BLOCK, SUBLANES, w), F32)
    return pl.pallas_call(
        _gdn_decode_kernel,
        grid=(bsz // SEQ_BLOCK, nhg),
        in_specs=[
            seg(0), seg(1), seg(2), seg(3),
            pl.BlockSpec((SEQ_BLOCK, LANES), lambda s, h: (r0 + s, 1 + h)),
            cst(0), cst(1), cst(2), wseg(0), wseg(1), wseg(2),
            pl.BlockSpec((None, SUBLANES, LANES), lambda s, h: (h, 0, 0)),
            pl.BlockSpec((1, GDN_DV), lambda s, h: (0, 0)),
            pl.BlockSpec((SEQ_BLOCK, GDN_HG, GDN_DK, GDN_DV), lambda s, h: (s, h, 0, 0)),
            pl.BlockSpec(memory_space=pl.ANY),
        ],
        out_specs=[
            pl.BlockSpec((SEQ_BLOCK, w), lambda s, h: (r0 + s, h)),
            cout, cout, cout,
            pl.BlockSpec((SEQ_BLOCK, GDN_HG, GDN_DK, GDN_DV), lambda s, h: (s, h, 0, 0)),
        ],
        out_shape=[
            jax.ShapeDtypeStruct(mix.shape, mix.dtype),
            jax.ShapeDtypeStruct((CONV_K - 1, bsz, GDN_QK), F32),
            jax.ShapeDtypeStruct((CONV_K - 1, bsz, GDN_QK), F32),
            jax.ShapeDtypeStruct((CONV_K - 1, bsz, GDN_V), F32),
            jax.ShapeDtypeStruct(state.shape, F32),
        ],
        scratch_shapes=[row_scratch] * 7,
        input_output_aliases={14: 0},
        compiler_params=_params("parallel", "arbitrary"),
        name="gdn_decode",
    )(zm, zm, zm, zm, zs, conv_t, conv_t, conv_t, conv_w, conv_w, conv_w, pc, norm[None, :], state, mix)


ML_HG = 4


def _mlstm_decode_kernel(q_ref, k_ref, v_ref, o_ref, zc_ref, g_ref, m_ref, n_ref, pc_ref, e_ref, nrm_ref, c_ref,
                         mixin_ref, mix_ref, n_out, m_out, c_out, q_s, kb_s, v_s, sa_s, sbqk_s, den_s, h_s):
    del mixin_ref
    g = g_ref[...]
    logi = g + pc_ref[0:1, :]
    logf = -_softplus(-pltpu.roll(g + pc_ref[1:2, :], LANES - ML_HEADS, 1))
    m_prev = m_ref[...]
    m_new = jnp.maximum(logf + m_prev, logi)
    m_out[...] = m_new
    e_mat = e_ref[...]
    sa_x = _mm_hi(jnp.exp(logf + m_prev - m_new), e_mat)
    sb_x = _mm_hi(jnp.exp(logi - m_new), e_mat)
    em_x = _mm_hi(jnp.exp(-m_new), e_mat)
    q = q_ref[...]
    k = k_ref[...] * (ML_DK ** -0.5)
    n_prev = n_ref[...]
    n_out[...] = n_prev * sa_x + sb_x * k
    _stash_rows(q_s, q)
    _stash_rows(kb_s, sb_x * k)
    _stash_rows(v_s, v_ref[...])
    _stash_rows(sa_s, sa_x)
    heads = range(ML_HG)
    ks = [slice(j * ML_DK, (j + 1) * ML_DK) for j in heads]
    vs = [slice(j * ML_DV, (j + 1) * ML_DV) for j in heads]
    sbqk_x, den_x = [], []
    for j in heads:
        shape = (SEQ_BLOCK, ML_DK)
        qk = jnp.sum(q[:, ks[j]] * k[:, ks[j]], axis=-1, keepdims=True)
        qn = jnp.sum(q[:, ks[j]] * n_prev[:, ks[j]], axis=-1, keepdims=True)
        sbqk = sb_x[:, ks[j]] * jnp.broadcast_to(qk, shape)
        sbqk_x.append(sbqk)
        den_x.append(jnp.maximum(jnp.abs(sa_x[:, ks[j]] * jnp.broadcast_to(qn, shape) + sbqk), em_x[:, ks[j]]))
    _stash_rows(sbqk_s, jnp.concatenate(sbqk_x, axis=1))
    _stash_rows(den_s, jnp.concatenate(den_x, axis=1))

    def body(i, carry):
        q_b, kb_b, v_b, sa_b, sbqk_b, den_b = q_s[i], kb_s[i], v_s[i], sa_s[i], sbqk_s[i], den_s[i]
        c_prev = [c_ref[i, j] for j in heads]
        qc = [_mm(q_b[:, ks[j]], c_prev[j]) for j in heads]
        outer = [_mm_tn(_row0(kb_b[:, ks[j]]), _row0(v_b[:, vs[j]])) for j in heads]
        for j in heads:
            lane0 = slice(j * ML_DK, j * ML_DK + 1)
            c_out[i, j] = c_prev[j] * sa_b[0:1, lane0] + outer[j]
            num = sa_b[:, lane0] * qc[j] + sbqk_b[:, lane0] * v_b[:, vs[j]]
            h_s[i, :, vs[j]] = num / den_b[:, lane0]
        return carry

    lax.fori_loop(0, SEQ_BLOCK, body, 0)
    h_all = _gather_rows(h_s)
    for j in heads:
        h = h_all[:, vs[j]]
        h = h * lax.rsqrt(jnp.mean(h * h, axis=-1, keepdims=True) + EPS) * nrm_ref[...]
        mix_ref[:, vs[j]] = (h * jax.nn.sigmoid(o_ref[:, vs[j]]) * _silu(zc_ref[:, vs[j]])).astype(BF16)


def _mlstm_decode(zm, zs, mix, c0, n0, m0, mp, i_bias, f_bias, norm):
    bsz = c0.shape[0]
    nhg = ML_HEADS // ML_HG
    wk, wv = ML_HG * ML_DK, ML_HG * ML_DV
    r0 = mp // SEQ_BLOCK
    pc = jnp.zeros((SUBLANES, LANES), F32).at[0, :ML_HEADS].set(i_bias).at[1, ML_HEADS:2 * ML_HEADS].set(f_bias)
    e_mat = (jnp.arange(ML_QK)[None, :] // ML_DK == jnp.arange(LANES)[:, None]).astype(F32)
    zrow = lambda w, col: pl.BlockSpec((SEQ_BLOCK, w), lambda s, h: (r0 + s, col(h)))
    vcol = 2 * ML_QK // wv
    return pl.pallas_call(
        _mlstm_decode_kernel,
        grid=(bsz // SEQ_BLOCK, nhg),
        in_specs=[
            zrow(wk, lambda h: h), zrow(wk, lambda h: nhg + h),
            zrow(wv, lambda h: vcol + h), zrow(wv, lambda h: vcol + nhg + h), zrow(wv, lambda h: vcol + 2 * nhg + h),
            zrow(LANES, lambda h: 0),
            pl.BlockSpec((SEQ_BLOCK, LANES), lambda s, h: (s, 0)),
            pl.BlockSpec((SEQ_BLOCK, wk), lambda s, h: (s, h)),
            pl.BlockSpec((SUBLANES, LANES), lambda s, h: (0, 0)),
            pl.BlockSpec((LANES, wk), lambda s, h: (0, h)),
            pl.BlockSpec((1, ML_DV), lambda s, h: (0, 0)),
            pl.BlockSpec((SEQ_BLOCK, ML_HG, ML_DK, ML_DV), lambda s, h: (s, h, 0, 0)),
            pl.BlockSpec(memory_space=pl.ANY),
        ],
        out_specs=[
            pl.BlockSpec((SEQ_BLOCK, wv), lambda s, h: (r0 + s, h)),
            pl.BlockSpec((SEQ_BLOCK, wk), lambda s, h: (s, h)),
            pl.BlockSpec((SEQ_BLOCK, LANES), lambda s, h: (s, 0)),
            pl.BlockSpec((SEQ_BLOCK, ML_HG, ML_DK, ML_DV), lambda s, h: (s, h, 0, 0)),
        ],
        out_shape=[
            jax.ShapeDtypeStruct(mix.shape, mix.dtype),
            jax.ShapeDtypeStruct((bsz, ML_QK), F32),
            jax.ShapeDtypeStruct((bsz, LANES), F32),
            jax.ShapeDtypeStruct(c0.shape, F32),
        ],
        scratch_shapes=[pltpu.VMEM((SEQ_BLOCK, SUBLANES, w), F32) for w in (wk, wk, wv, wk, wk, wk, wv)],
        input_output_aliases={12: 0},
        compiler_params=_params("parallel", "arbitrary"),
        name="mlstm_decode",
    )(zm, zm, zm, zm, zm, zs, jnp.pad(m0, ((0, 0), (0, LANES - ML_HEADS))), n0.reshape(bsz, ML_QK), pc, e_mat,
      norm[None, :], c0, mix)


def _cmlp_decode_kernel(u_ref, v_ref, z_ref, ws_ref, wb_ref, gain_ref, mixin_ref, mix_ref, vrows_ref):
    del mixin_ref
    for g in range(CM_GROUPS):
        gs = slice(g * CM_GROUP_DIM, (g + 1) * CM_GROUP_DIM)
        v = _gelu(v_ref[:, gs])
        v = v * lax.rsqrt(jnp.mean(v * v, axis=-1, keepdims=True) + EPS) * gain_ref[...]
        s = ws_ref[:, gs] * v + wb_ref[:, gs]
        mix_ref[:, gs] = (_gelu(u_ref[:, gs]) * s * _silu(z_ref[:, gs])).astype(BF16)
        vrows_ref[:, gs] = v


def _cmlp_decode(zm, mix, mp, bsz, v_gain, ws, wb):
    col0 = (2 * ML_QK + 3 * ML_V) // CM_WIDTH
    r0 = mp // bsz
    zrow = lambda col: pl.BlockSpec((bsz, CM_WIDTH), lambda i: (r0, col))
    par = pl.BlockSpec((1, CM_WIDTH), lambda i: (0, 0))
    return pl.pallas_call(
        _cmlp_decode_kernel,
        grid=(1,),
        in_specs=[zrow(col0), zrow(col0 + 1), zrow(col0 + 2), par, par,
                  pl.BlockSpec((1, CM_GROUP_DIM), lambda i: (0, 0)), pl.BlockSpec(memory_space=pl.ANY)],
        out_specs=[pl.BlockSpec((bsz, CM_WIDTH), lambda i: (r0, 0)), pl.BlockSpec((bsz, CM_WIDTH), lambda i: (0, 0))],
        out_shape=[jax.ShapeDtypeStruct(mix.shape, mix.dtype), jax.ShapeDtypeStruct((bsz, CM_WIDTH), F32)],
        input_output_aliases={6: 0},
        compiler_params=_params("arbitrary"),
        name="cmlp_decode",
    )(zm, zm, zm, jnp.repeat(ws[:, 0, 0], CM_GROUP_DIM)[None, :], jnp.repeat(wb[:, 0], CM_GROUP_DIM)[None, :],
      v_gain[None, :], mix)


def kernel(x_prompt, x_sample, state_ssd_conv, state_ssd, state_gdn_conv, state_gdn, state_mlstm_c,
           state_mlstm_n, state_mlstm_m, even_norm, even_w_in, ssd_conv_w, ssd_conv_b, ssd_dt_bias, ssd_a_log,
           ssd_d, ssd_norm, gdn_conv_w, gdn_dt_bias, gdn_a_log, gdn_norm, even_w_out, odd_norm, odd_w_in,
           mlstm_i_bias, mlstm_f_bias, mlstm_norm, cmlp_v_norm, cmlp_ws, cmlp_b, odd_w_out, final_norm):
    bp, seq, d = x_prompt.shape
    bs = x_sample.shape[0]
    mp = bp * seq
    mt = mp + bs
    x = jnp.concatenate([x_prompt.reshape(mp, d), x_sample.reshape(bs, d)], axis=0)

    keys = ("sc", "ss", "gc", "gs", "mc", "mn", "mm", "cv")
    outs_p = {k: [] for k in keys}
    outs_s = {k: [] for k in keys}
    for layer in range(DEPTH):
        i = layer // 2
        if layer % 2 == 0:
            w_main, w_small = _prep_even_w_in(even_w_in[i])
            zm, zs = _inproj(x, even_norm[i], w_main, w_small, tn=1024)
            zs_t = zs[:mp].T
            mix_a, cx, cbc, st = _ssd_prompt(zm, zs, zs_t, mt, bp, seq, ssd_conv_w[i], ssd_conv_b[i],
                                             ssd_dt_bias[i], ssd_a_log[i], ssd_d[i], ssd_norm[i])
            mix_b, cq, ck, cv, gst = _gdn_prompt(zm, zs, zs_t, mt, bp, seq, gdn_conv_w[i], gdn_dt_bias[i],
                                                 gdn_a_log[i], gdn_norm[i])
            hpg = SSD_HEADS // SSD_GROUPS
            outs_p["sc"].append(jnp.concatenate([cx, cbc], axis=-1))
            outs_p["ss"].append(st.reshape(bp, SSD_GROUPS, SSD_STATE, hpg, SSD_HEAD_DIM).transpose(0, 1, 3, 4, 2)
                                .reshape(bp, SSD_HEADS, SSD_HEAD_DIM, SSD_STATE))
            outs_p["gc"].append(jnp.concatenate([cq, ck, cv], axis=-1))
            outs_p["gs"].append(gst)
            mix_a, ncx, ncb, ncc, ss_s = _ssd_decode(
                zm, zs, mix_a, jnp.swapaxes(state_ssd_conv[i], 0, 1), state_ssd[i], mp, ssd_conv_w[i],
                ssd_conv_b[i], ssd_dt_bias[i], ssd_a_log[i], ssd_d[i], ssd_norm[i])
            mix_b, ncq, nck, ncv, gs_s = _gdn_decode(
                zm, zs, mix_b, jnp.swapaxes(state_gdn_conv[i], 0, 1), state_gdn[i], mp, gdn_conv_w[i],
                gdn_dt_bias[i], gdn_a_log[i], gdn_norm[i])
            outs_s["sc"].append(jnp.swapaxes(jnp.concatenate([ncx, ncb, ncc], axis=-1), 0, 1))
            outs_s["ss"].append(ss_s)
            outs_s["gc"].append(jnp.swapaxes(jnp.concatenate([ncq, nck, ncv], axis=-1), 0, 1))
            outs_s["gs"].append(gs_s)
            w_out = even_w_out[i].astype(BF16)
        else:
            w_main, w_small = _prep_odd_w_in(odd_w_in[i])
            zm, zs = _inproj(x, odd_norm[i], w_main, w_small, tn=1024)
            zs_t = zs[:mp].T
            mix_a, c_p, n_p, m_p = _mlstm_prompt(zm, zs, zs_t, mt, bp, seq, mlstm_i_bias[i], mlstm_f_bias[i],
                                                 mlstm_norm[i])
            mix_b, v_rows = _cmlp_prompt(zm, mt, bp, seq, cmlp_v_norm[i], cmlp_ws[i], cmlp_b[i])
            outs_p["mc"].append(c_p); outs_p["mn"].append(n_p); outs_p["mm"].append(m_p[:, :, 0])
            outs_p["cv"].append(v_rows)
            mix_a, n_s, m_s, c_s = _mlstm_decode(zm, zs, mix_a, state_mlstm_c[i], state_mlstm_n[i], state_mlstm_m[i],
                                                 mp, mlstm_i_bias[i], mlstm_f_bias[i], mlstm_norm[i])
            mix_b, v_row_s = _cmlp_decode(zm, mix_b, mp, bs, cmlp_v_norm[i], cmlp_ws[i], cmlp_b[i])
            outs_s["mc"].append(c_s)
            outs_s["mn"].append(n_s.reshape(bs, ML_HEADS, ML_DK))
            outs_s["mm"].append(m_s[:, :ML_HEADS])
            outs_s["cv"].append(v_row_s.reshape(bs, 1, CM_WIDTH))
            w_out = odd_w_out[i].astype(BF16)
        x = _outproj(x, mix_a, mix_b, w_out)

    y = _final_norm(x, final_norm)
    y_p = y[:mp].reshape(bp, seq, d)
    y_s = y[mp:].reshape(bs, 1, d)
    st = lambda o, k: jnp.stack(o[k])
    return (y_p, y_s, st(outs_p, "sc"), st(outs_s, "sc"), st(outs_p, "ss"), st(outs_s, "ss"),
            st(outs_p, "gc"), st(outs_s, "gc"), st(outs_p, "gs"), st(outs_s, "gs"),
            st(outs_p, "mc"), st(outs_s, "mc"), st(outs_p, "mn"), st(outs_s, "mn"),
            st(outs_p, "mm"), st(outs_s, "mm"), st(outs_p, "cv"), st(outs_s, "cv"))
```

```python
import jax
import jax.numpy as jnp
import numpy as np
from jax import lax
from jax.experimental import pallas as pl
from jax.experimental.pallas import tpu as pltpu

F32 = jnp.float32
BF16 = jnp.bfloat16
HI = lax.Precision.HIGHEST

D_MODEL = 2048
DEPTH = 4
CHUNK = 128
CONV_K = 4
EPS = 1e-6
NEG = -1e30

SSD_WIDTH = D_MODEL
SSD_HEAD_DIM = 64
SSD_HEADS = SSD_WIDTH // SSD_HEAD_DIM
SSD_STATE = 128
SSD_GROUPS = 4
SSD_GW = SSD_WIDTH // SSD_GROUPS
SSD_BC = 2 * SSD_GROUPS * SSD_STATE
SSD_CONV_DIM = SSD_WIDTH + SSD_BC
GDN_HEADS = 16
GDN_DK = 128
GDN_DV = 128
GDN_QK = GDN_HEADS * GDN_DK
GDN_V = GDN_HEADS * GDN_DV
GDN_CONV_DIM = 2 * GDN_QK + GDN_V
GDN_HG = 8
ML_HEADS = 8
ML_DK = 128
ML_DV = 256
ML_QK = ML_HEADS * ML_DK
ML_V = ML_HEADS * ML_DV
CM_WIDTH = D_MODEL // 2
CM_GROUPS = 8
CM_GROUP_DIM = CM_WIDTH // CM_GROUPS
CM_CHUNK = 128

LANES = 128
SUBLANES = 8

EVEN_MAIN = 6 * D_MODEL + SSD_BC
EVEN_SMALL = 3 * LANES
ODD_MAIN = 2 * ML_QK + 3 * ML_V + 3 * CM_WIDTH
ODD_SMALL = LANES

VMEM_LIMIT = 56 * 1024 * 1024
ROW_TILE_CAP = 1040
BF16_SUBLANES = 16


def _row_tile(m):
    return max(t for t in range(BF16_SUBLANES, ROW_TILE_CAP + 1, BF16_SUBLANES) if m % t == 0)


def _silu(x):
    return x * jax.nn.sigmoid(x)


def _softplus(x):
    return jnp.maximum(x, 0.0) + jnp.log1p(jnp.exp(-jnp.abs(x)))


def _gelu(x):
    return 0.5 * x * (1.0 + jnp.tanh(np.sqrt(2.0 / np.pi).astype(np.float32) * (x + 0.044715 * (x * x * x))))


def _mm(a, b):
    return jnp.dot(a.astype(BF16), b.astype(BF16), preferred_element_type=F32)


def _mm_nt(a, b):
    return lax.dot_general(a.astype(BF16), b.astype(BF16), (((1,), (1,)), ((), ())), preferred_element_type=F32)


def _mm_tn(a, b):
    return lax.dot_general(a.astype(BF16), b.astype(BF16), (((0,), (0,)), ((), ())), preferred_element_type=F32)


def _mm_hi(a, b):
    return jnp.dot(a, b, precision=HI, preferred_element_type=F32)


def _iota(shape, axis):
    return lax.broadcasted_iota(jnp.int32, shape, axis)


def _params(*sem):
    return pltpu.CompilerParams(dimension_semantics=sem, vmem_limit_bytes=VMEM_LIMIT)


def _inproj_kernel(x_ref, g_ref, w_ref, ws_ref, z_ref, zs_ref, xn_ref):
    @pl.when(pl.program_id(1) == 0)
    def _():
        x = x_ref[...]
        y = x * lax.rsqrt(jnp.mean(x * x, axis=-1, keepdims=True) + EPS)
        xn = (y * g_ref[...]).astype(BF16)
        xn_ref[...] = xn
        zs_ref[...] = jnp.dot(xn, ws_ref[...], preferred_element_type=F32)

    z_ref[...] = jnp.dot(xn_ref[...], w_ref[...], preferred_element_type=F32)


def _inproj(x, g, w_main, w_small, tn):
    m, d = x.shape
    n = w_main.shape[1]
    ns = w_small.shape[1]
    tm = _row_tile(m)
    return pl.pallas_call(
        _inproj_kernel,
        grid=(m // tm, n // tn),
        in_specs=[
            pl.BlockSpec((tm, d), lambda i, j: (i, 0)),
            pl.BlockSpec((1, d), lambda i, j: (0, 0)),
            pl.BlockSpec((d, tn), lambda i, j: (0, j)),
            pl.BlockSpec((d, ns), lambda i, j: (0, 0)),
        ],
        out_specs=[
            pl.BlockSpec((tm, tn), lambda i, j: (i, j)),
            pl.BlockSpec((tm, ns), lambda i, j: (i, 0)),
        ],
        out_shape=[jax.ShapeDtypeStruct((m, n), F32), jax.ShapeDtypeStruct((m, ns), F32)],
        scratch_shapes=[pltpu.VMEM((tm, d), BF16)],
        compiler_params=_params("parallel", "arbitrary"),
        name="inproj",
    )(x, g.reshape(1, d), w_main, w_small)


def _outproj_kernel(x_ref, ma_ref, mb_ref, wa_ref, wb_ref, o_ref):
    o_ref[...] = (x_ref[...] + jnp.dot(ma_ref[...], wa_ref[...], preferred_element_type=F32)
                  + jnp.dot(mb_ref[...], wb_ref[...], preferred_element_type=F32))


def _outproj(x, mix_a, mix_b, w):
    m, d = x.shape
    ka, kb = mix_a.shape[1], mix_b.shape[1]
    tm, tn = _row_tile(m), 512
    return pl.pallas_call(
        _outproj_kernel,
        grid=(m // tm, d // tn),
        in_specs=[
            pl.BlockSpec((tm, tn), lambda i, j: (i, j)),
            pl.BlockSpec((tm, ka), lambda i, j: (i, 0)),
            pl.BlockSpec((tm, kb), lambda i, j: (i, 0)),
            pl.BlockSpec((ka, tn), lambda i, j: (0, j)),
            pl.BlockSpec((kb, tn), lambda i, j: (ka // kb, j)),
        ],
        out_specs=pl.BlockSpec((tm, tn), lambda i, j: (i, j)),
        out_shape=jax.ShapeDtypeStruct((m, d), F32),
        compiler_params=_params("parallel", "arbitrary"),
        name="outproj",
    )(x, mix_a, mix_b, w, w)


def _final_norm_kernel(x_ref, g_ref, o_ref):
    x = x_ref[...]
    o_ref[...] = x * lax.rsqrt(jnp.mean(x * x, axis=-1, keepdims=True) + EPS) * g_ref[...]


def _final_norm(x, g, row0, rows, tm):
    d = x.shape[1]
    return pl.pallas_call(
        _final_norm_kernel,
        grid=(rows // tm,),
        in_specs=[pl.BlockSpec((tm, d), lambda i: (row0 // tm + i, 0)), pl.BlockSpec((1, d), lambda i: (0, 0))],
        out_specs=pl.BlockSpec((tm, d), lambda i: (i, 0)),
        out_shape=jax.ShapeDtypeStruct((rows, d), F32),
        compiler_params=_params("parallel"),
        name="final_norm",
    )(x, g.reshape(1, d))


def _pad_cols(w, width):
    return jnp.pad(w, ((0, 0), (0, width - w.shape[1])))


REPACK_TN = 512


def _repack_kernel(a_ref, b_ref, o_ref, *, shift, lo, hi):
    j = pl.program_id(0)
    shifted = (j >= lo) & (j < hi)

    @pl.when(shifted)
    def _():
        o_ref[...] = jnp.concatenate([a_ref[:, shift:], b_ref[:, :shift]], axis=1).astype(BF16)

    @pl.when(jnp.logical_not(shifted))
    def _():
        o_ref[...] = a_ref[...].astype(BF16)


def _repack(w_all, layer, n_out, a_idx, b_idx, shift, lo, hi):
    d = w_all.shape[1]
    kern = lambda a, b, o: _repack_kernel(a, b, o, shift=shift, lo=lo, hi=hi)
    return pl.pallas_call(
        kern,
        grid=(n_out // REPACK_TN,),
        in_specs=[pl.BlockSpec((None, d, REPACK_TN), lambda j: (layer, 0, a_idx(j))),
                  pl.BlockSpec((None, d, REPACK_TN), lambda j: (layer, 0, b_idx(j)))],
        out_specs=pl.BlockSpec((d, REPACK_TN), lambda j: (0, j)),
        out_shape=jax.ShapeDtypeStruct((d, n_out), BF16),
        compiler_params=_params("parallel"),
        name="repack",
    )(w_all, w_all)


def _prep_even_w_in(w_all, layer):
    o_bc = 2 * SSD_WIDTH
    o_dt = o_bc + SSD_BC
    o_q = o_dt + SSD_HEADS
    o_beta = o_q + GDN_CONV_DIM + GDN_V
    o_g = o_beta + GDN_HEADS
    n1, n2 = o_bc // REPACK_TN, (o_bc + o_beta - o_q) // REPACK_TN
    src2, src3 = o_dt // REPACK_TN, o_bc // REPACK_TN
    a_idx = lambda j: jnp.where(j < n1, j, jnp.where(j < n2, j - n1 + src2, j - n2 + src3))
    b_idx = lambda j: jnp.where((j >= n1) & (j < n2), j - n1 + src2 + 1, a_idx(j))
    main = _repack(w_all, layer, EVEN_MAIN, a_idx, b_idx, o_q - o_dt, n1, n2)
    w = w_all[layer]
    small = [_pad_cols(w[:, o_dt:o_q], LANES)]
    for hg in range(GDN_HEADS // GDN_HG):
        sl = slice(hg * GDN_HG, (hg + 1) * GDN_HG)
        small.append(_pad_cols(jnp.concatenate([w[:, o_beta:o_g][:, sl], w[:, o_g:][:, sl]], axis=1), LANES))
    return main, jnp.concatenate(small, axis=1).astype(BF16)


def _prep_odd_w_in(w_all, layer):
    o1 = 2 * ML_QK + 3 * ML_V
    o2 = o1 + 2 * ML_HEADS
    n1 = o1 // REPACK_TN
    a_idx = lambda j: j
    b_idx = lambda j: jnp.where(j >= n1, j + 1, j)
    main = _repack(w_all, layer, ODD_MAIN, a_idx, b_idx, o2 - o1, n1, ODD_MAIN // REPACK_TN)
    small = _pad_cols(w_all[layer][:, o1:o2], LANES).astype(BF16)
    return main, small


def _conv_chunk(ext_ref, u, w_ref, first):
    @pl.when(first)
    def _():
        ext_ref[0:SUBLANES, :] = jnp.zeros((SUBLANES, ext_ref.shape[1]), F32)

    ext_ref[SUBLANES:SUBLANES + CHUNK, :] = u
    out = w_ref[CONV_K - 1:CONV_K, :] * u
    for k in range(1, CONV_K):
        out = out + w_ref[CONV_K - 1 - k:CONV_K - k, :] * ext_ref[SUBLANES - k:SUBLANES - k + CHUNK, :]
    ext_ref[0:SUBLANES, :] = ext_ref[CHUNK:CHUNK + SUBLANES, :]
    return out


def _conv_tail(ext_ref):
    return ext_ref[SUBLANES - (CONV_K - 1):SUBLANES, :]


def _causal_masks():
    r = _iota((CHUNK, CHUNK), 0)
    c = _iota((CHUNK, CHUNK), 1)
    return r, c


def _ssd_prompt_kernel(za_ref, xs_ref, bc_ref, zc_ref, zr_ref, wx_ref, bx_ref, wbc_ref, bbc_ref, pc_ref, pr_ref,
                       d_ref, nrm_ref, mix_ref, cx_ref, cbc_ref, st_ref, extx, extbc, s_ref):
    c_id = pl.program_id(1)
    first = c_id == 0
    last = c_id == pl.num_programs(1) - 1

    @pl.when(first)
    def _():
        s_ref[...] = jnp.zeros(s_ref.shape, F32)

    xs = _silu(_conv_chunk(extx, xs_ref[...], wx_ref, first) + bx_ref[...])
    bc = _silu(_conv_chunk(extbc, bc_ref[...], wbc_ref, first) + bbc_ref[...])

    r, c = _causal_masks()
    causal = r >= c
    tril = jnp.where(causal, 1.0, 0.0)
    triu = jnp.where(r <= c, 1.0, 0.0)
    dt = _softplus(zc_ref[...] + pc_ref[0:1, :])
    la = _mm_hi(tril, dt * (-jnp.exp(pc_ref[1:2, :])))
    dtr = _softplus(zr_ref[...] + pr_ref[0])
    lar = _mm_hi(dtr * (-jnp.exp(pr_ref[1])), triu)
    la_last = la[CHUNK - 1:CHUNK, :]
    e_mat = jnp.where((_iota((LANES, SSD_WIDTH), 1) >> 6) == _iota((LANES, SSD_WIDTH), 0), 1.0, 0.0)
    ela_x = _mm_hi(jnp.exp(la), e_mat)
    wsx = _mm_hi(jnp.exp(la_last - la) * dt, e_mat)
    dec_x = _mm_hi(jnp.broadcast_to(jnp.exp(la_last), (SUBLANES, LANES)), e_mat)[0:1, :]
    lane_lo = _iota((CHUNK, LANES), 1) < SSD_HEAD_DIM

    hpg = SSD_HEADS // SSD_GROUPS
    for g in range(SSD_GROUPS):
        gs = slice(g * SSD_GW, (g + 1) * SSD_GW)
        bg = bc[:, g * SSD_STATE:(g + 1) * SSD_STATE]
        cg = bc[:, SSD_GROUPS * SSD_STATE + g * SSD_STATE:SSD_GROUPS * SSD_STATE + (g + 1) * SSD_STATE]
        cb = _mm_nt(cg, bg)
        ys = []
        for pair in range(hpg // 2):
            h0 = g * hpg + 2 * pair
            xpair = xs[:, h0 * SSD_HEAD_DIM:(h0 + 2) * SSD_HEAD_DIM]
            halves = []
            for hh in (h0, h0 + 1):
                seg = jnp.where(causal, la[:, hh:hh + 1] - lar[hh:hh + 1, :], NEG)
                lmat = jnp.exp(seg) * cb * dtr[hh:hh + 1, :]
                halves.append(_mm(lmat, xpair))
            ys.append(jnp.where(lane_lo, halves[0], halves[1]))
        y = jnp.concatenate(ys, axis=1)
        s_prev = s_ref[g]
        y = y + _mm(cg, s_prev) * ela_x[:, gs] + d_ref[:, gs] * xs[:, gs]
        y = y * _silu(za_ref[:, gs])
        y = y * lax.rsqrt(jnp.mean(y * y, axis=-1, keepdims=True) + EPS) * nrm_ref[:, gs]
        mix_ref[:, gs] = y.astype(BF16)
        s_ref[g] = s_prev * dec_x[:, gs] + _mm_tn(bg, xs[:, gs] * wsx[:, gs])

    @pl.when(last)
    def _():
        st_ref[0] = s_ref[...]
        cx_ref[0] = _conv_tail(extx)
        cbc_ref[0] = _conv_tail(extbc)


def _ssd_prompt(zm, zs, zs_t, m_total, bsz, seq, conv_w, conv_b, dt_bias, a_log, d_skip, norm):
    nc = seq // CHUNK
    rb = lambda b, c: b * nc + c
    pc = jnp.zeros((SUBLANES, LANES), F32).at[0, :SSD_HEADS].set(dt_bias).at[1, :SSD_HEADS].set(a_log)
    pr = jnp.stack([jnp.broadcast_to(dt_bias[:, None], (SSD_HEADS, CHUNK)),
                    jnp.broadcast_to(a_log[:, None], (SSD_HEADS, CHUNK))])
    full = lambda *shape: pl.BlockSpec(shape, lambda b, c: (0,) * len(shape))
    return pl.pallas_call(
        _ssd_prompt_kernel,
        grid=(bsz, nc),
        in_specs=[
            pl.BlockSpec((CHUNK, SSD_WIDTH), lambda b, c: (rb(b, c), 0)),
            pl.BlockSpec((CHUNK, SSD_WIDTH), lambda b, c: (rb(b, c), 1)),
            pl.BlockSpec((CHUNK, SSD_BC), lambda b, c: (rb(b, c), 6 * D_MODEL // SSD_BC)),
            pl.BlockSpec((CHUNK, LANES), lambda b, c: (rb(b, c), 0)),
            pl.BlockSpec((SSD_HEADS, CHUNK), lambda b, c: (0, rb(b, c))),
            full(CONV_K, SSD_WIDTH), full(1, SSD_WIDTH), full(CONV_K, SSD_BC), full(1, SSD_BC),
            full(SUBLANES, LANES), full(2, SSD_HEADS, CHUNK), full(1, SSD_WIDTH), full(1, SSD_WIDTH),
        ],
        out_specs=[
            pl.BlockSpec((CHUNK, SSD_WIDTH), lambda b, c: (rb(b, c), 0)),
            pl.BlockSpec((1, CONV_K - 1, SSD_WIDTH), lambda b, c: (b, 0, 0)),
            pl.BlockSpec((1, CONV_K - 1, SSD_BC), lambda b, c: (b, 0, 0)),
            pl.BlockSpec((1, SSD_GROUPS, SSD_STATE, SSD_GW), lambda b, c: (b, 0, 0, 0)),
        ],
        out_shape=[
            jax.ShapeDtypeStruct((m_total, SSD_WIDTH), BF16),
            jax.ShapeDtypeStruct((bsz, CONV_K - 1, SSD_WIDTH), F32),
            jax.ShapeDtypeStruct((bsz, CONV_K - 1, SSD_BC), F32),
            jax.ShapeDtypeStruct((bsz, SSD_GROUPS, SSD_STATE, SSD_GW), F32),
        ],
        scratch_shapes=[pltpu.VMEM((SUBLANES + CHUNK, SSD_WIDTH), F32), pltpu.VMEM((SUBLANES + CHUNK, SSD_BC), F32),
                        pltpu.VMEM((SSD_GROUPS, SSD_STATE, SSD_GW), F32)],
        compiler_params=_params("parallel", "arbitrary"),
        name="ssd_prompt",
    )(zm, zm, zm, zs, zs_t, conv_w[:, :SSD_WIDTH], conv_b[None, :SSD_WIDTH], conv_w[:, SSD_WIDTH:],
      conv_b[None, SSD_WIDTH:], pc, pr, jnp.repeat(d_skip, SSD_HEAD_DIM)[None, :], norm[None, :])


def _tri_inverse(mats, r, c):
    def corner(level):
        return ((r >> (level + 1)) == (c >> (level + 1))) & (((r >> level) & 1) == 1) & (((c >> level) & 1) == 0)

    eye = jnp.where(r == c, 1.0, 0.0)
    ts = [eye - jnp.where(corner(0), a, 0.0) for a in mats]
    for level in range(1, 7):
        cm = corner(level)
        xs = [_mm(t, jnp.where(cm, a, 0.0)) for t, a in zip(ts, mats)]
        ts = [t - _mm(x, t) for t, x in zip(ts, xs)]
    return ts


def _gdn_prompt_kernel(q_ref, k_ref, v_ref, zb_ref, zc_ref, zr_ref, wq_ref, wk_ref, wv_ref, pc_ref, pr_ref, nrm_ref,
                       mix_ref, cq_ref, ck_ref, cv_ref, st_ref, extq, extk, extv, s_ref):
    c_id = pl.program_id(2)
    first = c_id == 0
    last = c_id == pl.num_programs(2) - 1

    @pl.when(first)
    def _():
        s_ref[...] = jnp.zeros(s_ref.shape, F32)

    q_all = _silu(_conv_chunk(extq, q_ref[...], wq_ref, first))
    k_all = _silu(_conv_chunk(extk, k_ref[...], wk_ref, first))
    v_all = _silu(_conv_chunk(extv, v_ref[...], wv_ref, first))

    r, c = _causal_masks()
    causal = r >= c
    strict = r > c
    tril = jnp.where(causal, 1.0, 0.0)
    triu = jnp.where(r <= c, 1.0, 0.0)
    zc = zc_ref[...]
    beta_c = jax.nn.sigmoid(zc)
    gc_c = _mm_hi(tril, -jnp.exp(pc_ref[0:1, :]) * _softplus(zc + pc_ref[1:2, :]))
    gc_r = _mm_hi(-jnp.exp(pr_ref[0]) * _softplus(zr_ref[GDN_HG:2 * GDN_HG, :] + pr_ref[1]), triu)

    heads = range(GDN_HG)
    hs = [slice(j * GDN_DK, (j + 1) * GDN_DK) for j in heads]
    qh = [q_all[:, s] for s in hs]
    kh = [k_all[:, s] for s in hs]
    qh = [x * (lax.rsqrt(jnp.sum(x * x, axis=-1, keepdims=True) + EPS) * (GDN_DK ** -0.5)) for x in qh]
    kh = [x * lax.rsqrt(jnp.sum(x * x, axis=-1, keepdims=True) + EPS) for x in kh]
    gcc = [gc_c[:, GDN_HG + j:GDN_HG + j + 1] for j in heads]
    beta = [beta_c[:, j:j + 1] for j in heads]
    gam = [jnp.exp(jnp.where(causal, gcc[j] - gc_r[j:j + 1, :], NEG)) for j in heads]
    qkk = [_mm_nt(jnp.concatenate([qh[j], kh[j]], axis=0), kh[j]) for j in heads]
    aqk = [qkk[j][:CHUNK] * gam[j] for j in heads]
    tinv = _tri_inverse([jnp.where(strict, beta[j] * qkk[j][CHUNK:] * gam[j], 0.0) for j in heads], r, c)
    egc = [jnp.exp(g) for g in gcc]
    uw = [_mm(tinv[j], jnp.concatenate([beta[j] * v_all[:, hs[j]], (beta[j] * egc[j]) * kh[j]], axis=1))
          for j in heads]
    s_prev = [s_ref[j] for j in heads]
    ws_qs = [_mm(jnp.concatenate([uw[j][:, GDN_DV:], qh[j] * egc[j]], axis=0), s_prev[j]) for j in heads]
    vn = [uw[j][:, :GDN_DV] - ws_qs[j][:CHUNK] for j in heads]
    o = [ws_qs[j][CHUNK:] + _mm(aqk[j], vn[j]) for j in heads]
    for j in heads:
        gc_last = gcc[j][CHUNK - 1:CHUNK, :]
        s_ref[j] = s_prev[j] * jnp.exp(gc_last) + _mm_tn(kh[j] * jnp.exp(gc_last - gcc[j]), vn[j])
    for j in heads:
        on = o[j] * lax.rsqrt(jnp.mean(o[j] * o[j], axis=-1, keepdims=True) + EPS) * nrm_ref[...]
        mix_ref[:, hs[j]] = (on * _silu(zb_ref[:, hs[j]])).astype(BF16)

    @pl.when(last)
    def _():
        st_ref[0] = s_ref[...]
        cq_ref[0] = _conv_tail(extq)
        ck_ref[0] = _conv_tail(extk)
        cv_ref[0] = _conv_tail(extv)


def _gdn_prompt(zm, zs, zs_t, m_total, bsz, seq, conv_w, dt_bias, a_log, norm):
    nc = seq // CHUNK
    nhg = GDN_HEADS // GDN_HG
    w = GDN_HG * GDN_DK
    rb = lambda b, c: b * nc + c
    col0 = 2 * D_MODEL // w
    pc = jnp.zeros((nhg, SUBLANES, LANES), F32)
    pc = pc.at[:, 0, GDN_HG:2 * GDN_HG].set(a_log.reshape(nhg, GDN_HG))
    pc = pc.at[:, 1, GDN_HG:2 * GDN_HG].set(dt_bias.reshape(nhg, GDN_HG))
    pr = jnp.stack([jnp.broadcast_to(a_log.reshape(nhg, GDN_HG, 1), (nhg, GDN_HG, CHUNK)),
                    jnp.broadcast_to(dt_bias.reshape(nhg, GDN_HG, 1), (nhg, GDN_HG, CHUNK))], axis=1)
    seg = lambda s: pl.BlockSpec((CHUNK, w), lambda b, h, c: (rb(b, c), col0 + s * nhg + h))
    wseg = lambda s: pl.BlockSpec((CONV_K, w), lambda b, h, c: (0, s * nhg + h))
    cout = pl.BlockSpec((1, CONV_K - 1, w), lambda b, h, c: (b, 0, h))
    return pl.pallas_call(
        _gdn_prompt_kernel,
        grid=(bsz, nhg, nc),
        in_specs=[
            seg(0), seg(1), seg(2), seg(3),
            pl.BlockSpec((CHUNK, LANES), lambda b, h, c: (rb(b, c), 1 + h)),
            pl.BlockSpec((2 * GDN_HG, CHUNK), lambda b, h, c: ((1 + h) * LANES // (2 * GDN_HG), rb(b, c))),
            wseg(0), wseg(1), wseg(2),
            pl.BlockSpec((None, SUBLANES, LANES), lambda b, h, c: (h, 0, 0)),
            pl.BlockSpec((None, 2, GDN_HG, CHUNK), lambda b, h, c: (h, 0, 0, 0)),
            pl.BlockSpec((1, GDN_DV), lambda b, h, c: (0, 0)),
        ],
        out_specs=[
            pl.BlockSpec((CHUNK, w), lambda b, h, c: (rb(b, c), h)),
            cout, cout, cout,
            pl.BlockSpec((1, GDN_HG, GDN_DK, GDN_DV), lambda b, h, c: (b, h, 0, 0)),
        ],
        out_shape=[
            jax.ShapeDtypeStruct((m_total, GDN_V), BF16),
            jax.ShapeDtypeStruct((bsz, CONV_K - 1, GDN_QK), F32),
            jax.ShapeDtypeStruct((bsz, CONV_K - 1, GDN_QK), F32),
            jax.ShapeDtypeStruct((bsz, CONV_K - 1, GDN_V), F32),
            jax.ShapeDtypeStruct((bsz, GDN_HEADS, GDN_DK, GDN_DV), F32),
        ],
        scratch_shapes=[pltpu.VMEM((SUBLANES + CHUNK, w), F32)] * 3 + [pltpu.VMEM((GDN_HG, GDN_DK, GDN_DV), F32)],
        compiler_params=_params("parallel", "parallel", "arbitrary"),
        name="gdn_prompt",
    )(zm, zm, zm, zm, zs, zs_t, conv_w, conv_w, conv_w, pc, pr, norm[None, :])


def _mlstm_prompt_kernel(q_ref, k_ref, v_ref, o_ref, zc_ref, gc_ref, gr_ref, pc_ref, pr_ref, nrm_ref,
                         mix_ref, c_out, n_out, m_out, c_ref, n_ref, m_ref):
    c_id = pl.program_id(1)
    first = c_id == 0
    last = c_id == pl.num_programs(1) - 1

    @pl.when(first)
    def _():
        c_ref[...] = jnp.zeros(c_ref.shape, F32)
        n_ref[...] = jnp.zeros(n_ref.shape, F32)
        m_ref[...] = jnp.zeros(m_ref.shape, F32)

    r, c = _causal_masks()
    causal = r >= c
    tril = jnp.where(causal, 1.0, 0.0)
    triu = jnp.where(r <= c, 1.0, 0.0)
    gc = gc_ref[...]
    logi_c = gc + pc_ref[0:1, :]
    b_c = _mm_hi(tril, -_softplus(-(gc + pc_ref[1:2, :])))
    logi_r = gr_ref[0:ML_HEADS, :] + pr_ref[0]
    b_r = _mm_hi(-_softplus(-(gr_ref[ML_HEADS:2 * ML_HEADS, :] + pr_ref[1])), triu)

    heads = range(ML_HEADS)
    ks = [slice(j * ML_DK, (j + 1) * ML_DK) for j in heads]
    vs = [slice(j * ML_DV, (j + 1) * ML_DV) for j in heads]
    q = [q_ref[:, s] for s in ks]
    k = [k_ref[:, s] * (ML_DK ** -0.5) for s in ks]
    bc = [b_c[:, ML_HEADS + j:ML_HEADS + j + 1] for j in heads]
    dmat = [jnp.where(causal, bc[j] - b_r[j:j + 1, :] + logi_r[j:j + 1, :], NEG) for j in heads]
    m_intra = [jnp.max(x, axis=-1, keepdims=True) for x in dmat]
    p = [_mm_nt(q[j], k[j]) * jnp.exp(dmat[j] - m_intra[j]) for j in heads]
    h_intra = [_mm(p[j], v_ref[:, vs[j]]) for j in heads]
    c_prev = [c_ref[j] for j in heads]
    qc = [_mm(q[j], c_prev[j]) for j in heads]
    b_last = [x[CHUNK - 1:CHUNK, :] for x in bc]
    gk = [b_last[j] - bc[j] + logi_c[:, j:j + 1] for j in heads]
    m_k = [jnp.max(x, axis=0, keepdims=True) for x in gk]
    kw = [k[j] * jnp.exp(gk[j] - m_k[j]) for j in heads]
    c_loc = [_mm_tn(kw[j], v_ref[:, vs[j]]) for j in heads]
    for j in heads:
        n_intra = jnp.sum(p[j], axis=-1, keepdims=True)
        m_prev = m_ref[j:j + 1, 0:1]
        n_prev = n_ref[j:j + 1, :]
        mb = bc[j] + m_prev
        m_t = jnp.maximum(mb, m_intra[j])
        s_inter = jnp.exp(mb - m_t)
        s_intra = jnp.exp(m_intra[j] - m_t)
        num = s_inter * qc[j] + s_intra * h_intra[j]
        den = s_inter * jnp.sum(q[j] * n_prev, axis=-1, keepdims=True) + s_intra * n_intra
        h = num / jnp.maximum(jnp.abs(den), jnp.exp(-m_t))
        m_new = jnp.maximum(b_last[j] + m_prev, m_k[j])
        sa = jnp.exp(b_last[j] + m_prev - m_new)
        sb = jnp.exp(m_k[j] - m_new)
        c_ref[j] = c_prev[j] * sa + c_loc[j] * sb
        n_ref[j:j + 1, :] = n_prev * sa + jnp.sum(kw[j], axis=0, keepdims=True) * sb
        m_ref[j:j + 1, :] = jnp.broadcast_to(m_new, (1, LANES))
        h = h * lax.rsqrt(jnp.mean(h * h, axis=-1, keepdims=True) + EPS) * nrm_ref[...]
        mix_ref[:, vs[j]] = (h * jax.nn.sigmoid(o_ref[:, vs[j]]) * _silu(zc_ref[:, vs[j]])).astype(BF16)

    @pl.when(last)
    def _():
        c_out[0] = c_ref[...]
        n_out[0] = n_ref[...]
        m_out[0] = m_ref[...]


def _mlstm_prompt(zm, zs, zs_t, m_total, bsz, seq, i_bias, f_bias, norm):
    nc = seq // CHUNK
    rb = lambda b, c: b * nc + c
    pc = jnp.zeros((SUBLANES, LANES), F32).at[0, :ML_HEADS].set(i_bias).at[1, ML_HEADS:2 * ML_HEADS].set(f_bias)
    pr = jnp.stack([jnp.broadcast_to(i_bias[:, None], (ML_HEADS, CHUNK)),
                    jnp.broadcast_to(f_bias[:, None], (ML_HEADS, CHUNK))])
    full = lambda *shape: pl.BlockSpec(shape, lambda b, c: (0,) * len(shape))
    return pl.pallas_call(
        _mlstm_prompt_kernel,
        grid=(bsz, nc),
        in_specs=[
            pl.BlockSpec((CHUNK, ML_QK), lambda b, c: (rb(b, c), 0)),
            pl.BlockSpec((CHUNK, ML_QK), lambda b, c: (rb(b, c), 1)),
            pl.BlockSpec((CHUNK, ML_V), lambda b, c: (rb(b, c), 1)),
            pl.BlockSpec((CHUNK, ML_V), lambda b, c: (rb(b, c), 2)),
            pl.BlockSpec((CHUNK, ML_V), lambda b, c: (rb(b, c), 3)),
            pl.BlockSpec((CHUNK, LANES), lambda b, c: (rb(b, c), 0)),
            pl.BlockSpec((2 * ML_HEADS, CHUNK), lambda b, c: (0, rb(b, c))),
            full(SUBLANES, LANES), full(2, ML_HEADS, CHUNK), full(1, ML_DV),
        ],
        out_specs=[
            pl.BlockSpec((CHUNK, ML_V), lambda b, c: (rb(b, c), 0)),
            pl.BlockSpec((1, ML_HEADS, ML_DK, ML_DV), lambda b, c: (b, 0, 0, 0)),
            pl.BlockSpec((1, ML_HEADS, ML_DK), lambda b, c: (b, 0, 0)),
            pl.BlockSpec((1, ML_HEADS, LANES), lambda b, c: (b, 0, 0)),
        ],
        out_shape=[
            jax.ShapeDtypeStruct((m_total, ML_V), BF16),
            jax.ShapeDtypeStruct((bsz, ML_HEADS, ML_DK, ML_DV), F32),
            jax.ShapeDtypeStruct((bsz, ML_HEADS, ML_DK), F32),
            jax.ShapeDtypeStruct((bsz, ML_HEADS, LANES), F32),
        ],
        scratch_shapes=[pltpu.VMEM((ML_HEADS, ML_DK, ML_DV), F32), pltpu.VMEM((ML_HEADS, ML_DK), F32),
                        pltpu.VMEM((ML_HEADS, LANES), F32)],
        compiler_params=_params("parallel", "arbitrary"),
        name="mlstm_prompt",
    )(zm, zm, zm, zm, zm, zs, zs_t, pc, pr, norm[None, :])


def _cmlp_prompt_kernel(u_ref, v_ref, z_ref, ws_ref, wb_ref, gain_ref, mix_ref, vrows_ref):
    r, c = _causal_masks()
    causal = r >= c
    for g in range(CM_GROUPS):
        gs = slice(g * CM_GROUP_DIM, (g + 1) * CM_GROUP_DIM)
        v = _gelu(v_ref[:, gs])
        v = v * lax.rsqrt(jnp.mean(v * v, axis=-1, keepdims=True) + EPS) * gain_ref[...]
        s = _mm(jnp.where(causal, ws_ref[g], 0.0), v) + wb_ref[:, g:g + 1]
        mix_ref[:, gs] = (_gelu(u_ref[:, gs]) * s * _silu(z_ref[:, gs])).astype(BF16)
        vrows_ref[0, :, gs] = v


def _cmlp_prompt(zm, m_total, bsz, seq, v_gain, ws, wb):
    nc = seq // CM_CHUNK
    rb = lambda b, c: b * nc + c
    col0 = (2 * ML_QK + 3 * ML_V) // CM_WIDTH
    return pl.pallas_call(
        _cmlp_prompt_kernel,
        grid=(bsz, nc),
        in_specs=[
            pl.BlockSpec((CM_CHUNK, CM_WIDTH), lambda b, c: (rb(b, c), col0)),
            pl.BlockSpec((CM_CHUNK, CM_WIDTH), lambda b, c: (rb(b, c), col0 + 1)),
            pl.BlockSpec((CM_CHUNK, CM_WIDTH), lambda b, c: (rb(b, c), col0 + 2)),
            pl.BlockSpec((CM_GROUPS, CM_CHUNK, CM_CHUNK), lambda b, c: (0, 0, 0)),
            pl.BlockSpec((CM_CHUNK, CM_GROUPS), lambda b, c: (0, 0)),
            pl.BlockSpec((1, CM_GROUP_DIM), lambda b, c: (0, 0)),
        ],
        out_specs=[
            pl.BlockSpec((CM_CHUNK, CM_WIDTH), lambda b, c: (rb(b, c), 0)),
            pl.BlockSpec((1, CM_CHUNK, CM_WIDTH), lambda b, c: (b, 0, 0)),
        ],
        out_shape=[jax.ShapeDtypeStruct((m_total, CM_WIDTH), BF16),
                   jax.ShapeDtypeStruct((bsz, CM_CHUNK, CM_WIDTH), F32)],
        compiler_params=_params("parallel", "arbitrary"),
        name="cmlp_prompt",
    )(zm, zm, zm, ws, wb.T, v_gain[None, :])


SEQ_BLOCK = SUBLANES


def _conv_step(raw_ref, cin_ref, w_ref, cout_ref):
    u = raw_ref[...]
    out = w_ref[CONV_K - 1:CONV_K, :] * u
    for k in range(CONV_K - 1):
        out = out + w_ref[k:k + 1, :] * cin_ref[k]
    for k in range(CONV_K - 2):
        cout_ref[k] = cin_ref[k + 1]
    cout_ref[CONV_K - 2] = u
    return out


def _row0(row, fill=0.0):
    return jnp.where(_iota((SUBLANES, row.shape[1]), 0) == 0, row, fill)


def _stash_rows(dst_ref, val):
    for i in range(SEQ_BLOCK):
        dst_ref[i] = jnp.broadcast_to(val[i:i + 1, :], (SUBLANES, val.shape[1]))


def _gather_rows(src_ref):
    rid = _iota(src_ref.shape[1:], 0)
    acc = src_ref[0]
    for i in range(1, SEQ_BLOCK):
        acc = jnp.where(rid == i, src_ref[i], acc)
    return acc


def _split3(x):
    hi = x.astype(BF16).astype(F32)
    r1 = x - hi
    mid = r1.astype(BF16).astype(F32)
    lo = (r1 - mid).astype(BF16).astype(F32)
    return hi, mid, lo


def _ssd_decode_kernel(za_ref, xs_ref, b_ref, c_ref, zc_ref, cx_ref, cb_ref, cc_ref, wx_ref, bx_ref, wb_ref, bb_ref,
                       wc_ref, bc_ref, pc_ref, e_ref, a_ref, d_ref, nrm_ref, s_ref, *rest):
    mix_ref, ncx_ref, ncb_ref, ncc_ref, so_ref = rest[-5:]
    xs = _silu(_conv_step(xs_ref, cx_ref, wx_ref, ncx_ref) + bx_ref[...])
    bm = _silu(_conv_step(b_ref, cb_ref, wb_ref, ncb_ref) + bb_ref[...])
    cm = _silu(_conv_step(c_ref, cc_ref, wc_ref, ncc_ref) + bc_ref[...])
    dtx = _mm_hi(_softplus(zc_ref[...] + pc_ref[0:1, :]), e_ref[...])
    hi, mid, lo = _split3(jnp.exp(dtx * a_ref[...]))
    dx = dtx * xs
    rid = _iota((SUBLANES, SSD_GW), 0)
    ones_rows = jnp.where((_iota((SUBLANES, SSD_STATE), 0) >= 1) & (_iota((SUBLANES, SSD_STATE), 0) <= 3), 1.0, 0.0)
    hpg = SSD_HEADS // SSD_GROUPS
    seqs = range(SEQ_BLOCK)
    lmat = [jnp.where(rid == 0, dx[i:i + 1, :], jnp.where(rid == 1, hi[i:i + 1, :],
                      jnp.where(rid == 2, mid[i:i + 1, :], jnp.where(rid == 3, lo[i:i + 1, :], 0.0)))) for i in seqs]
    upd = [_mm_tn(lmat[i], jnp.concatenate([_row0(bm[i:i + 1, :]), ones_rows], axis=1)) for i in seqs]
    s_new = [s_ref[i].reshape(SSD_GW, SSD_STATE) * upd[i][:, SSD_STATE:] + upd[i][:, :SSD_STATE] for i in seqs]
    for i in seqs:
        so_ref[i] = s_new[i].reshape(hpg, SSD_HEAD_DIM, SSD_STATE)
    ys = [_mm_nt(jnp.broadcast_to(cm[i:i + 1, :], (SUBLANES, SSD_STATE)), s_new[i]) for i in seqs]
    y = ys[0]
    for i in seqs[1:]:
        y = jnp.where(rid == i, ys[i], y)
    y = y + d_ref[...] * xs
    y = y * _silu(za_ref[...])
    y = y * lax.rsqrt(jnp.mean(y * y, axis=-1, keepdims=True) + EPS) * nrm_ref[...]
    mix_ref[...] = y.astype(BF16)


def _stacked_state_io(states, layer, prev, block, index):
    spec = pl.BlockSpec((None,) + block, lambda *g: (layer,) + index(*g))
    extra_in = [] if prev is None else [prev]
    extra_specs = [] if prev is None else [pl.BlockSpec(memory_space=pl.ANY)]
    return spec, jax.ShapeDtypeStruct(states.shape, F32), extra_in, extra_specs


def _ssd_decode(zm, zs, mix, conv_t, states, layer, prev, mp, conv_w, conv_b, dt_bias, a_log, d_skip, norm):
    bsz = states.shape[1]
    hpg = SSD_HEADS // SSD_GROUPS
    st_spec, st_shape, extra_in, extra_specs = _stacked_state_io(
        states, layer, prev, (SEQ_BLOCK, hpg, SSD_HEAD_DIM, SSD_STATE), lambda s, g: (s, g, 0, 0))
    r0 = mp // SEQ_BLOCK
    pc = jnp.zeros((SUBLANES, LANES), F32).at[0, :SSD_HEADS].set(dt_bias)
    e_mat = (jnp.arange(SSD_WIDTH)[None, :] // SSD_HEAD_DIM == jnp.arange(LANES)[:, None]).astype(F32)
    a_x = jnp.repeat(-jnp.exp(a_log), SSD_HEAD_DIM)[None, :]
    nb = SSD_WIDTH // SSD_STATE
    zrow = lambda w, col: pl.BlockSpec((SEQ_BLOCK, w), lambda s, g: (r0 + s, col(g)))
    cst = lambda w, col: pl.BlockSpec((CONV_K - 1, SEQ_BLOCK, w), lambda s, g: (0, s, col(g)))
    par = lambda rows, w, col: pl.BlockSpec((rows, w), lambda s, g: (0, col(g)))
    mainb = 6 * D_MODEL // SSD_STATE
    return pl.pallas_call(
        _ssd_decode_kernel,
        grid=(bsz // SEQ_BLOCK, SSD_GROUPS),
        in_specs=[
            zrow(SSD_GW, lambda g: g), zrow(SSD_GW, lambda g: SSD_GROUPS + g),
            zrow(SSD_STATE, lambda g: mainb + g), zrow(SSD_STATE, lambda g: mainb + SSD_GROUPS + g),
            zrow(LANES, lambda g: 0),
            cst(SSD_GW, lambda g: g), cst(SSD_STATE, lambda g: nb + g), cst(SSD_STATE, lambda g: nb + SSD_GROUPS + g),
            par(CONV_K, SSD_GW, lambda g: g), par(1, SSD_GW, lambda g: g),
            par(CONV_K, SSD_STATE, lambda g: nb + g), par(1, SSD_STATE, lambda g: nb + g),
            par(CONV_K, SSD_STATE, lambda g: nb + SSD_GROUPS + g), par(1, SSD_STATE, lambda g: nb + SSD_GROUPS + g),
            par(SUBLANES, LANES, lambda g: 0), par(LANES, SSD_GW, lambda g: g),
            par(1, SSD_GW, lambda g: g), par(1, SSD_GW, lambda g: g), par(1, SSD_GW, lambda g: g),
            st_spec,
            pl.BlockSpec(memory_space=pl.ANY),
        ] + extra_specs,
        out_specs=[
            pl.BlockSpec((SEQ_BLOCK, SSD_GW), lambda s, g: (r0 + s, g)),
            cst(SSD_GW, lambda g: g), cst(SSD_STATE, lambda g: g), cst(SSD_STATE, lambda g: g),
            st_spec,
        ],
        out_shape=[
            jax.ShapeDtypeStruct(mix.shape, mix.dtype),
            jax.ShapeDtypeStruct((CONV_K - 1, bsz, SSD_WIDTH), F32),
            jax.ShapeDtypeStruct((CONV_K - 1, bsz, SSD_GROUPS * SSD_STATE), F32),
            jax.ShapeDtypeStruct((CONV_K - 1, bsz, SSD_GROUPS * SSD_STATE), F32),
            st_shape,
        ],
        input_output_aliases={20: 0} if prev is None else {20: 0, 21: 4},
        compiler_params=_params("parallel", "arbitrary"),
        name="ssd_decode",
    )(zm, zm, zm, zm, zs, conv_t, conv_t, conv_t, conv_w, conv_b[None, :], conv_w, conv_b[None, :], conv_w,
      conv_b[None, :], pc, e_mat, a_x, jnp.repeat(d_skip, SSD_HEAD_DIM)[None, :], norm[None, :], states, mix,
      *extra_in)


def _gdn_decode_kernel(q_ref, k_ref, v_ref, zb_ref, zc_ref, cq_ref, ck_ref, cv_ref, wq_ref, wk_ref, wv_ref, pc_ref,
                       nrm_ref, s_ref, *rest):
    mix_ref, ncq_ref, nck_ref, ncv_ref, so_ref, q_s, k_s, v_s, beta_s, eg_s, qk_s, o_s = rest[-12:]
    q_all = _silu(_conv_step(q_ref, cq_ref, wq_ref, ncq_ref))
    k_all = _silu(_conv_step(k_ref, ck_ref, wk_ref, nck_ref))
    _stash_rows(v_s, _silu(_conv_step(v_ref, cv_ref, wv_ref, ncv_ref)))
    zc = zc_ref[...]
    beta = jax.nn.sigmoid(zc)
    eg = jnp.exp(-jnp.exp(pc_ref[0:1, :]) * _softplus(zc + pc_ref[1:2, :]))
    heads = range(GDN_HG)
    hs = [slice(j * GDN_DK, (j + 1) * GDN_DK) for j in heads]
    qn, kn, beta_x, eg_x, qk_x = [], [], [], [], []
    shape = (SEQ_BLOCK, GDN_DK)
    for j in heads:
        qh, kh = q_all[:, hs[j]], k_all[:, hs[j]]
        qh = qh * (lax.rsqrt(jnp.sum(qh * qh, axis=-1, keepdims=True) + EPS) * (GDN_DK ** -0.5))
        kh = kh * lax.rsqrt(jnp.sum(kh * kh, axis=-1, keepdims=True) + EPS)
        qn.append(qh)
        kn.append(kh)
        beta_x.append(jnp.broadcast_to(beta[:, j:j + 1], shape))
        eg_x.append(jnp.broadcast_to(eg[:, GDN_HG + j:GDN_HG + j + 1], shape))
        qk_x.append(jnp.broadcast_to(jnp.sum(qh * kh, axis=-1, keepdims=True), shape))
    for ref, parts in ((q_s, qn), (k_s, kn), (beta_s, beta_x), (eg_s, eg_x), (qk_s, qk_x)):
        _stash_rows(ref, jnp.concatenate(parts, axis=1))

    rid = _iota((SUBLANES, GDN_DK), 0)

    def body(i, carry):
        k_b, q_b, v_b, beta_b, eg_b, qk_b = k_s[i], q_s[i], v_s[i], beta_s[i], eg_s[i], qk_s[i]
        s_prev = [s_ref[i, j] for j in heads]
        ks_qs = [_mm(jnp.where(rid == 0, k_b[:, hs[j]], q_b[:, hs[j]]), s_prev[j]) for j in heads]
        vn = [beta_b[0:1, hs[j]] * (v_b[0:1, hs[j]] - eg_b[0:1, hs[j]] * ks_qs[j][0:1, :]) for j in heads]
        outer = [_mm_tn(_row0(k_b[:, hs[j]]), _row0(vn[j])) for j in heads]
        for j in heads:
            so_ref[i, j] = s_prev[j] * eg_b[0:1, j * GDN_DK:j * GDN_DK + 1] + outer[j]
            o_row = eg_b[0:1, hs[j]] * ks_qs[j][1:2, :] + qk_b[0:1, hs[j]] * vn[j]
            o_s[i, :, hs[j]] = jnp.broadcast_to(o_row, (SUBLANES, GDN_DV))
        return carry

    lax.fori_loop(0, SEQ_BLOCK, body, 0)
    o_all = _gather_rows(o_s)
    for j in heads:
        o = o_all[:, hs[j]]
        o = o * lax.rsqrt(jnp.mean(o * o, axis=-1, keepdims=True) + EPS) * nrm_ref[...]
        mix_ref[:, hs[j]] = (o * _silu(zb_ref[:, hs[j]])).astype(BF16)


def _gdn_decode(zm, zs, mix, conv_t, states, layer, prev, mp, conv_w, dt_bias, a_log, norm):
    bsz = states.shape[1]
    st_spec, st_shape, extra_in, extra_specs = _stacked_state_io(
        states, layer, prev, (SEQ_BLOCK, GDN_HG, GDN_DK, GDN_DV), lambda s, h: (s, h, 0, 0))
    nhg = GDN_HEADS // GDN_HG
    w = GDN_HG * GDN_DK
    r0 = mp // SEQ_BLOCK
    col0 = 2 * D_MODEL // w
    pc = jnp.zeros((nhg, SUBLANES, LANES), F32)
    pc = pc.at[:, 0, GDN_HG:2 * GDN_HG].set(a_log.reshape(nhg, GDN_HG))
    pc = pc.at[:, 1, GDN_HG:2 * GDN_HG].set(dt_bias.reshape(nhg, GDN_HG))
    seg = lambda p: pl.BlockSpec((SEQ_BLOCK, w), lambda s, h: (r0 + s, col0 + p * nhg + h))
    cst = lambda p: pl.BlockSpec((CONV_K - 1, SEQ_BLOCK, w), lambda s, h: (0, s, p * nhg + h))
    wseg = lambda p: pl.BlockSpec((CONV_K, w), lambda s, h: (0, p * nhg + h))
    cout = pl.BlockSpec((CONV_K - 1, SEQ_BLOCK, w), lambda s, h: (0, s, h))
    row_scratch = pltpu.VMEM((SEQ_BLOCK, SUBLANES, w), F32)
    return pl.pallas_call(
        _gdn_decode_kernel,
        grid=(bsz // SEQ_BLOCK, nhg),
        in_specs=[
            seg(0), seg(1), seg(2), seg(3),
            pl.BlockSpec((SEQ_BLOCK, LANES), lambda s, h: (r0 + s, 1 + h)),
            cst(0), cst(1), cst(2), wseg(0), wseg(1), wseg(2),
            pl.BlockSpec((None, SUBLANES, LANES), lambda s, h: (h, 0, 0)),
            pl.BlockSpec((1, GDN_DV), lambda s, h: (0, 0)),
            st_spec,
            pl.BlockSpec(memory_space=pl.ANY),
        ] + extra_specs,
        out_specs=[
            pl.BlockSpec((SEQ_BLOCK, w), lambda s, h: (r0 + s, h)),
            cout, cout, cout,
            st_spec,
        ],
        out_shape=[
            jax.ShapeDtypeStruct(mix.shape, mix.dtype),
            jax.ShapeDtypeStruct((CONV_K - 1, bsz, GDN_QK), F32),
            jax.ShapeDtypeStruct((CONV_K - 1, bsz, GDN_QK), F32),
            jax.ShapeDtypeStruct((CONV_K - 1, bsz, GDN_V), F32),
            st_shape,
        ],
        scratch_shapes=[row_scratch] * 7,
        input_output_aliases={14: 0} if prev is None else {14: 0, 15: 4},
        compiler_params=_params("parallel", "arbitrary"),
        name="gdn_decode",
    )(zm, zm, zm, zm, zs, conv_t, conv_t, conv_t, conv_w, conv_w, conv_w, pc, norm[None, :], states, mix, *extra_in)


ML_HG = 4


def _mlstm_decode_kernel(q_ref, k_ref, v_ref, o_ref, zc_ref, g_ref, m_ref, n_ref, pc_ref, e_ref, nrm_ref, c_ref,
                         *rest):
    mix_ref, n_out, m_out, c_out, q_s, kb_s, v_s, sa_s, sbqk_s, den_s, h_s = rest[-11:]
    g = g_ref[...]
    logi = g + pc_ref[0:1, :]
    logf = -_softplus(-pltpu.roll(g + pc_ref[1:2, :], LANES - ML_HEADS, 1))
    m_prev = m_ref[...]
    m_new = jnp.maximum(logf + m_prev, logi)
    m_out[...] = m_new
    e_mat = e_ref[...]
    sa_x = _mm_hi(jnp.exp(logf + m_prev - m_new), e_mat)
    sb_x = _mm_hi(jnp.exp(logi - m_new), e_mat)
    em_x = _mm_hi(jnp.exp(-m_new), e_mat)
    q = q_ref[...]
    k = k_ref[...] * (ML_DK ** -0.5)
    n_prev = n_ref[...]
    n_out[...] = n_prev * sa_x + sb_x * k
    _stash_rows(q_s, q)
    _stash_rows(kb_s, sb_x * k)
    _stash_rows(v_s, v_ref[...])
    _stash_rows(sa_s, sa_x)
    heads = range(ML_HG)
    ks = [slice(j * ML_DK, (j + 1) * ML_DK) for j in heads]
    vs = [slice(j * ML_DV, (j + 1) * ML_DV) for j in heads]
    sbqk_x, den_x = [], []
    for j in heads:
        shape = (SEQ_BLOCK, ML_DK)
        qk = jnp.sum(q[:, ks[j]] * k[:, ks[j]], axis=-1, keepdims=True)
        qn = jnp.sum(q[:, ks[j]] * n_prev[:, ks[j]], axis=-1, keepdims=True)
        sbqk = sb_x[:, ks[j]] * jnp.broadcast_to(qk, shape)
        sbqk_x.append(sbqk)
        den_x.append(jnp.maximum(jnp.abs(sa_x[:, ks[j]] * jnp.broadcast_to(qn, shape) + sbqk), em_x[:, ks[j]]))
    _stash_rows(sbqk_s, jnp.concatenate(sbqk_x, axis=1))
    _stash_rows(den_s, jnp.concatenate(den_x, axis=1))

    def body(i, carry):
        q_b, kb_b, v_b, sa_b, sbqk_b, den_b = q_s[i], kb_s[i], v_s[i], sa_s[i], sbqk_s[i], den_s[i]
        c_prev = [c_ref[i, j] for j in heads]
        qc = [_mm(q_b[:, ks[j]], c_prev[j]) for j in heads]
        outer = [_mm_tn(_row0(kb_b[:, ks[j]]), _row0(v_b[:, vs[j]])) for j in heads]
        for j in heads:
            lane0 = slice(j * ML_DK, j * ML_DK + 1)
            c_out[i, j] = c_prev[j] * sa_b[0:1, lane0] + outer[j]
            num = sa_b[:, lane0] * qc[j] + sbqk_b[:, lane0] * v_b[:, vs[j]]
            h_s[i, :, vs[j]] = num / den_b[:, lane0]
        return carry

    lax.fori_loop(0, SEQ_BLOCK, body, 0)
    h_all = _gather_rows(h_s)
    for j in heads:
        h = h_all[:, vs[j]]
        h = h * lax.rsqrt(jnp.mean(h * h, axis=-1, keepdims=True) + EPS) * nrm_ref[...]
        mix_ref[:, vs[j]] = (h * jax.nn.sigmoid(o_ref[:, vs[j]]) * _silu(zc_ref[:, vs[j]])).astype(BF16)


def _mlstm_decode(zm, zs, mix, c_states, layer, prev, n0, m0, mp, i_bias, f_bias, norm):
    bsz = c_states.shape[1]
    st_spec, st_shape, extra_in, extra_specs = _stacked_state_io(
        c_states, layer, prev, (SEQ_BLOCK, ML_HG, ML_DK, ML_DV), lambda s, h: (s, h, 0, 0))
    nhg = ML_HEADS // ML_HG
    wk, wv = ML_HG * ML_DK, ML_HG * ML_DV
    r0 = mp // SEQ_BLOCK
    pc = jnp.zeros((SUBLANES, LANES), F32).at[0, :ML_HEADS].set(i_bias).at[1, ML_HEADS:2 * ML_HEADS].set(f_bias)
    e_mat = (jnp.arange(ML_QK)[None, :] // ML_DK == jnp.arange(LANES)[:, None]).astype(F32)
    zrow = lambda w, col: pl.BlockSpec((SEQ_BLOCK, w), lambda s, h: (r0 + s, col(h)))
    vcol = 2 * ML_QK // wv
    return pl.pallas_call(
        _mlstm_decode_kernel,
        grid=(bsz // SEQ_BLOCK, nhg),
        in_specs=[
            zrow(wk, lambda h: h), zrow(wk, lambda h: nhg + h),
            zrow(wv, lambda h: vcol + h), zrow(wv, lambda h: vcol + nhg + h), zrow(wv, lambda h: vcol + 2 * nhg + h),
            zrow(LANES, lambda h: 0),
            pl.BlockSpec((SEQ_BLOCK, LANES), lambda s, h: (s, 0)),
            pl.BlockSpec((SEQ_BLOCK, wk), lambda s, h: (s, h)),
            pl.BlockSpec((SUBLANES, LANES), lambda s, h: (0, 0)),
            pl.BlockSpec((LANES, wk), lambda s, h: (0, h)),
            pl.BlockSpec((1, ML_DV), lambda s, h: (0, 0)),
            st_spec,
            pl.BlockSpec(memory_space=pl.ANY),
        ] + extra_specs,
        out_specs=[
            pl.BlockSpec((SEQ_BLOCK, wv), lambda s, h: (r0 + s, h)),
            pl.BlockSpec((SEQ_BLOCK, wk), lambda s, h: (s, h)),
            pl.BlockSpec((SEQ_BLOCK, LANES), lambda s, h: (s, 0)),
            st_spec,
        ],
        out_shape=[
            jax.ShapeDtypeStruct(mix.shape, mix.dtype),
            jax.ShapeDtypeStruct((bsz, ML_QK), F32),
            jax.ShapeDtypeStruct((bsz, LANES), F32),
            st_shape,
        ],
        scratch_shapes=[pltpu.VMEM((SEQ_BLOCK, SUBLANES, w), F32) for w in (wk, wk, wv, wk, wk, wk, wv)],
        input_output_aliases={12: 0} if prev is None else {12: 0, 13: 3},
        compiler_params=_params("parallel", "arbitrary"),
        name="mlstm_decode",
    )(zm, zm, zm, zm, zm, zs, jnp.pad(m0, ((0, 0), (0, LANES - ML_HEADS))), n0.reshape(bsz, ML_QK), pc, e_mat,
      norm[None, :], c_states, mix, *extra_in)


def _cmlp_decode_kernel(u_ref, v_ref, z_ref, ws_ref, wb_ref, gain_ref, mixin_ref, mix_ref, vrows_ref):
    del mixin_ref
    for g in range(CM_GROUPS):
        gs = slice(g * CM_GROUP_DIM, (g + 1) * CM_GROUP_DIM)
        v = _gelu(v_ref[:, gs])
        v = v * lax.rsqrt(jnp.mean(v * v, axis=-1, keepdims=True) + EPS) * gain_ref[...]
        s = ws_ref[:, gs] * v + wb_ref[:, gs]
        mix_ref[:, gs] = (_gelu(u_ref[:, gs]) * s * _silu(z_ref[:, gs])).astype(BF16)
        vrows_ref[:, gs] = v


def _cmlp_decode(zm, mix, mp, bsz, v_gain, ws, wb):
    col0 = (2 * ML_QK + 3 * ML_V) // CM_WIDTH
    r0 = mp // bsz
    zrow = lambda col: pl.BlockSpec((bsz, CM_WIDTH), lambda i: (r0, col))
    par = pl.BlockSpec((1, CM_WIDTH), lambda i: (0, 0))
    return pl.pallas_call(
        _cmlp_decode_kernel,
        grid=(1,),
        in_specs=[zrow(col0), zrow(col0 + 1), zrow(col0 + 2), par, par,
                  pl.BlockSpec((1, CM_GROUP_DIM), lambda i: (0, 0)), pl.BlockSpec(memory_space=pl.ANY)],
        out_specs=[pl.BlockSpec((bsz, CM_WIDTH), lambda i: (r0, 0)), pl.BlockSpec((bsz, CM_WIDTH), lambda i: (0, 0))],
        out_shape=[jax.ShapeDtypeStruct(mix.shape, mix.dtype), jax.ShapeDtypeStruct((bsz, CM_WIDTH), F32)],
        input_output_aliases={6: 0},
        compiler_params=_params("arbitrary"),
        name="cmlp_decode",
    )(zm, zm, zm, jnp.repeat(ws[:, 0, 0], CM_GROUP_DIM)[None, :], jnp.repeat(wb[:, 0], CM_GROUP_DIM)[None, :],
      v_gain[None, :], mix)


def kernel(x_prompt, x_sample, state_ssd_conv, state_ssd, state_gdn_conv, state_gdn, state_mlstm_c,
           state_mlstm_n, state_mlstm_m, even_norm, even_w_in, ssd_conv_w, ssd_conv_b, ssd_dt_bias, ssd_a_log,
           ssd_d, ssd_norm, gdn_conv_w, gdn_dt_bias, gdn_a_log, gdn_norm, even_w_out, odd_norm, odd_w_in,
           mlstm_i_bias, mlstm_f_bias, mlstm_norm, cmlp_v_norm, cmlp_ws, cmlp_b, odd_w_out, final_norm):
    bp, seq, d = x_prompt.shape
    bs = x_sample.shape[0]
    mp = bp * seq
    mt = mp + bs
    x = jnp.concatenate([x_prompt.reshape(mp, d), x_sample.reshape(bs, d)], axis=0)

    keys = ("sc", "ss", "gc", "gs", "mc", "mn", "mm", "cv")
    outs_p = {k: [] for k in keys}
    outs_s = {k: [] for k in keys}
    ss_all = gs_all = mc_all = None
    for layer in range(DEPTH):
        i = layer // 2
        if layer % 2 == 0:
            w_main, w_small = _prep_even_w_in(even_w_in, i)
            zm, zs = _inproj(x, even_norm[i], w_main, w_small, tn=1024)
            zs_t = zs[:mp].T
            mix_a, cx, cbc, st = _ssd_prompt(zm, zs, zs_t, mt, bp, seq, ssd_conv_w[i], ssd_conv_b[i],
                                             ssd_dt_bias[i], ssd_a_log[i], ssd_d[i], ssd_norm[i])
            mix_b, cq, ck, cv, gst = _gdn_prompt(zm, zs, zs_t, mt, bp, seq, gdn_conv_w[i], gdn_dt_bias[i],
                                                 gdn_a_log[i], gdn_norm[i])
            hpg = SSD_HEADS // SSD_GROUPS
            outs_p["sc"].append(jnp.concatenate([cx, cbc], axis=-1))
            outs_p["ss"].append(st.reshape(bp, SSD_GROUPS, SSD_STATE, hpg, SSD_HEAD_DIM).transpose(0, 1, 3, 4, 2)
                                .reshape(bp, SSD_HEADS, SSD_HEAD_DIM, SSD_STATE))
            outs_p["gc"].append(jnp.concatenate([cq, ck, cv], axis=-1))
            outs_p["gs"].append(gst)
            mix_a, ncx, ncb, ncc, ss_all = _ssd_decode(
                zm, zs, mix_a, jnp.swapaxes(state_ssd_conv[i], 0, 1), state_ssd, i, ss_all, mp, ssd_conv_w[i],
                ssd_conv_b[i], ssd_dt_bias[i], ssd_a_log[i], ssd_d[i], ssd_norm[i])
            mix_b, ncq, nck, ncv, gs_all = _gdn_decode(
                zm, zs, mix_b, jnp.swapaxes(state_gdn_conv[i], 0, 1), state_gdn, i, gs_all, mp, gdn_conv_w[i],
                gdn_dt_bias[i], gdn_a_log[i], gdn_norm[i])
            outs_s["sc"].append(jnp.swapaxes(jnp.concatenate([ncx, ncb, ncc], axis=-1), 0, 1))
            outs_s["gc"].append(jnp.swapaxes(jnp.concatenate([ncq, nck, ncv], axis=-1), 0, 1))
            w_out = even_w_out[i].astype(BF16)
        else:
            w_main, w_small = _prep_odd_w_in(odd_w_in, i)
            zm, zs = _inproj(x, odd_norm[i], w_main, w_small, tn=1024)
            zs_t = zs[:mp].T
            mix_a, c_p, n_p, m_p = _mlstm_prompt(zm, zs, zs_t, mt, bp, seq, mlstm_i_bias[i], mlstm_f_bias[i],
                                                 mlstm_norm[i])
            mix_b, v_rows = _cmlp_prompt(zm, mt, bp, seq, cmlp_v_norm[i], cmlp_ws[i], cmlp_b[i])
            outs_p["mc"].append(c_p); outs_p["mn"].append(n_p); outs_p["mm"].append(m_p[:, :, 0])
            outs_p["cv"].append(v_rows)
            mix_a, n_s, m_s, mc_all = _mlstm_decode(zm, zs, mix_a, state_mlstm_c, i, mc_all, state_mlstm_n[i],
                                                    state_mlstm_m[i], mp, mlstm_i_bias[i], mlstm_f_bias[i],
                                                    mlstm_norm[i])
            mix_b, v_row_s = _cmlp_decode(zm, mix_b, mp, bs, cmlp_v_norm[i], cmlp_ws[i], cmlp_b[i])
            outs_s["mn"].append(n_s.reshape(bs, ML_HEADS, ML_DK))
            outs_s["mm"].append(m_s[:, :ML_HEADS])
            outs_s["cv"].append(v_row_s.reshape(bs, 1, CM_WIDTH))
            w_out = odd_w_out[i].astype(BF16)
        x = _outproj(x, mix_a, mix_b, w_out)

    y_p = _final_norm(x, final_norm, 0, mp, 512).reshape(bp, seq, d)
    y_s = _final_norm(x, final_norm, mp, bs, bs).reshape(bs, 1, d)
    st = lambda o, k: jnp.stack(o[k])
    return (y_p, y_s, st(outs_p, "sc"), st(outs_s, "sc"), st(outs_p, "ss"), ss_all,
            st(outs_p, "gc"), st(outs_s, "gc"), st(outs_p, "gs"), gs_all,
            st(outs_p, "mc"), mc_all, st(outs_p, "mn"), st(outs_s, "mn"),
            st(outs_p, "mm"), st(outs_s, "mm"), st(outs_p, "cv"), st(outs_s, "cv"))
```

```python
import jax
import jax.numpy as jnp
import numpy as np
from jax import lax
from jax.experimental import pallas as pl
from jax.experimental.pallas import tpu as pltpu

F32 = jnp.float32
BF16 = jnp.bfloat16
HI = lax.Precision.HIGHEST

D_MODEL = 2048
DEPTH = 4
CHUNK = 128
CONV_K = 4
EPS = 1e-6
NEG = -1e30

SSD_WIDTH = D_MODEL
SSD_HEAD_DIM = 64
SSD_HEADS = SSD_WIDTH // SSD_HEAD_DIM
SSD_STATE = 128
SSD_GROUPS = 4
SSD_GW = SSD_WIDTH // SSD_GROUPS
SSD_BC = 2 * SSD_GROUPS * SSD_STATE
SSD_CONV_DIM = SSD_WIDTH + SSD_BC
GDN_HEADS = 16
GDN_DK = 128
GDN_DV = 128
GDN_QK = GDN_HEADS * GDN_DK
GDN_V = GDN_HEADS * GDN_DV
GDN_CONV_DIM = 2 * GDN_QK + GDN_V
GDN_HG = 8
ML_HEADS = 8
ML_DK = 128
ML_DV = 256
ML_QK = ML_HEADS * ML_DK
ML_V = ML_HEADS * ML_DV
CM_WIDTH = D_MODEL // 2
CM_GROUPS = 8
CM_GROUP_DIM = CM_WIDTH // CM_GROUPS
CM_CHUNK = 128

LANES = 128
SUBLANES = 8

EVEN_MAIN = 6 * D_MODEL + SSD_BC
EVEN_SMALL = 3 * LANES
ODD_MAIN = 2 * ML_QK + 3 * ML_V + 3 * CM_WIDTH
ODD_SMALL = LANES

VMEM_LIMIT = 56 * 1024 * 1024
ROW_TILE_CAP = 1040
BF16_SUBLANES = 16


def _row_tile(m):
    return max(t for t in range(BF16_SUBLANES, ROW_TILE_CAP + 1, BF16_SUBLANES) if m % t == 0)


def _silu(x):
    return x * jax.nn.sigmoid(x)


def _softplus(x):
    return jnp.maximum(x, 0.0) + jnp.log1p(jnp.exp(-jnp.abs(x)))


def _gelu(x):
    return 0.5 * x * (1.0 + jnp.tanh(np.sqrt(2.0 / np.pi).astype(np.float32) * (x + 0.044715 * (x * x * x))))


def _mm(a, b):
    return jnp.dot(a.astype(BF16), b.astype(BF16), preferred_element_type=F32)


def _mm_nt(a, b):
    return lax.dot_general(a.astype(BF16), b.astype(BF16), (((1,), (1,)), ((), ())), preferred_element_type=F32)


def _mm_tn(a, b):
    return lax.dot_general(a.astype(BF16), b.astype(BF16), (((0,), (0,)), ((), ())), preferred_element_type=F32)


def _mm_hi(a, b):
    return jnp.dot(a, b, precision=HI, preferred_element_type=F32)


def _split3(x):
    hi = x.astype(BF16).astype(F32)
    r1 = x - hi
    mid = r1.astype(BF16).astype(F32)
    lo = (r1 - mid).astype(BF16).astype(F32)
    return hi, mid, lo


def _mm_sel(a, sel):
    hi, mid, lo = _split3(a)
    return (_mm(hi, sel) + _mm(mid, sel)) + _mm(lo, sel)


def _iota(shape, axis):
    return lax.broadcasted_iota(jnp.int32, shape, axis)


def _params(*sem):
    return pltpu.CompilerParams(dimension_semantics=sem, vmem_limit_bytes=VMEM_LIMIT)


def _inproj_kernel(x_ref, g_ref, w_ref, ws_ref, z_ref, zs_ref, xn_ref):
    @pl.when(pl.program_id(1) == 0)
    def _():
        x = x_ref[...]
        y = x * lax.rsqrt(jnp.mean(x * x, axis=-1, keepdims=True) + EPS)
        xn = (y * g_ref[...]).astype(BF16)
        xn_ref[...] = xn
        zs_ref[...] = _mm_nt(xn, ws_ref[...])

    z_ref[...] = _mm_nt(xn_ref[...], w_ref[...])


def _inproj(x, g, w_main, w_small, tn):
    m, d = x.shape
    n = w_main.shape[0]
    ns = w_small.shape[0]
    tm = _row_tile(m)
    return pl.pallas_call(
        _inproj_kernel,
        grid=(m // tm, n // tn),
        in_specs=[
            pl.BlockSpec((tm, d), lambda i, j: (i, 0)),
            pl.BlockSpec((1, d), lambda i, j: (0, 0)),
            pl.BlockSpec((tn, d), lambda i, j: (j, 0)),
            pl.BlockSpec((ns, d), lambda i, j: (0, 0)),
        ],
        out_specs=[
            pl.BlockSpec((tm, tn), lambda i, j: (i, j)),
            pl.BlockSpec((tm, ns), lambda i, j: (i, 0)),
        ],
        out_shape=[jax.ShapeDtypeStruct((m, n), F32), jax.ShapeDtypeStruct((m, ns), F32)],
        scratch_shapes=[pltpu.VMEM((tm, d), BF16)],
        compiler_params=_params("parallel", "arbitrary"),
        name="inproj",
    )(x, g.reshape(1, d), w_main, w_small)


def _outproj_kernel(x_ref, ma_ref, mb_ref, wa_ref, wb_ref, o_ref):
    o_ref[...] = (x_ref[...] + jnp.dot(ma_ref[...], wa_ref[...], preferred_element_type=F32)
                  + jnp.dot(mb_ref[...], wb_ref[...], preferred_element_type=F32))


def _outproj(x, mix_a, mix_b, w):
    m, d = x.shape
    ka, kb = mix_a.shape[1], mix_b.shape[1]
    tm, tn = _row_tile(m), 512
    return pl.pallas_call(
        _outproj_kernel,
        grid=(m // tm, d // tn),
        in_specs=[
            pl.BlockSpec((tm, tn), lambda i, j: (i, j)),
            pl.BlockSpec((tm, ka), lambda i, j: (i, 0)),
            pl.BlockSpec((tm, kb), lambda i, j: (i, 0)),
            pl.BlockSpec((ka, tn), lambda i, j: (0, j)),
            pl.BlockSpec((kb, tn), lambda i, j: (ka // kb, j)),
        ],
        out_specs=pl.BlockSpec((tm, tn), lambda i, j: (i, j)),
        out_shape=jax.ShapeDtypeStruct((m, d), F32),
        compiler_params=_params("parallel", "arbitrary"),
        name="outproj",
    )(x, mix_a, mix_b, w, w)


def _final_norm_kernel(x_ref, g_ref, o_ref):
    x = x_ref[...]
    o_ref[...] = x * lax.rsqrt(jnp.mean(x * x, axis=-1, keepdims=True) + EPS) * g_ref[...]


def _final_norm(x, g, row0, rows, tm):
    d = x.shape[1]
    return pl.pallas_call(
        _final_norm_kernel,
        grid=(rows // tm,),
        in_specs=[pl.BlockSpec((tm, d), lambda i: (row0 // tm + i, 0)), pl.BlockSpec((1, d), lambda i: (0, 0))],
        out_specs=pl.BlockSpec((tm, d), lambda i: (i, 0)),
        out_shape=jax.ShapeDtypeStruct((rows, d), F32),
        compiler_params=_params("parallel"),
        name="final_norm",
    )(x, g.reshape(1, d))


REPACK_TN = 512


def _repack_kernel(a_ref, b_ref, o_ref, *, shift, lo, hi):
    j = pl.program_id(0)
    shifted = (j >= lo) & (j < hi)

    @pl.when(shifted)
    def _():
        o_ref[...] = jnp.concatenate([a_ref[shift:, :], b_ref[...]], axis=0).astype(BF16)

    @pl.when(jnp.logical_not(shifted))
    def _():
        o_ref[...] = a_ref[...].astype(BF16)


def _repack(wt_all, layer, n_out, a_idx, shift, lo, hi):
    d = wt_all.shape[2]
    per = REPACK_TN // shift
    kern = lambda a, b, o: _repack_kernel(a, b, o, shift=shift, lo=lo, hi=hi)
    return pl.pallas_call(
        kern,
        grid=(n_out // REPACK_TN,),
        in_specs=[pl.BlockSpec((None, REPACK_TN, d), lambda j: (layer, a_idx(j), 0)),
                  pl.BlockSpec((None, shift, d), lambda j: (layer, (a_idx(j) + 1) * per, 0))],
        out_specs=pl.BlockSpec((REPACK_TN, d), lambda j: (j, 0)),
        out_shape=jax.ShapeDtypeStruct((n_out, d), BF16),
        compiler_params=_params("parallel"),
        name="repack",
    )(wt_all, wt_all)


def _small_pack_kernel(*refs, layout):
    pieces, o_ref = refs[:-1], refs[-1]
    d = o_ref.shape[1]
    for blk, idxs in enumerate(layout):
        rows = [pieces[i][...] for i in idxs] + [jnp.zeros((LANES - SUBLANES * len(idxs), d), F32)]
        o_ref[blk * LANES:(blk + 1) * LANES, :] = jnp.concatenate(rows, axis=0).astype(BF16)


def _small_pack(wt_all, layer, src_rows, layout):
    d = wt_all.shape[2]
    kern = lambda *refs: _small_pack_kernel(*refs, layout=layout)
    spec = lambda r: pl.BlockSpec((None, SUBLANES, d), lambda i: (layer, r // SUBLANES, 0))
    return pl.pallas_call(
        kern,
        grid=(1,),
        in_specs=[spec(r) for r in src_rows],
        out_specs=pl.BlockSpec((len(layout) * LANES, d), lambda i: (0, 0)),
        out_shape=jax.ShapeDtypeStruct((len(layout) * LANES, d), BF16),
        compiler_params=_params("arbitrary"),
        name="small_pack",
    )(*([wt_all] * len(src_rows)))


def _prep_even_w_in(wt_all, layer):
    o_bc = 2 * SSD_WIDTH
    o_dt = o_bc + SSD_BC
    o_q = o_dt + SSD_HEADS
    o_beta = o_q + GDN_CONV_DIM + GDN_V
    o_g = o_beta + GDN_HEADS
    n1, n2 = o_bc // REPACK_TN, (o_bc + o_beta - o_q) // REPACK_TN
    src2, src3 = o_dt // REPACK_TN, o_bc // REPACK_TN
    a_idx = lambda j: jnp.where(j < n1, j, jnp.where(j < n2, j - n1 + src2, j - n2 + src3))
    main = _repack(wt_all, layer, EVEN_MAIN, a_idx, o_q - o_dt, n1, n2)
    n_dt = SSD_HEADS // SUBLANES
    src = [o_dt + SUBLANES * i for i in range(n_dt)]
    layout = [tuple(range(n_dt))]
    for hg in range(GDN_HEADS // GDN_HG):
        src += [o_beta + hg * GDN_HG, o_g + hg * GDN_HG]
        layout.append((len(src) - 2, len(src) - 1))
    return main, _small_pack(wt_all, layer, src, layout)


def _prep_odd_w_in(wt_all, layer):
    o1 = 2 * ML_QK + 3 * ML_V
    o2 = o1 + 2 * ML_HEADS
    main = _repack(wt_all, layer, ODD_MAIN, lambda j: j, o2 - o1, o1 // REPACK_TN, ODD_MAIN // REPACK_TN)
    small = _small_pack(wt_all, layer, [o1, o1 + ML_HEADS], [(0, 1)])
    return main, small


def _conv_chunk(ext_ref, u, w_ref, first):
    @pl.when(first)
    def _():
        ext_ref[0:SUBLANES, :] = jnp.zeros((SUBLANES, ext_ref.shape[1]), F32)

    ext_ref[SUBLANES:SUBLANES + CHUNK, :] = u
    out = w_ref[CONV_K - 1:CONV_K, :] * u
    for k in range(1, CONV_K):
        out = out + w_ref[CONV_K - 1 - k:CONV_K - k, :] * ext_ref[SUBLANES - k:SUBLANES - k + CHUNK, :]
    ext_ref[0:SUBLANES, :] = ext_ref[CHUNK:CHUNK + SUBLANES, :]
    return out


def _conv_tail(ext_ref):
    return ext_ref[SUBLANES - (CONV_K - 1):SUBLANES, :]


def _causal_masks():
    r = _iota((CHUNK, CHUNK), 0)
    c = _iota((CHUNK, CHUNK), 1)
    return r, c


def _ssd_prompt_kernel(za_ref, xs_ref, bc_ref, zc_ref, zr_ref, wx_ref, bx_ref, wbc_ref, bbc_ref, pc_ref, pr_ref,
                       d_ref, nrm_ref, mix_ref, cx_ref, cbc_ref, st_ref, extx, extbc, s_ref):
    c_id = pl.program_id(1)
    first = c_id == 0
    last = c_id == pl.num_programs(1) - 1

    @pl.when(first)
    def _():
        s_ref[...] = jnp.zeros(s_ref.shape, F32)

    xs = _silu(_conv_chunk(extx, xs_ref[...], wx_ref, first) + bx_ref[...])
    bc = _silu(_conv_chunk(extbc, bc_ref[...], wbc_ref, first) + bbc_ref[...])

    r, c = _causal_masks()
    causal = r >= c
    tril = jnp.where(causal, 1.0, 0.0)
    triu = jnp.where(r <= c, 1.0, 0.0)
    dt = _softplus(zc_ref[...] + pc_ref[0:1, :])
    la = _mm_hi(tril, dt * (-jnp.exp(pc_ref[1:2, :])))
    dtr = _softplus(zr_ref[...] + pr_ref[0])
    lar = _mm_hi(dtr * (-jnp.exp(pr_ref[1])), triu)
    la_last = la[CHUNK - 1:CHUNK, :]
    e_mat = jnp.where((_iota((LANES, SSD_WIDTH), 1) >> 6) == _iota((LANES, SSD_WIDTH), 0), 1.0, 0.0)
    ela_x = _mm_sel(jnp.exp(la), e_mat)
    wsx = _mm_sel(jnp.exp(la_last - la) * dt, e_mat)
    dec_x = _mm_sel(jnp.broadcast_to(jnp.exp(la_last), (SUBLANES, LANES)), e_mat)[0:1, :]
    lane_lo = _iota((CHUNK, LANES), 1) < SSD_HEAD_DIM

    hpg = SSD_HEADS // SSD_GROUPS
    for g in range(SSD_GROUPS):
        gs = slice(g * SSD_GW, (g + 1) * SSD_GW)
        bg = bc[:, g * SSD_STATE:(g + 1) * SSD_STATE]
        cg = bc[:, SSD_GROUPS * SSD_STATE + g * SSD_STATE:SSD_GROUPS * SSD_STATE + (g + 1) * SSD_STATE]
        cb = _mm_nt(cg, bg)
        ys = []
        for pair in range(hpg // 2):
            h0 = g * hpg + 2 * pair
            xpair = xs[:, h0 * SSD_HEAD_DIM:(h0 + 2) * SSD_HEAD_DIM]
            halves = []
            for hh in (h0, h0 + 1):
                seg = jnp.where(causal, la[:, hh:hh + 1] - lar[hh:hh + 1, :], NEG)
                lmat = jnp.exp(seg) * cb * dtr[hh:hh + 1, :]
                halves.append(_mm(lmat, xpair))
            ys.append(jnp.where(lane_lo, halves[0], halves[1]))
        y = jnp.concatenate(ys, axis=1)
        s_prev = s_ref[g]
        y = y + _mm(cg, s_prev) * ela_x[:, gs] + d_ref[:, gs] * xs[:, gs]
        y = y * _silu(za_ref[:, gs])
        y = y * lax.rsqrt(jnp.mean(y * y, axis=-1, keepdims=True) + EPS) * nrm_ref[:, gs]
        mix_ref[:, gs] = y.astype(BF16)
        s_ref[g] = s_prev * dec_x[:, gs] + _mm_tn(bg, xs[:, gs] * wsx[:, gs])

    @pl.when(last)
    def _():
        st_ref[0] = s_ref[...]
        cx_ref[0] = _conv_tail(extx)
        cbc_ref[0] = _conv_tail(extbc)


def _ssd_prompt(zm, zs, zs_t, m_total, bsz, seq, conv_w, conv_b, dt_bias, a_log, d_skip, norm):
    nc = seq // CHUNK
    rb = lambda b, c: b * nc + c
    pc = jnp.zeros((SUBLANES, LANES), F32).at[0, :SSD_HEADS].set(dt_bias).at[1, :SSD_HEADS].set(a_log)
    pr = jnp.stack([jnp.broadcast_to(dt_bias[:, None], (SSD_HEADS, CHUNK)),
                    jnp.broadcast_to(a_log[:, None], (SSD_HEADS, CHUNK))])
    full = lambda *shape: pl.BlockSpec(shape, lambda b, c: (0,) * len(shape))
    return pl.pallas_call(
        _ssd_prompt_kernel,
        grid=(bsz, nc),
        in_specs=[
            pl.BlockSpec((CHUNK, SSD_WIDTH), lambda b, c: (rb(b, c), 0)),
            pl.BlockSpec((CHUNK, SSD_WIDTH), lambda b, c: (rb(b, c), 1)),
            pl.BlockSpec((CHUNK, SSD_BC), lambda b, c: (rb(b, c), 6 * D_MODEL // SSD_BC)),
            pl.BlockSpec((CHUNK, LANES), lambda b, c: (rb(b, c), 0)),
            pl.BlockSpec((SSD_HEADS, CHUNK), lambda b, c: (0, rb(b, c))),
            full(CONV_K, SSD_WIDTH), full(1, SSD_WIDTH), full(CONV_K, SSD_BC), full(1, SSD_BC),
            full(SUBLANES, LANES), full(2, SSD_HEADS, CHUNK), full(1, SSD_WIDTH), full(1, SSD_WIDTH),
        ],
        out_specs=[
            pl.BlockSpec((CHUNK, SSD_WIDTH), lambda b, c: (rb(b, c), 0)),
            pl.BlockSpec((1, CONV_K - 1, SSD_WIDTH), lambda b, c: (b, 0, 0)),
            pl.BlockSpec((1, CONV_K - 1, SSD_BC), lambda b, c: (b, 0, 0)),
            pl.BlockSpec((1, SSD_GROUPS, SSD_STATE, SSD_GW), lambda b, c: (b, 0, 0, 0)),
        ],
        out_shape=[
            jax.ShapeDtypeStruct((m_total, SSD_WIDTH), BF16),
            jax.ShapeDtypeStruct((bsz, CONV_K - 1, SSD_WIDTH), F32),
            jax.ShapeDtypeStruct((bsz, CONV_K - 1, SSD_BC), F32),
            jax.ShapeDtypeStruct((bsz, SSD_GROUPS, SSD_STATE, SSD_GW), F32),
        ],
        scratch_shapes=[pltpu.VMEM((SUBLANES + CHUNK, SSD_WIDTH), F32), pltpu.VMEM((SUBLANES + CHUNK, SSD_BC), F32),
                        pltpu.VMEM((SSD_GROUPS, SSD_STATE, SSD_GW), F32)],
        compiler_params=_params("parallel", "arbitrary"),
        name="ssd_prompt",
    )(zm, zm, zm, zs, zs_t, conv_w[:, :SSD_WIDTH], conv_b[None, :SSD_WIDTH], conv_w[:, SSD_WIDTH:],
      conv_b[None, SSD_WIDTH:], pc, pr, jnp.repeat(d_skip, SSD_HEAD_DIM)[None, :], norm[None, :])


def _tri_inverse(mats, r, c):
    def corner(level):
        return ((r >> (level + 1)) == (c >> (level + 1))) & (((r >> level) & 1) == 1) & (((c >> level) & 1) == 0)

    eye = jnp.where(r == c, 1.0, 0.0)
    ts = [eye - jnp.where(corner(0), a, 0.0) for a in mats]
    for level in range(1, 7):
        cm = corner(level)
        xs = [_mm(t, jnp.where(cm, a, 0.0)) for t, a in zip(ts, mats)]
        ts = [t - _mm(x, t) for t, x in zip(ts, xs)]
    return ts


def _gdn_prompt_kernel(q_ref, k_ref, v_ref, zb_ref, zc_ref, zr_ref, wq_ref, wk_ref, wv_ref, pc_ref, pr_ref, nrm_ref,
                       mix_ref, cq_ref, ck_ref, cv_ref, st_ref, extq, extk, extv, s_ref):
    c_id = pl.program_id(2)
    first = c_id == 0
    last = c_id == pl.num_programs(2) - 1

    @pl.when(first)
    def _():
        s_ref[...] = jnp.zeros(s_ref.shape, F32)

    q_all = _silu(_conv_chunk(extq, q_ref[...], wq_ref, first))
    k_all = _silu(_conv_chunk(extk, k_ref[...], wk_ref, first))
    v_all = _silu(_conv_chunk(extv, v_ref[...], wv_ref, first))

    r, c = _causal_masks()
    causal = r >= c
    strict = r > c
    tril = jnp.where(causal, 1.0, 0.0)
    triu = jnp.where(r <= c, 1.0, 0.0)
    zc = zc_ref[...]
    beta_c = jax.nn.sigmoid(zc)
    gc_c = _mm_hi(tril, -jnp.exp(pc_ref[0:1, :]) * _softplus(zc + pc_ref[1:2, :]))
    gc_r = _mm_hi(-jnp.exp(pr_ref[0]) * _softplus(zr_ref[GDN_HG:2 * GDN_HG, :] + pr_ref[1]), triu)

    heads = range(GDN_HG)
    hs = [slice(j * GDN_DK, (j + 1) * GDN_DK) for j in heads]
    qh = [q_all[:, s] for s in hs]
    kh = [k_all[:, s] for s in hs]
    qh = [x * (lax.rsqrt(jnp.sum(x * x, axis=-1, keepdims=True) + EPS) * (GDN_DK ** -0.5)) for x in qh]
    kh = [x * lax.rsqrt(jnp.sum(x * x, axis=-1, keepdims=True) + EPS) for x in kh]
    gcc = [gc_c[:, GDN_HG + j:GDN_HG + j + 1] for j in heads]
    beta = [beta_c[:, j:j + 1] for j in heads]
    gam = [jnp.exp(jnp.where(causal, gcc[j] - gc_r[j:j + 1, :], NEG)) for j in heads]
    qkk = [_mm_nt(jnp.concatenate([qh[j], kh[j]], axis=0), kh[j]) for j in heads]
    aqk = [qkk[j][:CHUNK] * gam[j] for j in heads]
    tinv = _tri_inverse([jnp.where(strict, beta[j] * qkk[j][CHUNK:] * gam[j], 0.0) for j in heads], r, c)
    egc = [jnp.exp(g) for g in gcc]
    uw = [_mm(tinv[j], jnp.concatenate([beta[j] * v_all[:, hs[j]], (beta[j] * egc[j]) * kh[j]], axis=1))
          for j in heads]
    s_prev = [s_ref[j] for j in heads]
    ws_qs = [_mm(jnp.concatenate([uw[j][:, GDN_DV:], qh[j] * egc[j]], axis=0), s_prev[j]) for j in heads]
    vn = [uw[j][:, :GDN_DV] - ws_qs[j][:CHUNK] for j in heads]
    o = [ws_qs[j][CHUNK:] + _mm(aqk[j], vn[j]) for j in heads]
    for j in heads:
        gc_last = gcc[j][CHUNK - 1:CHUNK, :]
        s_ref[j] = s_prev[j] * jnp.exp(gc_last) + _mm_tn(kh[j] * jnp.exp(gc_last - gcc[j]), vn[j])
    for j in heads:
        on = o[j] * lax.rsqrt(jnp.mean(o[j] * o[j], axis=-1, keepdims=True) + EPS) * nrm_ref[...]
        mix_ref[:, hs[j]] = (on * _silu(zb_ref[:, hs[j]])).astype(BF16)

    @pl.when(last)
    def _():
        st_ref[0] = s_ref[...]
        cq_ref[0] = _conv_tail(extq)
        ck_ref[0] = _conv_tail(extk)
        cv_ref[0] = _conv_tail(extv)


def _gdn_prompt(zm, zs, zs_t, m_total, bsz, seq, conv_w, dt_bias, a_log, norm):
    nc = seq // CHUNK
    nhg = GDN_HEADS // GDN_HG
    w = GDN_HG * GDN_DK
    rb = lambda b, c: b * nc + c
    col0 = 2 * D_MODEL // w
    pc = jnp.zeros((nhg, SUBLANES, LANES), F32)
    pc = pc.at[:, 0, GDN_HG:2 * GDN_HG].set(a_log.reshape(nhg, GDN_HG))
    pc = pc.at[:, 1, GDN_HG:2 * GDN_HG].set(dt_bias.reshape(nhg, GDN_HG))
    pr = jnp.stack([jnp.broadcast_to(a_log.reshape(nhg, GDN_HG, 1), (nhg, GDN_HG, CHUNK)),
                    jnp.broadcast_to(dt_bias.reshape(nhg, GDN_HG, 1), (nhg, GDN_HG, CHUNK))], axis=1)
    seg = lambda s: pl.BlockSpec((CHUNK, w), lambda b, h, c: (rb(b, c), col0 + s * nhg + h))
    wseg = lambda s: pl.BlockSpec((CONV_K, w), lambda b, h, c: (0, s * nhg + h))
    cout = pl.BlockSpec((1, CONV_K - 1, w), lambda b, h, c: (b, 0, h))
    return pl.pallas_call(
        _gdn_prompt_kernel,
        grid=(bsz, nhg, nc),
        in_specs=[
            seg(0), seg(1), seg(2), seg(3),
            pl.BlockSpec((CHUNK, LANES), lambda b, h, c: (rb(b, c), 1 + h)),
            pl.BlockSpec((2 * GDN_HG, CHUNK), lambda b, h, c: ((1 + h) * LANES // (2 * GDN_HG), rb(b, c))),
            wseg(0), wseg(1), wseg(2),
            pl.BlockSpec((None, SUBLANES, LANES), lambda b, h, c: (h, 0, 0)),
            pl.BlockSpec((None, 2, GDN_HG, CHUNK), lambda b, h, c: (h, 0, 0, 0)),
            pl.BlockSpec((1, GDN_DV), lambda b, h, c: (0, 0)),
        ],
        out_specs=[
            pl.BlockSpec((CHUNK, w), lambda b, h, c: (rb(b, c), h)),
            cout, cout, cout,
            pl.BlockSpec((1, GDN_HG, GDN_DK, GDN_DV), lambda b, h, c: (b, h, 0, 0)),
        ],
        out_shape=[
            jax.ShapeDtypeStruct((m_total, GDN_V), BF16),
            jax.ShapeDtypeStruct((bsz, CONV_K - 1, GDN_QK), F32),
            jax.ShapeDtypeStruct((bsz, CONV_K - 1, GDN_QK), F32),
            jax.ShapeDtypeStruct((bsz, CONV_K - 1, GDN_V), F32),
            jax.ShapeDtypeStruct((bsz, GDN_HEADS, GDN_DK, GDN_DV), F32),
        ],
        scratch_shapes=[pltpu.VMEM((SUBLANES + CHUNK, w), F32)] * 3 + [pltpu.VMEM((GDN_HG, GDN_DK, GDN_DV), F32)],
        compiler_params=_params("parallel", "parallel", "arbitrary"),
        name="gdn_prompt",
    )(zm, zm, zm, zm, zs, zs_t, conv_w, conv_w, conv_w, pc, pr, norm[None, :])


def _mlstm_prompt_kernel(q_ref, k_ref, v_ref, o_ref, zc_ref, gc_ref, gr_ref, pc_ref, pr_ref, nrm_ref,
                         mix_ref, c_out, n_out, m_out, c_ref, n_ref, m_ref):
    c_id = pl.program_id(1)
    first = c_id == 0
    last = c_id == pl.num_programs(1) - 1

    @pl.when(first)
    def _():
        c_ref[...] = jnp.zeros(c_ref.shape, F32)
        n_ref[...] = jnp.zeros(n_ref.shape, F32)
        m_ref[...] = jnp.zeros(m_ref.shape, F32)

    r, c = _causal_masks()
    causal = r >= c
    tril = jnp.where(causal, 1.0, 0.0)
    triu = jnp.where(r <= c, 1.0, 0.0)
    gc = gc_ref[...]
    logi_c = gc + pc_ref[0:1, :]
    b_c = _mm_hi(tril, -_softplus(-(gc + pc_ref[1:2, :])))
    logi_r = gr_ref[0:ML_HEADS, :] + pr_ref[0]
    b_r = _mm_hi(-_softplus(-(gr_ref[ML_HEADS:2 * ML_HEADS, :] + pr_ref[1])), triu)

    heads = range(ML_HEADS)
    ks = [slice(j * ML_DK, (j + 1) * ML_DK) for j in heads]
    vs = [slice(j * ML_DV, (j + 1) * ML_DV) for j in heads]
    q = [q_ref[:, s] for s in ks]
    k = [k_ref[:, s] * (ML_DK ** -0.5) for s in ks]
    bc = [b_c[:, ML_HEADS + j:ML_HEADS + j + 1] for j in heads]
    dmat = [jnp.where(causal, bc[j] - b_r[j:j + 1, :] + logi_r[j:j + 1, :], NEG) for j in heads]
    m_intra = [jnp.max(x, axis=-1, keepdims=True) for x in dmat]
    p = [_mm_nt(q[j], k[j]) * jnp.exp(dmat[j] - m_intra[j]) for j in heads]
    h_intra = [_mm(p[j], v_ref[:, vs[j]]) for j in heads]
    c_prev = [c_ref[j] for j in heads]
    qc = [_mm(q[j], c_prev[j]) for j in heads]
    b_last = [x[CHUNK - 1:CHUNK, :] for x in bc]
    gk = [b_last[j] - bc[j] + logi_c[:, j:j + 1] for j in heads]
    m_k = [jnp.max(x, axis=0, keepdims=True) for x in gk]
    kw = [k[j] * jnp.exp(gk[j] - m_k[j]) for j in heads]
    c_loc = [_mm_tn(kw[j], v_ref[:, vs[j]]) for j in heads]
    for j in heads:
        n_intra = jnp.sum(p[j], axis=-1, keepdims=True)
        m_prev = m_ref[j:j + 1, 0:1]
        n_prev = n_ref[j:j + 1, :]
        mb = bc[j] + m_prev
        m_t = jnp.maximum(mb, m_intra[j])
        s_inter = jnp.exp(mb - m_t)
        s_intra = jnp.exp(m_intra[j] - m_t)
        num = s_inter * qc[j] + s_intra * h_intra[j]
        den = s_inter * jnp.sum(q[j] * n_prev, axis=-1, keepdims=True) + s_intra * n_intra
        h = num / jnp.maximum(jnp.abs(den), jnp.exp(-m_t))
        m_new = jnp.maximum(b_last[j] + m_prev, m_k[j])
        sa = jnp.exp(b_last[j] + m_prev - m_new)
        sb = jnp.exp(m_k[j] - m_new)
        c_ref[j] = c_prev[j] * sa + c_loc[j] * sb
        n_ref[j:j + 1, :] = n_prev * sa + jnp.sum(kw[j], axis=0, keepdims=True) * sb
        m_ref[j:j + 1, :] = jnp.broadcast_to(m_new, (1, LANES))
        h = h * lax.rsqrt(jnp.mean(h * h, axis=-1, keepdims=True) + EPS) * nrm_ref[...]
        mix_ref[:, vs[j]] = (h * jax.nn.sigmoid(o_ref[:, vs[j]]) * _silu(zc_ref[:, vs[j]])).astype(BF16)

    @pl.when(last)
    def _():
        c_out[0] = c_ref[...]
        n_out[0] = n_ref[...]
        m_out[0] = m_ref[...]


def _mlstm_prompt(zm, zs, zs_t, m_total, bsz, seq, i_bias, f_bias, norm):
    nc = seq // CHUNK
    rb = lambda b, c: b * nc + c
    pc = jnp.zeros((SUBLANES, LANES), F32).at[0, :ML_HEADS].set(i_bias).at[1, ML_HEADS:2 * ML_HEADS].set(f_bias)
    pr = jnp.stack([jnp.broadcast_to(i_bias[:, None], (ML_HEADS, CHUNK)),
                    jnp.broadcast_to(f_bias[:, None], (ML_HEADS, CHUNK))])
    full = lambda *shape: pl.BlockSpec(shape, lambda b, c: (0,) * len(shape))
    return pl.pallas_call(
        _mlstm_prompt_kernel,
        grid=(bsz, nc),
        in_specs=[
            pl.BlockSpec((CHUNK, ML_QK), lambda b, c: (rb(b, c), 0)),
            pl.BlockSpec((CHUNK, ML_QK), lambda b, c: (rb(b, c), 1)),
            pl.BlockSpec((CHUNK, ML_V), lambda b, c: (rb(b, c), 1)),
            pl.BlockSpec((CHUNK, ML_V), lambda b, c: (rb(b, c), 2)),
            pl.BlockSpec((CHUNK, ML_V), lambda b, c: (rb(b, c), 3)),
            pl.BlockSpec((CHUNK, LANES), lambda b, c: (rb(b, c), 0)),
            pl.BlockSpec((2 * ML_HEADS, CHUNK), lambda b, c: (0, rb(b, c))),
            full(SUBLANES, LANES), full(2, ML_HEADS, CHUNK), full(1, ML_DV),
        ],
        out_specs=[
            pl.BlockSpec((CHUNK, ML_V), lambda b, c: (rb(b, c), 0)),
            pl.BlockSpec((1, ML_HEADS, ML_DK, ML_DV), lambda b, c: (b, 0, 0, 0)),
            pl.BlockSpec((1, ML_HEADS, ML_DK), lambda b, c: (b, 0, 0)),
            pl.BlockSpec((1, ML_HEADS, LANES), lambda b, c: (b, 0, 0)),
        ],
        out_shape=[
            jax.ShapeDtypeStruct((m_total, ML_V), BF16),
            jax.ShapeDtypeStruct((bsz, ML_HEADS, ML_DK, ML_DV), F32),
            jax.ShapeDtypeStruct((bsz, ML_HEADS, ML_DK), F32),
            jax.ShapeDtypeStruct((bsz, ML_HEADS, LANES), F32),
        ],
        scratch_shapes=[pltpu.VMEM((ML_HEADS, ML_DK, ML_DV), F32), pltpu.VMEM((ML_HEADS, ML_DK), F32),
                        pltpu.VMEM((ML_HEADS, LANES), F32)],
        compiler_params=_params("parallel", "arbitrary"),
        name="mlstm_prompt",
    )(zm, zm, zm, zm, zm, zs, zs_t, pc, pr, norm[None, :])


def _cmlp_prompt_kernel(u_ref, v_ref, z_ref, ws_ref, wb_ref, gain_ref, mix_ref, vrows_ref):
    r, c = _causal_masks()
    causal = r >= c
    for g in range(CM_GROUPS):
        gs = slice(g * CM_GROUP_DIM, (g + 1) * CM_GROUP_DIM)
        v = _gelu(v_ref[:, gs])
        v = v * lax.rsqrt(jnp.mean(v * v, axis=-1, keepdims=True) + EPS) * gain_ref[...]
        s = _mm(jnp.where(causal, ws_ref[g], 0.0), v) + wb_ref[:, g:g + 1]
        mix_ref[:, gs] = (_gelu(u_ref[:, gs]) * s * _silu(z_ref[:, gs])).astype(BF16)
        vrows_ref[0, :, gs] = v


def _cmlp_prompt(zm, m_total, bsz, seq, v_gain, ws, wb):
    nc = seq // CM_CHUNK
    rb = lambda b, c: b * nc + c
    col0 = (2 * ML_QK + 3 * ML_V) // CM_WIDTH
    return pl.pallas_call(
        _cmlp_prompt_kernel,
        grid=(bsz, nc),
        in_specs=[
            pl.BlockSpec((CM_CHUNK, CM_WIDTH), lambda b, c: (rb(b, c), col0)),
            pl.BlockSpec((CM_CHUNK, CM_WIDTH), lambda b, c: (rb(b, c), col0 + 1)),
            pl.BlockSpec((CM_CHUNK, CM_WIDTH), lambda b, c: (rb(b, c), col0 + 2)),
            pl.BlockSpec((CM_GROUPS, CM_CHUNK, CM_CHUNK), lambda b, c: (0, 0, 0)),
            pl.BlockSpec((CM_CHUNK, CM_GROUPS), lambda b, c: (0, 0)),
            pl.BlockSpec((1, CM_GROUP_DIM), lambda b, c: (0, 0)),
        ],
        out_specs=[
            pl.BlockSpec((CM_CHUNK, CM_WIDTH), lambda b, c: (rb(b, c), 0)),
            pl.BlockSpec((1, CM_CHUNK, CM_WIDTH), lambda b, c: (b, 0, 0)),
        ],
        out_shape=[jax.ShapeDtypeStruct((m_total, CM_WIDTH), BF16),
                   jax.ShapeDtypeStruct((bsz, CM_CHUNK, CM_WIDTH), F32)],
        compiler_params=_params("parallel", "arbitrary"),
        name="cmlp_prompt",
    )(zm, zm, zm, ws, wb.T, v_gain[None, :])


SEQ_BLOCK = SUBLANES


def _conv_step(raw_ref, cin_ref, w_ref, cout_ref):
    u = raw_ref[...]
    out = w_ref[CONV_K - 1:CONV_K, :] * u
    for k in range(CONV_K - 1):
        out = out + w_ref[k:k + 1, :] * cin_ref[k]
    for k in range(CONV_K - 2):
        cout_ref[k] = cin_ref[k + 1]
    cout_ref[CONV_K - 2] = u
    return out


def _row0(row, fill=0.0):
    return jnp.where(_iota((SUBLANES, row.shape[1]), 0) == 0, row, fill)


def _stash_rows(dst_ref, val):
    for i in range(SEQ_BLOCK):
        dst_ref[i] = jnp.broadcast_to(val[i:i + 1, :], (SUBLANES, val.shape[1]))


def _gather_rows(src_ref):
    rid = _iota(src_ref.shape[1:], 0)
    acc = src_ref[0]
    for i in range(1, SEQ_BLOCK):
        acc = jnp.where(rid == i, src_ref[i], acc)
    return acc


def _ssd_decode_kernel(za_ref, xs_ref, b_ref, c_ref, zc_ref, cx_ref, cb_ref, cc_ref, wx_ref, bx_ref, wb_ref, bb_ref,
                       wc_ref, bc_ref, pc_ref, e_ref, a_ref, d_ref, nrm_ref, s_ref, *rest):
    mix_ref, ncx_ref, ncb_ref, ncc_ref, so_ref = rest[-5:]
    xs = _silu(_conv_step(xs_ref, cx_ref, wx_ref, ncx_ref) + bx_ref[...])
    bm = _silu(_conv_step(b_ref, cb_ref, wb_ref, ncb_ref) + bb_ref[...])
    cm = _silu(_conv_step(c_ref, cc_ref, wc_ref, ncc_ref) + bc_ref[...])
    dtx = _mm_hi(_softplus(zc_ref[...] + pc_ref[0:1, :]), e_ref[...])
    hi, mid, lo = _split3(jnp.exp(dtx * a_ref[...]))
    dx = dtx * xs
    rid = _iota((SUBLANES, SSD_GW), 0)
    ones_rows = jnp.where((_iota((SUBLANES, SSD_STATE), 0) >= 1) & (_iota((SUBLANES, SSD_STATE), 0) <= 3), 1.0, 0.0)
    hpg = SSD_HEADS // SSD_GROUPS
    seqs = range(SEQ_BLOCK)
    lmat = [jnp.where(rid == 0, dx[i:i + 1, :], jnp.where(rid == 1, hi[i:i + 1, :],
                      jnp.where(rid == 2, mid[i:i + 1, :], jnp.where(rid == 3, lo[i:i + 1, :], 0.0)))) for i in seqs]
    upd = [_mm_tn(lmat[i], jnp.concatenate([_row0(bm[i:i + 1, :]), ones_rows], axis=1)) for i in seqs]
    s_new = [s_ref[i].reshape(SSD_GW, SSD_STATE) * upd[i][:, SSD_STATE:] + upd[i][:, :SSD_STATE] for i in seqs]
    for i in seqs:
        so_ref[i] = s_new[i].reshape(hpg, SSD_HEAD_DIM, SSD_STATE)
    ys = [_mm_nt(jnp.broadcast_to(cm[i:i + 1, :], (SUBLANES, SSD_STATE)), s_new[i]) for i in seqs]
    y = ys[0]
    for i in seqs[1:]:
        y = jnp.where(rid == i, ys[i], y)
    y = y + d_ref[...] * xs
    y = y * _silu(za_ref[...])
    y = y * lax.rsqrt(jnp.mean(y * y, axis=-1, keepdims=True) + EPS) * nrm_ref[...]
    mix_ref[...] = y.astype(BF16)


def _stacked_state_io(states, layer, prev, block, index):
    spec = pl.BlockSpec((None,) + block, lambda *g: (layer,) + index(*g))
    extra_in = [] if prev is None else [prev]
    extra_specs = [] if prev is None else [pl.BlockSpec(memory_space=pl.ANY)]
    return spec, jax.ShapeDtypeStruct(states.shape, F32), extra_in, extra_specs


def _ssd_decode(zm, zs, mix, conv_t, states, layer, prev, mp, conv_w, conv_b, dt_bias, a_log, d_skip, norm):
    bsz = states.shape[1]
    hpg = SSD_HEADS // SSD_GROUPS
    st_spec, st_shape, extra_in, extra_specs = _stacked_state_io(
        states, layer, prev, (SEQ_BLOCK, hpg, SSD_HEAD_DIM, SSD_STATE), lambda s, g: (s, g, 0, 0))
    r0 = mp // SEQ_BLOCK
    pc = jnp.zeros((SUBLANES, LANES), F32).at[0, :SSD_HEADS].set(dt_bias)
    e_mat = (jnp.arange(SSD_WIDTH)[None, :] // SSD_HEAD_DIM == jnp.arange(LANES)[:, None]).astype(F32)
    a_x = jnp.repeat(-jnp.exp(a_log), SSD_HEAD_DIM)[None, :]
    nb = SSD_WIDTH // SSD_STATE
    zrow = lambda w, col: pl.BlockSpec((SEQ_BLOCK, w), lambda s, g: (r0 + s, col(g)))
    cst = lambda w, col: pl.BlockSpec((CONV_K - 1, SEQ_BLOCK, w), lambda s, g: (0, s, col(g)))
    par = lambda rows, w, col: pl.BlockSpec((rows, w), lambda s, g: (0, col(g)))
    mainb = 6 * D_MODEL // SSD_STATE
    return pl.pallas_call(
        _ssd_decode_kernel,
        grid=(bsz // SEQ_BLOCK, SSD_GROUPS),
        in_specs=[
            zrow(SSD_GW, lambda g: g), zrow(SSD_GW, lambda g: SSD_GROUPS + g),
            zrow(SSD_STATE, lambda g: mainb + g), zrow(SSD_STATE, lambda g: mainb + SSD_GROUPS + g),
            zrow(LANES, lambda g: 0),
            cst(SSD_GW, lambda g: g), cst(SSD_STATE, lambda g: nb + g), cst(SSD_STATE, lambda g: nb + SSD_GROUPS + g),
            par(CONV_K, SSD_GW, lambda g: g), par(1, SSD_GW, lambda g: g),
            par(CONV_K, SSD_STATE, lambda g: nb + g), par(1, SSD_STATE, lambda g: nb + g),
            par(CONV_K, SSD_STATE, lambda g: nb + SSD_GROUPS + g), par(1, SSD_STATE, lambda g: nb + SSD_GROUPS + g),
            par(SUBLANES, LANES, lambda g: 0), par(LANES, SSD_GW, lambda g: g),
            par(1, SSD_GW, lambda g: g), par(1, SSD_GW, lambda g: g), par(1, SSD_GW, lambda g: g),
            st_spec,
            pl.BlockSpec(memory_space=pl.ANY),
        ] + extra_specs,
        out_specs=[
            pl.BlockSpec((SEQ_BLOCK, SSD_GW), lambda s, g: (r0 + s, g)),
            cst(SSD_GW, lambda g: g), cst(SSD_STATE, lambda g: g), cst(SSD_STATE, lambda g: g),
            st_spec,
        ],
        out_shape=[
            jax.ShapeDtypeStruct(mix.shape, mix.dtype),
            jax.ShapeDtypeStruct((CONV_K - 1, bsz, SSD_WIDTH), F32),
            jax.ShapeDtypeStruct((CONV_K - 1, bsz, SSD_GROUPS * SSD_STATE), F32),
            jax.ShapeDtypeStruct((CONV_K - 1, bsz, SSD_GROUPS * SSD_STATE), F32),
            st_shape,
        ],
        input_output_aliases={20: 0} if prev is None else {20: 0, 21: 4},
        compiler_params=_params("parallel", "arbitrary"),
        name="ssd_decode",
    )(zm, zm, zm, zm, zs, conv_t, conv_t, conv_t, conv_w, conv_b[None, :], conv_w, conv_b[None, :], conv_w,
      conv_b[None, :], pc, e_mat, a_x, jnp.repeat(d_skip, SSD_HEAD_DIM)[None, :], norm[None, :], states, mix,
      *extra_in)


def _gdn_decode_kernel(q_ref, k_ref, v_ref, zb_ref, zc_ref, cq_ref, ck_ref, cv_ref, wq_ref, wk_ref, wv_ref, pc_ref,
                       nrm_ref, s_ref, *rest):
    mix_ref, ncq_ref, nck_ref, ncv_ref, so_ref, q_s, k_s, v_s, beta_s, eg_s, qk_s, o_s = rest[-12:]
    q_all = _silu(_conv_step(q_ref, cq_ref, wq_ref, ncq_ref))
    k_all = _silu(_conv_step(k_ref, ck_ref, wk_ref, nck_ref))
    _stash_rows(v_s, _silu(_conv_step(v_ref, cv_ref, wv_ref, ncv_ref)))
    zc = zc_ref[...]
    beta = jax.nn.sigmoid(zc)
    eg = jnp.exp(-jnp.exp(pc_ref[0:1, :]) * _softplus(zc + pc_ref[1:2, :]))
    heads = range(GDN_HG)
    hs = [slice(j * GDN_DK, (j + 1) * GDN_DK) for j in heads]
    qn, kn, beta_x, eg_x, qk_x = [], [], [], [], []
    shape = (SEQ_BLOCK, GDN_DK)
    for j in heads:
        qh, kh = q_all[:, hs[j]], k_all[:, hs[j]]
        qh = qh * (lax.rsqrt(jnp.sum(qh * qh, axis=-1, keepdims=True) + EPS) * (GDN_DK ** -0.5))
        kh = kh * lax.rsqrt(jnp.sum(kh * kh, axis=-1, keepdims=True) + EPS)
        qn.append(qh)
        kn.append(kh)
        beta_x.append(jnp.broadcast_to(beta[:, j:j + 1], shape))
        eg_x.append(jnp.broadcast_to(eg[:, GDN_HG + j:GDN_HG + j + 1], shape))
        qk_x.append(jnp.broadcast_to(jnp.sum(qh * kh, axis=-1, keepdims=True), shape))
    for ref, parts in ((q_s, qn), (k_s, kn), (beta_s, beta_x), (eg_s, eg_x), (qk_s, qk_x)):
        _stash_rows(ref, jnp.concatenate(parts, axis=1))

    rid = _iota((SUBLANES, GDN_DK), 0)

    def body(i, carry):
        k_b, q_b, v_b, beta_b, eg_b, qk_b = k_s[i], q_s[i], v_s[i], beta_s[i], eg_s[i], qk_s[i]
        s_prev = [s_ref[i, j] for j in heads]
        ks_qs = [_mm(jnp.where(rid == 0, k_b[:, hs[j]], q_b[:, hs[j]]), s_prev[j]) for j in heads]
        vn = [beta_b[0:1, hs[j]] * (v_b[0:1, hs[j]] - eg_b[0:1, hs[j]] * ks_qs[j][0:1, :]) for j in heads]
        outer = [_mm_tn(_row0(k_b[:, hs[j]]), _row0(vn[j])) for j in heads]
        for j in heads:
            so_ref[i, j] = s_prev[j] * eg_b[0:1, j * GDN_DK:j * GDN_DK + 1] + outer[j]
            o_row = eg_b[0:1, hs[j]] * ks_qs[j][1:2, :] + qk_b[0:1, hs[j]] * vn[j]
            o_s[i, :, hs[j]] = jnp.broadcast_to(o_row, (SUBLANES, GDN_DV))
        return carry

    lax.fori_loop(0, SEQ_BLOCK, body, 0)
    o_all = _gather_rows(o_s)
    for j in heads:
        o = o_all[:, hs[j]]
        o = o * lax.rsqrt(jnp.mean(o * o, axis=-1, keepdims=True) + EPS) * nrm_ref[...]
        mix_ref[:, hs[j]] = (o * _silu(zb_ref[:, hs[j]])).astype(BF16)


def _gdn_decode(zm, zs, mix, conv_t, states, layer, prev, mp, conv_w, dt_bias, a_log, norm):
    bsz = states.shape[1]
    st_spec, st_shape, extra_in, extra_specs = _stacked_state_io(
        states, layer, prev, (SEQ_BLOCK, GDN_HG, GDN_DK, GDN_DV), lambda s, h: (s, h, 0, 0))
    nhg = GDN_HEADS // GDN_HG
    w = GDN_HG * GDN_DK
    r0 = mp // SEQ_BLOCK
    col0 = 2 * D_MODEL // w
    pc = jnp.zeros((nhg, SUBLANES, LANES), F32)
    pc = pc.at[:, 0, GDN_HG:2 * GDN_HG].set(a_log.reshape(nhg, GDN_HG))
    pc = pc.at[:, 1, GDN_HG:2 * GDN_HG].set(dt_bias.reshape(nhg, GDN_HG))
    seg = lambda p: pl.BlockSpec((SEQ_BLOCK, w), lambda s, h: (r0 + s, col0 + p * nhg + h))
    cst = lambda p: pl.BlockSpec((CONV_K - 1, SEQ_BLOCK, w), lambda s, h: (0, s, p * nhg + h))
    wseg = lambda p: pl.BlockSpec((CONV_K, w), lambda s, h: (0, p * nhg + h))
    cout = pl.BlockSpec((CONV_K - 1, SEQ_BLOCK, w), lambda s, h: (0, s, h))
    row_scratch = pltpu.VMEM((SEQ_BLOCK, SUBLANES, w), F32)
    return pl.pallas_call(
        _gdn_decode_kernel,
        grid=(bsz // SEQ_BLOCK, nhg),
        in_specs=[
            seg(0), seg(1), seg(2), seg(3),
            pl.BlockSpec((SEQ_BLOCK, LANES), lambda s, h: (r0 + s, 1 + h)),
            cst(0), cst(1), cst(2), wseg(0), wseg(1), wseg(2),
            pl.BlockSpec((None, SUBLANES, LANES), lambda s, h: (h, 0, 0)),
            pl.BlockSpec((1, GDN_DV), lambda s, h: (0, 0)),
            st_spec,
            pl.BlockSpec(memory_space=pl.ANY),
        ] + extra_specs,
        out_specs=[
            pl.BlockSpec((SEQ_BLOCK, w), lambda s, h: (r0 + s, h)),
            cout, cout, cout,
            st_spec,
        ],
        out_shape=[
            jax.ShapeDtypeStruct(mix.shape, mix.dtype),
            jax.ShapeDtypeStruct((CONV_K - 1, bsz, GDN_QK), F32),
            jax.ShapeDtypeStruct((CONV_K - 1, bsz, GDN_QK), F32),
            jax.ShapeDtypeStruct((CONV_K - 1, bsz, GDN_V), F32),
            st_shape,
        ],
        scratch_shapes=[row_scratch] * 7,
        input_output_aliases={14: 0} if prev is None else {14: 0, 15: 4},
        compiler_params=_params("parallel", "arbitrary"),
        name="gdn_decode",
    )(zm, zm, zm, zm, zs, conv_t, conv_t, conv_t, conv_w, conv_w, conv_w, pc, norm[None, :], states, mix, *extra_in)


ML_HG = 4


def _mlstm_decode_kernel(q_ref, k_ref, v_ref, o_ref, zc_ref, g_ref, m_ref, n_ref, pc_ref, e_ref, nrm_ref, c_ref,
                         *rest):
    mix_ref, n_out, m_out, c_out, q_s, kb_s, v_s, sa_s, sbqk_s, den_s, h_s = rest[-11:]
    g = g_ref[...]
    logi = g + pc_ref[0:1, :]
    logf = -_softplus(-pltpu.roll(g + pc_ref[1:2, :], LANES - ML_HEADS, 1))
    m_prev = m_ref[...]
    m_new = jnp.maximum(logf + m_prev, logi)
    m_out[...] = m_new
    e_mat = e_ref[...]
    sa_x = _mm_hi(jnp.exp(logf + m_prev - m_new), e_mat)
    sb_x = _mm_hi(jnp.exp(logi - m_new), e_mat)
    em_x = _mm_hi(jnp.exp(-m_new), e_mat)
    q = q_ref[...]
    k = k_ref[...] * (ML_DK ** -0.5)
    n_prev = n_ref[...]
    n_out[...] = n_prev * sa_x + sb_x * k
    _stash_rows(q_s, q)
    _stash_rows(kb_s, sb_x * k)
    _stash_rows(v_s, v_ref[...])
    _stash_rows(sa_s, sa_x)
    heads = range(ML_HG)
    ks = [slice(j * ML_DK, (j + 1) * ML_DK) for j in heads]
    vs = [slice(j * ML_DV, (j + 1) * ML_DV) for j in heads]
    sbqk_x, den_x = [], []
    for j in heads:
        shape = (SEQ_BLOCK, ML_DK)
        qk = jnp.sum(q[:, ks[j]] * k[:, ks[j]], axis=-1, keepdims=True)
        qn = jnp.sum(q[:, ks[j]] * n_prev[:, ks[j]], axis=-1, keepdims=True)
        sbqk = sb_x[:, ks[j]] * jnp.broadcast_to(qk, shape)
        sbqk_x.append(sbqk)
        den_x.append(jnp.maximum(jnp.abs(sa_x[:, ks[j]] * jnp.broadcast_to(qn, shape) + sbqk), em_x[:, ks[j]]))
    _stash_rows(sbqk_s, jnp.concatenate(sbqk_x, axis=1))
    _stash_rows(den_s, jnp.concatenate(den_x, axis=1))

    def body(i, carry):
        q_b, kb_b, v_b, sa_b, sbqk_b, den_b = q_s[i], kb_s[i], v_s[i], sa_s[i], sbqk_s[i], den_s[i]
        c_prev = [c_ref[i, j] for j in heads]
        qc = [_mm(q_b[:, ks[j]], c_prev[j]) for j in heads]
        outer = [_mm_tn(_row0(kb_b[:, ks[j]]), _row0(v_b[:, vs[j]])) for j in heads]
        for j in heads:
            lane0 = slice(j * ML_DK, j * ML_DK + 1)
            c_out[i, j] = c_prev[j] * sa_b[0:1, lane0] + outer[j]
            num = sa_b[:, lane0] * qc[j] + sbqk_b[:, lane0] * v_b[:, vs[j]]
            h_s[i, :, vs[j]] = num / den_b[:, lane0]
        return carry

    lax.fori_loop(0, SEQ_BLOCK, body, 0)
    h_all = _gather_rows(h_s)
    for j in heads:
        h = h_all[:, vs[j]]
        h = h * lax.rsqrt(jnp.mean(h * h, axis=-1, keepdims=True) + EPS) * nrm_ref[...]
        mix_ref[:, vs[j]] = (h * jax.nn.sigmoid(o_ref[:, vs[j]]) * _silu(zc_ref[:, vs[j]])).astype(BF16)


def _mlstm_decode(zm, zs, mix, c_states, layer, prev, n0, m0, mp, i_bias, f_bias, norm):
    bsz = c_states.shape[1]
    st_spec, st_shape, extra_in, extra_specs = _stacked_state_io(
        c_states, layer, prev, (SEQ_BLOCK, ML_HG, ML_DK, ML_DV), lambda s, h: (s, h, 0, 0))
    nhg = ML_HEADS // ML_HG
    wk, wv = ML_HG * ML_DK, ML_HG * ML_DV
    r0 = mp // SEQ_BLOCK
    pc = jnp.zeros((SUBLANES, LANES), F32).at[0, :ML_HEADS].set(i_bias).at[1, ML_HEADS:2 * ML_HEADS].set(f_bias)
    e_mat = (jnp.arange(ML_QK)[None, :] // ML_DK == jnp.arange(LANES)[:, None]).astype(F32)
    zrow = lambda w, col: pl.BlockSpec((SEQ_BLOCK, w), lambda s, h: (r0 + s, col(h)))
    vcol = 2 * ML_QK // wv
    return pl.pallas_call(
        _mlstm_decode_kernel,
        grid=(bsz // SEQ_BLOCK, nhg),
        in_specs=[
            zrow(wk, lambda h: h), zrow(wk, lambda h: nhg + h),
            zrow(wv, lambda h: vcol + h), zrow(wv, lambda h: vcol + nhg + h), zrow(wv, lambda h: vcol + 2 * nhg + h),
            zrow(LANES, lambda h: 0),
            pl.BlockSpec((SEQ_BLOCK, LANES), lambda s, h: (s, 0)),
            pl.BlockSpec((SEQ_BLOCK, wk), lambda s, h: (s, h)),
            pl.BlockSpec((SUBLANES, LANES), lambda s, h: (0, 0)),
            pl.BlockSpec((LANES, wk), lambda s, h: (0, h)),
            pl.BlockSpec((1, ML_DV), lambda s, h: (0, 0)),
            st_spec,
            pl.BlockSpec(memory_space=pl.ANY),
        ] + extra_specs,
        out_specs=[
            pl.BlockSpec((SEQ_BLOCK, wv), lambda s, h: (r0 + s, h)),
            pl.BlockSpec((SEQ_BLOCK, wk), lambda s, h: (s, h)),
            pl.BlockSpec((SEQ_BLOCK, LANES), lambda s, h: (s, 0)),
            st_spec,
        ],
        out_shape=[
            jax.ShapeDtypeStruct(mix.shape, mix.dtype),
            jax.ShapeDtypeStruct((bsz, ML_QK), F32),
            jax.ShapeDtypeStruct((bsz, LANES), F32),
            st_shape,
        ],
        scratch_shapes=[pltpu.VMEM((SEQ_BLOCK, SUBLANES, w), F32) for w in (wk, wk, wv, wk, wk, wk, wv)],
        input_output_aliases={12: 0} if prev is None else {12: 0, 13: 3},
        compiler_params=_params("parallel", "arbitrary"),
        name="mlstm_decode",
    )(zm, zm, zm, zm, zm, zs, jnp.pad(m0, ((0, 0), (0, LANES - ML_HEADS))), n0.reshape(bsz, ML_QK), pc, e_mat,
      norm[None, :], c_states, mix, *extra_in)


def _cmlp_decode_kernel(u_ref, v_ref, z_ref, ws_ref, wb_ref, gain_ref, mixin_ref, mix_ref, vrows_ref):
    del mixin_ref
    for g in range(CM_GROUPS):
        gs = slice(g * CM_GROUP_DIM, (g + 1) * CM_GROUP_DIM)
        v = _gelu(v_ref[:, gs])
        v = v * lax.rsqrt(jnp.mean(v * v, axis=-1, keepdims=True) + EPS) * gain_ref[...]
        s = ws_ref[:, gs] * v + wb_ref[:, gs]
        mix_ref[:, gs] = (_gelu(u_ref[:, gs]) * s * _silu(z_ref[:, gs])).astype(BF16)
        vrows_ref[:, gs] = v


def _cmlp_decode(zm, mix, mp, bsz, v_gain, ws, wb):
    col0 = (2 * ML_QK + 3 * ML_V) // CM_WIDTH
    r0 = mp // bsz
    zrow = lambda col: pl.BlockSpec((bsz, CM_WIDTH), lambda i: (r0, col))
    par = pl.BlockSpec((1, CM_WIDTH), lambda i: (0, 0))
    return pl.pallas_call(
        _cmlp_decode_kernel,
        grid=(1,),
        in_specs=[zrow(col0), zrow(col0 + 1), zrow(col0 + 2), par, par,
                  pl.BlockSpec((1, CM_GROUP_DIM), lambda i: (0, 0)), pl.BlockSpec(memory_space=pl.ANY)],
        out_specs=[pl.BlockSpec((bsz, CM_WIDTH), lambda i: (r0, 0)), pl.BlockSpec((bsz, CM_WIDTH), lambda i: (0, 0))],
        out_shape=[jax.ShapeDtypeStruct(mix.shape, mix.dtype), jax.ShapeDtypeStruct((bsz, CM_WIDTH), F32)],
        input_output_aliases={6: 0},
        compiler_params=_params("arbitrary"),
        name="cmlp_decode",
    )(zm, zm, zm, jnp.repeat(ws[:, 0, 0], CM_GROUP_DIM)[None, :], jnp.repeat(wb[:, 0], CM_GROUP_DIM)[None, :],
      v_gain[None, :], mix)


def kernel(x_prompt, x_sample, state_ssd_conv, state_ssd, state_gdn_conv, state_gdn, state_mlstm_c,
           state_mlstm_n, state_mlstm_m, even_norm, even_w_in, ssd_conv_w, ssd_conv_b, ssd_dt_bias, ssd_a_log,
           ssd_d, ssd_norm, gdn_conv_w, gdn_dt_bias, gdn_a_log, gdn_norm, even_w_out, odd_norm, odd_w_in,
           mlstm_i_bias, mlstm_f_bias, mlstm_norm, cmlp_v_norm, cmlp_ws, cmlp_b, odd_w_out, final_norm):
    bp, seq, d = x_prompt.shape
    bs = x_sample.shape[0]
    mp = bp * seq
    mt = mp + bs
    x = jnp.concatenate([x_prompt.reshape(mp, d), x_sample.reshape(bs, d)], axis=0)

    keys = ("sc", "ss", "gc", "gs", "mc", "mn", "mm", "cv")
    outs_p = {k: [] for k in keys}
    outs_s = {k: [] for k in keys}
    ss_all = gs_all = mc_all = None
    even_wt_in = jnp.swapaxes(even_w_in, 1, 2)
    odd_wt_in = jnp.swapaxes(odd_w_in, 1, 2)
    for layer in range(DEPTH):
        i = layer // 2
        if layer % 2 == 0:
            w_main, w_small = _prep_even_w_in(even_wt_in, i)
            zm, zs = _inproj(x, even_norm[i], w_main, w_small, tn=1024)
            zs_t = zs[:mp].T
            mix_a, cx, cbc, st = _ssd_prompt(zm, zs, zs_t, mt, bp, seq, ssd_conv_w[i], ssd_conv_b[i],
                                             ssd_dt_bias[i], ssd_a_log[i], ssd_d[i], ssd_norm[i])
            mix_b, cq, ck, cv, gst = _gdn_prompt(zm, zs, zs_t, mt, bp, seq, gdn_conv_w[i], gdn_dt_bias[i],
                                                 gdn_a_log[i], gdn_norm[i])
            hpg = SSD_HEADS // SSD_GROUPS
            outs_p["sc"].append(jnp.concatenate([cx, cbc], axis=-1))
            outs_p["ss"].append(st.reshape(bp, SSD_GROUPS, SSD_STATE, hpg, SSD_HEAD_DIM).transpose(0, 1, 3, 4, 2)
                                .reshape(bp, SSD_HEADS, SSD_HEAD_DIM, SSD_STATE))
            outs_p["gc"].append(jnp.concatenate([cq, ck, cv], axis=-1))
            outs_p["gs"].append(gst)
            mix_a, ncx, ncb, ncc, ss_all = _ssd_decode(
                zm, zs, mix_a, jnp.swapaxes(state_ssd_conv[i], 0, 1), state_ssd, i, ss_all, mp, ssd_conv_w[i],
                ssd_conv_b[i], ssd_dt_bias[i], ssd_a_log[i], ssd_d[i], ssd_norm[i])
            mix_b, ncq, nck, ncv, gs_all = _gdn_decode(
                zm, zs, mix_b, jnp.swapaxes(state_gdn_conv[i], 0, 1), state_gdn, i, gs_all, mp, gdn_conv_w[i],
                gdn_dt_bias[i], gdn_a_log[i], gdn_norm[i])
            outs_s["sc"].append(jnp.swapaxes(jnp.concatenate([ncx, ncb, ncc], axis=-1), 0, 1))
            outs_s["gc"].append(jnp.swapaxes(jnp.concatenate([ncq, nck, ncv], axis=-1), 0, 1))
            w_out = even_w_out[i].astype(BF16)
        else:
            w_main, w_small = _prep_odd_w_in(odd_wt_in, i)
            zm, zs = _inproj(x, odd_norm[i], w_main, w_small, tn=1024)
            zs_t = zs[:mp].T
            mix_a, c_p, n_p, m_p = _mlstm_prompt(zm, zs, zs_t, mt, bp, seq, mlstm_i_bias[i], mlstm_f_bias[i],
                                                 mlstm_norm[i])
            mix_b, v_rows = _cmlp_prompt(zm, mt, bp, seq, cmlp_v_norm[i], cmlp_ws[i], cmlp_b[i])
            outs_p["mc"].append(c_p); outs_p["mn"].append(n_p); outs_p["mm"].append(m_p[:, :, 0])
            outs_p["cv"].append(v_rows)
            mix_a, n_s, m_s, mc_all = _mlstm_decode(zm, zs, mix_a, state_mlstm_c, i, mc_all, state_mlstm_n[i],
                                                    state_mlstm_m[i], mp, mlstm_i_bias[i], mlstm_f_bias[i],
                                                    mlstm_norm[i])
            mix_b, v_row_s = _cmlp_decode(zm, mix_b, mp, bs, cmlp_v_norm[i], cmlp_ws[i], cmlp_b[i])
            outs_s["mn"].append(n_s.reshape(bs, ML_HEADS, ML_DK))
            outs_s["mm"].append(m_s[:, :ML_HEADS])
            outs_s["cv"].append(v_row_s.reshape(bs, 1, CM_WIDTH))
            w_out = odd_w_out[i].astype(BF16)
        x = _outproj(x, mix_a, mix_b, w_out)

    y_p = _final_norm(x, final_norm, 0, mp, 512).reshape(bp, seq, d)
    y_s = _final_norm(x, final_norm, mp, bs, bs).reshape(bs, 1, d)
    st = lambda o, k: jnp.stack(o[k])
    return (y_p, y_s, st(outs_p, "sc"), st(outs_s, "sc"), st(outs_p, "ss"), ss_all,
            st(outs_p, "gc"), st(outs_s, "gc"), st(outs_p, "gs"), gs_all,
            st(outs_p, "mc"), mc_all, st(outs_p, "mn"), st(outs_s, "mn"),
            st(outs_p, "mm"), st(outs_s, "mm"), st(outs_p, "cv"), st(outs_s, "cv"))
```

```python
import jax
import jax.numpy as jnp
import numpy as np
from jax import lax
from jax.experimental import pallas as pl
from jax.experimental.pallas import tpu as pltpu

F32 = jnp.float32
BF16 = jnp.bfloat16
HI = lax.Precision.HIGHEST

D_MODEL = 2048
DEPTH = 4
CHUNK = 128
CONV_K = 4
EPS = 1e-6
NEG = -1e30

SSD_WIDTH = D_MODEL
SSD_HEAD_DIM = 64
SSD_HEADS = SSD_WIDTH // SSD_HEAD_DIM
SSD_STATE = 128
SSD_GROUPS = 4
SSD_GW = SSD_WIDTH // SSD_GROUPS
SSD_BC = 2 * SSD_GROUPS * SSD_STATE
SSD_CONV_DIM = SSD_WIDTH + SSD_BC
GDN_HEADS = 16
GDN_DK = 128
GDN_DV = 128
GDN_QK = GDN_HEADS * GDN_DK
GDN_V = GDN_HEADS * GDN_DV
GDN_CONV_DIM = 2 * GDN_QK + GDN_V
GDN_HG = 8
PROMPT_CPS = 2
ML_HEADS = 8
ML_DK = 128
ML_DV = 256
ML_QK = ML_HEADS * ML_DK
ML_V = ML_HEADS * ML_DV
CM_WIDTH = D_MODEL // 2
CM_GROUPS = 8
CM_GROUP_DIM = CM_WIDTH // CM_GROUPS
CM_CHUNK = 128

LANES = 128
SUBLANES = 8

EVEN_MAIN = 6 * D_MODEL + SSD_BC
EVEN_SMALL = 3 * LANES
ODD_MAIN = 2 * ML_QK + 3 * ML_V + 3 * CM_WIDTH
ODD_SMALL = LANES

VMEM_LIMIT = 56 * 1024 * 1024
ROW_TILE_CAP = 1040
BF16_SUBLANES = 16


def _row_tile(m):
    return max(t for t in range(BF16_SUBLANES, ROW_TILE_CAP + 1, BF16_SUBLANES) if m % t == 0)


def _silu(x):
    return x * jax.nn.sigmoid(x)


def _softplus(x):
    return jnp.maximum(x, 0.0) + jnp.log1p(jnp.exp(-jnp.abs(x)))


def _gelu(x):
    return 0.5 * x * (1.0 + jnp.tanh(np.sqrt(2.0 / np.pi).astype(np.float32) * (x + 0.044715 * (x * x * x))))


def _mm(a, b):
    return jnp.dot(a.astype(BF16), b.astype(BF16), preferred_element_type=F32)


def _mm_nt(a, b):
    return lax.dot_general(a.astype(BF16), b.astype(BF16), (((1,), (1,)), ((), ())), preferred_element_type=F32)


def _mm_tn(a, b):
    return lax.dot_general(a.astype(BF16), b.astype(BF16), (((0,), (0,)), ((), ())), preferred_element_type=F32)


def _mm_hi(a, b):
    return jnp.dot(a, b, precision=HI, preferred_element_type=F32)


def _split3(x):
    hi = x.astype(BF16).astype(F32)
    r1 = x - hi
    mid = r1.astype(BF16).astype(F32)
    lo = (r1 - mid).astype(BF16).astype(F32)
    return hi, mid, lo


def _mm_sel(a, sel):
    hi, mid, lo = _split3(a)
    return (_mm(hi, sel) + _mm(mid, sel)) + _mm(lo, sel)


def _iota(shape, axis):
    return lax.broadcasted_iota(jnp.int32, shape, axis)


def _params(*sem):
    return pltpu.CompilerParams(dimension_semantics=sem, vmem_limit_bytes=VMEM_LIMIT)


def _inproj_kernel(x_ref, g_ref, w_ref, ws_ref, z_ref, zs_ref, xn_ref):
    @pl.when(pl.program_id(1) == 0)
    def _():
        x = x_ref[...]
        y = x * lax.rsqrt(jnp.mean(x * x, axis=-1, keepdims=True) + EPS)
        xn = (y * g_ref[...]).astype(BF16)
        xn_ref[...] = xn
        zs_ref[...] = _mm_nt(xn, ws_ref[...])

    z_ref[...] = _mm_nt(xn_ref[...], w_ref[...])


def _inproj(x, g, w_main, w_small, tn):
    m, d = x.shape
    n = w_main.shape[0]
    ns = w_small.shape[0]
    tm = _row_tile(m)
    return pl.pallas_call(
        _inproj_kernel,
        grid=(m // tm, n // tn),
        in_specs=[
            pl.BlockSpec((tm, d), lambda i, j: (i, 0)),
            pl.BlockSpec((1, d), lambda i, j: (0, 0)),
            pl.BlockSpec((tn, d), lambda i, j: (j, 0)),
            pl.BlockSpec((ns, d), lambda i, j: (0, 0)),
        ],
        out_specs=[
            pl.BlockSpec((tm, tn), lambda i, j: (i, j)),
            pl.BlockSpec((tm, ns), lambda i, j: (i, 0)),
        ],
        out_shape=[jax.ShapeDtypeStruct((m, n), F32), jax.ShapeDtypeStruct((m, ns), F32)],
        scratch_shapes=[pltpu.VMEM((tm, d), BF16)],
        compiler_params=_params("parallel", "arbitrary"),
        name="inproj",
    )(x, g.reshape(1, d), w_main, w_small)


def _outproj_kernel(x_ref, ma_ref, mb_ref, wa_ref, wb_ref, o_ref):
    o_ref[...] = (x_ref[...] + jnp.dot(ma_ref[...], wa_ref[...], preferred_element_type=F32)
                  + jnp.dot(mb_ref[...], wb_ref[...], preferred_element_type=F32))


def _outproj(x, mix_a, mix_b, w):
    m, d = x.shape
    ka, kb = mix_a.shape[1], mix_b.shape[1]
    tm, tn = _row_tile(m), 512
    return pl.pallas_call(
        _outproj_kernel,
        grid=(m // tm, d // tn),
        in_specs=[
            pl.BlockSpec((tm, tn), lambda i, j: (i, j)),
            pl.BlockSpec((tm, ka), lambda i, j: (i, 0)),
            pl.BlockSpec((tm, kb), lambda i, j: (i, 0)),
            pl.BlockSpec((ka, tn), lambda i, j: (0, j)),
            pl.BlockSpec((kb, tn), lambda i, j: (ka // kb, j)),
        ],
        out_specs=pl.BlockSpec((tm, tn), lambda i, j: (i, j)),
        out_shape=jax.ShapeDtypeStruct((m, d), F32),
        compiler_params=_params("parallel", "arbitrary"),
        name="outproj",
    )(x, mix_a, mix_b, w, w)


def _final_norm_kernel(x_ref, g_ref, o_ref):
    x = x_ref[...]
    o_ref[...] = x * lax.rsqrt(jnp.mean(x * x, axis=-1, keepdims=True) + EPS) * g_ref[...]


def _final_norm(x, g, row0, rows, tm):
    d = x.shape[1]
    return pl.pallas_call(
        _final_norm_kernel,
        grid=(rows // tm,),
        in_specs=[pl.BlockSpec((tm, d), lambda i: (row0 // tm + i, 0)), pl.BlockSpec((1, d), lambda i: (0, 0))],
        out_specs=pl.BlockSpec((tm, d), lambda i: (i, 0)),
        out_shape=jax.ShapeDtypeStruct((rows, d), F32),
        compiler_params=_params("parallel"),
        name="final_norm",
    )(x, g.reshape(1, d))


REPACK_TN = 512


def _repack_kernel(a_ref, b_ref, o_ref, *, shift, lo, hi):
    j = pl.program_id(0)
    shifted = (j >= lo) & (j < hi)

    @pl.when(shifted)
    def _():
        o_ref[...] = jnp.concatenate([a_ref[shift:, :], b_ref[...]], axis=0).astype(BF16)

    @pl.when(jnp.logical_not(shifted))
    def _():
        o_ref[...] = a_ref[...].astype(BF16)


def _repack(wt_all, layer, n_out, a_idx, shift, lo, hi):
    d = wt_all.shape[2]
    per = REPACK_TN // shift
    kern = lambda a, b, o: _repack_kernel(a, b, o, shift=shift, lo=lo, hi=hi)
    return pl.pallas_call(
        kern,
        grid=(n_out // REPACK_TN,),
        in_specs=[pl.BlockSpec((None, REPACK_TN, d), lambda j: (layer, a_idx(j), 0)),
                  pl.BlockSpec((None, shift, d), lambda j: (layer, (a_idx(j) + 1) * per, 0))],
        out_specs=pl.BlockSpec((REPACK_TN, d), lambda j: (j, 0)),
        out_shape=jax.ShapeDtypeStruct((n_out, d), BF16),
        compiler_params=_params("parallel"),
        name="repack",
    )(wt_all, wt_all)


def _small_pack_kernel(*refs, layout):
    pieces, o_ref = refs[:-1], refs[-1]
    d = o_ref.shape[1]
    for blk, idxs in enumerate(layout):
        rows = [pieces[i][...] for i in idxs] + [jnp.zeros((LANES - SUBLANES * len(idxs), d), F32)]
        o_ref[blk * LANES:(blk + 1) * LANES, :] = jnp.concatenate(rows, axis=0).astype(BF16)


def _small_pack(wt_all, layer, src_rows, layout):
    d = wt_all.shape[2]
    kern = lambda *refs: _small_pack_kernel(*refs, layout=layout)
    spec = lambda r: pl.BlockSpec((None, SUBLANES, d), lambda i: (layer, r // SUBLANES, 0))
    return pl.pallas_call(
        kern,
        grid=(1,),
        in_specs=[spec(r) for r in src_rows],
        out_specs=pl.BlockSpec((len(layout) * LANES, d), lambda i: (0, 0)),
        out_shape=jax.ShapeDtypeStruct((len(layout) * LANES, d), BF16),
        compiler_params=_params("arbitrary"),
        name="small_pack",
    )(*([wt_all] * len(src_rows)))


def _prep_even_w_in(wt_all, layer):
    o_bc = 2 * SSD_WIDTH
    o_dt = o_bc + SSD_BC
    o_q = o_dt + SSD_HEADS
    o_beta = o_q + GDN_CONV_DIM + GDN_V
    o_g = o_beta + GDN_HEADS
    n1, n2 = o_bc // REPACK_TN, (o_bc + o_beta - o_q) // REPACK_TN
    src2, src3 = o_dt // REPACK_TN, o_bc // REPACK_TN
    a_idx = lambda j: jnp.where(j < n1, j, jnp.where(j < n2, j - n1 + src2, j - n2 + src3))
    main = _repack(wt_all, layer, EVEN_MAIN, a_idx, o_q - o_dt, n1, n2)
    n_dt = SSD_HEADS // SUBLANES
    src = [o_dt + SUBLANES * i for i in range(n_dt)]
    layout = [tuple(range(n_dt))]
    for hg in range(GDN_HEADS // GDN_HG):
        src += [o_beta + hg * GDN_HG, o_g + hg * GDN_HG]
        layout.append((len(src) - 2, len(src) - 1))
    return main, _small_pack(wt_all, layer, src, layout)


def _prep_odd_w_in(wt_all, layer):
    o1 = 2 * ML_QK + 3 * ML_V
    o2 = o1 + 2 * ML_HEADS
    main = _repack(wt_all, layer, ODD_MAIN, lambda j: j, o2 - o1, o1 // REPACK_TN, ODD_MAIN // REPACK_TN)
    small = _small_pack(wt_all, layer, [o1, o1 + ML_HEADS], [(0, 1)])
    return main, small


def _conv_chunk(ext_ref, u, w_ref, first):
    t = u.shape[0]

    @pl.when(first)
    def _():
        ext_ref[0:SUBLANES, :] = jnp.zeros((SUBLANES, ext_ref.shape[1]), F32)

    prev = ext_ref[0:SUBLANES, :]
    rid = _iota((SUBLANES, u.shape[1]), 0)
    out = w_ref[CONV_K - 1:CONV_K, :] * u
    for k in range(1, CONV_K):
        rolled = pltpu.roll(u, k, 0)
        head = jnp.where(rid < k, pltpu.roll(prev, k, 0), rolled[0:SUBLANES, :])
        shifted = jnp.concatenate([head, rolled[SUBLANES:, :]], axis=0)
        out = out + w_ref[CONV_K - 1 - k:CONV_K - k, :] * shifted
    ext_ref[0:SUBLANES, :] = u[t - SUBLANES:, :]
    return out


def _conv_tail(ext_ref):
    return ext_ref[SUBLANES - (CONV_K - 1):SUBLANES, :]


def _causal_masks():
    r = _iota((CHUNK, CHUNK), 0)
    c = _iota((CHUNK, CHUNK), 1)
    return r, c


def _ssd_prompt_kernel(za_ref, xs_ref, bc_ref, zc_ref, zr_ref, wx_ref, bx_ref, wbc_ref, bbc_ref, pc_ref, pr_ref,
                       d_ref, nrm_ref, mix_ref, cx_ref, cbc_ref, st_ref, extx, extbc, s_ref):
    c_id = pl.program_id(1)
    first = c_id == 0
    last = c_id == pl.num_programs(1) - 1

    @pl.when(first)
    def _():
        s_ref[...] = jnp.zeros(s_ref.shape, F32)

    xs = _silu(_conv_chunk(extx, xs_ref[...], wx_ref, first) + bx_ref[...])
    bc = _silu(_conv_chunk(extbc, bc_ref[...], wbc_ref, first) + bbc_ref[...])

    r, c = _causal_masks()
    causal = r >= c
    tril = jnp.where(causal, 1.0, 0.0)
    triu = jnp.where(r <= c, 1.0, 0.0)
    dt = _softplus(zc_ref[...] + pc_ref[0:1, :])
    la = _mm_hi(tril, dt * (-jnp.exp(pc_ref[1:2, :])))
    dtr = _softplus(zr_ref[...] + pr_ref[0])
    lar = _mm_hi(dtr * (-jnp.exp(pr_ref[1])), triu)
    la_last = la[CHUNK - 1:CHUNK, :]
    e_mat = jnp.where((_iota((LANES, SSD_WIDTH), 1) >> 6) == _iota((LANES, SSD_WIDTH), 0), 1.0, 0.0)
    ela_x = _mm_sel(jnp.exp(la), e_mat)
    wsx = _mm_sel(jnp.exp(la_last - la) * dt, e_mat)
    dec_x = _mm_sel(jnp.broadcast_to(jnp.exp(la_last), (SUBLANES, LANES)), e_mat)[0:1, :]
    lane_lo = _iota((CHUNK, LANES), 1) < SSD_HEAD_DIM

    hpg = SSD_HEADS // SSD_GROUPS
    for g in range(SSD_GROUPS):
        gs = slice(g * SSD_GW, (g + 1) * SSD_GW)
        bg = bc[:, g * SSD_STATE:(g + 1) * SSD_STATE]
        cg = bc[:, SSD_GROUPS * SSD_STATE + g * SSD_STATE:SSD_GROUPS * SSD_STATE + (g + 1) * SSD_STATE]
        cb = _mm_nt(cg, bg)
        ys = []
        for pair in range(hpg // 2):
            h0 = g * hpg + 2 * pair
            xpair = xs[:, h0 * SSD_HEAD_DIM:(h0 + 2) * SSD_HEAD_DIM]
            halves = []
            for hh in (h0, h0 + 1):
                seg = jnp.where(causal, la[:, hh:hh + 1] - lar[hh:hh + 1, :], NEG)
                lmat = jnp.exp(seg) * cb * dtr[hh:hh + 1, :]
                halves.append(_mm(lmat, xpair))
            ys.append(jnp.where(lane_lo, halves[0], halves[1]))
        y = jnp.concatenate(ys, axis=1)
        s_prev = s_ref[g]
        y = y + _mm(cg, s_prev) * ela_x[:, gs] + d_ref[:, gs] * xs[:, gs]
        y = y * _silu(za_ref[:, gs])
        y = y * lax.rsqrt(jnp.mean(y * y, axis=-1, keepdims=True) + EPS) * nrm_ref[:, gs]
        mix_ref[:, gs] = y.astype(BF16)
        s_ref[g] = s_prev * dec_x[:, gs] + _mm_tn(bg, xs[:, gs] * wsx[:, gs])

    @pl.when(last)
    def _():
        st_ref[0] = s_ref[...]
        cx_ref[0] = _conv_tail(extx)
        cbc_ref[0] = _conv_tail(extbc)


def _ssd_prompt(zm, zs, zs_t, m_total, bsz, seq, conv_w, conv_b, dt_bias, a_log, d_skip, norm):
    nc = seq // CHUNK
    rb = lambda b, c: b * nc + c
    pc = jnp.zeros((SUBLANES, LANES), F32).at[0, :SSD_HEADS].set(dt_bias).at[1, :SSD_HEADS].set(a_log)
    pr = jnp.stack([jnp.broadcast_to(dt_bias[:, None], (SSD_HEADS, CHUNK)),
                    jnp.broadcast_to(a_log[:, None], (SSD_HEADS, CHUNK))])
    full = lambda *shape: pl.BlockSpec(shape, lambda b, c: (0,) * len(shape))
    return pl.pallas_call(
        _ssd_prompt_kernel,
        grid=(bsz, nc),
        in_specs=[
            pl.BlockSpec((CHUNK, SSD_WIDTH), lambda b, c: (rb(b, c), 0)),
            pl.BlockSpec((CHUNK, SSD_WIDTH), lambda b, c: (rb(b, c), 1)),
            pl.BlockSpec((CHUNK, SSD_BC), lambda b, c: (rb(b, c), 6 * D_MODEL // SSD_BC)),
            pl.BlockSpec((CHUNK, LANES), lambda b, c: (rb(b, c), 0)),
            pl.BlockSpec((SSD_HEADS, CHUNK), lambda b, c: (0, rb(b, c))),
            full(CONV_K, SSD_WIDTH), full(1, SSD_WIDTH), full(CONV_K, SSD_BC), full(1, SSD_BC),
            full(SUBLANES, LANES), full(2, SSD_HEADS, CHUNK), full(1, SSD_WIDTH), full(1, SSD_WIDTH),
        ],
        out_specs=[
            pl.BlockSpec((CHUNK, SSD_WIDTH), lambda b, c: (rb(b, c), 0)),
            pl.BlockSpec((1, CONV_K - 1, SSD_WIDTH), lambda b, c: (b, 0, 0)),
            pl.BlockSpec((1, CONV_K - 1, SSD_BC), lambda b, c: (b, 0, 0)),
            pl.BlockSpec((1, SSD_GROUPS, SSD_STATE, SSD_GW), lambda b, c: (b, 0, 0, 0)),
        ],
        out_shape=[
            jax.ShapeDtypeStruct((m_total, SSD_WIDTH), BF16),
            jax.ShapeDtypeStruct((bsz, CONV_K - 1, SSD_WIDTH), F32),
            jax.ShapeDtypeStruct((bsz, CONV_K - 1, SSD_BC), F32),
            jax.ShapeDtypeStruct((bsz, SSD_GROUPS, SSD_STATE, SSD_GW), F32),
        ],
        scratch_shapes=[pltpu.VMEM((SUBLANES, SSD_WIDTH), F32), pltpu.VMEM((SUBLANES, SSD_BC), F32),
                        pltpu.VMEM((SSD_GROUPS, SSD_STATE, SSD_GW), F32)],
        compiler_params=_params("parallel", "arbitrary"),
        name="ssd_prompt",
    )(zm, zm, zm, zs, zs_t, conv_w[:, :SSD_WIDTH], conv_b[None, :SSD_WIDTH], conv_w[:, SSD_WIDTH:],
      conv_b[None, SSD_WIDTH:], pc, pr, jnp.repeat(d_skip, SSD_HEAD_DIM)[None, :], norm[None, :])


def _tri_inverse(mats, r, c):
    def corner(level):
        return ((r >> (level + 1)) == (c >> (level + 1))) & (((r >> level) & 1) == 1) & (((c >> level) & 1) == 0)

    eye = jnp.where(r == c, 1.0, 0.0)
    ts = [eye - jnp.where(corner(0), a, 0.0) for a in mats]
    for level in range(1, 7):
        cm = corner(level)
        xs = [_mm(t, jnp.where(cm, a, 0.0)) for t, a in zip(ts, mats)]
        ts = [t - _mm(x, t) for t, x in zip(ts, xs)]
    return ts


def _gdn_prompt_kernel(q_ref, k_ref, v_ref, zb_ref, zc_ref, zr_ref, wq_ref, wk_ref, wv_ref, pc_ref, pr_ref, nrm_ref,
                       mix_ref, cq_ref, ck_ref, cv_ref, st_ref, extq, extk, extv, s_ref):
    c_id = pl.program_id(2)
    first = c_id == 0
    last = c_id == pl.num_programs(2) - 1

    @pl.when(first)
    def _():
        s_ref[...] = jnp.zeros(s_ref.shape, F32)

    q_all = _silu(_conv_chunk(extq, q_ref[...], wq_ref, first))
    k_all = _silu(_conv_chunk(extk, k_ref[...], wk_ref, first))
    v_all = _silu(_conv_chunk(extv, v_ref[...], wv_ref, first))

    r, c = _causal_masks()
    causal = r >= c
    strict = r > c
    tril = jnp.where(causal, 1.0, 0.0)
    triu = jnp.where(r <= c, 1.0, 0.0)
    chunks = range(PROMPT_CPS)
    rows = [slice(ch * CHUNK, (ch + 1) * CHUNK) for ch in chunks]
    zc = zc_ref[...]
    beta_c = jax.nn.sigmoid(zc)
    g_c = -jnp.exp(pc_ref[0:1, :]) * _softplus(zc + pc_ref[1:2, :])
    gc_c = [_mm_hi(tril, g_c[rw, :]) for rw in rows]
    g_r = -jnp.exp(pr_ref[0]) * _softplus(zr_ref[GDN_HG:2 * GDN_HG, :] + pr_ref[1])
    gc_r = [_mm_hi(g_r[:, rw], triu) for rw in rows]

    heads = range(GDN_HG)
    hs = [slice(j * GDN_DK, (j + 1) * GDN_DK) for j in heads]
    units = [(ch, j) for ch in chunks for j in heads]
    qh = [q_all[rows[ch], hs[j]] for ch, j in units]
    kh = [k_all[rows[ch], hs[j]] for ch, j in units]
    qh = [x * (lax.rsqrt(jnp.sum(x * x, axis=-1, keepdims=True) + EPS) * (GDN_DK ** -0.5)) for x in qh]
    kh = [x * lax.rsqrt(jnp.sum(x * x, axis=-1, keepdims=True) + EPS) for x in kh]
    gcc = [gc_c[ch][:, GDN_HG + j:GDN_HG + j + 1] for ch, j in units]
    beta = [beta_c[rows[ch], j:j + 1] for ch, j in units]
    gam = [jnp.exp(jnp.where(causal, gcc[u] - gc_r[ch][j:j + 1, :], NEG)) for u, (ch, j) in enumerate(units)]
    qkk = [_mm_nt(jnp.concatenate([qh[u], kh[u]], axis=0), kh[u]) for u in range(len(units))]
    aqk = [qkk[u][:CHUNK] * gam[u] for u in range(len(units))]
    tinv = _tri_inverse([jnp.where(strict, beta[u] * qkk[u][CHUNK:] * gam[u], 0.0) for u in range(len(units))], r, c)
    egc = [jnp.exp(g) for g in gcc]
    uw = [_mm(tinv[u], jnp.concatenate([beta[u] * v_all[rows[ch], hs[j]], (beta[u] * egc[u]) * kh[u]], axis=1))
          for u, (ch, j) in enumerate(units)]
    s_cur = [s_ref[j] for j in heads]
    for ch in chunks:
        us = [ch * GDN_HG + j for j in heads]
        ws_qs = [_mm(jnp.concatenate([uw[u][:, GDN_DV:], qh[u] * egc[u]], axis=0), s_cur[j]) for j, u in enumerate(us)]
        vn = [uw[u][:, :GDN_DV] - ws_qs[j][:CHUNK] for j, u in enumerate(us)]
        o = [ws_qs[j][CHUNK:] + _mm(aqk[u], vn[j]) for j, u in enumerate(us)]
        nxt = []
        for j, u in enumerate(us):
            gc_last = gcc[u][CHUNK - 1:CHUNK, :]
            nxt.append(s_cur[j] * jnp.exp(gc_last) + _mm_tn(kh[u] * jnp.exp(gc_last - gcc[u]), vn[j]))
        s_cur = nxt
        for j in heads:
            on = o[j] * lax.rsqrt(jnp.mean(o[j] * o[j], axis=-1, keepdims=True) + EPS) * nrm_ref[...]
            mix_ref[rows[ch], hs[j]] = (on * _silu(zb_ref[rows[ch], hs[j]])).astype(BF16)
    for j in heads:
        s_ref[j] = s_cur[j]

    @pl.when(last)
    def _():
        st_ref[0] = s_ref[...]
        cq_ref[0] = _conv_tail(extq)
        ck_ref[0] = _conv_tail(extk)
        cv_ref[0] = _conv_tail(extv)


def _gdn_prompt(zm, zs, zs_t, m_total, bsz, seq, conv_w, dt_bias, a_log, norm):
    tt = PROMPT_CPS * CHUNK
    nc = seq // tt
    nhg = GDN_HEADS // GDN_HG
    w = GDN_HG * GDN_DK
    rb = lambda b, c: b * nc + c
    col0 = 2 * D_MODEL // w
    pc = jnp.zeros((nhg, SUBLANES, LANES), F32)
    pc = pc.at[:, 0, GDN_HG:2 * GDN_HG].set(a_log.reshape(nhg, GDN_HG))
    pc = pc.at[:, 1, GDN_HG:2 * GDN_HG].set(dt_bias.reshape(nhg, GDN_HG))
    pr = jnp.stack([jnp.broadcast_to(a_log.reshape(nhg, GDN_HG, 1), (nhg, GDN_HG, tt)),
                    jnp.broadcast_to(dt_bias.reshape(nhg, GDN_HG, 1), (nhg, GDN_HG, tt))], axis=1)
    seg = lambda s: pl.BlockSpec((tt, w), lambda b, h, c: (rb(b, c), col0 + s * nhg + h))
    wseg = lambda s: pl.BlockSpec((CONV_K, w), lambda b, h, c: (0, s * nhg + h))
    cout = pl.BlockSpec((1, CONV_K - 1, w), lambda b, h, c: (b, 0, h))
    return pl.pallas_call(
        _gdn_prompt_kernel,
        grid=(bsz, nhg, nc),
        in_specs=[
            seg(0), seg(1), seg(2), seg(3),
            pl.BlockSpec((tt, LANES), lambda b, h, c: (rb(b, c), 1 + h)),
            pl.BlockSpec((2 * GDN_HG, tt), lambda b, h, c: ((1 + h) * LANES // (2 * GDN_HG), rb(b, c))),
            wseg(0), wseg(1), wseg(2),
            pl.BlockSpec((None, SUBLANES, LANES), lambda b, h, c: (h, 0, 0)),
            pl.BlockSpec((None, 2, GDN_HG, tt), lambda b, h, c: (h, 0, 0, 0)),
            pl.BlockSpec((1, GDN_DV), lambda b, h, c: (0, 0)),
        ],
        out_specs=[
            pl.BlockSpec((tt, w), lambda b, h, c: (rb(b, c), h)),
            cout, cout, cout,
            pl.BlockSpec((1, GDN_HG, GDN_DK, GDN_DV), lambda b, h, c: (b, h, 0, 0)),
        ],
        out_shape=[
            jax.ShapeDtypeStruct((m_total, GDN_V), BF16),
            jax.ShapeDtypeStruct((bsz, CONV_K - 1, GDN_QK), F32),
            jax.ShapeDtypeStruct((bsz, CONV_K - 1, GDN_QK), F32),
            jax.ShapeDtypeStruct((bsz, CONV_K - 1, GDN_V), F32),
            jax.ShapeDtypeStruct((bsz, GDN_HEADS, GDN_DK, GDN_DV), F32),
        ],
        scratch_shapes=[pltpu.VMEM((SUBLANES, w), F32)] * 3 + [pltpu.VMEM((GDN_HG, GDN_DK, GDN_DV), F32)],
        compiler_params=_params("parallel", "parallel", "arbitrary"),
        name="gdn_prompt",
    )(zm, zm, zm, zm, zs, zs_t, conv_w, conv_w, conv_w, pc, pr, norm[None, :])


def _mlstm_prompt_kernel(q_ref, k_ref, v_ref, o_ref, zc_ref, gc_ref, gr_ref, pc_ref, pr_ref, nrm_ref,
                         mix_ref, c_out, n_out, m_out, c_ref, n_ref, m_ref):
    c_id = pl.program_id(1)
    first = c_id == 0
    last = c_id == pl.num_programs(1) - 1

    @pl.when(first)
    def _():
        c_ref[...] = jnp.zeros(c_ref.shape, F32)
        n_ref[...] = jnp.zeros(n_ref.shape, F32)
        m_ref[...] = jnp.zeros(m_ref.shape, F32)

    r, c = _causal_masks()
    causal = r >= c
    tril = jnp.where(causal, 1.0, 0.0)
    triu = jnp.where(r <= c, 1.0, 0.0)
    chunks = range(PROMPT_CPS)
    rows = [slice(ch * CHUNK, (ch + 1) * CHUNK) for ch in chunks]
    gc = gc_ref[...]
    logi_c = gc + pc_ref[0:1, :]
    logf_c = -_softplus(-(gc + pc_ref[1:2, :]))
    b_c = [_mm_hi(tril, logf_c[rw, :]) for rw in rows]
    logi_r = gr_ref[0:ML_HEADS, :] + pr_ref[0]
    logf_r = -_softplus(-(gr_ref[ML_HEADS:2 * ML_HEADS, :] + pr_ref[1]))
    b_r = [_mm_hi(logf_r[:, rw], triu) for rw in rows]

    heads = range(ML_HEADS)
    ks = [slice(j * ML_DK, (j + 1) * ML_DK) for j in heads]
    vs = [slice(j * ML_DV, (j + 1) * ML_DV) for j in heads]
    units = [(ch, j) for ch in chunks for j in heads]
    nu = range(len(units))
    q = [q_ref[rows[ch], ks[j]] for ch, j in units]
    k = [k_ref[rows[ch], ks[j]] * (ML_DK ** -0.5) for ch, j in units]
    v = [v_ref[rows[ch], vs[j]] for ch, j in units]
    bc = [b_c[ch][:, ML_HEADS + j:ML_HEADS + j + 1] for ch, j in units]
    dmat = [jnp.where(causal, bc[u] - b_r[ch][j:j + 1, :] + logi_r[j:j + 1, rows[ch]], NEG)
            for u, (ch, j) in enumerate(units)]
    m_intra = [jnp.max(x, axis=-1, keepdims=True) for x in dmat]
    p = [_mm_nt(q[u], k[u]) * jnp.exp(dmat[u] - m_intra[u]) for u in nu]
    h_intra = [_mm(p[u], v[u]) for u in nu]
    n_intra = [jnp.sum(p[u], axis=-1, keepdims=True) for u in nu]
    b_last = [x[CHUNK - 1:CHUNK, :] for x in bc]
    gk = [b_last[u] - bc[u] + logi_c[rows[ch], j:j + 1] for u, (ch, j) in enumerate(units)]
    m_k = [jnp.max(x, axis=0, keepdims=True) for x in gk]
    kw = [k[u] * jnp.exp(gk[u] - m_k[u]) for u in nu]
    c_loc = [_mm_tn(kw[u], v[u]) for u in nu]
    n_loc = [jnp.sum(kw[u], axis=0, keepdims=True) for u in nu]
    c_cur = [c_ref[j] for j in heads]
    n_cur = [n_ref[j:j + 1, :] for j in heads]
    m_cur = [m_ref[j:j + 1, 0:1] for j in heads]
    for ch in chunks:
        us = [ch * ML_HEADS + j for j in heads]
        qc = [_mm(q[u], c_cur[j]) for j, u in enumerate(us)]
        for j, u in enumerate(us):
            mb = bc[u] + m_cur[j]
            m_t = jnp.maximum(mb, m_intra[u])
            s_inter = jnp.exp(mb - m_t)
            s_intra = jnp.exp(m_intra[u] - m_t)
            num = s_inter * qc[j] + s_intra * h_intra[u]
            den = s_inter * jnp.sum(q[u] * n_cur[j], axis=-1, keepdims=True) + s_intra * n_intra[u]
            h = num / jnp.maximum(jnp.abs(den), jnp.exp(-m_t))
            m_new = jnp.maximum(b_last[u] + m_cur[j], m_k[u])
            sa = jnp.exp(b_last[u] + m_cur[j] - m_new)
            sb = jnp.exp(m_k[u] - m_new)
            c_cur[j] = c_cur[j] * sa + c_loc[u] * sb
            n_cur[j] = n_cur[j] * sa + n_loc[u] * sb
            m_cur[j] = m_new
            h = h * lax.rsqrt(jnp.mean(h * h, axis=-1, keepdims=True) + EPS) * nrm_ref[...]
            mix_ref[rows[ch], vs[j]] = (h * jax.nn.sigmoid(o_ref[rows[ch], vs[j]])
                                        * _silu(zc_ref[rows[ch], vs[j]])).astype(BF16)
    for j in heads:
        c_ref[j] = c_cur[j]
        n_ref[j:j + 1, :] = n_cur[j]
        m_ref[j:j + 1, :] = jnp.broadcast_to(m_cur[j], (1, LANES))

    @pl.when(last)
    def _():
        c_out[0] = c_ref[...]
        n_out[0] = n_ref[...]
        m_out[0] = m_ref[...]


def _mlstm_prompt(zm, zs, zs_t, m_total, bsz, seq, i_bias, f_bias, norm):
    tt = PROMPT_CPS * CHUNK
    nc = seq // tt
    rb = lambda b, c: b * nc + c
    pc = jnp.zeros((SUBLANES, LANES), F32).at[0, :ML_HEADS].set(i_bias).at[1, ML_HEADS:2 * ML_HEADS].set(f_bias)
    pr = jnp.stack([jnp.broadcast_to(i_bias[:, None], (ML_HEADS, tt)),
                    jnp.broadcast_to(f_bias[:, None], (ML_HEADS, tt))])
    full = lambda *shape: pl.BlockSpec(shape, lambda b, c: (0,) * len(shape))
    return pl.pallas_call(
        _mlstm_prompt_kernel,
        grid=(bsz, nc),
        in_specs=[
            pl.BlockSpec((tt, ML_QK), lambda b, c: (rb(b, c), 0)),
            pl.BlockSpec((tt, ML_QK), lambda b, c: (rb(b, c), 1)),
            pl.BlockSpec((tt, ML_V), lambda b, c: (rb(b, c), 1)),
            pl.BlockSpec((tt, ML_V), lambda b, c: (rb(b, c), 2)),
            pl.BlockSpec((tt, ML_V), lambda b, c: (rb(b, c), 3)),
            pl.BlockSpec((tt, LANES), lambda b, c: (rb(b, c), 0)),
            pl.BlockSpec((2 * ML_HEADS, tt), lambda b, c: (0, rb(b, c))),
            full(SUBLANES, LANES), full(2, ML_HEADS, tt), full(1, ML_DV),
        ],
        out_specs=[
            pl.BlockSpec((tt, ML_V), lambda b, c: (rb(b, c), 0)),
            pl.BlockSpec((1, ML_HEADS, ML_DK, ML_DV), lambda b, c: (b, 0, 0, 0)),
            pl.BlockSpec((1, ML_HEADS, ML_DK), lambda b, c: (b, 0, 0)),
            pl.BlockSpec((1, ML_HEADS, LANES), lambda b, c: (b, 0, 0)),
        ],
        out_shape=[
            jax.ShapeDtypeStruct((m_total, ML_V), BF16),
            jax.ShapeDtypeStruct((bsz, ML_HEADS, ML_DK, ML_DV), F32),
            jax.ShapeDtypeStruct((bsz, ML_HEADS, ML_DK), F32),
            jax.ShapeDtypeStruct((bsz, ML_HEADS, LANES), F32),
        ],
        scratch_shapes=[pltpu.VMEM((ML_HEADS, ML_DK, ML_DV), F32), pltpu.VMEM((ML_HEADS, ML_DK), F32),
                        pltpu.VMEM((ML_HEADS, LANES), F32)],
        compiler_params=_params("parallel", "arbitrary"),
        name="mlstm_prompt",
    )(zm, zm, zm, zm, zm, zs, zs_t, pc, pr, norm[None, :])


def _cmlp_prompt_kernel(u_ref, v_ref, z_ref, ws_ref, wb_ref, gain_ref, mix_ref, vrows_ref):
    r, c = _causal_masks()
    causal = r >= c
    for g in range(CM_GROUPS):
        gs = slice(g * CM_GROUP_DIM, (g + 1) * CM_GROUP_DIM)
        v = _gelu(v_ref[:, gs])
        v = v * lax.rsqrt(jnp.mean(v * v, axis=-1, keepdims=True) + EPS) * gain_ref[...]
        s = _mm(jnp.where(causal, ws_ref[g], 0.0), v) + wb_ref[:, g:g + 1]
        mix_ref[:, gs] = (_gelu(u_ref[:, gs]) * s * _silu(z_ref[:, gs])).astype(BF16)
        vrows_ref[0, :, gs] = v


def _cmlp_prompt(zm, m_total, bsz, seq, v_gain, ws, wb):
    nc = seq // CM_CHUNK
    rb = lambda b, c: b * nc + c
    col0 = (2 * ML_QK + 3 * ML_V) // CM_WIDTH
    return pl.pallas_call(
        _cmlp_prompt_kernel,
        grid=(bsz, nc),
        in_specs=[
            pl.BlockSpec((CM_CHUNK, CM_WIDTH), lambda b, c: (rb(b, c), col0)),
            pl.BlockSpec((CM_CHUNK, CM_WIDTH), lambda b, c: (rb(b, c), col0 + 1)),
            pl.BlockSpec((CM_CHUNK, CM_WIDTH), lambda b, c: (rb(b, c), col0 + 2)),
            pl.BlockSpec((CM_GROUPS, CM_CHUNK, CM_CHUNK), lambda b, c: (0, 0, 0)),
            pl.BlockSpec((CM_CHUNK, CM_GROUPS), lambda b, c: (0, 0)),
            pl.BlockSpec((1, CM_GROUP_DIM), lambda b, c: (0, 0)),
        ],
        out_specs=[
            pl.BlockSpec((CM_CHUNK, CM_WIDTH), lambda b, c: (rb(b, c), 0)),
            pl.BlockSpec((1, CM_CHUNK, CM_WIDTH), lambda b, c: (b, 0, 0)),
        ],
        out_shape=[jax.ShapeDtypeStruct((m_total, CM_WIDTH), BF16),
                   jax.ShapeDtypeStruct((bsz, CM_CHUNK, CM_WIDTH), F32)],
        compiler_params=_params("parallel", "arbitrary"),
        name="cmlp_prompt",
    )(zm, zm, zm, ws, wb.T, v_gain[None, :])


SEQ_BLOCK = SUBLANES


def _conv_step(raw_ref, cin_ref, w_ref, cout_ref):
    u = raw_ref[...]
    out = w_ref[CONV_K - 1:CONV_K, :] * u
    for k in range(CONV_K - 1):
        out = out + w_ref[k:k + 1, :] * cin_ref[k]
    for k in range(CONV_K - 2):
        cout_ref[k] = cin_ref[k + 1]
    cout_ref[CONV_K - 2] = u
    return out


def _row0(row, fill=0.0):
    return jnp.where(_iota((SUBLANES, row.shape[1]), 0) == 0, row, fill)


def _stash_rows(dst_ref, val):
    for i in range(SEQ_BLOCK):
        dst_ref[i] = jnp.broadcast_to(val[i:i + 1, :], (SUBLANES, val.shape[1]))


def _gather_rows(src_ref):
    rid = _iota(src_ref.shape[1:], 0)
    acc = src_ref[0]
    for i in range(1, SEQ_BLOCK):
        acc = jnp.where(rid == i, src_ref[i], acc)
    return acc


def _ssd_decode_kernel(za_ref, xs_ref, b_ref, c_ref, zc_ref, cx_ref, cb_ref, cc_ref, wx_ref, bx_ref, wb_ref, bb_ref,
                       wc_ref, bc_ref, pc_ref, e_ref, a_ref, d_ref, nrm_ref, s_ref, *rest):
    mix_ref, ncx_ref, ncb_ref, ncc_ref, so_ref = rest[-5:]
    xs = _silu(_conv_step(xs_ref, cx_ref, wx_ref, ncx_ref) + bx_ref[...])
    bm = _silu(_conv_step(b_ref, cb_ref, wb_ref, ncb_ref) + bb_ref[...])
    cm = _silu(_conv_step(c_ref, cc_ref, wc_ref, ncc_ref) + bc_ref[...])
    dtx = _mm_hi(_softplus(zc_ref[...] + pc_ref[0:1, :]), e_ref[...])
    hi, mid, lo = _split3(jnp.exp(dtx * a_ref[...]))
    dx = dtx * xs
    rid = _iota((SUBLANES, SSD_GW), 0)
    ones_rows = jnp.where((_iota((SUBLANES, SSD_STATE), 0) >= 1) & (_iota((SUBLANES, SSD_STATE), 0) <= 3), 1.0, 0.0)
    hpg = SSD_HEADS // SSD_GROUPS
    seqs = range(SEQ_BLOCK)
    lmat = [jnp.where(rid == 0, dx[i:i + 1, :], jnp.where(rid == 1, hi[i:i + 1, :],
                      jnp.where(rid == 2, mid[i:i + 1, :], jnp.where(rid == 3, lo[i:i + 1, :], 0.0)))) for i in seqs]
    upd = [_mm_tn(lmat[i], jnp.concatenate([_row0(bm[i:i + 1, :]), ones_rows], axis=1)) for i in seqs]
    s_new = [s_ref[i].reshape(SSD_GW, SSD_STATE) * upd[i][:, SSD_STATE:] + upd[i][:, :SSD_STATE] for i in seqs]
    for i in seqs:
        so_ref[i] = s_new[i].reshape(hpg, SSD_HEAD_DIM, SSD_STATE)
    ys = [_mm_nt(jnp.broadcast_to(cm[i:i + 1, :], (SUBLANES, SSD_STATE)), s_new[i]) for i in seqs]
    y = ys[0]
    for i in seqs[1:]:
        y = jnp.where(rid == i, ys[i], y)
    y = y + d_ref[...] * xs
    y = y * _silu(za_ref[...])
    y = y * lax.rsqrt(jnp.mean(y * y, axis=-1, keepdims=True) + EPS) * nrm_ref[...]
    mix_ref[...] = y.astype(BF16)


def _stacked_state_io(states, layer, prev, block, index):
    spec = pl.BlockSpec((None,) + block, lambda *g: (layer,) + index(*g))
    extra_in = [] if prev is None else [prev]
    extra_specs = [] if prev is None else [pl.BlockSpec(memory_space=pl.ANY)]
    return spec, jax.ShapeDtypeStruct(states.shape, F32), extra_in, extra_specs


def _ssd_decode(zm, zs, mix, conv_t, states, layer, prev, mp, conv_w, conv_b, dt_bias, a_log, d_skip, norm):
    bsz = states.shape[1]
    hpg = SSD_HEADS // SSD_GROUPS
    st_spec, st_shape, extra_in, extra_specs = _stacked_state_io(
        states, layer, prev, (SEQ_BLOCK, hpg, SSD_HEAD_DIM, SSD_STATE), lambda s, g: (s, g, 0, 0))
    r0 = mp // SEQ_BLOCK
    pc = jnp.zeros((SUBLANES, LANES), F32).at[0, :SSD_HEADS].set(dt_bias)
    e_mat = (jnp.arange(SSD_WIDTH)[None, :] // SSD_HEAD_DIM == jnp.arange(LANES)[:, None]).astype(F32)
    a_x = jnp.repeat(-jnp.exp(a_log), SSD_HEAD_DIM)[None, :]
    nb = SSD_WIDTH // SSD_STATE
    zrow = lambda w, col: pl.BlockSpec((SEQ_BLOCK, w), lambda s, g: (r0 + s, col(g)))
    cst = lambda w, col: pl.BlockSpec((CONV_K - 1, SEQ_BLOCK, w), lambda s, g: (0, s, col(g)))
    par = lambda rows, w, col: pl.BlockSpec((rows, w), lambda s, g: (0, col(g)))
    mainb = 6 * D_MODEL // SSD_STATE
    return pl.pallas_call(
        _ssd_decode_kernel,
        grid=(bsz // SEQ_BLOCK, SSD_GROUPS),
        in_specs=[
            zrow(SSD_GW, lambda g: g), zrow(SSD_GW, lambda g: SSD_GROUPS + g),
            zrow(SSD_STATE, lambda g: mainb + g), zrow(SSD_STATE, lambda g: mainb + SSD_GROUPS + g),
            zrow(LANES, lambda g: 0),
            cst(SSD_GW, lambda g: g), cst(SSD_STATE, lambda g: nb + g), cst(SSD_STATE, lambda g: nb + SSD_GROUPS + g),
            par(CONV_K, SSD_GW, lambda g: g), par(1, SSD_GW, lambda g: g),
            par(CONV_K, SSD_STATE, lambda g: nb + g), par(1, SSD_STATE, lambda g: nb + g),
            par(CONV_K, SSD_STATE, lambda g: nb + SSD_GROUPS + g), par(1, SSD_STATE, lambda g: nb + SSD_GROUPS + g),
            par(SUBLANES, LANES, lambda g: 0), par(LANES, SSD_GW, lambda g: g),
            par(1, SSD_GW, lambda g: g), par(1, SSD_GW, lambda g: g), par(1, SSD_GW, lambda g: g),
            st_spec,
            pl.BlockSpec(memory_space=pl.ANY),
        ] + extra_specs,
        out_specs=[
            pl.BlockSpec((SEQ_BLOCK, SSD_GW), lambda s, g: (r0 + s, g)),
            cst(SSD_GW, lambda g: g), cst(SSD_STATE, lambda g: g), cst(SSD_STATE, lambda g: g),
            st_spec,
        ],
        out_shape=[
            jax.ShapeDtypeStruct(mix.shape, mix.dtype),
            jax.ShapeDtypeStruct((CONV_K - 1, bsz, SSD_WIDTH), F32),
            jax.ShapeDtypeStruct((CONV_K - 1, bsz, SSD_GROUPS * SSD_STATE), F32),
            jax.ShapeDtypeStruct((CONV_K - 1, bsz, SSD_GROUPS * SSD_STATE), F32),
            st_shape,
        ],
        input_output_aliases={20: 0} if prev is None else {20: 0, 21: 4},
        compiler_params=_params("parallel", "arbitrary"),
        name="ssd_decode",
    )(zm, zm, zm, zm, zs, conv_t, conv_t, conv_t, conv_w, conv_b[None, :], conv_w, conv_b[None, :], conv_w,
      conv_b[None, :], pc, e_mat, a_x, jnp.repeat(d_skip, SSD_HEAD_DIM)[None, :], norm[None, :], states, mix,
      *extra_in)


def _gdn_decode_kernel(q_ref, k_ref, v_ref, zb_ref, zc_ref, cq_ref, ck_ref, cv_ref, wq_ref, wk_ref, wv_ref, pc_ref,
                       nrm_ref, s_ref, *rest):
    mix_ref, ncq_ref, nck_ref, ncv_ref, so_ref, q_s, k_s, v_s, beta_s, eg_s, qk_s, o_s = rest[-12:]
    q_all = _silu(_conv_step(q_ref, cq_ref, wq_ref, ncq_ref))
    k_all = _silu(_conv_step(k_ref, ck_ref, wk_ref, nck_ref))
    _stash_rows(v_s, _silu(_conv_step(v_ref, cv_ref, wv_ref, ncv_ref)))
    zc = zc_ref[...]
    beta = jax.nn.sigmoid(zc)
    eg = jnp.exp(-jnp.exp(pc_ref[0:1, :]) * _softplus(zc + pc_ref[1:2, :]))
    heads = range(GDN_HG)
    hs = [slice(j * GDN_DK, (j + 1) * GDN_DK) for j in heads]
    qn, kn, beta_x, eg_x, qk_x = [], [], [], [], []
    shape = (SEQ_BLOCK, GDN_DK)
    for j in heads:
        qh, kh = q_all[:, hs[j]], k_all[:, hs[j]]
        qh = qh * (lax.rsqrt(jnp.sum(qh * qh, axis=-1, keepdims=True) + EPS) * (GDN_DK ** -0.5))
        kh = kh * lax.rsqrt(jnp.sum(kh * kh, axis=-1, keepdims=True) + EPS)
        qn.append(qh)
        kn.append(kh)
        beta_x.append(jnp.broadcast_to(beta[:, j:j + 1], shape))
        eg_x.append(jnp.broadcast_to(eg[:, GDN_HG + j:GDN_HG + j + 1], shape))
        qk_x.append(jnp.broadcast_to(jnp.sum(qh * kh, axis=-1, keepdims=True), shape))
    for ref, parts in ((q_s, qn), (k_s, kn), (beta_s, beta_x), (eg_s, eg_x), (qk_s, qk_x)):
        _stash_rows(ref, jnp.concatenate(parts, axis=1))

    rid = _iota((SUBLANES, GDN_DK), 0)

    def body(i, carry):
        k_b, q_b, v_b, beta_b, eg_b, qk_b = k_s[i], q_s[i], v_s[i], beta_s[i], eg_s[i], qk_s[i]
        s_prev = [s_ref[i, j] for j in heads]
        ks_qs = [_mm(jnp.where(rid == 0, k_b[:, hs[j]], q_b[:, hs[j]]), s_prev[j]) for j in heads]
        vn = [beta_b[0:1, hs[j]] * (v_b[0:1, hs[j]] - eg_b[0:1, hs[j]] * ks_qs[j][0:1, :]) for j in heads]
        outer = [_mm_tn(_row0(k_b[:, hs[j]]), _row0(vn[j])) for j in heads]
        for j in heads:
            so_ref[i, j] = s_prev[j] * eg_b[0:1, j * GDN_DK:j * GDN_DK + 1] + outer[j]
            o_row = eg_b[0:1, hs[j]] * ks_qs[j][1:2, :] + qk_b[0:1, hs[j]] * vn[j]
            o_s[i, :, hs[j]] = jnp.broadcast_to(o_row, (SUBLANES, GDN_DV))
        return carry

    lax.fori_loop(0, SEQ_BLOCK, body, 0)
    o_all = _gather_rows(o_s)
    for j in heads:
        o = o_all[:, hs[j]]
        o = o * lax.rsqrt(jnp.mean(o * o, axis=-1, keepdims=True) + EPS) * nrm_ref[...]
        mix_ref[:, hs[j]] = (o * _silu(zb_ref[:, hs[j]])).astype(BF16)


def _gdn_decode(zm, zs, mix, conv_t, states, layer, prev, mp, conv_w, dt_bias, a_log, norm):
    bsz = states.shape[1]
    st_spec, st_shape, extra_in, extra_specs = _stacked_state_io(
        states, layer, prev, (SEQ_BLOCK, GDN_HG, GDN_DK, GDN_DV), lambda s, h: (s, h, 0, 0))
    nhg = GDN_HEADS // GDN_HG
    w = GDN_HG * GDN_DK
    r0 = mp // SEQ_BLOCK
    col0 = 2 * D_MODEL // w
    pc = jnp.zeros((nhg, SUBLANES, LANES), F32)
    pc = pc.at[:, 0, GDN_HG:2 * GDN_HG].set(a_log.reshape(nhg, GDN_HG))
    pc = pc.at[:, 1, GDN_HG:2 * GDN_HG].set(dt_bias.reshape(nhg, GDN_HG))
    seg = lambda p: pl.BlockSpec((SEQ_BLOCK, w), lambda s, h: (r0 + s, col0 + p * nhg + h))
    cst = lambda p: pl.BlockSpec((CONV_K - 1, SEQ_BLOCK, w), lambda s, h: (0, s, p * nhg + h))
    wseg = lambda p: pl.BlockSpec((CONV_K, w), lambda s, h: (0, p * nhg + h))
    cout = pl.BlockSpec((CONV_K - 1, SEQ_BLOCK, w), lambda s, h: (0, s, h))
    row_scratch = pltpu.VMEM((SEQ_BLOCK, SUBLANES, w), F32)
    return pl.pallas_call(
        _gdn_decode_kernel,
        grid=(bsz // SEQ_BLOCK, nhg),
        in_specs=[
            seg(0), seg(1), seg(2), seg(3),
            pl.BlockSpec((SEQ_BLOCK, LANES), lambda s, h: (r0 + s, 1 + h)),
            cst(0), cst(1), cst(2), wseg(0), wseg(1), wseg(2),
            pl.BlockSpec((None, SUBLANES, LANES), lambda s, h: (h, 0, 0)),
            pl.BlockSpec((1, GDN_DV), lambda s, h: (0, 0)),
            st_spec,
            pl.BlockSpec(memory_space=pl.ANY),
        ] + extra_specs,
        out_specs=[
            pl.BlockSpec((SEQ_BLOCK, w), lambda s, h: (r0 + s, h)),
            cout, cout, cout,
            st_spec,
        ],
        out_shape=[
            jax.ShapeDtypeStruct(mix.shape, mix.dtype),
            jax.ShapeDtypeStruct((CONV_K - 1, bsz, GDN_QK), F32),
            jax.ShapeDtypeStruct((CONV_K - 1, bsz, GDN_QK), F32),
            jax.ShapeDtypeStruct((CONV_K - 1, bsz, GDN_V), F32),
            st_shape,
        ],
        scratch_shapes=[row_scratch] * 7,
        input_output_aliases={14: 0} if prev is None else {14: 0, 15: 4},
        compiler_params=_params("parallel", "arbitrary"),
        name="gdn_decode",
    )(zm, zm, zm, zm, zs, conv_t, conv_t, conv_t, conv_w, conv_w, conv_w, pc, norm[None, :], states, mix, *extra_in)


ML_HG = 4


def _mlstm_decode_kernel(q_ref, k_ref, v_ref, o_ref, zc_ref, g_ref, m_ref, n_ref, pc_ref, e_ref, nrm_ref, c_ref,
                         *rest):
    mix_ref, n_out, m_out, c_out, q_s, kb_s, v_s, sa_s, sbqk_s, den_s, h_s = rest[-11:]
    g = g_ref[...]
    logi = g + pc_ref[0:1, :]
    logf = -_softplus(-pltpu.roll(g + pc_ref[1:2, :], LANES - ML_HEADS, 1))
    m_prev = m_ref[...]
    m_new = jnp.maximum(logf + m_prev, logi)
    m_out[...] = m_new
    e_mat = e_ref[...]
    sa_x = _mm_hi(jnp.exp(logf + m_prev - m_new), e_mat)
    sb_x = _mm_hi(jnp.exp(logi - m_new), e_mat)
    em_x = _mm_hi(jnp.exp(-m_new), e_mat)
    q = q_ref[...]
    k = k_ref[...] * (ML_DK ** -0.5)
    n_prev = n_ref[...]
    n_out[...] = n_prev * sa_x + sb_x * k
    _stash_rows(q_s, q)
    _stash_rows(kb_s, sb_x * k)
    _stash_rows(v_s, v_ref[...])
    _stash_rows(sa_s, sa_x)
    heads = range(ML_HG)
    ks = [slice(j * ML_DK, (j + 1) * ML_DK) for j in heads]
    vs = [slice(j * ML_DV, (j + 1) * ML_DV) for j in heads]
    sbqk_x, den_x = [], []
    for j in heads:
        shape = (SEQ_BLOCK, ML_DK)
        qk = jnp.sum(q[:, ks[j]] * k[:, ks[j]], axis=-1, keepdims=True)
        qn = jnp.sum(q[:, ks[j]] * n_prev[:, ks[j]], axis=-1, keepdims=True)
        sbqk = sb_x[:, ks[j]] * jnp.broadcast_to(qk, shape)
        sbqk_x.append(sbqk)
        den_x.append(jnp.maximum(jnp.abs(sa_x[:, ks[j]] * jnp.broadcast_to(qn, shape) + sbqk), em_x[:, ks[j]]))
    _stash_rows(sbqk_s, jnp.concatenate(sbqk_x, axis=1))
    _stash_rows(den_s, jnp.concatenate(den_x, axis=1))

    def body(i, carry):
        q_b, kb_b, v_b, sa_b, sbqk_b, den_b = q_s[i], kb_s[i], v_s[i], sa_s[i], sbqk_s[i], den_s[i]
        c_prev = [c_ref[i, j] for j in heads]
        qc = [_mm(q_b[:, ks[j]], c_prev[j]) for j in heads]
        outer = [_mm_tn(_row0(kb_b[:, ks[j]]), _row0(v_b[:, vs[j]])) for j in heads]
        for j in heads:
            lane0 = slice(j * ML_DK, j * ML_DK + 1)
            c_out[i, j] = c_prev[j] * sa_b[0:1, lane0] + outer[j]
            num = sa_b[:, lane0] * qc[j] + sbqk_b[:, lane0] * v_b[:, vs[j]]
            h_s[i, :, vs[j]] = num / den_b[:, lane0]
        return carry

    lax.fori_loop(0, SEQ_BLOCK, body, 0)
    h_all = _gather_rows(h_s)
    for j in heads:
        h = h_all[:, vs[j]]
        h = h * lax.rsqrt(jnp.mean(h * h, axis=-1, keepdims=True) + EPS) * nrm_ref[...]
        mix_ref[:, vs[j]] = (h * jax.nn.sigmoid(o_ref[:, vs[j]]) * _silu(zc_ref[:, vs[j]])).astype(BF16)


def _mlstm_decode(zm, zs, mix, c_states, layer, prev, n0, m0, mp, i_bias, f_bias, norm):
    bsz = c_states.shape[1]
    st_spec, st_shape, extra_in, extra_specs = _stacked_state_io(
        c_states, layer, prev, (SEQ_BLOCK, ML_HG, ML_DK, ML_DV), lambda s, h: (s, h, 0, 0))
    nhg = ML_HEADS // ML_HG
    wk, wv = ML_HG * ML_DK, ML_HG * ML_DV
    r0 = mp // SEQ_BLOCK
    pc = jnp.zeros((SUBLANES, LANES), F32).at[0, :ML_HEADS].set(i_bias).at[1, ML_HEADS:2 * ML_HEADS].set(f_bias)
    e_mat = (jnp.arange(ML_QK)[None, :] // ML_DK == jnp.arange(LANES)[:, None]).astype(F32)
    zrow = lambda w, col: pl.BlockSpec((SEQ_BLOCK, w), lambda s, h: (r0 + s, col(h)))
    vcol = 2 * ML_QK // wv
    return pl.pallas_call(
        _mlstm_decode_kernel,
        grid=(bsz // SEQ_BLOCK, nhg),
        in_specs=[
            zrow(wk, lambda h: h), zrow(wk, lambda h: nhg + h),
            zrow(wv, lambda h: vcol + h), zrow(wv, lambda h: vcol + nhg + h), zrow(wv, lambda h: vcol + 2 * nhg + h),
            zrow(LANES, lambda h: 0),
            pl.BlockSpec((SEQ_BLOCK, LANES), lambda s, h: (s, 0)),
            pl.BlockSpec((SEQ_BLOCK, wk), lambda s, h: (s, h)),
            pl.BlockSpec((SUBLANES, LANES), lambda s, h: (0, 0)),
            pl.BlockSpec((LANES, wk), lambda s, h: (0, h)),
            pl.BlockSpec((1, ML_DV), lambda s, h: (0, 0)),
            st_spec,
            pl.BlockSpec(memory_space=pl.ANY),
        ] + extra_specs,
        out_specs=[
            pl.BlockSpec((SEQ_BLOCK, wv), lambda s, h: (r0 + s, h)),
            pl.BlockSpec((SEQ_BLOCK, wk), lambda s, h: (s, h)),
            pl.BlockSpec((SEQ_BLOCK, LANES), lambda s, h: (s, 0)),
            st_spec,
        ],
        out_shape=[
            jax.ShapeDtypeStruct(mix.shape, mix.dtype),
            jax.ShapeDtypeStruct((bsz, ML_QK), F32),
            jax.ShapeDtypeStruct((bsz, LANES), F32),
            st_shape,
        ],
        scratch_shapes=[pltpu.VMEM((SEQ_BLOCK, SUBLANES, w), F32) for w in (wk, wk, wv, wk, wk, wk, wv)],
        input_output_aliases={12: 0} if prev is None else {12: 0, 13: 3},
        compiler_params=_params("parallel", "arbitrary"),
        name="mlstm_decode",
    )(zm, zm, zm, zm, zm, zs, jnp.pad(m0, ((0, 0), (0, LANES - ML_HEADS))), n0.reshape(bsz, ML_QK), pc, e_mat,
      norm[None, :], c_states, mix, *extra_in)


def _cmlp_decode_kernel(u_ref, v_ref, z_ref, ws_ref, wb_ref, gain_ref, mixin_ref, mix_ref, vrows_ref):
    del mixin_ref
    for g in range(CM_GROUPS):
        gs = slice(g * CM_GROUP_DIM, (g + 1) * CM_GROUP_DIM)
        v = _gelu(v_ref[:, gs])
        v = v * lax.rsqrt(jnp.mean(v * v, axis=-1, keepdims=True) + EPS) * gain_ref[...]
        s = ws_ref[:, gs] * v + wb_ref[:, gs]
        mix_ref[:, gs] = (_gelu(u_ref[:, gs]) * s * _silu(z_ref[:, gs])).astype(BF16)
        vrows_ref[:, gs] = v


def _cmlp_decode(zm, mix, mp, bsz, v_gain, ws, wb):
    col0 = (2 * ML_QK + 3 * ML_V) // CM_WIDTH
    r0 = mp // bsz
    zrow = lambda col: pl.BlockSpec((bsz, CM_WIDTH), lambda i: (r0, col))
    par = pl.BlockSpec((1, CM_WIDTH), lambda i: (0, 0))
    return pl.pallas_call(
        _cmlp_decode_kernel,
        grid=(1,),
        in_specs=[zrow(col0), zrow(col0 + 1), zrow(col0 + 2), par, par,
                  pl.BlockSpec((1, CM_GROUP_DIM), lambda i: (0, 0)), pl.BlockSpec(memory_space=pl.ANY)],
        out_specs=[pl.BlockSpec((bsz, CM_WIDTH), lambda i: (r0, 0)), pl.BlockSpec((bsz, CM_WIDTH), lambda i: (0, 0))],
        out_shape=[jax.ShapeDtypeStruct(mix.shape, mix.dtype), jax.ShapeDtypeStruct((bsz, CM_WIDTH), F32)],
        input_output_aliases={6: 0},
        compiler_params=_params("arbitrary"),
        name="cmlp_decode",
    )(zm, zm, zm, jnp.repeat(ws[:, 0, 0], CM_GROUP_DIM)[None, :], jnp.repeat(wb[:, 0], CM_GROUP_DIM)[None, :],
      v_gain[None, :], mix)


def kernel(x_prompt, x_sample, state_ssd_conv, state_ssd, state_gdn_conv, state_gdn, state_mlstm_c,
           state_mlstm_n, state_mlstm_m, even_norm, even_w_in, ssd_conv_w, ssd_conv_b, ssd_dt_bias, ssd_a_log,
           ssd_d, ssd_norm, gdn_conv_w, gdn_dt_bias, gdn_a_log, gdn_norm, even_w_out, odd_norm, odd_w_in,
           mlstm_i_bias, mlstm_f_bias, mlstm_norm, cmlp_v_norm, cmlp_ws, cmlp_b, odd_w_out, final_norm):
    bp, seq, d = x_prompt.shape
    bs = x_sample.shape[0]
    mp = bp * seq
    mt = mp + bs
    x = jnp.concatenate([x_prompt.reshape(mp, d), x_sample.reshape(bs, d)], axis=0)

    keys = ("sc", "ss", "gc", "gs", "mc", "mn", "mm", "cv")
    outs_p = {k: [] for k in keys}
    outs_s = {k: [] for k in keys}
    ss_all = gs_all = mc_all = None
    even_wt_in = jnp.swapaxes(even_w_in, 1, 2)
    odd_wt_in = jnp.swapaxes(odd_w_in, 1, 2)
    for layer in range(DEPTH):
        i = layer // 2
        if layer % 2 == 0:
            w_main, w_small = _prep_even_w_in(even_wt_in, i)
            zm, zs = _inproj(x, even_norm[i], w_main, w_small, tn=1024)
            zs_t = zs[:mp].T
            mix_a, cx, cbc, st = _ssd_prompt(zm, zs, zs_t, mt, bp, seq, ssd_conv_w[i], ssd_conv_b[i],
                                             ssd_dt_bias[i], ssd_a_log[i], ssd_d[i], ssd_norm[i])
            mix_b, cq, ck, cv, gst = _gdn_prompt(zm, zs, zs_t, mt, bp, seq, gdn_conv_w[i], gdn_dt_bias[i],
                                                 gdn_a_log[i], gdn_norm[i])
            hpg = SSD_HEADS // SSD_GROUPS
            outs_p["sc"].append(jnp.concatenate([cx, cbc], axis=-1))
            outs_p["ss"].append(st.reshape(bp, SSD_GROUPS, SSD_STATE, hpg, SSD_HEAD_DIM).transpose(0, 1, 3, 4, 2)
                                .reshape(bp, SSD_HEADS, SSD_HEAD_DIM, SSD_STATE))
            outs_p["gc"].append(jnp.concatenate([cq, ck, cv], axis=-1))
            outs_p["gs"].append(gst)
            mix_a, ncx, ncb, ncc, ss_all = _ssd_decode(
                zm, zs, mix_a, jnp.swapaxes(state_ssd_conv[i], 0, 1), state_ssd, i, ss_all, mp, ssd_conv_w[i],
                ssd_conv_b[i], ssd_dt_bias[i], ssd_a_log[i], ssd_d[i], ssd_norm[i])
            mix_b, ncq, nck, ncv, gs_all = _gdn_decode(
                zm, zs, mix_b, jnp.swapaxes(state_gdn_conv[i], 0, 1), state_gdn, i, gs_all, mp, gdn_conv_w[i],
                gdn_dt_bias[i], gdn_a_log[i], gdn_norm[i])
            outs_s["sc"].append(jnp.swapaxes(jnp.concatenate([ncx, ncb, ncc], axis=-1), 0, 1))
            outs_s["gc"].append(jnp.swapaxes(jnp.concatenate([ncq, nck, ncv], axis=-1), 0, 1))
            w_out = even_w_out[i].astype(BF16)
        else:
            w_main, w_small = _prep_odd_w_in(odd_wt_in, i)
            zm, zs = _inproj(x, odd_norm[i], w_main, w_small, tn=1024)
            zs_t = zs[:mp].T
            mix_a, c_p, n_p, m_p = _mlstm_prompt(zm, zs, zs_t, mt, bp, seq, mlstm_i_bias[i], mlstm_f_bias[i],
                                                 mlstm_norm[i])
            mix_b, v_rows = _cmlp_prompt(zm, mt, bp, seq, cmlp_v_norm[i], cmlp_ws[i], cmlp_b[i])
            outs_p["mc"].append(c_p); outs_p["mn"].append(n_p); outs_p["mm"].append(m_p[:, :, 0])
            outs_p["cv"].append(v_rows)
            mix_a, n_s, m_s, mc_all = _mlstm_decode(zm, zs, mix_a, state_mlstm_c, i, mc_all, state_mlstm_n[i],
                                                    state_mlstm_m[i], mp, mlstm_i_bias[i], mlstm_f_bias[i],
                                                    mlstm_norm[i])
            mix_b, v_row_s = _cmlp_decode(zm, mix_b, mp, bs, cmlp_v_norm[i], cmlp_ws[i], cmlp_b[i])
            outs_s["mn"].append(n_s.reshape(bs, ML_HEADS, ML_DK))
            outs_s["mm"].append(m_s[:, :ML_HEADS])
            outs_s["cv"].append(v_row_s.reshape(bs, 1, CM_WIDTH))
            w_out = odd_w_out[i].astype(BF16)
        x = _outproj(x, mix_a, mix_b, w_out)

    y_p = _final_norm(x, final_norm, 0, mp, 512).reshape(bp, seq, d)
    y_s = _final_norm(x, final_norm, mp, bs, bs).reshape(bs, 1, d)
    st = lambda o, k: jnp.stack(o[k])
    return (y_p, y_s, st(outs_p, "sc"), st(outs_s, "sc"), st(outs_p, "ss"), ss_all,
            st(outs_p, "gc"), st(outs_s, "gc"), st(outs_p, "gs"), gs_all,
            st(outs_p, "mc"), mc_all, st(outs_p, "mn"), st(outs_s, "mn"),
            st(outs_p, "mm"), st(outs_s, "mm"), st(outs_p, "cv"), st(outs_s, "cv"))
```

```python
import jax
import jax.numpy as jnp
import numpy as np
from jax import lax
from jax.experimental import pallas as pl
from jax.experimental.pallas import tpu as pltpu

F32 = jnp.float32
BF16 = jnp.bfloat16
HI = lax.Precision.HIGHEST

D_MODEL = 2048
DEPTH = 4
CHUNK = 128
CONV_K = 4
EPS = 1e-6
NEG = -1e30

SSD_WIDTH = D_MODEL
SSD_HEAD_DIM = 64
SSD_HEADS = SSD_WIDTH // SSD_HEAD_DIM
SSD_STATE = 128
SSD_GROUPS = 4
SSD_GW = SSD_WIDTH // SSD_GROUPS
SSD_BC = 2 * SSD_GROUPS * SSD_STATE
SSD_CONV_DIM = SSD_WIDTH + SSD_BC
GDN_HEADS = 16
GDN_DK = 128
GDN_DV = 128
GDN_QK = GDN_HEADS * GDN_DK
GDN_V = GDN_HEADS * GDN_DV
GDN_CONV_DIM = 2 * GDN_QK + GDN_V
GDN_HG = 8
PROMPT_CPS = 2
ML_HEADS = 8
ML_DK = 128
ML_DV = 256
ML_QK = ML_HEADS * ML_DK
ML_V = ML_HEADS * ML_DV
CM_WIDTH = D_MODEL // 2
CM_GROUPS = 8
CM_GROUP_DIM = CM_WIDTH // CM_GROUPS
CM_CHUNK = 128

LANES = 128
SUBLANES = 8

EVEN_MAIN = 6 * D_MODEL + SSD_BC
EVEN_SMALL = 3 * LANES
ODD_MAIN = 2 * ML_QK + 3 * ML_V + 3 * CM_WIDTH
ODD_SMALL = LANES

VMEM_LIMIT = 56 * 1024 * 1024
ROW_TILE_CAP = 1040
BF16_SUBLANES = 16


def _row_tile(m):
    return max(t for t in range(BF16_SUBLANES, ROW_TILE_CAP + 1, BF16_SUBLANES) if m % t == 0)


def _silu(x):
    return x * jax.nn.sigmoid(x)


def _softplus(x):
    return jnp.maximum(x, 0.0) + jnp.log1p(jnp.exp(-jnp.abs(x)))


def _gelu(x):
    return 0.5 * x * (1.0 + jnp.tanh(np.sqrt(2.0 / np.pi).astype(np.float32) * (x + 0.044715 * (x * x * x))))


def _mm(a, b):
    return jnp.dot(a.astype(BF16), b.astype(BF16), preferred_element_type=F32)


def _mm_nt(a, b):
    return lax.dot_general(a.astype(BF16), b.astype(BF16), (((1,), (1,)), ((), ())), preferred_element_type=F32)


def _mm_tn(a, b):
    return lax.dot_general(a.astype(BF16), b.astype(BF16), (((0,), (0,)), ((), ())), preferred_element_type=F32)


def _mm_hi(a, b):
    return jnp.dot(a, b, precision=HI, preferred_element_type=F32)


def _split3(x):
    hi = x.astype(BF16).astype(F32)
    r1 = x - hi
    mid = r1.astype(BF16).astype(F32)
    lo = (r1 - mid).astype(BF16).astype(F32)
    return hi, mid, lo


def _mm_sel(a, sel):
    hi, mid, lo = _split3(a)
    return (_mm(hi, sel) + _mm(mid, sel)) + _mm(lo, sel)


def _iota(shape, axis):
    return lax.broadcasted_iota(jnp.int32, shape, axis)


def _params(*sem):
    return pltpu.CompilerParams(dimension_semantics=sem, vmem_limit_bytes=VMEM_LIMIT)


def _inproj_kernel(x_ref, g_ref, w_ref, ws_ref, z_ref, zs_ref, xn_ref):
    @pl.when(pl.program_id(1) == 0)
    def _():
        x = x_ref[...]
        y = x * lax.rsqrt(jnp.mean(x * x, axis=-1, keepdims=True) + EPS)
        xn = (y * g_ref[...]).astype(BF16)
        xn_ref[...] = xn
        zs_ref[...] = _mm_nt(xn, ws_ref[...])

    z_ref[...] = _mm_nt(xn_ref[...], w_ref[...])


def _inproj(x, g, w_main, w_small, tn):
    m, d = x.shape
    n = w_main.shape[0]
    ns = w_small.shape[0]
    tm = _row_tile(m)
    return pl.pallas_call(
        _inproj_kernel,
        grid=(m // tm, n // tn),
        in_specs=[
            pl.BlockSpec((tm, d), lambda i, j: (i, 0)),
            pl.BlockSpec((1, d), lambda i, j: (0, 0)),
            pl.BlockSpec((tn, d), lambda i, j: (j, 0)),
            pl.BlockSpec((ns, d), lambda i, j: (0, 0)),
        ],
        out_specs=[
            pl.BlockSpec((tm, tn), lambda i, j: (i, j)),
            pl.BlockSpec((tm, ns), lambda i, j: (i, 0)),
        ],
        out_shape=[jax.ShapeDtypeStruct((m, n), F32), jax.ShapeDtypeStruct((m, ns), F32)],
        scratch_shapes=[pltpu.VMEM((tm, d), BF16)],
        compiler_params=_params("parallel", "arbitrary"),
        name="inproj",
    )(x, g.reshape(1, d), w_main, w_small)


def _outproj_kernel(x_ref, ma_ref, mb_ref, wa_ref, wb_ref, o_ref):
    o_ref[...] = (x_ref[...] + jnp.dot(ma_ref[...], wa_ref[...], preferred_element_type=F32)
                  + jnp.dot(mb_ref[...], wb_ref[...], preferred_element_type=F32))


def _outproj(x, mix_a, mix_b, w):
    m, d = x.shape
    ka, kb = mix_a.shape[1], mix_b.shape[1]
    tm, tn = _row_tile(m), 512
    return pl.pallas_call(
        _outproj_kernel,
        grid=(m // tm, d // tn),
        in_specs=[
            pl.BlockSpec((tm, tn), lambda i, j: (i, j)),
            pl.BlockSpec((tm, ka), lambda i, j: (i, 0)),
            pl.BlockSpec((tm, kb), lambda i, j: (i, 0)),
            pl.BlockSpec((ka, tn), lambda i, j: (0, j)),
            pl.BlockSpec((kb, tn), lambda i, j: (ka // kb, j)),
        ],
        out_specs=pl.BlockSpec((tm, tn), lambda i, j: (i, j)),
        out_shape=jax.ShapeDtypeStruct((m, d), F32),
        compiler_params=_params("parallel", "arbitrary"),
        name="outproj",
    )(x, mix_a, mix_b, w, w)


def _final_norm_kernel(x_ref, g_ref, o_ref):
    x = x_ref[...]
    o_ref[...] = x * lax.rsqrt(jnp.mean(x * x, axis=-1, keepdims=True) + EPS) * g_ref[...]


def _final_norm(x, g, row0, rows, tm):
    d = x.shape[1]
    return pl.pallas_call(
        _final_norm_kernel,
        grid=(rows // tm,),
        in_specs=[pl.BlockSpec((tm, d), lambda i: (row0 // tm + i, 0)), pl.BlockSpec((1, d), lambda i: (0, 0))],
        out_specs=pl.BlockSpec((tm, d), lambda i: (i, 0)),
        out_shape=jax.ShapeDtypeStruct((rows, d), F32),
        compiler_params=_params("parallel"),
        name="final_norm",
    )(x, g.reshape(1, d))


REPACK_TN = 512


def _repack_kernel(a_ref, b_ref, o_ref, *, shift, lo, hi):
    j = pl.program_id(0)
    shifted = (j >= lo) & (j < hi)

    @pl.when(shifted)
    def _():
        o_ref[...] = jnp.concatenate([a_ref[shift:, :], b_ref[...]], axis=0).astype(BF16)

    @pl.when(jnp.logical_not(shifted))
    def _():
        o_ref[...] = a_ref[...].astype(BF16)


def _repack(wt_all, layer, n_out, a_idx, shift, lo, hi):
    d = wt_all.shape[2]
    per = REPACK_TN // shift
    kern = lambda a, b, o: _repack_kernel(a, b, o, shift=shift, lo=lo, hi=hi)
    return pl.pallas_call(
        kern,
        grid=(n_out // REPACK_TN,),
        in_specs=[pl.BlockSpec((None, REPACK_TN, d), lambda j: (layer, a_idx(j), 0)),
                  pl.BlockSpec((None, shift, d), lambda j: (layer, (a_idx(j) + 1) * per, 0))],
        out_specs=pl.BlockSpec((REPACK_TN, d), lambda j: (j, 0)),
        out_shape=jax.ShapeDtypeStruct((n_out, d), BF16),
        compiler_params=_params("parallel"),
        name="repack",
    )(wt_all, wt_all)


def _small_pack_kernel(*refs, layout):
    pieces, o_ref = refs[:-1], refs[-1]
    d = o_ref.shape[1]
    for blk, idxs in enumerate(layout):
        rows = [pieces[i][...] for i in idxs] + [jnp.zeros((LANES - SUBLANES * len(idxs), d), F32)]
        o_ref[blk * LANES:(blk + 1) * LANES, :] = jnp.concatenate(rows, axis=0).astype(BF16)


def _small_pack(wt_all, layer, src_rows, layout):
    d = wt_all.shape[2]
    kern = lambda *refs: _small_pack_kernel(*refs, layout=layout)
    spec = lambda r: pl.BlockSpec((None, SUBLANES, d), lambda i: (layer, r // SUBLANES, 0))
    return pl.pallas_call(
        kern,
        grid=(1,),
        in_specs=[spec(r) for r in src_rows],
        out_specs=pl.BlockSpec((len(layout) * LANES, d), lambda i: (0, 0)),
        out_shape=jax.ShapeDtypeStruct((len(layout) * LANES, d), BF16),
        compiler_params=_params("arbitrary"),
        name="small_pack",
    )(*([wt_all] * len(src_rows)))


def _prep_even_w_in(wt_all, layer):
    o_bc = 2 * SSD_WIDTH
    o_dt = o_bc + SSD_BC
    o_q = o_dt + SSD_HEADS
    o_beta = o_q + GDN_CONV_DIM + GDN_V
    o_g = o_beta + GDN_HEADS
    n1, n2 = o_bc // REPACK_TN, (o_bc + o_beta - o_q) // REPACK_TN
    src2, src3 = o_dt // REPACK_TN, o_bc // REPACK_TN
    a_idx = lambda j: jnp.where(j < n1, j, jnp.where(j < n2, j - n1 + src2, j - n2 + src3))
    main = _repack(wt_all, layer, EVEN_MAIN, a_idx, o_q - o_dt, n1, n2)
    n_dt = SSD_HEADS // SUBLANES
    src = [o_dt + SUBLANES * i for i in range(n_dt)]
    layout = [tuple(range(n_dt))]
    for hg in range(GDN_HEADS // GDN_HG):
        src += [o_beta + hg * GDN_HG, o_g + hg * GDN_HG]
        layout.append((len(src) - 2, len(src) - 1))
    return main, _small_pack(wt_all, layer, src, layout)


def _prep_odd_w_in(wt_all, layer):
    o1 = 2 * ML_QK + 3 * ML_V
    o2 = o1 + 2 * ML_HEADS
    main = _repack(wt_all, layer, ODD_MAIN, lambda j: j, o2 - o1, o1 // REPACK_TN, ODD_MAIN // REPACK_TN)
    small = _small_pack(wt_all, layer, [o1, o1 + ML_HEADS], [(0, 1)])
    return main, small


def _conv_chunk(ext_ref, u, w_ref, first):
    t = u.shape[0]

    @pl.when(first)
    def _():
        ext_ref[0:SUBLANES, :] = jnp.zeros((SUBLANES, ext_ref.shape[1]), F32)

    prev = ext_ref[0:SUBLANES, :]
    rid = _iota((SUBLANES, u.shape[1]), 0)
    out = w_ref[CONV_K - 1:CONV_K, :] * u
    for k in range(1, CONV_K):
        rolled = pltpu.roll(u, k, 0)
        head = jnp.where(rid < k, pltpu.roll(prev, k, 0), rolled[0:SUBLANES, :])
        shifted = jnp.concatenate([head, rolled[SUBLANES:, :]], axis=0)
        out = out + w_ref[CONV_K - 1 - k:CONV_K - k, :] * shifted
    ext_ref[0:SUBLANES, :] = u[t - SUBLANES:, :]
    return out


def _conv_tail(ext_ref):
    return ext_ref[SUBLANES - (CONV_K - 1):SUBLANES, :]


def _causal_masks():
    r = _iota((CHUNK, CHUNK), 0)
    c = _iota((CHUNK, CHUNK), 1)
    return r, c


def _ssd_prompt_kernel(za_ref, xs_ref, bc_ref, zc_ref, zr_ref, wx_ref, bx_ref, wbc_ref, bbc_ref, pc_ref, pr_ref,
                       d_ref, nrm_ref, mix_ref, cx_ref, cbc_ref, st_ref, extx, extbc, s_ref):
    c_id = pl.program_id(1)
    first = c_id == 0
    last = c_id == pl.num_programs(1) - 1

    @pl.when(first)
    def _():
        s_ref[...] = jnp.zeros(s_ref.shape, F32)

    xs = _silu(_conv_chunk(extx, xs_ref[...], wx_ref, first) + bx_ref[...])
    bc = _silu(_conv_chunk(extbc, bc_ref[...], wbc_ref, first) + bbc_ref[...])

    r, c = _causal_masks()
    causal = r >= c
    tril = jnp.where(causal, 1.0, 0.0)
    triu = jnp.where(r <= c, 1.0, 0.0)
    dt = _softplus(zc_ref[...] + pc_ref[0:1, :])
    la = _mm_hi(tril, dt * (-jnp.exp(pc_ref[1:2, :])))
    dtr = _softplus(zr_ref[...] + pr_ref[0])
    lar = _mm_hi(dtr * (-jnp.exp(pr_ref[1])), triu)
    la_last = la[CHUNK - 1:CHUNK, :]
    e_mat = jnp.where((_iota((LANES, SSD_WIDTH), 1) >> 6) == _iota((LANES, SSD_WIDTH), 0), 1.0, 0.0)
    ela_x = _mm_sel(jnp.exp(la), e_mat)
    wsx = _mm_sel(jnp.exp(la_last - la) * dt, e_mat)
    dec_x = _mm_sel(jnp.broadcast_to(jnp.exp(la_last), (SUBLANES, LANES)), e_mat)[0:1, :]
    lane_lo = _iota((CHUNK, LANES), 1) < SSD_HEAD_DIM

    hpg = SSD_HEADS // SSD_GROUPS
    for g in range(SSD_GROUPS):
        gs = slice(g * SSD_GW, (g + 1) * SSD_GW)
        bg = bc[:, g * SSD_STATE:(g + 1) * SSD_STATE]
        cg = bc[:, SSD_GROUPS * SSD_STATE + g * SSD_STATE:SSD_GROUPS * SSD_STATE + (g + 1) * SSD_STATE]
        cb = _mm_nt(cg, bg)
        ys = []
        for pair in range(hpg // 2):
            h0 = g * hpg + 2 * pair
            xpair = xs[:, h0 * SSD_HEAD_DIM:(h0 + 2) * SSD_HEAD_DIM]
            halves = []
            for hh in (h0, h0 + 1):
                seg = jnp.where(causal, la[:, hh:hh + 1] - lar[hh:hh + 1, :], NEG)
                lmat = jnp.exp(seg) * cb * dtr[hh:hh + 1, :]
                halves.append(_mm(lmat, xpair))
            ys.append(jnp.where(lane_lo, halves[0], halves[1]))
        y = jnp.concatenate(ys, axis=1)
        s_prev = s_ref[g]
        y = y + _mm(cg, s_prev) * ela_x[:, gs] + d_ref[:, gs] * xs[:, gs]
        y = y * _silu(za_ref[:, gs])
        y = y * lax.rsqrt(jnp.mean(y * y, axis=-1, keepdims=True) + EPS) * nrm_ref[:, gs]
        mix_ref[:, gs] = y.astype(BF16)
        s_ref[g] = s_prev * dec_x[:, gs] + _mm_tn(bg, xs[:, gs] * wsx[:, gs])

    @pl.when(last)
    def _():
        st_ref[0] = s_ref[...]
        cx_ref[0] = _conv_tail(extx)
        cbc_ref[0] = _conv_tail(extbc)


def _ssd_prompt(zm, zs, zs_t, m_total, bsz, seq, conv_w, conv_b, dt_bias, a_log, d_skip, norm):
    nc = seq // CHUNK
    rb = lambda b, c: b * nc + c
    pc = jnp.zeros((SUBLANES, LANES), F32).at[0, :SSD_HEADS].set(dt_bias).at[1, :SSD_HEADS].set(a_log)
    pr = jnp.stack([jnp.broadcast_to(dt_bias[:, None], (SSD_HEADS, CHUNK)),
                    jnp.broadcast_to(a_log[:, None], (SSD_HEADS, CHUNK))])
    full = lambda *shape: pl.BlockSpec(shape, lambda b, c: (0,) * len(shape))
    return pl.pallas_call(
        _ssd_prompt_kernel,
        grid=(bsz, nc),
        in_specs=[
            pl.BlockSpec((CHUNK, SSD_WIDTH), lambda b, c: (rb(b, c), 0)),
            pl.BlockSpec((CHUNK, SSD_WIDTH), lambda b, c: (rb(b, c), 1)),
            pl.BlockSpec((CHUNK, SSD_BC), lambda b, c: (rb(b, c), 6 * D_MODEL // SSD_BC)),
            pl.BlockSpec((CHUNK, LANES), lambda b, c: (rb(b, c), 0)),
            pl.BlockSpec((SSD_HEADS, CHUNK), lambda b, c: (0, rb(b, c))),
            full(CONV_K, SSD_WIDTH), full(1, SSD_WIDTH), full(CONV_K, SSD_BC), full(1, SSD_BC),
            full(SUBLANES, LANES), full(2, SSD_HEADS, CHUNK), full(1, SSD_WIDTH), full(1, SSD_WIDTH),
        ],
        out_specs=[
            pl.BlockSpec((CHUNK, SSD_WIDTH), lambda b, c: (rb(b, c), 0)),
            pl.BlockSpec((1, CONV_K - 1, SSD_WIDTH), lambda b, c: (b, 0, 0)),
            pl.BlockSpec((1, CONV_K - 1, SSD_BC), lambda b, c: (b, 0, 0)),
            pl.BlockSpec((1, SSD_GROUPS, SSD_STATE, SSD_GW), lambda b, c: (b, 0, 0, 0)),
        ],
        out_shape=[
            jax.ShapeDtypeStruct((m_total, SSD_WIDTH), BF16),
            jax.ShapeDtypeStruct((bsz, CONV_K - 1, SSD_WIDTH), F32),
            jax.ShapeDtypeStruct((bsz, CONV_K - 1, SSD_BC), F32),
            jax.ShapeDtypeStruct((bsz, SSD_GROUPS, SSD_STATE, SSD_GW), F32),
        ],
        scratch_shapes=[pltpu.VMEM((SUBLANES, SSD_WIDTH), F32), pltpu.VMEM((SUBLANES, SSD_BC), F32),
                        pltpu.VMEM((SSD_GROUPS, SSD_STATE, SSD_GW), F32)],
        compiler_params=_params("parallel", "arbitrary"),
        name="ssd_prompt",
    )(zm, zm, zm, zs, zs_t, conv_w[:, :SSD_WIDTH], conv_b[None, :SSD_WIDTH], conv_w[:, SSD_WIDTH:],
      conv_b[None, SSD_WIDTH:], pc, pr, jnp.repeat(d_skip, SSD_HEAD_DIM)[None, :], norm[None, :])


def _tri_inverse(mats, r, c):
    def corner(level):
        return ((r >> (level + 1)) == (c >> (level + 1))) & (((r >> level) & 1) == 1) & (((c >> level) & 1) == 0)

    eye = jnp.where(r == c, 1.0, 0.0)
    ts = [eye - jnp.where(corner(0), a, 0.0) for a in mats]
    for level in range(1, 7):
        cm = corner(level)
        xs = [_mm(t, jnp.where(cm, a, 0.0)) for t, a in zip(ts, mats)]
        ts = [t - _mm(x, t) for t, x in zip(ts, xs)]
    return ts


def _gdn_prompt_kernel(q_ref, k_ref, v_ref, zb_ref, zc_ref, zr_ref, wq_ref, wk_ref, wv_ref, pc_ref, pr_ref, nrm_ref,
                       mix_ref, cq_ref, ck_ref, cv_ref, st_ref, extq, extk, extv, s_ref):
    c_id = pl.program_id(2)
    first = c_id == 0
    last = c_id == pl.num_programs(2) - 1

    @pl.when(first)
    def _():
        s_ref[...] = jnp.zeros(s_ref.shape, F32)

    q_all = _silu(_conv_chunk(extq, q_ref[...], wq_ref, first))
    k_all = _silu(_conv_chunk(extk, k_ref[...], wk_ref, first))
    v_all = _silu(_conv_chunk(extv, v_ref[...], wv_ref, first))

    r, c = _causal_masks()
    causal = r >= c
    strict = r > c
    tril = jnp.where(causal, 1.0, 0.0)
    triu = jnp.where(r <= c, 1.0, 0.0)
    chunks = range(PROMPT_CPS)
    rows = [slice(ch * CHUNK, (ch + 1) * CHUNK) for ch in chunks]
    zc = zc_ref[...]
    beta_c = jax.nn.sigmoid(zc)
    g_c = -jnp.exp(pc_ref[0:1, :]) * _softplus(zc + pc_ref[1:2, :])
    gc_c = [_mm_hi(tril, g_c[rw, :]) for rw in rows]
    g_r = -jnp.exp(pr_ref[0]) * _softplus(zr_ref[GDN_HG:2 * GDN_HG, :] + pr_ref[1])
    gc_r = [_mm_hi(g_r[:, rw], triu) for rw in rows]

    heads = range(GDN_HG)
    hs = [slice(j * GDN_DK, (j + 1) * GDN_DK) for j in heads]
    units = [(ch, j) for ch in chunks for j in heads]
    qh = [q_all[rows[ch], hs[j]] for ch, j in units]
    kh = [k_all[rows[ch], hs[j]] for ch, j in units]
    qh = [x * (lax.rsqrt(jnp.sum(x * x, axis=-1, keepdims=True) + EPS) * (GDN_DK ** -0.5)) for x in qh]
    kh = [x * lax.rsqrt(jnp.sum(x * x, axis=-1, keepdims=True) + EPS) for x in kh]
    gcc = [gc_c[ch][:, GDN_HG + j:GDN_HG + j + 1] for ch, j in units]
    beta = [beta_c[rows[ch], j:j + 1] for ch, j in units]
    gam = [jnp.exp(jnp.where(causal, gcc[u] - gc_r[ch][j:j + 1, :], NEG)) for u, (ch, j) in enumerate(units)]
    qkk = [_mm_nt(jnp.concatenate([qh[u], kh[u]], axis=0), kh[u]) for u in range(len(units))]
    aqk = [qkk[u][:CHUNK] * gam[u] for u in range(len(units))]
    tinv = _tri_inverse([jnp.where(strict, beta[u] * qkk[u][CHUNK:] * gam[u], 0.0) for u in range(len(units))], r, c)
    egc = [jnp.exp(g) for g in gcc]
    uw = [_mm(tinv[u], jnp.concatenate([beta[u] * v_all[rows[ch], hs[j]], (beta[u] * egc[u]) * kh[u]], axis=1))
          for u, (ch, j) in enumerate(units)]
    s_cur = [s_ref[j] for j in heads]
    for ch in chunks:
        us = [ch * GDN_HG + j for j in heads]
        ws_qs = [_mm(jnp.concatenate([uw[u][:, GDN_DV:], qh[u] * egc[u]], axis=0), s_cur[j]) for j, u in enumerate(us)]
        vn = [uw[u][:, :GDN_DV] - ws_qs[j][:CHUNK] for j, u in enumerate(us)]
        o = [ws_qs[j][CHUNK:] + _mm(aqk[u], vn[j]) for j, u in enumerate(us)]
        nxt = []
        for j, u in enumerate(us):
            gc_last = gcc[u][CHUNK - 1:CHUNK, :]
            nxt.append(s_cur[j] * jnp.exp(gc_last) + _mm_tn(kh[u] * jnp.exp(gc_last - gcc[u]), vn[j]))
        s_cur = nxt
        for j in heads:
            on = o[j] * lax.rsqrt(jnp.mean(o[j] * o[j], axis=-1, keepdims=True) + EPS) * nrm_ref[...]
            mix_ref[rows[ch], hs[j]] = (on * _silu(zb_ref[rows[ch], hs[j]])).astype(BF16)
    for j in heads:
        s_ref[j] = s_cur[j]

    @pl.when(last)
    def _():
        st_ref[0] = s_ref[...]
        cq_ref[0] = _conv_tail(extq)
        ck_ref[0] = _conv_tail(extk)
        cv_ref[0] = _conv_tail(extv)


def _gdn_prompt(zm, zs, zs_t, m_total, bsz, seq, conv_w, dt_bias, a_log, norm):
    tt = PROMPT_CPS * CHUNK
    nc = seq // tt
    nhg = GDN_HEADS // GDN_HG
    w = GDN_HG * GDN_DK
    rb = lambda b, c: b * nc + c
    col0 = 2 * D_MODEL // w
    pc = jnp.zeros((nhg, SUBLANES, LANES), F32)
    pc = pc.at[:, 0, GDN_HG:2 * GDN_HG].set(a_log.reshape(nhg, GDN_HG))
    pc = pc.at[:, 1, GDN_HG:2 * GDN_HG].set(dt_bias.reshape(nhg, GDN_HG))
    pr = jnp.stack([jnp.broadcast_to(a_log.reshape(nhg, GDN_HG, 1), (nhg, GDN_HG, tt)),
                    jnp.broadcast_to(dt_bias.reshape(nhg, GDN_HG, 1), (nhg, GDN_HG, tt))], axis=1)
    seg = lambda s: pl.BlockSpec((tt, w), lambda b, h, c: (rb(b, c), col0 + s * nhg + h))
    wseg = lambda s: pl.BlockSpec((CONV_K, w), lambda b, h, c: (0, s * nhg + h))
    cout = pl.BlockSpec((1, CONV_K - 1, w), lambda b, h, c: (b, 0, h))
    return pl.pallas_call(
        _gdn_prompt_kernel,
        grid=(bsz, nhg, nc),
        in_specs=[
            seg(0), seg(1), seg(2), seg(3),
            pl.BlockSpec((tt, LANES), lambda b, h, c: (rb(b, c), 1 + h)),
            pl.BlockSpec((2 * GDN_HG, tt), lambda b, h, c: ((1 + h) * LANES // (2 * GDN_HG), rb(b, c))),
            wseg(0), wseg(1), wseg(2),
            pl.BlockSpec((None, SUBLANES, LANES), lambda b, h, c: (h, 0, 0)),
            pl.BlockSpec((None, 2, GDN_HG, tt), lambda b, h, c: (h, 0, 0, 0)),
            pl.BlockSpec((1, GDN_DV), lambda b, h, c: (0, 0)),
        ],
        out_specs=[
            pl.BlockSpec((tt, w), lambda b, h, c: (rb(b, c), h)),
            cout, cout, cout,
            pl.BlockSpec((1, GDN_HG, GDN_DK, GDN_DV), lambda b, h, c: (b, h, 0, 0)),
        ],
        out_shape=[
            jax.ShapeDtypeStruct((m_total, GDN_V), BF16),
            jax.ShapeDtypeStruct((bsz, CONV_K - 1, GDN_QK), F32),
            jax.ShapeDtypeStruct((bsz, CONV_K - 1, GDN_QK), F32),
            jax.ShapeDtypeStruct((bsz, CONV_K - 1, GDN_V), F32),
            jax.ShapeDtypeStruct((bsz, GDN_HEADS, GDN_DK, GDN_DV), F32),
        ],
        scratch_shapes=[pltpu.VMEM((SUBLANES, w), F32)] * 3 + [pltpu.VMEM((GDN_HG, GDN_DK, GDN_DV), F32)],
        compiler_params=_params("parallel", "parallel", "arbitrary"),
        name="gdn_prompt",
    )(zm, zm, zm, zm, zs, zs_t, conv_w, conv_w, conv_w, pc, pr, norm[None, :])


def _mlstm_prompt_kernel(q_ref, k_ref, v_ref, o_ref, zc_ref, gc_ref, gr_ref, pc_ref, pr_ref, nrm_ref,
                         mix_ref, c_out, n_out, m_out, c_ref, n_ref, m_ref):
    c_id = pl.program_id(1)
    first = c_id == 0
    last = c_id == pl.num_programs(1) - 1

    @pl.when(first)
    def _():
        c_ref[...] = jnp.zeros(c_ref.shape, F32)
        n_ref[...] = jnp.zeros(n_ref.shape, F32)
        m_ref[...] = jnp.zeros(m_ref.shape, F32)

    r, c = _causal_masks()
    causal = r >= c
    tril = jnp.where(causal, 1.0, 0.0)
    triu = jnp.where(r <= c, 1.0, 0.0)
    chunks = range(PROMPT_CPS)
    rows = [slice(ch * CHUNK, (ch + 1) * CHUNK) for ch in chunks]
    gc = gc_ref[...]
    logi_c = gc + pc_ref[0:1, :]
    logf_c = -_softplus(-(gc + pc_ref[1:2, :]))
    b_c = [_mm_hi(tril, logf_c[rw, :]) for rw in rows]
    logi_r = gr_ref[0:ML_HEADS, :] + pr_ref[0]
    logf_r = -_softplus(-(gr_ref[ML_HEADS:2 * ML_HEADS, :] + pr_ref[1]))
    b_r = [_mm_hi(logf_r[:, rw], triu) for rw in rows]

    heads = range(ML_HEADS)
    ks = [slice(j * ML_DK, (j + 1) * ML_DK) for j in heads]
    vs = [slice(j * ML_DV, (j + 1) * ML_DV) for j in heads]
    units = [(ch, j) for ch in chunks for j in heads]
    nu = range(len(units))
    q = [q_ref[rows[ch], ks[j]] for ch, j in units]
    k = [k_ref[rows[ch], ks[j]] * (ML_DK ** -0.5) for ch, j in units]
    v = [v_ref[rows[ch], vs[j]] for ch, j in units]
    bc = [b_c[ch][:, ML_HEADS + j:ML_HEADS + j + 1] for ch, j in units]
    dmat = [jnp.where(causal, bc[u] - b_r[ch][j:j + 1, :] + logi_r[j:j + 1, rows[ch]], NEG)
            for u, (ch, j) in enumerate(units)]
    m_intra = [jnp.max(x, axis=-1, keepdims=True) for x in dmat]
    p = [_mm_nt(q[u], k[u]) * jnp.exp(dmat[u] - m_intra[u]) for u in nu]
    h_intra = [_mm(p[u], v[u]) for u in nu]
    n_intra = [jnp.sum(p[u], axis=-1, keepdims=True) for u in nu]
    b_last = [x[CHUNK - 1:CHUNK, :] for x in bc]
    gk = [b_last[u] - bc[u] + logi_c[rows[ch], j:j + 1] for u, (ch, j) in enumerate(units)]
    m_k = [jnp.max(x, axis=0, keepdims=True) for x in gk]
    kw = [k[u] * jnp.exp(gk[u] - m_k[u]) for u in nu]
    c_loc = [_mm_tn(kw[u], v[u]) for u in nu]
    n_loc = [jnp.sum(kw[u], axis=0, keepdims=True) for u in nu]
    c_cur = [c_ref[j] for j in heads]
    n_cur = [n_ref[j:j + 1, :] for j in heads]
    m_cur = [m_ref[j:j + 1, 0:1] for j in heads]
    for ch in chunks:
        us = [ch * ML_HEADS + j for j in heads]
        qc = [_mm(q[u], c_cur[j]) for j, u in enumerate(us)]
        for j, u in enumerate(us):
            mb = bc[u] + m_cur[j]
            m_t = jnp.maximum(mb, m_intra[u])
            s_inter = jnp.exp(mb - m_t)
            s_intra = jnp.exp(m_intra[u] - m_t)
            num = s_inter * qc[j] + s_intra * h_intra[u]
            den = s_inter * jnp.sum(q[u] * n_cur[j], axis=-1, keepdims=True) + s_intra * n_intra[u]
            h = num / jnp.maximum(jnp.abs(den), jnp.exp(-m_t))
            m_new = jnp.maximum(b_last[u] + m_cur[j], m_k[u])
            sa = jnp.exp(b_last[u] + m_cur[j] - m_new)
            sb = jnp.exp(m_k[u] - m_new)
            c_cur[j] = c_cur[j] * sa + c_loc[u] * sb
            n_cur[j] = n_cur[j] * sa + n_loc[u] * sb
            m_cur[j] = m_new
            h = h * lax.rsqrt(jnp.mean(h * h, axis=-1, keepdims=True) + EPS) * nrm_ref[...]
            mix_ref[rows[ch], vs[j]] = (h * jax.nn.sigmoid(o_ref[rows[ch], vs[j]])
                                        * _silu(zc_ref[rows[ch], vs[j]])).astype(BF16)
    for j in heads:
        c_ref[j] = c_cur[j]
        n_ref[j:j + 1, :] = n_cur[j]
        m_ref[j:j + 1, :] = jnp.broadcast_to(m_cur[j], (1, LANES))

    @pl.when(last)
    def _():
        c_out[0] = c_ref[...]
        n_out[0] = n_ref[...]
        m_out[0] = m_ref[...]


def _mlstm_prompt(zm, zs, zs_t, m_total, bsz, seq, i_bias, f_bias, norm):
    tt = PROMPT_CPS * CHUNK
    nc = seq // tt
    rb = lambda b, c: b * nc + c
    pc = jnp.zeros((SUBLANES, LANES), F32).at[0, :ML_HEADS].set(i_bias).at[1, ML_HEADS:2 * ML_HEADS].set(f_bias)
    pr = jnp.stack([jnp.broadcast_to(i_bias[:, None], (ML_HEADS, tt)),
                    jnp.broadcast_to(f_bias[:, None], (ML_HEADS, tt))])
    full = lambda *shape: pl.BlockSpec(shape, lambda b, c: (0,) * len(shape))
    return pl.pallas_call(
        _mlstm_prompt_kernel,
        grid=(bsz, nc),
        in_specs=[
            pl.BlockSpec((tt, ML_QK), lambda b, c: (rb(b, c), 0)),
            pl.BlockSpec((tt, ML_QK), lambda b, c: (rb(b, c), 1)),
            pl.BlockSpec((tt, ML_V), lambda b, c: (rb(b, c), 1)),
            pl.BlockSpec((tt, ML_V), lambda b, c: (rb(b, c), 2)),
            pl.BlockSpec((tt, ML_V), lambda b, c: (rb(b, c), 3)),
            pl.BlockSpec((tt, LANES), lambda b, c: (rb(b, c), 0)),
            pl.BlockSpec((2 * ML_HEADS, tt), lambda b, c: (0, rb(b, c))),
            full(SUBLANES, LANES), full(2, ML_HEADS, tt), full(1, ML_DV),
        ],
        out_specs=[
            pl.BlockSpec((tt, ML_V), lambda b, c: (rb(b, c), 0)),
            pl.BlockSpec((1, ML_HEADS, ML_DK, ML_DV), lambda b, c: (b, 0, 0, 0)),
            pl.BlockSpec((1, ML_HEADS, ML_DK), lambda b, c: (b, 0, 0)),
            pl.BlockSpec((1, ML_HEADS, LANES), lambda b, c: (b, 0, 0)),
        ],
        out_shape=[
            jax.ShapeDtypeStruct((m_total, ML_V), BF16),
            jax.ShapeDtypeStruct((bsz, ML_HEADS, ML_DK, ML_DV), F32),
            jax.ShapeDtypeStruct((bsz, ML_HEADS, ML_DK), F32),
            jax.ShapeDtypeStruct((bsz, ML_HEADS, LANES), F32),
        ],
        scratch_shapes=[pltpu.VMEM((ML_HEADS, ML_DK, ML_DV), F32), pltpu.VMEM((ML_HEADS, ML_DK), F32),
                        pltpu.VMEM((ML_HEADS, LANES), F32)],
        compiler_params=_params("parallel", "arbitrary"),
        name="mlstm_prompt",
    )(zm, zm, zm, zm, zm, zs, zs_t, pc, pr, norm[None, :])


def _cmlp_prompt_kernel(u_ref, v_ref, z_ref, ws_ref, wb_ref, gain_ref, mix_ref, vrows_ref):
    r, c = _causal_masks()
    causal = r >= c
    for g in range(CM_GROUPS):
        gs = slice(g * CM_GROUP_DIM, (g + 1) * CM_GROUP_DIM)
        v = _gelu(v_ref[:, gs])
        v = v * lax.rsqrt(jnp.mean(v * v, axis=-1, keepdims=True) + EPS) * gain_ref[...]
        s = _mm(jnp.where(causal, ws_ref[g], 0.0), v) + wb_ref[:, g:g + 1]
        mix_ref[:, gs] = (_gelu(u_ref[:, gs]) * s * _silu(z_ref[:, gs])).astype(BF16)
        vrows_ref[0, :, gs] = v


def _cmlp_prompt(zm, m_total, bsz, seq, v_gain, ws, wb):
    nc = seq // CM_CHUNK
    rb = lambda b, c: b * nc + c
    col0 = (2 * ML_QK + 3 * ML_V) // CM_WIDTH
    return pl.pallas_call(
        _cmlp_prompt_kernel,
        grid=(bsz, nc),
        in_specs=[
            pl.BlockSpec((CM_CHUNK, CM_WIDTH), lambda b, c: (rb(b, c), col0)),
            pl.BlockSpec((CM_CHUNK, CM_WIDTH), lambda b, c: (rb(b, c), col0 + 1)),
            pl.BlockSpec((CM_CHUNK, CM_WIDTH), lambda b, c: (rb(b, c), col0 + 2)),
            pl.BlockSpec((CM_GROUPS, CM_CHUNK, CM_CHUNK), lambda b, c: (0, 0, 0)),
            pl.BlockSpec((CM_CHUNK, CM_GROUPS), lambda b, c: (0, 0)),
            pl.BlockSpec((1, CM_GROUP_DIM), lambda b, c: (0, 0)),
        ],
        out_specs=[
            pl.BlockSpec((CM_CHUNK, CM_WIDTH), lambda b, c: (rb(b, c), 0)),
            pl.BlockSpec((1, CM_CHUNK, CM_WIDTH), lambda b, c: (b, 0, 0)),
        ],
        out_shape=[jax.ShapeDtypeStruct((m_total, CM_WIDTH), BF16),
                   jax.ShapeDtypeStruct((bsz, CM_CHUNK, CM_WIDTH), F32)],
        compiler_params=_params("parallel", "arbitrary"),
        name="cmlp_prompt",
    )(zm, zm, zm, ws, wb.T, v_gain[None, :])


SEQ_BLOCK = SUBLANES
DEC_UNROLL = 2


def _conv_step(raw_ref, cin_ref, w_ref, cout_ref):
    u = raw_ref[...]
    out = w_ref[CONV_K - 1:CONV_K, :] * u
    for k in range(CONV_K - 1):
        out = out + w_ref[k:k + 1, :] * cin_ref[k]
    for k in range(CONV_K - 2):
        cout_ref[k] = cin_ref[k + 1]
    cout_ref[CONV_K - 2] = u
    return out


def _row0(row, fill=0.0):
    return jnp.where(_iota((SUBLANES, row.shape[1]), 0) == 0, row, fill)


def _stash_rows(dst_ref, val):
    for i in range(SEQ_BLOCK):
        dst_ref[i] = jnp.broadcast_to(val[i:i + 1, :], (SUBLANES, val.shape[1]))


def _gather_rows(src_ref):
    rid = _iota(src_ref.shape[1:], 0)
    acc = src_ref[0]
    for i in range(1, SEQ_BLOCK):
        acc = jnp.where(rid == i, src_ref[i], acc)
    return acc


def _ssd_decode_kernel(za_ref, xs_ref, b_ref, c_ref, zc_ref, cx_ref, cb_ref, cc_ref, wx_ref, bx_ref, wb_ref, bb_ref,
                       wc_ref, bc_ref, pc_ref, e_ref, a_ref, d_ref, nrm_ref, s_ref, *rest):
    mix_ref, ncx_ref, ncb_ref, ncc_ref, so_ref = rest[-5:]
    xs = _silu(_conv_step(xs_ref, cx_ref, wx_ref, ncx_ref) + bx_ref[...])
    bm = _silu(_conv_step(b_ref, cb_ref, wb_ref, ncb_ref) + bb_ref[...])
    cm = _silu(_conv_step(c_ref, cc_ref, wc_ref, ncc_ref) + bc_ref[...])
    dtx = _mm_hi(_softplus(zc_ref[...] + pc_ref[0:1, :]), e_ref[...])
    hi, mid, lo = _split3(jnp.exp(dtx * a_ref[...]))
    dx = dtx * xs
    rid = _iota((SUBLANES, SSD_GW), 0)
    ones_rows = jnp.where((_iota((SUBLANES, SSD_STATE), 0) >= 1) & (_iota((SUBLANES, SSD_STATE), 0) <= 3), 1.0, 0.0)
    hpg = SSD_HEADS // SSD_GROUPS
    seqs = range(SEQ_BLOCK)
    lmat = [jnp.where(rid == 0, dx[i:i + 1, :], jnp.where(rid == 1, hi[i:i + 1, :],
                      jnp.where(rid == 2, mid[i:i + 1, :], jnp.where(rid == 3, lo[i:i + 1, :], 0.0)))) for i in seqs]
    upd = [_mm_tn(lmat[i], jnp.concatenate([_row0(bm[i:i + 1, :]), ones_rows], axis=1)) for i in seqs]
    s_new = [s_ref[i].reshape(SSD_GW, SSD_STATE) * upd[i][:, SSD_STATE:] + upd[i][:, :SSD_STATE] for i in seqs]
    for i in seqs:
        so_ref[i] = s_new[i].reshape(hpg, SSD_HEAD_DIM, SSD_STATE)
    ys = [_mm_nt(jnp.broadcast_to(cm[i:i + 1, :], (SUBLANES, SSD_STATE)), s_new[i]) for i in seqs]
    y = ys[0]
    for i in seqs[1:]:
        y = jnp.where(rid == i, ys[i], y)
    y = y + d_ref[...] * xs
    y = y * _silu(za_ref[...])
    y = y * lax.rsqrt(jnp.mean(y * y, axis=-1, keepdims=True) + EPS) * nrm_ref[...]
    mix_ref[...] = y.astype(BF16)


def _stacked_state_io(states, layer, prev, block, index):
    spec = pl.BlockSpec((None,) + block, lambda *g: (layer,) + index(*g))
    extra_in = [] if prev is None else [prev]
    extra_specs = [] if prev is None else [pl.BlockSpec(memory_space=pl.ANY)]
    return spec, jax.ShapeDtypeStruct(states.shape, F32), extra_in, extra_specs


def _ssd_decode(zm, zs, mix, conv_t, states, layer, prev, mp, conv_w, conv_b, dt_bias, a_log, d_skip, norm):
    bsz = states.shape[1]
    hpg = SSD_HEADS // SSD_GROUPS
    st_spec, st_shape, extra_in, extra_specs = _stacked_state_io(
        states, layer, prev, (SEQ_BLOCK, hpg, SSD_HEAD_DIM, SSD_STATE), lambda s, g: (s, g, 0, 0))
    r0 = mp // SEQ_BLOCK
    pc = jnp.zeros((SUBLANES, LANES), F32).at[0, :SSD_HEADS].set(dt_bias)
    e_mat = (jnp.arange(SSD_WIDTH)[None, :] // SSD_HEAD_DIM == jnp.arange(LANES)[:, None]).astype(F32)
    a_x = jnp.repeat(-jnp.exp(a_log), SSD_HEAD_DIM)[None, :]
    nb = SSD_WIDTH // SSD_STATE
    zrow = lambda w, col: pl.BlockSpec((SEQ_BLOCK, w), lambda s, g: (r0 + s, col(g)))
    cst = lambda w, col: pl.BlockSpec((CONV_K - 1, SEQ_BLOCK, w), lambda s, g: (0, s, col(g)))
    par = lambda rows, w, col: pl.BlockSpec((rows, w), lambda s, g: (0, col(g)))
    mainb = 6 * D_MODEL // SSD_STATE
    return pl.pallas_call(
        _ssd_decode_kernel,
        grid=(bsz // SEQ_BLOCK, SSD_GROUPS),
        in_specs=[
            zrow(SSD_GW, lambda g: g), zrow(SSD_GW, lambda g: SSD_GROUPS + g),
            zrow(SSD_STATE, lambda g: mainb + g), zrow(SSD_STATE, lambda g: mainb + SSD_GROUPS + g),
            zrow(LANES, lambda g: 0),
            cst(SSD_GW, lambda g: g), cst(SSD_STATE, lambda g: nb + g), cst(SSD_STATE, lambda g: nb + SSD_GROUPS + g),
            par(CONV_K, SSD_GW, lambda g: g), par(1, SSD_GW, lambda g: g),
            par(CONV_K, SSD_STATE, lambda g: nb + g), par(1, SSD_STATE, lambda g: nb + g),
            par(CONV_K, SSD_STATE, lambda g: nb + SSD_GROUPS + g), par(1, SSD_STATE, lambda g: nb + SSD_GROUPS + g),
            par(SUBLANES, LANES, lambda g: 0), par(LANES, SSD_GW, lambda g: g),
            par(1, SSD_GW, lambda g: g), par(1, SSD_GW, lambda g: g), par(1, SSD_GW, lambda g: g),
            st_spec,
            pl.BlockSpec(memory_space=pl.ANY),
        ] + extra_specs,
        out_specs=[
            pl.BlockSpec((SEQ_BLOCK, SSD_GW), lambda s, g: (r0 + s, g)),
            cst(SSD_GW, lambda g: g), cst(SSD_STATE, lambda g: g), cst(SSD_STATE, lambda g: g),
            st_spec,
        ],
        out_shape=[
            jax.ShapeDtypeStruct(mix.shape, mix.dtype),
            jax.ShapeDtypeStruct((CONV_K - 1, bsz, SSD_WIDTH), F32),
            jax.ShapeDtypeStruct((CONV_K - 1, bsz, SSD_GROUPS * SSD_STATE), F32),
            jax.ShapeDtypeStruct((CONV_K - 1, bsz, SSD_GROUPS * SSD_STATE), F32),
            st_shape,
        ],
        input_output_aliases={20: 0} if prev is None else {20: 0, 21: 4},
        compiler_params=_params("parallel", "arbitrary"),
        name="ssd_decode",
    )(zm, zm, zm, zm, zs, conv_t, conv_t, conv_t, conv_w, conv_b[None, :], conv_w, conv_b[None, :], conv_w,
      conv_b[None, :], pc, e_mat, a_x, jnp.repeat(d_skip, SSD_HEAD_DIM)[None, :], norm[None, :], states, mix,
      *extra_in)


def _gdn_decode_kernel(q_ref, k_ref, v_ref, zb_ref, zc_ref, cq_ref, ck_ref, cv_ref, wq_ref, wk_ref, wv_ref, pc_ref,
                       nrm_ref, s_ref, *rest):
    mix_ref, ncq_ref, nck_ref, ncv_ref, so_ref, q_s, k_s, v_s, beta_s, eg_s, qk_s, o_s = rest[-12:]
    q_all = _silu(_conv_step(q_ref, cq_ref, wq_ref, ncq_ref))
    k_all = _silu(_conv_step(k_ref, ck_ref, wk_ref, nck_ref))
    _stash_rows(v_s, _silu(_conv_step(v_ref, cv_ref, wv_ref, ncv_ref)))
    zc = zc_ref[...]
    beta = jax.nn.sigmoid(zc)
    eg = jnp.exp(-jnp.exp(pc_ref[0:1, :]) * _softplus(zc + pc_ref[1:2, :]))
    heads = range(GDN_HG)
    hs = [slice(j * GDN_DK, (j + 1) * GDN_DK) for j in heads]
    qn, kn, beta_x, eg_x, qk_x = [], [], [], [], []
    shape = (SEQ_BLOCK, GDN_DK)
    for j in heads:
        qh, kh = q_all[:, hs[j]], k_all[:, hs[j]]
        qh = qh * (lax.rsqrt(jnp.sum(qh * qh, axis=-1, keepdims=True) + EPS) * (GDN_DK ** -0.5))
        kh = kh * lax.rsqrt(jnp.sum(kh * kh, axis=-1, keepdims=True) + EPS)
        qn.append(qh)
        kn.append(kh)
        beta_x.append(jnp.broadcast_to(beta[:, j:j + 1], shape))
        eg_x.append(jnp.broadcast_to(eg[:, GDN_HG + j:GDN_HG + j + 1], shape))
        qk_x.append(jnp.broadcast_to(jnp.sum(qh * kh, axis=-1, keepdims=True), shape))
    for ref, parts in ((q_s, qn), (k_s, kn), (beta_s, beta_x), (eg_s, eg_x), (qk_s, qk_x)):
        _stash_rows(ref, jnp.concatenate(parts, axis=1))

    rid = _iota((SUBLANES, GDN_DK), 0)

    def body(it, carry):
        seqs = [it * DEC_UNROLL + u for u in range(DEC_UNROLL)]
        k_b, q_b, v_b = [k_s[i] for i in seqs], [q_s[i] for i in seqs], [v_s[i] for i in seqs]
        beta_b, eg_b, qk_b = [beta_s[i] for i in seqs], [eg_s[i] for i in seqs], [qk_s[i] for i in seqs]
        units = [(u, j) for u in range(DEC_UNROLL) for j in heads]
        s_prev = [s_ref[seqs[u], j] for u, j in units]
        ks_qs = [_mm(jnp.where(rid == 0, k_b[u][:, hs[j]], q_b[u][:, hs[j]]), s_prev[n])
                 for n, (u, j) in enumerate(units)]
        vn = [beta_b[u][0:1, hs[j]] * (v_b[u][0:1, hs[j]] - eg_b[u][0:1, hs[j]] * ks_qs[n][0:1, :])
              for n, (u, j) in enumerate(units)]
        outer = [_mm_tn(_row0(k_b[u][:, hs[j]]), _row0(vn[n])) for n, (u, j) in enumerate(units)]
        for n, (u, j) in enumerate(units):
            so_ref[seqs[u], j] = s_prev[n] * eg_b[u][0:1, j * GDN_DK:j * GDN_DK + 1] + outer[n]
            o_row = eg_b[u][0:1, hs[j]] * ks_qs[n][1:2, :] + qk_b[u][0:1, hs[j]] * vn[n]
            o_s[seqs[u], :, hs[j]] = jnp.broadcast_to(o_row, (SUBLANES, GDN_DV))
        return carry

    lax.fori_loop(0, SEQ_BLOCK // DEC_UNROLL, body, 0)
    o_all = _gather_rows(o_s)
    for j in heads:
        o = o_all[:, hs[j]]
        o = o * lax.rsqrt(jnp.mean(o * o, axis=-1, keepdims=True) + EPS) * nrm_ref[...]
        mix_ref[:, hs[j]] = (o * _silu(zb_ref[:, hs[j]])).astype(BF16)


def _gdn_decode(zm, zs, mix, conv_t, states, layer, prev, mp, conv_w, dt_bias, a_log, norm):
    bsz = states.shape[1]
    st_spec, st_shape, extra_in, extra_specs = _stacked_state_io(
        states, layer, prev, (SEQ_BLOCK, GDN_HG, GDN_DK, GDN_DV), lambda s, h: (s, h, 0, 0))
    nhg = GDN_HEADS // GDN_HG
    w = GDN_HG * GDN_DK
    r0 = mp // SEQ_BLOCK
    col0 = 2 * D_MODEL // w
    pc = jnp.zeros((nhg, SUBLANES, LANES), F32)
    pc = pc.at[:, 0, GDN_HG:2 * GDN_HG].set(a_log.reshape(nhg, GDN_HG))
    pc = pc.at[:, 1, GDN_HG:2 * GDN_HG].set(dt_bias.reshape(nhg, GDN_HG))
    seg = lambda p: pl.BlockSpec((SEQ_BLOCK, w), lambda s, h: (r0 + s, col0 + p * nhg + h))
    cst = lambda p: pl.BlockSpec((CONV_K - 1, SEQ_BLOCK, w), lambda s, h: (0, s, p * nhg + h))
    wseg = lambda p: pl.BlockSpec((CONV_K, w), lambda s, h: (0, p * nhg + h))
    cout = pl.BlockSpec((CONV_K - 1, SEQ_BLOCK, w), lambda s, h: (0, s, h))
    row_scratch = pltpu.VMEM((SEQ_BLOCK, SUBLANES, w), F32)
    return pl.pallas_call(
        _gdn_decode_kernel,
        grid=(bsz // SEQ_BLOCK, nhg),
        in_specs=[
            seg(0), seg(1), seg(2), seg(3),
            pl.BlockSpec((SEQ_BLOCK, LANES), lambda s, h: (r0 + s, 1 + h)),
            cst(0), cst(1), cst(2), wseg(0), wseg(1), wseg(2),
            pl.BlockSpec((None, SUBLANES, LANES), lambda s, h: (h, 0, 0)),
            pl.BlockSpec((1, GDN_DV), lambda s, h: (0, 0)),
            st_spec,
            pl.BlockSpec(memory_space=pl.ANY),
        ] + extra_specs,
        out_specs=[
            pl.BlockSpec((SEQ_BLOCK, w), lambda s, h: (r0 + s, h)),
            cout, cout, cout,
            st_spec,
        ],
        out_shape=[
            jax.ShapeDtypeStruct(mix.shape, mix.dtype),
            jax.ShapeDtypeStruct((CONV_K - 1, bsz, GDN_QK), F32),
            jax.ShapeDtypeStruct((CONV_K - 1, bsz, GDN_QK), F32),
            jax.ShapeDtypeStruct((CONV_K - 1, bsz, GDN_V), F32),
            st_shape,
        ],
        scratch_shapes=[row_scratch] * 7,
        input_output_aliases={14: 0} if prev is None else {14: 0, 15: 4},
        compiler_params=_params("parallel", "arbitrary"),
        name="gdn_decode",
    )(zm, zm, zm, zm, zs, conv_t, conv_t, conv_t, conv_w, conv_w, conv_w, pc, norm[None, :], states, mix, *extra_in)


ML_HG = 4


def _mlstm_decode_kernel(q_ref, k_ref, v_ref, o_ref, zc_ref, g_ref, m_ref, n_ref, pc_ref, e_ref, nrm_ref, c_ref,
                         *rest):
    mix_ref, n_out, m_out, c_out, q_s, kb_s, v_s, sa_s, sbqk_s, den_s, h_s = rest[-11:]
    g = g_ref[...]
    logi = g + pc_ref[0:1, :]
    logf = -_softplus(-pltpu.roll(g + pc_ref[1:2, :], LANES - ML_HEADS, 1))
    m_prev = m_ref[...]
    m_new = jnp.maximum(logf + m_prev, logi)
    m_out[...] = m_new
    e_mat = e_ref[...]
    sa_x = _mm_hi(jnp.exp(logf + m_prev - m_new), e_mat)
    sb_x = _mm_hi(jnp.exp(logi - m_new), e_mat)
    em_x = _mm_hi(jnp.exp(-m_new), e_mat)
    q = q_ref[...]
    k = k_ref[...] * (ML_DK ** -0.5)
    n_prev = n_ref[...]
    n_out[...] = n_prev * sa_x + sb_x * k
    _stash_rows(q_s, q)
    _stash_rows(kb_s, sb_x * k)
    _stash_rows(v_s, v_ref[...])
    _stash_rows(sa_s, sa_x)
    heads = range(ML_HG)
    ks = [slice(j * ML_DK, (j + 1) * ML_DK) for j in heads]
    vs = [slice(j * ML_DV, (j + 1) * ML_DV) for j in heads]
    sbqk_x, den_x = [], []
    for j in heads:
        shape = (SEQ_BLOCK, ML_DK)
        qk = jnp.sum(q[:, ks[j]] * k[:, ks[j]], axis=-1, keepdims=True)
        qn = jnp.sum(q[:, ks[j]] * n_prev[:, ks[j]], axis=-1, keepdims=True)
        sbqk = sb_x[:, ks[j]] * jnp.broadcast_to(qk, shape)
        sbqk_x.append(sbqk)
        den_x.append(jnp.maximum(jnp.abs(sa_x[:, ks[j]] * jnp.broadcast_to(qn, shape) + sbqk), em_x[:, ks[j]]))
    _stash_rows(sbqk_s, jnp.concatenate(sbqk_x, axis=1))
    _stash_rows(den_s, jnp.concatenate(den_x, axis=1))

    def body(it, carry):
        seqs = [it * DEC_UNROLL + u for u in range(DEC_UNROLL)]
        q_b, kb_b, v_b = [q_s[i] for i in seqs], [kb_s[i] for i in seqs], [v_s[i] for i in seqs]
        sa_b, sbqk_b, den_b = [sa_s[i] for i in seqs], [sbqk_s[i] for i in seqs], [den_s[i] for i in seqs]
        units = [(u, j) for u in range(DEC_UNROLL) for j in heads]
        c_prev = [c_ref[seqs[u], j] for u, j in units]
        qc = [_mm(q_b[u][:, ks[j]], c_prev[n]) for n, (u, j) in enumerate(units)]
        outer = [_mm_tn(_row0(kb_b[u][:, ks[j]]), _row0(v_b[u][:, vs[j]])) for u, j in units]
        for n, (u, j) in enumerate(units):
            lane0 = slice(j * ML_DK, j * ML_DK + 1)
            c_out[seqs[u], j] = c_prev[n] * sa_b[u][0:1, lane0] + outer[n]
            num = sa_b[u][:, lane0] * qc[n] + sbqk_b[u][:, lane0] * v_b[u][:, vs[j]]
            h_s[seqs[u], :, vs[j]] = num / den_b[u][:, lane0]
        return carry

    lax.fori_loop(0, SEQ_BLOCK // DEC_UNROLL, body, 0)
    h_all = _gather_rows(h_s)
    for j in heads:
        h = h_all[:, vs[j]]
        h = h * lax.rsqrt(jnp.mean(h * h, axis=-1, keepdims=True) + EPS) * nrm_ref[...]
        mix_ref[:, vs[j]] = (h * jax.nn.sigmoid(o_ref[:, vs[j]]) * _silu(zc_ref[:, vs[j]])).astype(BF16)


def _mlstm_decode(zm, zs, mix, c_states, layer, prev, n0, m0, mp, i_bias, f_bias, norm):
    bsz = c_states.shape[1]
    st_spec, st_shape, extra_in, extra_specs = _stacked_state_io(
        c_states, layer, prev, (SEQ_BLOCK, ML_HG, ML_DK, ML_DV), lambda s, h: (s, h, 0, 0))
    nhg = ML_HEADS // ML_HG
    wk, wv = ML_HG * ML_DK, ML_HG * ML_DV
    r0 = mp // SEQ_BLOCK
    pc = jnp.zeros((SUBLANES, LANES), F32).at[0, :ML_HEADS].set(i_bias).at[1, ML_HEADS:2 * ML_HEADS].set(f_bias)
    e_mat = (jnp.arange(ML_QK)[None, :] // ML_DK == jnp.arange(LANES)[:, None]).astype(F32)
    zrow = lambda w, col: pl.BlockSpec((SEQ_BLOCK, w), lambda s, h: (r0 + s, col(h)))
    vcol = 2 * ML_QK // wv
    return pl.pallas_call(
        _mlstm_decode_kernel,
        grid=(bsz // SEQ_BLOCK, nhg),
        in_specs=[
            zrow(wk, lambda h: h), zrow(wk, lambda h: nhg + h),
            zrow(wv, lambda h: vcol + h), zrow(wv, lambda h: vcol + nhg + h), zrow(wv, lambda h: vcol + 2 * nhg + h),
            zrow(LANES, lambda h: 0),
            pl.BlockSpec((SEQ_BLOCK, LANES), lambda s, h: (s, 0)),
            pl.BlockSpec((SEQ_BLOCK, wk), lambda s, h: (s, h)),
            pl.BlockSpec((SUBLANES, LANES), lambda s, h: (0, 0)),
            pl.BlockSpec((LANES, wk), lambda s, h: (0, h)),
            pl.BlockSpec((1, ML_DV), lambda s, h: (0, 0)),
            st_spec,
            pl.BlockSpec(memory_space=pl.ANY),
        ] + extra_specs,
        out_specs=[
            pl.BlockSpec((SEQ_BLOCK, wv), lambda s, h: (r0 + s, h)),
            pl.BlockSpec((SEQ_BLOCK, wk), lambda s, h: (s, h)),
            pl.BlockSpec((SEQ_BLOCK, LANES), lambda s, h: (s, 0)),
            st_spec,
        ],
        out_shape=[
            jax.ShapeDtypeStruct(mix.shape, mix.dtype),
            jax.ShapeDtypeStruct((bsz, ML_QK), F32),
            jax.ShapeDtypeStruct((bsz, LANES), F32),
            st_shape,
        ],
        scratch_shapes=[pltpu.VMEM((SEQ_BLOCK, SUBLANES, w), F32) for w in (wk, wk, wv, wk, wk, wk, wv)],
        input_output_aliases={12: 0} if prev is None else {12: 0, 13: 3},
        compiler_params=_params("parallel", "arbitrary"),
        name="mlstm_decode",
    )(zm, zm, zm, zm, zm, zs, jnp.pad(m0, ((0, 0), (0, LANES - ML_HEADS))), n0.reshape(bsz, ML_QK), pc, e_mat,
      norm[None, :], c_states, mix, *extra_in)


def _cmlp_decode_kernel(u_ref, v_ref, z_ref, ws_ref, wb_ref, gain_ref, mixin_ref, mix_ref, vrows_ref):
    del mixin_ref
    for g in range(CM_GROUPS):
        gs = slice(g * CM_GROUP_DIM, (g + 1) * CM_GROUP_DIM)
        v = _gelu(v_ref[:, gs])
        v = v * lax.rsqrt(jnp.mean(v * v, axis=-1, keepdims=True) + EPS) * gain_ref[...]
        s = ws_ref[:, gs] * v + wb_ref[:, gs]
        mix_ref[:, gs] = (_gelu(u_ref[:, gs]) * s * _silu(z_ref[:, gs])).astype(BF16)
        vrows_ref[:, gs] = v


def _cmlp_decode(zm, mix, mp, bsz, v_gain, ws, wb):
    col0 = (2 * ML_QK + 3 * ML_V) // CM_WIDTH
    r0 = mp // bsz
    zrow = lambda col: pl.BlockSpec((bsz, CM_WIDTH), lambda i: (r0, col))
    par = pl.BlockSpec((1, CM_WIDTH), lambda i: (0, 0))
    return pl.pallas_call(
        _cmlp_decode_kernel,
        grid=(1,),
        in_specs=[zrow(col0), zrow(col0 + 1), zrow(col0 + 2), par, par,
                  pl.BlockSpec((1, CM_GROUP_DIM), lambda i: (0, 0)), pl.BlockSpec(memory_space=pl.ANY)],
        out_specs=[pl.BlockSpec((bsz, CM_WIDTH), lambda i: (r0, 0)), pl.BlockSpec((bsz, CM_WIDTH), lambda i: (0, 0))],
        out_shape=[jax.ShapeDtypeStruct(mix.shape, mix.dtype), jax.ShapeDtypeStruct((bsz, CM_WIDTH), F32)],
        input_output_aliases={6: 0},
        compiler_params=_params("arbitrary"),
        name="cmlp_decode",
    )(zm, zm, zm, jnp.repeat(ws[:, 0, 0], CM_GROUP_DIM)[None, :], jnp.repeat(wb[:, 0], CM_GROUP_DIM)[None, :],
      v_gain[None, :], mix)


def kernel(x_prompt, x_sample, state_ssd_conv, state_ssd, state_gdn_conv, state_gdn, state_mlstm_c,
           state_mlstm_n, state_mlstm_m, even_norm, even_w_in, ssd_conv_w, ssd_conv_b, ssd_dt_bias, ssd_a_log,
           ssd_d, ssd_norm, gdn_conv_w, gdn_dt_bias, gdn_a_log, gdn_norm, even_w_out, odd_norm, odd_w_in,
           mlstm_i_bias, mlstm_f_bias, mlstm_norm, cmlp_v_norm, cmlp_ws, cmlp_b, odd_w_out, final_norm):
    bp, seq, d = x_prompt.shape
    bs = x_sample.shape[0]
    mp = bp * seq
    mt = mp + bs
    x = jnp.concatenate([x_prompt.reshape(mp, d), x_sample.reshape(bs, d)], axis=0)

    keys = ("sc", "ss", "gc", "gs", "mc", "mn", "mm", "cv")
    outs_p = {k: [] for k in keys}
    outs_s = {k: [] for k in keys}
    ss_all = gs_all = mc_all = None
    even_wt_in = jnp.swapaxes(even_w_in, 1, 2)
    odd_wt_in = jnp.swapaxes(odd_w_in, 1, 2)
    for layer in range(DEPTH):
        i = layer // 2
        if layer % 2 == 0:
            w_main, w_small = _prep_even_w_in(even_wt_in, i)
            zm, zs = _inproj(x, even_norm[i], w_main, w_small, tn=1024)
            zs_t = zs[:mp].T
            mix_a, cx, cbc, st = _ssd_prompt(zm, zs, zs_t, mt, bp, seq, ssd_conv_w[i], ssd_conv_b[i],
                                             ssd_dt_bias[i], ssd_a_log[i], ssd_d[i], ssd_norm[i])
            mix_b, cq, ck, cv, gst = _gdn_prompt(zm, zs, zs_t, mt, bp, seq, gdn_conv_w[i], gdn_dt_bias[i],
                                                 gdn_a_log[i], gdn_norm[i])
            hpg = SSD_HEADS // SSD_GROUPS
            outs_p["sc"].append(jnp.concatenate([cx, cbc], axis=-1))
            outs_p["ss"].append(st.reshape(bp, SSD_GROUPS, SSD_STATE, hpg, SSD_HEAD_DIM).transpose(0, 1, 3, 4, 2)
                                .reshape(bp, SSD_HEADS, SSD_HEAD_DIM, SSD_STATE))
            outs_p["gc"].append(jnp.concatenate([cq, ck, cv], axis=-1))
            outs_p["gs"].append(gst)
            mix_a, ncx, ncb, ncc, ss_all = _ssd_decode(
                zm, zs, mix_a, jnp.swapaxes(state_ssd_conv[i], 0, 1), state_ssd, i, ss_all, mp, ssd_conv_w[i],
                ssd_conv_b[i], ssd_dt_bias[i], ssd_a_log[i], ssd_d[i], ssd_norm[i])
            mix_b, ncq, nck, ncv, gs_all = _gdn_decode(
                zm, zs, mix_b, jnp.swapaxes(state_gdn_conv[i], 0, 1), state_gdn, i, gs_all, mp, gdn_conv_w[i],
                gdn_dt_bias[i], gdn_a_log[i], gdn_norm[i])
            outs_s["sc"].append(jnp.swapaxes(jnp.concatenate([ncx, ncb, ncc], axis=-1), 0, 1))
            outs_s["gc"].append(jnp.swapaxes(jnp.concatenate([ncq, nck, ncv], axis=-1), 0, 1))
            w_out = even_w_out[i].astype(BF16)
        else:
            w_main, w_small = _prep_odd_w_in(odd_wt_in, i)
            zm, zs = _inproj(x, odd_norm[i], w_main, w_small, tn=1024)
            zs_t = zs[:mp].T
            mix_a, c_p, n_p, m_p = _mlstm_prompt(zm, zs, zs_t, mt, bp, seq, mlstm_i_bias[i], mlstm_f_bias[i],
                                                 mlstm_norm[i])
            mix_b, v_rows = _cmlp_prompt(zm, mt, bp, seq, cmlp_v_norm[i], cmlp_ws[i], cmlp_b[i])
            outs_p["mc"].append(c_p); outs_p["mn"].append(n_p); outs_p["mm"].append(m_p[:, :, 0])
            outs_p["cv"].append(v_rows)
            mix_a, n_s, m_s, mc_all = _mlstm_decode(zm, zs, mix_a, state_mlstm_c, i, mc_all, state_mlstm_n[i],
                                                    state_mlstm_m[i], mp, mlstm_i_bias[i], mlstm_f_bias[i],
                                                    mlstm_norm[i])
            mix_b, v_row_s = _cmlp_decode(zm, mix_b, mp, bs, cmlp_v_norm[i], cmlp_ws[i], cmlp_b[i])
            outs_s["mn"].append(n_s.reshape(bs, ML_HEADS, ML_DK))
            outs_s["mm"].append(m_s[:, :ML_HEADS])
            outs_s["cv"].append(v_row_s.reshape(bs, 1, CM_WIDTH))
            w_out = odd_w_out[i].astype(BF16)
        x = _outproj(x, mix_a, mix_b, w_out)

    y_p = _final_norm(x, final_norm, 0, mp, 512).reshape(bp, seq, d)
    y_s = _final_norm(x, final_norm, mp, bs, bs).reshape(bs, 1, d)
    st = lambda o, k: jnp.stack(o[k])
    return (y_p, y_s, st(outs_p, "sc"), st(outs_s, "sc"), st(outs_p, "ss"), ss_all,
            st(outs_p, "gc"), st(outs_s, "gc"), st(outs_p, "gs"), gs_all,
            st(outs_p, "mc"), mc_all, st(outs_p, "mn"), st(outs_s, "mn"),
            st(outs_p, "mm"), st(outs_s, "mm"), st(outs_p, "cv"), st(outs_s, "cv"))
```

```python
import jax
import jax.numpy as jnp
import numpy as np
from jax import lax
from jax.experimental import pallas as pl
from jax.experimental.pallas import tpu as pltpu

F32 = jnp.float32
BF16 = jnp.bfloat16
HI = lax.Precision.HIGHEST

D_MODEL = 2048
DEPTH = 4
CHUNK = 128
CONV_K = 4
EPS = 1e-6
NEG = -1e30

SSD_WIDTH = D_MODEL
SSD_HEAD_DIM = 64
SSD_HEADS = SSD_WIDTH // SSD_HEAD_DIM
SSD_STATE = 128
SSD_GROUPS = 4
SSD_GW = SSD_WIDTH // SSD_GROUPS
SSD_BC = 2 * SSD_GROUPS * SSD_STATE
SSD_CONV_DIM = SSD_WIDTH + SSD_BC
GDN_HEADS = 16
GDN_DK = 128
GDN_DV = 128
GDN_QK = GDN_HEADS * GDN_DK
GDN_V = GDN_HEADS * GDN_DV
GDN_CONV_DIM = 2 * GDN_QK + GDN_V
GDN_HG = 8
PROMPT_CPS = 2
ML_HEADS = 8
ML_DK = 128
ML_DV = 256
ML_QK = ML_HEADS * ML_DK
ML_V = ML_HEADS * ML_DV
CM_WIDTH = D_MODEL // 2
CM_GROUPS = 8
CM_GROUP_DIM = CM_WIDTH // CM_GROUPS
CM_CHUNK = 128

LANES = 128
SUBLANES = 8

EVEN_MAIN = 6 * D_MODEL + SSD_BC
EVEN_SMALL = 3 * LANES
ODD_MAIN = 2 * ML_QK + 3 * ML_V + 3 * CM_WIDTH
ODD_SMALL = LANES

VMEM_LIMIT = 56 * 1024 * 1024
ROW_TILE_CAP = 1040
BF16_SUBLANES = 16


def _row_tile(m):
    return max(t for t in range(BF16_SUBLANES, ROW_TILE_CAP + 1, BF16_SUBLANES) if m % t == 0)


def _silu(x):
    return x * jax.nn.sigmoid(x)


def _softplus(x):
    return jnp.maximum(x, 0.0) + jnp.log1p(jnp.exp(-jnp.abs(x)))


def _gelu(x):
    return 0.5 * x * (1.0 + jnp.tanh(np.sqrt(2.0 / np.pi).astype(np.float32) * (x + 0.044715 * (x * x * x))))


def _mm(a, b):
    return jnp.dot(a.astype(BF16), b.astype(BF16), preferred_element_type=F32)


def _mm_nt(a, b):
    return lax.dot_general(a.astype(BF16), b.astype(BF16), (((1,), (1,)), ((), ())), preferred_element_type=F32)


def _mm_tn(a, b):
    return lax.dot_general(a.astype(BF16), b.astype(BF16), (((0,), (0,)), ((), ())), preferred_element_type=F32)


def _mm_hi(a, b):
    return jnp.dot(a, b, precision=HI, preferred_element_type=F32)


def _split3(x):
    hi = x.astype(BF16).astype(F32)
    r1 = x - hi
    mid = r1.astype(BF16).astype(F32)
    lo = (r1 - mid).astype(BF16).astype(F32)
    return hi, mid, lo


def _mm_sel(a, sel):
    hi, mid, lo = _split3(a)
    return (_mm(hi, sel) + _mm(mid, sel)) + _mm(lo, sel)


def _iota(shape, axis):
    return lax.broadcasted_iota(jnp.int32, shape, axis)


def _params(*sem):
    return pltpu.CompilerParams(dimension_semantics=sem, vmem_limit_bytes=VMEM_LIMIT)


def _inproj_kernel(x_ref, g_ref, w_ref, ws_ref, z_ref, zs_ref, xn_ref):
    @pl.when(pl.program_id(1) == 0)
    def _():
        x = x_ref[...]
        y = x * lax.rsqrt(jnp.mean(x * x, axis=-1, keepdims=True) + EPS)
        xn = (y * g_ref[...]).astype(BF16)
        xn_ref[...] = xn
        zs_ref[...] = _mm_nt(xn, ws_ref[...])

    z_ref[...] = _mm_nt(xn_ref[...], w_ref[...])


def _inproj(x, g, w_main, w_small, tn):
    m, d = x.shape
    n = w_main.shape[0]
    ns = w_small.shape[0]
    tm = _row_tile(m)
    return pl.pallas_call(
        _inproj_kernel,
        grid=(m // tm, n // tn),
        in_specs=[
            pl.BlockSpec((tm, d), lambda i, j: (i, 0)),
            pl.BlockSpec((1, d), lambda i, j: (0, 0)),
            pl.BlockSpec((tn, d), lambda i, j: (j, 0)),
            pl.BlockSpec((ns, d), lambda i, j: (0, 0)),
        ],
        out_specs=[
            pl.BlockSpec((tm, tn), lambda i, j: (i, j)),
            pl.BlockSpec((tm, ns), lambda i, j: (i, 0)),
        ],
        out_shape=[jax.ShapeDtypeStruct((m, n), F32), jax.ShapeDtypeStruct((m, ns), F32)],
        scratch_shapes=[pltpu.VMEM((tm, d), BF16)],
        compiler_params=_params("parallel", "arbitrary"),
        name="inproj",
    )(x, g.reshape(1, d), w_main, w_small)


def _outproj_kernel(x_ref, ma_ref, mb_ref, wa_ref, wb_ref, o_ref):
    o_ref[...] = (x_ref[...] + jnp.dot(ma_ref[...], wa_ref[...], preferred_element_type=F32)
                  + jnp.dot(mb_ref[...], wb_ref[...], preferred_element_type=F32))


def _outproj(x, mix_a, mix_b, w):
    m, d = x.shape
    ka, kb = mix_a.shape[1], mix_b.shape[1]
    tm, tn = _row_tile(m), 512
    return pl.pallas_call(
        _outproj_kernel,
        grid=(m // tm, d // tn),
        in_specs=[
            pl.BlockSpec((tm, tn), lambda i, j: (i, j)),
            pl.BlockSpec((tm, ka), lambda i, j: (i, 0)),
            pl.BlockSpec((tm, kb), lambda i, j: (i, 0)),
            pl.BlockSpec((ka, tn), lambda i, j: (0, j)),
            pl.BlockSpec((kb, tn), lambda i, j: (ka // kb, j)),
        ],
        out_specs=pl.BlockSpec((tm, tn), lambda i, j: (i, j)),
        out_shape=jax.ShapeDtypeStruct((m, d), F32),
        compiler_params=_params("parallel", "arbitrary"),
        name="outproj",
    )(x, mix_a, mix_b, w, w)


def _final_norm_kernel(x_ref, g_ref, o_ref):
    x = x_ref[...]
    o_ref[...] = x * lax.rsqrt(jnp.mean(x * x, axis=-1, keepdims=True) + EPS) * g_ref[...]


def _final_norm(x, g, row0, rows, tm):
    d = x.shape[1]
    return pl.pallas_call(
        _final_norm_kernel,
        grid=(rows // tm,),
        in_specs=[pl.BlockSpec((tm, d), lambda i: (row0 // tm + i, 0)), pl.BlockSpec((1, d), lambda i: (0, 0))],
        out_specs=pl.BlockSpec((tm, d), lambda i: (i, 0)),
        out_shape=jax.ShapeDtypeStruct((rows, d), F32),
        compiler_params=_params("parallel"),
        name="final_norm",
    )(x, g.reshape(1, d))


REPACK_TN = 512


def _repack_kernel(a_ref, b_ref, o_ref, *, shift, lo, hi):
    j = pl.program_id(0)
    shifted = (j >= lo) & (j < hi)

    @pl.when(shifted)
    def _():
        o_ref[...] = jnp.concatenate([a_ref[shift:, :], b_ref[...]], axis=0).astype(BF16)

    @pl.when(jnp.logical_not(shifted))
    def _():
        o_ref[...] = a_ref[...].astype(BF16)


def _repack(wt_all, layer, n_out, a_idx, shift, lo, hi):
    d = wt_all.shape[2]
    per = REPACK_TN // shift
    kern = lambda a, b, o: _repack_kernel(a, b, o, shift=shift, lo=lo, hi=hi)
    return pl.pallas_call(
        kern,
        grid=(n_out // REPACK_TN,),
        in_specs=[pl.BlockSpec((None, REPACK_TN, d), lambda j: (layer, a_idx(j), 0)),
                  pl.BlockSpec((None, shift, d), lambda j: (layer, (a_idx(j) + 1) * per, 0))],
        out_specs=pl.BlockSpec((REPACK_TN, d), lambda j: (j, 0)),
        out_shape=jax.ShapeDtypeStruct((n_out, d), BF16),
        compiler_params=_params("parallel"),
        name="repack",
    )(wt_all, wt_all)


def _small_pack_kernel(*refs, layout):
    pieces, o_ref = refs[:-1], refs[-1]
    d = o_ref.shape[1]
    for blk, idxs in enumerate(layout):
        rows = [pieces[i][...] for i in idxs] + [jnp.zeros((LANES - SUBLANES * len(idxs), d), F32)]
        o_ref[blk * LANES:(blk + 1) * LANES, :] = jnp.concatenate(rows, axis=0).astype(BF16)


def _small_pack(wt_all, layer, src_rows, layout):
    d = wt_all.shape[2]
    kern = lambda *refs: _small_pack_kernel(*refs, layout=layout)
    spec = lambda r: pl.BlockSpec((None, SUBLANES, d), lambda i: (layer, r // SUBLANES, 0))
    return pl.pallas_call(
        kern,
        grid=(1,),
        in_specs=[spec(r) for r in src_rows],
        out_specs=pl.BlockSpec((len(layout) * LANES, d), lambda i: (0, 0)),
        out_shape=jax.ShapeDtypeStruct((len(layout) * LANES, d), BF16),
        compiler_params=_params("arbitrary"),
        name="small_pack",
    )(*([wt_all] * len(src_rows)))


def _prep_even_w_in(wt_all, layer):
    o_bc = 2 * SSD_WIDTH
    o_dt = o_bc + SSD_BC
    o_q = o_dt + SSD_HEADS
    o_beta = o_q + GDN_CONV_DIM + GDN_V
    o_g = o_beta + GDN_HEADS
    n1, n2 = o_bc // REPACK_TN, (o_bc + o_beta - o_q) // REPACK_TN
    src2, src3 = o_dt // REPACK_TN, o_bc // REPACK_TN
    a_idx = lambda j: jnp.where(j < n1, j, jnp.where(j < n2, j - n1 + src2, j - n2 + src3))
    main = _repack(wt_all, layer, EVEN_MAIN, a_idx, o_q - o_dt, n1, n2)
    n_dt = SSD_HEADS // SUBLANES
    src = [o_dt + SUBLANES * i for i in range(n_dt)]
    layout = [tuple(range(n_dt))]
    for hg in range(GDN_HEADS // GDN_HG):
        src += [o_beta + hg * GDN_HG, o_g + hg * GDN_HG]
        layout.append((len(src) - 2, len(src) - 1))
    return main, _small_pack(wt_all, layer, src, layout)


def _prep_odd_w_in(wt_all, layer):
    o1 = 2 * ML_QK + 3 * ML_V
    o2 = o1 + 2 * ML_HEADS
    main = _repack(wt_all, layer, ODD_MAIN, lambda j: j, o2 - o1, o1 // REPACK_TN, ODD_MAIN // REPACK_TN)
    small = _small_pack(wt_all, layer, [o1, o1 + ML_HEADS], [(0, 1)])
    return main, small


def _conv_chunk(ext_ref, u, w_ref, first):
    t = u.shape[0]

    @pl.when(first)
    def _():
        ext_ref[0:SUBLANES, :] = jnp.zeros((SUBLANES, ext_ref.shape[1]), F32)

    prev = ext_ref[0:SUBLANES, :]
    rid = _iota((SUBLANES, u.shape[1]), 0)
    out = w_ref[CONV_K - 1:CONV_K, :] * u
    for k in range(1, CONV_K):
        rolled = pltpu.roll(u, k, 0)
        head = jnp.where(rid < k, pltpu.roll(prev, k, 0), rolled[0:SUBLANES, :])
        shifted = jnp.concatenate([head, rolled[SUBLANES:, :]], axis=0)
        out = out + w_ref[CONV_K - 1 - k:CONV_K - k, :] * shifted
    ext_ref[0:SUBLANES, :] = u[t - SUBLANES:, :]
    return out


def _conv_tail(ext_ref):
    return ext_ref[SUBLANES - (CONV_K - 1):SUBLANES, :]


def _causal_masks():
    r = _iota((CHUNK, CHUNK), 0)
    c = _iota((CHUNK, CHUNK), 1)
    return r, c


def _ssd_prompt_kernel(za_ref, xs_ref, bc_ref, zc_ref, zr_ref, wx_ref, bx_ref, wbc_ref, bbc_ref, pc_ref, pr_ref,
                       d_ref, nrm_ref, mix_ref, cx_ref, cbc_ref, st_ref, extx, extbc, s_ref):
    c_id = pl.program_id(1)
    first = c_id == 0
    last = c_id == pl.num_programs(1) - 1

    @pl.when(first)
    def _():
        s_ref[...] = jnp.zeros(s_ref.shape, F32)

    xs = _silu(_conv_chunk(extx, xs_ref[...], wx_ref, first) + bx_ref[...])
    bc = _silu(_conv_chunk(extbc, bc_ref[...], wbc_ref, first) + bbc_ref[...])

    r, c = _causal_masks()
    causal = r >= c
    tril = jnp.where(causal, 1.0, 0.0)
    triu = jnp.where(r <= c, 1.0, 0.0)
    dt = _softplus(zc_ref[...] + pc_ref[0:1, :])
    la = _mm_hi(tril, dt * (-jnp.exp(pc_ref[1:2, :])))
    dtr = _softplus(zr_ref[...] + pr_ref[0])
    lar = _mm_hi(dtr * (-jnp.exp(pr_ref[1])), triu)
    la_last = la[CHUNK - 1:CHUNK, :]
    e_mat = jnp.where((_iota((LANES, SSD_WIDTH), 1) >> 6) == _iota((LANES, SSD_WIDTH), 0), 1.0, 0.0)
    ela_x = _mm_sel(jnp.exp(la), e_mat)
    wsx = _mm_sel(jnp.exp(la_last - la) * dt, e_mat)
    dec_x = _mm_sel(jnp.broadcast_to(jnp.exp(la_last), (SUBLANES, LANES)), e_mat)[0:1, :]
    lane_lo = _iota((CHUNK, LANES), 1) < SSD_HEAD_DIM

    hpg = SSD_HEADS // SSD_GROUPS
    for g in range(SSD_GROUPS):
        gs = slice(g * SSD_GW, (g + 1) * SSD_GW)
        bg = bc[:, g * SSD_STATE:(g + 1) * SSD_STATE]
        cg = bc[:, SSD_GROUPS * SSD_STATE + g * SSD_STATE:SSD_GROUPS * SSD_STATE + (g + 1) * SSD_STATE]
        cb = _mm_nt(cg, bg)
        ys = []
        for pair in range(hpg // 2):
            h0 = g * hpg + 2 * pair
            xpair = xs[:, h0 * SSD_HEAD_DIM:(h0 + 2) * SSD_HEAD_DIM]
            halves = []
            for hh in (h0, h0 + 1):
                seg = jnp.where(causal, la[:, hh:hh + 1] - lar[hh:hh + 1, :], NEG)
                lmat = jnp.exp(seg) * cb * dtr[hh:hh + 1, :]
                halves.append(_mm(lmat, xpair))
            ys.append(jnp.where(lane_lo, halves[0], halves[1]))
        y = jnp.concatenate(ys, axis=1)
        s_prev = s_ref[g]
        y = y + _mm(cg, s_prev) * ela_x[:, gs] + d_ref[:, gs] * xs[:, gs]
        y = y * _silu(za_ref[:, gs])
        y = y * lax.rsqrt(jnp.mean(y * y, axis=-1, keepdims=True) + EPS) * nrm_ref[:, gs]
        mix_ref[:, gs] = y.astype(BF16)
        s_ref[g] = s_prev * dec_x[:, gs] + _mm_tn(bg, xs[:, gs] * wsx[:, gs])

    @pl.when(last)
    def _():
        st_ref[0] = s_ref[...]
        cx_ref[0] = _conv_tail(extx)
        cbc_ref[0] = _conv_tail(extbc)


def _ssd_prompt(zm, zs, zs_t, m_total, bsz, seq, conv_w, conv_b, dt_bias, a_log, d_skip, norm):
    nc = seq // CHUNK
    rb = lambda b, c: b * nc + c
    pc = jnp.zeros((SUBLANES, LANES), F32).at[0, :SSD_HEADS].set(dt_bias).at[1, :SSD_HEADS].set(a_log)
    pr = jnp.stack([jnp.broadcast_to(dt_bias[:, None], (SSD_HEADS, CHUNK)),
                    jnp.broadcast_to(a_log[:, None], (SSD_HEADS, CHUNK))])
    full = lambda *shape: pl.BlockSpec(shape, lambda b, c: (0,) * len(shape))
    return pl.pallas_call(
        _ssd_prompt_kernel,
        grid=(bsz, nc),
        in_specs=[
            pl.BlockSpec((CHUNK, SSD_WIDTH), lambda b, c: (rb(b, c), 0)),
            pl.BlockSpec((CHUNK, SSD_WIDTH), lambda b, c: (rb(b, c), 1)),
            pl.BlockSpec((CHUNK, SSD_BC), lambda b, c: (rb(b, c), 6 * D_MODEL // SSD_BC)),
            pl.BlockSpec((CHUNK, LANES), lambda b, c: (rb(b, c), 0)),
            pl.BlockSpec((SSD_HEADS, CHUNK), lambda b, c: (0, rb(b, c))),
            full(CONV_K, SSD_WIDTH), full(1, SSD_WIDTH), full(CONV_K, SSD_BC), full(1, SSD_BC),
            full(SUBLANES, LANES), full(2, SSD_HEADS, CHUNK), full(1, SSD_WIDTH), full(1, SSD_WIDTH),
        ],
        out_specs=[
            pl.BlockSpec((CHUNK, SSD_WIDTH), lambda b, c: (rb(b, c), 0)),
            pl.BlockSpec((1, CONV_K - 1, SSD_WIDTH), lambda b, c: (b, 0, 0)),
            pl.BlockSpec((1, CONV_K - 1, SSD_BC), lambda b, c: (b, 0, 0)),
            pl.BlockSpec((1, SSD_GROUPS, SSD_STATE, SSD_GW), lambda b, c: (b, 0, 0, 0)),
        ],
        out_shape=[
            jax.ShapeDtypeStruct((m_total, SSD_WIDTH), BF16),
            jax.ShapeDtypeStruct((bsz, CONV_K - 1, SSD_WIDTH), F32),
            jax.ShapeDtypeStruct((bsz, CONV_K - 1, SSD_BC), F32),
            jax.ShapeDtypeStruct((bsz, SSD_GROUPS, SSD_STATE, SSD_GW), F32),
        ],
        scratch_shapes=[pltpu.VMEM((SUBLANES, SSD_WIDTH), F32), pltpu.VMEM((SUBLANES, SSD_BC), F32),
                        pltpu.VMEM((SSD_GROUPS, SSD_STATE, SSD_GW), F32)],
        compiler_params=_params("parallel", "arbitrary"),
        name="ssd_prompt",
    )(zm, zm, zm, zs, zs_t, conv_w[:, :SSD_WIDTH], conv_b[None, :SSD_WIDTH], conv_w[:, SSD_WIDTH:],
      conv_b[None, SSD_WIDTH:], pc, pr, jnp.repeat(d_skip, SSD_HEAD_DIM)[None, :], norm[None, :])


def _tri_inverse(mats, r, c):
    def corner(level):
        return ((r >> (level + 1)) == (c >> (level + 1))) & (((r >> level) & 1) == 1) & (((c >> level) & 1) == 0)

    eye = jnp.where(r == c, 1.0, 0.0)
    ts = [eye - jnp.where(corner(0), a, 0.0) for a in mats]
    for level in range(1, 7):
        cm = corner(level)
        xs = [_mm(t, jnp.where(cm, a, 0.0)) for t, a in zip(ts, mats)]
        ts = [t - _mm(x, t) for t, x in zip(ts, xs)]
    return ts


def _gdn_prompt_kernel(q_ref, k_ref, v_ref, zb_ref, zc_ref, zr_ref, wq_ref, wk_ref, wv_ref, pc_ref, pr_ref, nrm_ref,
                       mix_ref, cq_ref, ck_ref, cv_ref, st_ref, extq, extk, extv, s_ref):
    c_id = pl.program_id(2)
    first = c_id == 0
    last = c_id == pl.num_programs(2) - 1

    @pl.when(first)
    def _():
        s_ref[...] = jnp.zeros(s_ref.shape, F32)

    q_all = _silu(_conv_chunk(extq, q_ref[...], wq_ref, first))
    k_all = _silu(_conv_chunk(extk, k_ref[...], wk_ref, first))
    v_all = _silu(_conv_chunk(extv, v_ref[...], wv_ref, first))

    r, c = _causal_masks()
    causal = r >= c
    strict = r > c
    tril = jnp.where(causal, 1.0, 0.0)
    triu = jnp.where(r <= c, 1.0, 0.0)
    chunks = range(PROMPT_CPS)
    rows = [slice(ch * CHUNK, (ch + 1) * CHUNK) for ch in chunks]
    zc = zc_ref[...]
    beta_c = jax.nn.sigmoid(zc)
    g_c = -jnp.exp(pc_ref[0:1, :]) * _softplus(zc + pc_ref[1:2, :])
    gc_c = [_mm_hi(tril, g_c[rw, :]) for rw in rows]
    g_r = -jnp.exp(pr_ref[0]) * _softplus(zr_ref[GDN_HG:2 * GDN_HG, :] + pr_ref[1])
    gc_r = [_mm_hi(g_r[:, rw], triu) for rw in rows]

    heads = range(GDN_HG)
    hs = [slice(j * GDN_DK, (j + 1) * GDN_DK) for j in heads]
    units = [(ch, j) for ch in chunks for j in heads]
    qh = [q_all[rows[ch], hs[j]] for ch, j in units]
    kh = [k_all[rows[ch], hs[j]] for ch, j in units]
    qh = [x * (lax.rsqrt(jnp.sum(x * x, axis=-1, keepdims=True) + EPS) * (GDN_DK ** -0.5)) for x in qh]
    kh = [x * lax.rsqrt(jnp.sum(x * x, axis=-1, keepdims=True) + EPS) for x in kh]
    gcc = [gc_c[ch][:, GDN_HG + j:GDN_HG + j + 1] for ch, j in units]
    beta = [beta_c[rows[ch], j:j + 1] for ch, j in units]
    gam = [jnp.exp(jnp.where(causal, gcc[u] - gc_r[ch][j:j + 1, :], NEG)) for u, (ch, j) in enumerate(units)]
    qkk = [_mm_nt(jnp.concatenate([qh[u], kh[u]], axis=0), kh[u]) for u in range(len(units))]
    aqk = [qkk[u][:CHUNK] * gam[u] for u in range(len(units))]
    tinv = _tri_inverse([jnp.where(strict, beta[u] * qkk[u][CHUNK:] * gam[u], 0.0) for u in range(len(units))], r, c)
    egc = [jnp.exp(g) for g in gcc]
    uw = [_mm(tinv[u], jnp.concatenate([beta[u] * v_all[rows[ch], hs[j]], (beta[u] * egc[u]) * kh[u]], axis=1))
          for u, (ch, j) in enumerate(units)]
    s_cur = [s_ref[j] for j in heads]
    for ch in chunks:
        us = [ch * GDN_HG + j for j in heads]
        ws_qs = [_mm(jnp.concatenate([uw[u][:, GDN_DV:], qh[u] * egc[u]], axis=0), s_cur[j]) for j, u in enumerate(us)]
        vn = [uw[u][:, :GDN_DV] - ws_qs[j][:CHUNK] for j, u in enumerate(us)]
        o = [ws_qs[j][CHUNK:] + _mm(aqk[u], vn[j]) for j, u in enumerate(us)]
        nxt = []
        for j, u in enumerate(us):
            gc_last = gcc[u][CHUNK - 1:CHUNK, :]
            nxt.append(s_cur[j] * jnp.exp(gc_last) + _mm_tn(kh[u] * jnp.exp(gc_last - gcc[u]), vn[j]))
        s_cur = nxt
        for j in heads:
            on = o[j] * lax.rsqrt(jnp.mean(o[j] * o[j], axis=-1, keepdims=True) + EPS) * nrm_ref[...]
            mix_ref[rows[ch], hs[j]] = (on * _silu(zb_ref[rows[ch], hs[j]])).astype(BF16)
    for j in heads:
        s_ref[j] = s_cur[j]

    @pl.when(last)
    def _():
        st_ref[0] = s_ref[...]
        cq_ref[0] = _conv_tail(extq)
        ck_ref[0] = _conv_tail(extk)
        cv_ref[0] = _conv_tail(extv)


def _gdn_prompt(zm, zs, zs_t, m_total, bsz, seq, conv_w, dt_bias, a_log, norm):
    tt = PROMPT_CPS * CHUNK
    nc = seq // tt
    nhg = GDN_HEADS // GDN_HG
    w = GDN_HG * GDN_DK
    rb = lambda b, c: b * nc + c
    col0 = 2 * D_MODEL // w
    pc = jnp.zeros((nhg, SUBLANES, LANES), F32)
    pc = pc.at[:, 0, GDN_HG:2 * GDN_HG].set(a_log.reshape(nhg, GDN_HG))
    pc = pc.at[:, 1, GDN_HG:2 * GDN_HG].set(dt_bias.reshape(nhg, GDN_HG))
    pr = jnp.stack([jnp.broadcast_to(a_log.reshape(nhg, GDN_HG, 1), (nhg, GDN_HG, tt)),
                    jnp.broadcast_to(dt_bias.reshape(nhg, GDN_HG, 1), (nhg, GDN_HG, tt))], axis=1)
    seg = lambda s: pl.BlockSpec((tt, w), lambda b, h, c: (rb(b, c), col0 + s * nhg + h))
    wseg = lambda s: pl.BlockSpec((CONV_K, w), lambda b, h, c: (0, s * nhg + h))
    cout = pl.BlockSpec((1, CONV_K - 1, w), lambda b, h, c: (b, 0, h))
    return pl.pallas_call(
        _gdn_prompt_kernel,
        grid=(bsz, nhg, nc),
        in_specs=[
            seg(0), seg(1), seg(2), seg(3),
            pl.BlockSpec((tt, LANES), lambda b, h, c: (rb(b, c), 1 + h)),
            pl.BlockSpec((2 * GDN_HG, tt), lambda b, h, c: ((1 + h) * LANES // (2 * GDN_HG), rb(b, c))),
            wseg(0), wseg(1), wseg(2),
            pl.BlockSpec((None, SUBLANES, LANES), lambda b, h, c: (h, 0, 0)),
            pl.BlockSpec((None, 2, GDN_HG, tt), lambda b, h, c: (h, 0, 0, 0)),
            pl.BlockSpec((1, GDN_DV), lambda b, h, c: (0, 0)),
        ],
        out_specs=[
            pl.BlockSpec((tt, w), lambda b, h, c: (rb(b, c), h)),
            cout, cout, cout,
            pl.BlockSpec((1, GDN_HG, GDN_DK, GDN_DV), lambda b, h, c: (b, h, 0, 0)),
        ],
        out_shape=[
            jax.ShapeDtypeStruct((m_total, GDN_V), BF16),
            jax.ShapeDtypeStruct((bsz, CONV_K - 1, GDN_QK), F32),
            jax.ShapeDtypeStruct((bsz, CONV_K - 1, GDN_QK), F32),
            jax.ShapeDtypeStruct((bsz, CONV_K - 1, GDN_V), F32),
            jax.ShapeDtypeStruct((bsz, GDN_HEADS, GDN_DK, GDN_DV), F32),
        ],
        scratch_shapes=[pltpu.VMEM((SUBLANES, w), F32)] * 3 + [pltpu.VMEM((GDN_HG, GDN_DK, GDN_DV), F32)],
        compiler_params=_params("parallel", "parallel", "arbitrary"),
        name="gdn_prompt",
    )(zm, zm, zm, zm, zs, zs_t, conv_w, conv_w, conv_w, pc, pr, norm[None, :])


def _mlstm_prompt_kernel(q_ref, k_ref, v_ref, o_ref, zc_ref, gc_ref, gr_ref, pc_ref, pr_ref, nrm_ref,
                         mix_ref, c_out, n_out, m_out, c_ref, n_ref, m_ref):
    c_id = pl.program_id(1)
    first = c_id == 0
    last = c_id == pl.num_programs(1) - 1

    @pl.when(first)
    def _():
        c_ref[...] = jnp.zeros(c_ref.shape, F32)
        n_ref[...] = jnp.zeros(n_ref.shape, F32)
        m_ref[...] = jnp.zeros(m_ref.shape, F32)

    r, c = _causal_masks()
    causal = r >= c
    tril = jnp.where(causal, 1.0, 0.0)
    triu = jnp.where(r <= c, 1.0, 0.0)
    chunks = range(PROMPT_CPS)
    rows = [slice(ch * CHUNK, (ch + 1) * CHUNK) for ch in chunks]
    gc = gc_ref[...]
    logi_c = gc + pc_ref[0:1, :]
    logf_c = -_softplus(-(gc + pc_ref[1:2, :]))
    b_c = [_mm_hi(tril, logf_c[rw, :]) for rw in rows]
    logi_r = gr_ref[0:ML_HEADS, :] + pr_ref[0]
    logf_r = -_softplus(-(gr_ref[ML_HEADS:2 * ML_HEADS, :] + pr_ref[1]))
    b_r = [_mm_hi(logf_r[:, rw], triu) for rw in rows]

    heads = range(ML_HEADS)
    ks = [slice(j * ML_DK, (j + 1) * ML_DK) for j in heads]
    vs = [slice(j * ML_DV, (j + 1) * ML_DV) for j in heads]
    units = [(ch, j) for ch in chunks for j in heads]
    nu = range(len(units))
    q = [q_ref[rows[ch], ks[j]] for ch, j in units]
    k = [k_ref[rows[ch], ks[j]] * (ML_DK ** -0.5) for ch, j in units]
    v = [v_ref[rows[ch], vs[j]] for ch, j in units]
    bc = [b_c[ch][:, ML_HEADS + j:ML_HEADS + j + 1] for ch, j in units]
    dmat = [jnp.where(causal, bc[u] - b_r[ch][j:j + 1, :] + logi_r[j:j + 1, rows[ch]], NEG)
            for u, (ch, j) in enumerate(units)]
    m_intra = [jnp.max(x, axis=-1, keepdims=True) for x in dmat]
    p = [_mm_nt(q[u], k[u]) * jnp.exp(dmat[u] - m_intra[u]) for u in nu]
    h_intra = [_mm(p[u], v[u]) for u in nu]
    n_intra = [jnp.sum(p[u], axis=-1, keepdims=True) for u in nu]
    b_last = [x[CHUNK - 1:CHUNK, :] for x in bc]
    gk = [b_last[u] - bc[u] + logi_c[rows[ch], j:j + 1] for u, (ch, j) in enumerate(units)]
    m_k = [jnp.max(x, axis=0, keepdims=True) for x in gk]
    kw = [k[u] * jnp.exp(gk[u] - m_k[u]) for u in nu]
    c_loc = [_mm_tn(kw[u], v[u]) for u in nu]
    n_loc = [jnp.sum(kw[u], axis=0, keepdims=True) for u in nu]
    c_cur = [c_ref[j] for j in heads]
    n_cur = [n_ref[j:j + 1, :] for j in heads]
    m_cur = [m_ref[j:j + 1, 0:1] for j in heads]
    for ch in chunks:
        us = [ch * ML_HEADS + j for j in heads]
        qc = [_mm(q[u], c_cur[j]) for j, u in enumerate(us)]
        for j, u in enumerate(us):
            mb = bc[u] + m_cur[j]
            m_t = jnp.maximum(mb, m_intra[u])
            s_inter = jnp.exp(mb - m_t)
            s_intra = jnp.exp(m_intra[u] - m_t)
            num = s_inter * qc[j] + s_intra * h_intra[u]
            den = s_inter * jnp.sum(q[u] * n_cur[j], axis=-1, keepdims=True) + s_intra * n_intra[u]
            h = num / jnp.maximum(jnp.abs(den), jnp.exp(-m_t))
            m_new = jnp.maximum(b_last[u] + m_cur[j], m_k[u])
            sa = jnp.exp(b_last[u] + m_cur[j] - m_new)
            sb = jnp.exp(m_k[u] - m_new)
            c_cur[j] = c_cur[j] * sa + c_loc[u] * sb
            n_cur[j] = n_cur[j] * sa + n_loc[u] * sb
            m_cur[j] = m_new
            h = h * lax.rsqrt(jnp.mean(h * h, axis=-1, keepdims=True) + EPS) * nrm_ref[...]
            mix_ref[rows[ch], vs[j]] = (h * jax.nn.sigmoid(o_ref[rows[ch], vs[j]])
                                        * _silu(zc_ref[rows[ch], vs[j]])).astype(BF16)
    for j in heads:
        c_ref[j] = c_cur[j]
        n_ref[j:j + 1, :] = n_cur[j]
        m_ref[j:j + 1, :] = jnp.broadcast_to(m_cur[j], (1, LANES))

    @pl.when(last)
    def _():
        c_out[0] = c_ref[...]
        n_out[0] = n_ref[...]
        m_out[0] = m_ref[...]


def _mlstm_prompt(zm, zs, zs_t, m_total, bsz, seq, i_bias, f_bias, norm):
    tt = PROMPT_CPS * CHUNK
    nc = seq // tt
    rb = lambda b, c: b * nc + c
    pc = jnp.zeros((SUBLANES, LANES), F32).at[0, :ML_HEADS].set(i_bias).at[1, ML_HEADS:2 * ML_HEADS].set(f_bias)
    pr = jnp.stack([jnp.broadcast_to(i_bias[:, None], (ML_HEADS, tt)),
                    jnp.broadcast_to(f_bias[:, None], (ML_HEADS, tt))])
    full = lambda *shape: pl.BlockSpec(shape, lambda b, c: (0,) * len(shape))
    return pl.pallas_call(
        _mlstm_prompt_kernel,
        grid=(bsz, nc),
        in_specs=[
            pl.BlockSpec((tt, ML_QK), lambda b, c: (rb(b, c), 0)),
            pl.BlockSpec((tt, ML_QK), lambda b, c: (rb(b, c), 1)),
            pl.BlockSpec((tt, ML_V), lambda b, c: (rb(b, c), 1)),
            pl.BlockSpec((tt, ML_V), lambda b, c: (rb(b, c), 2)),
            pl.BlockSpec((tt, ML_V), lambda b, c: (rb(b, c), 3)),
            pl.BlockSpec((tt, LANES), lambda b, c: (rb(b, c), 0)),
            pl.BlockSpec((2 * ML_HEADS, tt), lambda b, c: (0, rb(b, c))),
            full(SUBLANES, LANES), full(2, ML_HEADS, tt), full(1, ML_DV),
        ],
        out_specs=[
            pl.BlockSpec((tt, ML_V), lambda b, c: (rb(b, c), 0)),
            pl.BlockSpec((1, ML_HEADS, ML_DK, ML_DV), lambda b, c: (b, 0, 0, 0)),
            pl.BlockSpec((1, ML_HEADS, ML_DK), lambda b, c: (b, 0, 0)),
            pl.BlockSpec((1, ML_HEADS, LANES), lambda b, c: (b, 0, 0)),
        ],
        out_shape=[
            jax.ShapeDtypeStruct((m_total, ML_V), BF16),
            jax.ShapeDtypeStruct((bsz, ML_HEADS, ML_DK, ML_DV), F32),
            jax.ShapeDtypeStruct((bsz, ML_HEADS, ML_DK), F32),
            jax.ShapeDtypeStruct((bsz, ML_HEADS, LANES), F32),
        ],
        scratch_shapes=[pltpu.VMEM((ML_HEADS, ML_DK, ML_DV), F32), pltpu.VMEM((ML_HEADS, ML_DK), F32),
                        pltpu.VMEM((ML_HEADS, LANES), F32)],
        compiler_params=_params("parallel", "arbitrary"),
        name="mlstm_prompt",
    )(zm, zm, zm, zm, zm, zs, zs_t, pc, pr, norm[None, :])


def _cmlp_prompt_kernel(u_ref, v_ref, z_ref, ws_ref, wb_ref, gain_ref, mix_ref, vrows_ref):
    r, c = _causal_masks()
    causal = r >= c
    for g in range(CM_GROUPS):
        gs = slice(g * CM_GROUP_DIM, (g + 1) * CM_GROUP_DIM)
        v = _gelu(v_ref[:, gs])
        v = v * lax.rsqrt(jnp.mean(v * v, axis=-1, keepdims=True) + EPS) * gain_ref[...]
        s = _mm(jnp.where(causal, ws_ref[g], 0.0), v) + wb_ref[:, g:g + 1]
        mix_ref[:, gs] = (_gelu(u_ref[:, gs]) * s * _silu(z_ref[:, gs])).astype(BF16)
        vrows_ref[0, :, gs] = v


def _cmlp_prompt(zm, m_total, bsz, seq, v_gain, ws, wb):
    nc = seq // CM_CHUNK
    rb = lambda b, c: b * nc + c
    col0 = (2 * ML_QK + 3 * ML_V) // CM_WIDTH
    return pl.pallas_call(
        _cmlp_prompt_kernel,
        grid=(bsz, nc),
        in_specs=[
            pl.BlockSpec((CM_CHUNK, CM_WIDTH), lambda b, c: (rb(b, c), col0)),
            pl.BlockSpec((CM_CHUNK, CM_WIDTH), lambda b, c: (rb(b, c), col0 + 1)),
            pl.BlockSpec((CM_CHUNK, CM_WIDTH), lambda b, c: (rb(b, c), col0 + 2)),
            pl.BlockSpec((CM_GROUPS, CM_CHUNK, CM_CHUNK), lambda b, c: (0, 0, 0)),
            pl.BlockSpec((CM_CHUNK, CM_GROUPS), lambda b, c: (0, 0)),
            pl.BlockSpec((1, CM_GROUP_DIM), lambda b, c: (0, 0)),
        ],
        out_specs=[
            pl.BlockSpec((CM_CHUNK, CM_WIDTH), lambda b, c: (rb(b, c), 0)),
            pl.BlockSpec((1, CM_CHUNK, CM_WIDTH), lambda b, c: (b, 0, 0)),
        ],
        out_shape=[jax.ShapeDtypeStruct((m_total, CM_WIDTH), BF16),
                   jax.ShapeDtypeStruct((bsz, CM_CHUNK, CM_WIDTH), F32)],
        compiler_params=_params("parallel", "arbitrary"),
        name="cmlp_prompt",
    )(zm, zm, zm, ws, wb.T, v_gain[None, :])


SEQ_BLOCK = SUBLANES
DEC_UNROLL = 2
SSD_DG = 2


def _conv_step(raw_ref, cin_ref, w_ref, cout_ref):
    u = raw_ref[...]
    out = w_ref[CONV_K - 1:CONV_K, :] * u
    for k in range(CONV_K - 1):
        out = out + w_ref[k:k + 1, :] * cin_ref[k]
    for k in range(CONV_K - 2):
        cout_ref[k] = cin_ref[k + 1]
    cout_ref[CONV_K - 2] = u
    return out


def _row0(row, fill=0.0):
    return jnp.where(_iota((SUBLANES, row.shape[1]), 0) == 0, row, fill)


def _stash_rows(dst_ref, val):
    for i in range(SEQ_BLOCK):
        dst_ref[i] = jnp.broadcast_to(val[i:i + 1, :], (SUBLANES, val.shape[1]))


def _gather_rows(src_ref):
    rid = _iota(src_ref.shape[1:], 0)
    acc = src_ref[0]
    for i in range(1, SEQ_BLOCK):
        acc = jnp.where(rid == i, src_ref[i], acc)
    return acc


def _ssd_decode_kernel(za_ref, xs_ref, b_ref, c_ref, zc_ref, cx_ref, cb_ref, cc_ref, wx_ref, bx_ref, wb_ref, bb_ref,
                       wc_ref, bc_ref, pc_ref, e_ref, a_ref, d_ref, nrm_ref, s_ref, *rest):
    mix_ref, ncx_ref, ncb_ref, ncc_ref, so_ref = rest[-5:]
    xs = _silu(_conv_step(xs_ref, cx_ref, wx_ref, ncx_ref) + bx_ref[...])
    bm = _silu(_conv_step(b_ref, cb_ref, wb_ref, ncb_ref) + bb_ref[...])
    cm = _silu(_conv_step(c_ref, cc_ref, wc_ref, ncc_ref) + bc_ref[...])
    dtx = _mm_hi(_softplus(zc_ref[...] + pc_ref[0:1, :]), e_ref[...])
    hi, mid, lo = _split3(jnp.exp(dtx * a_ref[...]))
    dx = dtx * xs
    rid = _iota((SUBLANES, SSD_GW), 0)
    ones_rows = jnp.where((_iota((SUBLANES, SSD_STATE), 0) >= 1) & (_iota((SUBLANES, SSD_STATE), 0) <= 3), 1.0, 0.0)
    hpg = SSD_HEADS // SSD_GROUPS
    units = [(g, i) for g in range(SSD_DG) for i in range(SEQ_BLOCK)]
    gs = [slice(g * SSD_GW, (g + 1) * SSD_GW) for g in range(SSD_DG)]
    ns = [slice(g * SSD_STATE, (g + 1) * SSD_STATE) for g in range(SSD_DG)]
    lmat = [jnp.where(rid == 0, dx[i:i + 1, gs[g]], jnp.where(rid == 1, hi[i:i + 1, gs[g]],
                      jnp.where(rid == 2, mid[i:i + 1, gs[g]], jnp.where(rid == 3, lo[i:i + 1, gs[g]], 0.0))))
            for g, i in units]
    upd = [_mm_tn(lmat[n], jnp.concatenate([_row0(bm[i:i + 1, ns[g]]), ones_rows], axis=1))
           for n, (g, i) in enumerate(units)]
    s_new = [s_ref[i, g * hpg:(g + 1) * hpg].reshape(SSD_GW, SSD_STATE) * upd[n][:, SSD_STATE:]
             + upd[n][:, :SSD_STATE] for n, (g, i) in enumerate(units)]
    for n, (g, i) in enumerate(units):
        so_ref[i, g * hpg:(g + 1) * hpg] = s_new[n].reshape(hpg, SSD_HEAD_DIM, SSD_STATE)
    ys = [_mm_nt(jnp.broadcast_to(cm[i:i + 1, ns[g]], (SUBLANES, SSD_STATE)), s_new[n])
          for n, (g, i) in enumerate(units)]
    for g in range(SSD_DG):
        y = ys[g * SEQ_BLOCK]
        for i in range(1, SEQ_BLOCK):
            y = jnp.where(rid == i, ys[g * SEQ_BLOCK + i], y)
        y = y + d_ref[:, gs[g]] * xs[:, gs[g]]
        y = y * _silu(za_ref[:, gs[g]])
        y = y * lax.rsqrt(jnp.mean(y * y, axis=-1, keepdims=True) + EPS) * nrm_ref[:, gs[g]]
        mix_ref[:, gs[g]] = y.astype(BF16)


def _stacked_state_io(states, layer, prev, block, index):
    spec = pl.BlockSpec((None,) + block, lambda *g: (layer,) + index(*g))
    extra_in = [] if prev is None else [prev]
    extra_specs = [] if prev is None else [pl.BlockSpec(memory_space=pl.ANY)]
    return spec, jax.ShapeDtypeStruct(states.shape, F32), extra_in, extra_specs


def _ssd_decode(zm, zs, mix, conv_t, states, layer, prev, mp, conv_w, conv_b, dt_bias, a_log, d_skip, norm):
    bsz = states.shape[1]
    hpg = SSD_HEADS // SSD_GROUPS
    gw, nw, ng = SSD_DG * SSD_GW, SSD_DG * SSD_STATE, SSD_GROUPS // SSD_DG
    st_spec, st_shape, extra_in, extra_specs = _stacked_state_io(
        states, layer, prev, (SEQ_BLOCK, SSD_DG * hpg, SSD_HEAD_DIM, SSD_STATE), lambda s, g: (s, g, 0, 0))
    r0 = mp // SEQ_BLOCK
    pc = jnp.zeros((SUBLANES, LANES), F32).at[0, :SSD_HEADS].set(dt_bias)
    e_mat = (jnp.arange(SSD_WIDTH)[None, :] // SSD_HEAD_DIM == jnp.arange(LANES)[:, None]).astype(F32)
    a_x = jnp.repeat(-jnp.exp(a_log), SSD_HEAD_DIM)[None, :]
    nb = SSD_WIDTH // nw
    zrow = lambda w, col: pl.BlockSpec((SEQ_BLOCK, w), lambda s, g: (r0 + s, col(g)))
    cst = lambda w, col: pl.BlockSpec((CONV_K - 1, SEQ_BLOCK, w), lambda s, g: (0, s, col(g)))
    par = lambda rows, w, col: pl.BlockSpec((rows, w), lambda s, g: (0, col(g)))
    mainb = 6 * D_MODEL // nw
    return pl.pallas_call(
        _ssd_decode_kernel,
        grid=(bsz // SEQ_BLOCK, ng),
        in_specs=[
            zrow(gw, lambda g: g), zrow(gw, lambda g: ng + g),
            zrow(nw, lambda g: mainb + g), zrow(nw, lambda g: mainb + ng + g),
            zrow(LANES, lambda g: 0),
            cst(gw, lambda g: g), cst(nw, lambda g: nb + g), cst(nw, lambda g: nb + ng + g),
            par(CONV_K, gw, lambda g: g), par(1, gw, lambda g: g),
            par(CONV_K, nw, lambda g: nb + g), par(1, nw, lambda g: nb + g),
            par(CONV_K, nw, lambda g: nb + ng + g), par(1, nw, lambda g: nb + ng + g),
            par(SUBLANES, LANES, lambda g: 0), par(LANES, gw, lambda g: g),
            par(1, gw, lambda g: g), par(1, gw, lambda g: g), par(1, gw, lambda g: g),
            st_spec,
            pl.BlockSpec(memory_space=pl.ANY),
        ] + extra_specs,
        out_specs=[
            pl.BlockSpec((SEQ_BLOCK, gw), lambda s, g: (r0 + s, g)),
            cst(gw, lambda g: g), cst(nw, lambda g: g), cst(nw, lambda g: g),
            st_spec,
        ],
        out_shape=[
            jax.ShapeDtypeStruct(mix.shape, mix.dtype),
            jax.ShapeDtypeStruct((CONV_K - 1, bsz, SSD_WIDTH), F32),
            jax.ShapeDtypeStruct((CONV_K - 1, bsz, SSD_GROUPS * SSD_STATE), F32),
            jax.ShapeDtypeStruct((CONV_K - 1, bsz, SSD_GROUPS * SSD_STATE), F32),
            st_shape,
        ],
        input_output_aliases={20: 0} if prev is None else {20: 0, 21: 4},
        compiler_params=_params("parallel", "arbitrary"),
        name="ssd_decode",
    )(zm, zm, zm, zm, zs, conv_t, conv_t, conv_t, conv_w, conv_b[None, :], conv_w, conv_b[None, :], conv_w,
      conv_b[None, :], pc, e_mat, a_x, jnp.repeat(d_skip, SSD_HEAD_DIM)[None, :], norm[None, :], states, mix,
      *extra_in)


def _gdn_decode_kernel(q_ref, k_ref, v_ref, zb_ref, zc_ref, cq_ref, ck_ref, cv_ref, wq_ref, wk_ref, wv_ref, pc_ref,
                       nrm_ref, s_ref, *rest):
    mix_ref, ncq_ref, nck_ref, ncv_ref, so_ref, q_s, k_s, v_s, beta_s, eg_s, qk_s, o_s = rest[-12:]
    q_all = _silu(_conv_step(q_ref, cq_ref, wq_ref, ncq_ref))
    k_all = _silu(_conv_step(k_ref, ck_ref, wk_ref, nck_ref))
    _stash_rows(v_s, _silu(_conv_step(v_ref, cv_ref, wv_ref, ncv_ref)))
    zc = zc_ref[...]
    beta = jax.nn.sigmoid(zc)
    eg = jnp.exp(-jnp.exp(pc_ref[0:1, :]) * _softplus(zc + pc_ref[1:2, :]))
    heads = range(GDN_HG)
    hs = [slice(j * GDN_DK, (j + 1) * GDN_DK) for j in heads]
    qn, kn, beta_x, eg_x, qk_x = [], [], [], [], []
    shape = (SEQ_BLOCK, GDN_DK)
    for j in heads:
        qh, kh = q_all[:, hs[j]], k_all[:, hs[j]]
        qh = qh * (lax.rsqrt(jnp.sum(qh * qh, axis=-1, keepdims=True) + EPS) * (GDN_DK ** -0.5))
        kh = kh * lax.rsqrt(jnp.sum(kh * kh, axis=-1, keepdims=True) + EPS)
        qn.append(qh)
        kn.append(kh)
        beta_x.append(jnp.broadcast_to(beta[:, j:j + 1], shape))
        eg_x.append(jnp.broadcast_to(eg[:, GDN_HG + j:GDN_HG + j + 1], shape))
        qk_x.append(jnp.broadcast_to(jnp.sum(qh * kh, axis=-1, keepdims=True), shape))
    for ref, parts in ((q_s, qn), (k_s, kn), (beta_s, beta_x), (eg_s, eg_x), (qk_s, qk_x)):
        _stash_rows(ref, jnp.concatenate(parts, axis=1))

    rid = _iota((SUBLANES, GDN_DK), 0)

    def body(it, carry):
        seqs = [it * DEC_UNROLL + u for u in range(DEC_UNROLL)]
        k_b, q_b, v_b = [k_s[i] for i in seqs], [q_s[i] for i in seqs], [v_s[i] for i in seqs]
        beta_b, eg_b, qk_b = [beta_s[i] for i in seqs], [eg_s[i] for i in seqs], [qk_s[i] for i in seqs]
        units = [(u, j) for u in range(DEC_UNROLL) for j in heads]
        s_prev = [s_ref[seqs[u], j] for u, j in units]
        ks_qs = [_mm(jnp.where(rid == 0, k_b[u][:, hs[j]], q_b[u][:, hs[j]]), s_prev[n])
                 for n, (u, j) in enumerate(units)]
        vn = [beta_b[u][0:1, hs[j]] * (v_b[u][0:1, hs[j]] - eg_b[u][0:1, hs[j]] * ks_qs[n][0:1, :])
              for n, (u, j) in enumerate(units)]
        outer = [_mm_tn(_row0(k_b[u][:, hs[j]]), _row0(vn[n])) for n, (u, j) in enumerate(units)]
        for n, (u, j) in enumerate(units):
            so_ref[seqs[u], j] = s_prev[n] * eg_b[u][0:1, j * GDN_DK:j * GDN_DK + 1] + outer[n]
            o_row = eg_b[u][0:1, hs[j]] * ks_qs[n][1:2, :] + qk_b[u][0:1, hs[j]] * vn[n]
            o_s[seqs[u], :, hs[j]] = jnp.broadcast_to(o_row, (SUBLANES, GDN_DV))
        return carry

    lax.fori_loop(0, SEQ_BLOCK // DEC_UNROLL, body, 0)
    o_all = _gather_rows(o_s)
    for j in heads:
        o = o_all[:, hs[j]]
        o = o * lax.rsqrt(jnp.mean(o * o, axis=-1, keepdims=True) + EPS) * nrm_ref[...]
        mix_ref[:, hs[j]] = (o * _silu(zb_ref[:, hs[j]])).astype(BF16)


def _gdn_decode(zm, zs, mix, conv_t, states, layer, prev, mp, conv_w, dt_bias, a_log, norm):
    bsz = states.shape[1]
    st_spec, st_shape, extra_in, extra_specs = _stacked_state_io(
        states, layer, prev, (SEQ_BLOCK, GDN_HG, GDN_DK, GDN_DV), lambda s, h: (s, h, 0, 0))
    nhg = GDN_HEADS // GDN_HG
    w = GDN_HG * GDN_DK
    r0 = mp // SEQ_BLOCK
    col0 = 2 * D_MODEL // w
    pc = jnp.zeros((nhg, SUBLANES, LANES), F32)
    pc = pc.at[:, 0, GDN_HG:2 * GDN_HG].set(a_log.reshape(nhg, GDN_HG))
    pc = pc.at[:, 1, GDN_HG:2 * GDN_HG].set(dt_bias.reshape(nhg, GDN_HG))
    seg = lambda p: pl.BlockSpec((SEQ_BLOCK, w), lambda s, h: (r0 + s, col0 + p * nhg + h))
    cst = lambda p: pl.BlockSpec((CONV_K - 1, SEQ_BLOCK, w), lambda s, h: (0, s, p * nhg + h))
    wseg = lambda p: pl.BlockSpec((CONV_K, w), lambda s, h: (0, p * nhg + h))
    cout = pl.BlockSpec((CONV_K - 1, SEQ_BLOCK, w), lambda s, h: (0, s, h))
    row_scratch = pltpu.VMEM((SEQ_BLOCK, SUBLANES, w), F32)
    return pl.pallas_call(
        _gdn_decode_kernel,
        grid=(bsz // SEQ_BLOCK, nhg),
        in_specs=[
            seg(0), seg(1), seg(2), seg(3),
            pl.BlockSpec((SEQ_BLOCK, LANES), lambda s, h: (r0 + s, 1 + h)),
            cst(0), cst(1), cst(2), wseg(0), wseg(1), wseg(2),
            pl.BlockSpec((None, SUBLANES, LANES), lambda s, h: (h, 0, 0)),
            pl.BlockSpec((1, GDN_DV), lambda s, h: (0, 0)),
            st_spec,
            pl.BlockSpec(memory_space=pl.ANY),
        ] + extra_specs,
        out_specs=[
            pl.BlockSpec((SEQ_BLOCK, w), lambda s, h: (r0 + s, h)),
            cout, cout, cout,
            st_spec,
        ],
        out_shape=[
            jax.ShapeDtypeStruct(mix.shape, mix.dtype),
            jax.ShapeDtypeStruct((CONV_K - 1, bsz, GDN_QK), F32),
            jax.ShapeDtypeStruct((CONV_K - 1, bsz, GDN_QK), F32),
            jax.ShapeDtypeStruct((CONV_K - 1, bsz, GDN_V), F32),
            st_shape,
        ],
        scratch_shapes=[row_scratch] * 7,
        input_output_aliases={14: 0} if prev is None else {14: 0, 15: 4},
        compiler_params=_params("parallel", "arbitrary"),
        name="gdn_decode",
    )(zm, zm, zm, zm, zs, conv_t, conv_t, conv_t, conv_w, conv_w, conv_w, pc, norm[None, :], states, mix, *extra_in)


ML_HG = 4


def _mlstm_decode_kernel(q_ref, k_ref, v_ref, o_ref, zc_ref, g_ref, m_ref, n_ref, pc_ref, e_ref, nrm_ref, c_ref,
                         *rest):
    mix_ref, n_out, m_out, c_out, q_s, kb_s, v_s, sa_s, sbqk_s, den_s, h_s = rest[-11:]
    g = g_ref[...]
    logi = g + pc_ref[0:1, :]
    logf = -_softplus(-pltpu.roll(g + pc_ref[1:2, :], LANES - ML_HEADS, 1))
    m_prev = m_ref[...]
    m_new = jnp.maximum(logf + m_prev, logi)
    m_out[...] = m_new
    e_mat = e_ref[...]
    sa_x = _mm_hi(jnp.exp(logf + m_prev - m_new), e_mat)
    sb_x = _mm_hi(jnp.exp(logi - m_new), e_mat)
    em_x = _mm_hi(jnp.exp(-m_new), e_mat)
    q = q_ref[...]
    k = k_ref[...] * (ML_DK ** -0.5)
    n_prev = n_ref[...]
    n_out[...] = n_prev * sa_x + sb_x * k
    _stash_rows(q_s, q)
    _stash_rows(kb_s, sb_x * k)
    _stash_rows(v_s, v_ref[...])
    _stash_rows(sa_s, sa_x)
    heads = range(ML_HG)
    ks = [slice(j * ML_DK, (j + 1) * ML_DK) for j in heads]
    vs = [slice(j * ML_DV, (j + 1) * ML_DV) for j in heads]
    sbqk_x, den_x = [], []
    for j in heads:
        shape = (SEQ_BLOCK, ML_DK)
        qk = jnp.sum(q[:, ks[j]] * k[:, ks[j]], axis=-1, keepdims=True)
        qn = jnp.sum(q[:, ks[j]] * n_prev[:, ks[j]], axis=-1, keepdims=True)
        sbqk = sb_x[:, ks[j]] * jnp.broadcast_to(qk, shape)
        sbqk_x.append(sbqk)
        den_x.append(jnp.maximum(jnp.abs(sa_x[:, ks[j]] * jnp.broadcast_to(qn, shape) + sbqk), em_x[:, ks[j]]))
    _stash_rows(sbqk_s, jnp.concatenate(sbqk_x, axis=1))
    _stash_rows(den_s, jnp.concatenate(den_x, axis=1))

    def body(it, carry):
        seqs = [it * DEC_UNROLL + u for u in range(DEC_UNROLL)]
        q_b, kb_b, v_b = [q_s[i] for i in seqs], [kb_s[i] for i in seqs], [v_s[i] for i in seqs]
        sa_b, sbqk_b, den_b = [sa_s[i] for i in seqs], [sbqk_s[i] for i in seqs], [den_s[i] for i in seqs]
        units = [(u, j) for u in range(DEC_UNROLL) for j in heads]
        c_prev = [c_ref[seqs[u], j] for u, j in units]
        qc = [_mm(q_b[u][:, ks[j]], c_prev[n]) for n, (u, j) in enumerate(units)]
        outer = [_mm_tn(_row0(kb_b[u][:, ks[j]]), _row0(v_b[u][:, vs[j]])) for u, j in units]
        for n, (u, j) in enumerate(units):
            lane0 = slice(j * ML_DK, j * ML_DK + 1)
            c_out[seqs[u], j] = c_prev[n] * sa_b[u][0:1, lane0] + outer[n]
            num = sa_b[u][:, lane0] * qc[n] + sbqk_b[u][:, lane0] * v_b[u][:, vs[j]]
            h_s[seqs[u], :, vs[j]] = num / den_b[u][:, lane0]
        return carry

    lax.fori_loop(0, SEQ_BLOCK // DEC_UNROLL, body, 0)
    h_all = _gather_rows(h_s)
    for j in heads:
        h = h_all[:, vs[j]]
        h = h * lax.rsqrt(jnp.mean(h * h, axis=-1, keepdims=True) + EPS) * nrm_ref[...]
        mix_ref[:, vs[j]] = (h * jax.nn.sigmoid(o_ref[:, vs[j]]) * _silu(zc_ref[:, vs[j]])).astype(BF16)


def _mlstm_decode(zm, zs, mix, c_states, layer, prev, n0, m0, mp, i_bias, f_bias, norm):
    bsz = c_states.shape[1]
    st_spec, st_shape, extra_in, extra_specs = _stacked_state_io(
        c_states, layer, prev, (SEQ_BLOCK, ML_HG, ML_DK, ML_DV), lambda s, h: (s, h, 0, 0))
    nhg = ML_HEADS // ML_HG
    wk, wv = ML_HG * ML_DK, ML_HG * ML_DV
    r0 = mp // SEQ_BLOCK
    pc = jnp.zeros((SUBLANES, LANES), F32).at[0, :ML_HEADS].set(i_bias).at[1, ML_HEADS:2 * ML_HEADS].set(f_bias)
    e_mat = (jnp.arange(ML_QK)[None, :] // ML_DK == jnp.arange(LANES)[:, None]).astype(F32)
    zrow = lambda w, col: pl.BlockSpec((SEQ_BLOCK, w), lambda s, h: (r0 + s, col(h)))
    vcol = 2 * ML_QK // wv
    return pl.pallas_call(
        _mlstm_decode_kernel,
        grid=(bsz // SEQ_BLOCK, nhg),
        in_specs=[
            zrow(wk, lambda h: h), zrow(wk, lambda h: nhg + h),
            zrow(wv, lambda h: vcol + h), zrow(wv, lambda h: vcol + nhg + h), zrow(wv, lambda h: vcol + 2 * nhg + h),
            zrow(LANES, lambda h: 0),
            pl.BlockSpec((SEQ_BLOCK, LANES), lambda s, h: (s, 0)),
            pl.BlockSpec((SEQ_BLOCK, wk), lambda s, h: (s, h)),
            pl.BlockSpec((SUBLANES, LANES), lambda s, h: (0, 0)),
            pl.BlockSpec((LANES, wk), lambda s, h: (0, h)),
            pl.BlockSpec((1, ML_DV), lambda s, h: (0, 0)),
            st_spec,
            pl.BlockSpec(memory_space=pl.ANY),
        ] + extra_specs,
        out_specs=[
            pl.BlockSpec((SEQ_BLOCK, wv), lambda s, h: (r0 + s, h)),
            pl.BlockSpec((SEQ_BLOCK, wk), lambda s, h: (s, h)),
            pl.BlockSpec((SEQ_BLOCK, LANES), lambda s, h: (s, 0)),
            st_spec,
        ],
        out_shape=[
            jax.ShapeDtypeStruct(mix.shape, mix.dtype),
            jax.ShapeDtypeStruct((bsz, ML_QK), F32),
            jax.ShapeDtypeStruct((bsz, LANES), F32),
            st_shape,
        ],
        scratch_shapes=[pltpu.VMEM((SEQ_BLOCK, SUBLANES, w), F32) for w in (wk, wk, wv, wk, wk, wk, wv)],
        input_output_aliases={12: 0} if prev is None else {12: 0, 13: 3},
        compiler_params=_params("parallel", "arbitrary"),
        name="mlstm_decode",
    )(zm, zm, zm, zm, zm, zs, jnp.pad(m0, ((0, 0), (0, LANES - ML_HEADS))), n0.reshape(bsz, ML_QK), pc, e_mat,
      norm[None, :], c_states, mix, *extra_in)


def _cmlp_decode_kernel(u_ref, v_ref, z_ref, ws_ref, wb_ref, gain_ref, mixin_ref, mix_ref, vrows_ref):
    del mixin_ref
    for g in range(CM_GROUPS):
        gs = slice(g * CM_GROUP_DIM, (g + 1) * CM_GROUP_DIM)
        v = _gelu(v_ref[:, gs])
        v = v * lax.rsqrt(jnp.mean(v * v, axis=-1, keepdims=True) + EPS) * gain_ref[...]
        s = ws_ref[:, gs] * v + wb_ref[:, gs]
        mix_ref[:, gs] = (_gelu(u_ref[:, gs]) * s * _silu(z_ref[:, gs])).astype(BF16)
        vrows_ref[:, gs] = v


def _cmlp_decode(zm, mix, mp, bsz, v_gain, ws, wb):
    col0 = (2 * ML_QK + 3 * ML_V) // CM_WIDTH
    r0 = mp // bsz
    zrow = lambda col: pl.BlockSpec((bsz, CM_WIDTH), lambda i: (r0, col))
    par = pl.BlockSpec((1, CM_WIDTH), lambda i: (0, 0))
    return pl.pallas_call(
        _cmlp_decode_kernel,
        grid=(1,),
        in_specs=[zrow(col0), zrow(col0 + 1), zrow(col0 + 2), par, par,
                  pl.BlockSpec((1, CM_GROUP_DIM), lambda i: (0, 0)), pl.BlockSpec(memory_space=pl.ANY)],
        out_specs=[pl.BlockSpec((bsz, CM_WIDTH), lambda i: (r0, 0)), pl.BlockSpec((bsz, CM_WIDTH), lambda i: (0, 0))],
        out_shape=[jax.ShapeDtypeStruct(mix.shape, mix.dtype), jax.ShapeDtypeStruct((bsz, CM_WIDTH), F32)],
        input_output_aliases={6: 0},
        compiler_params=_params("arbitrary"),
        name="cmlp_decode",
    )(zm, zm, zm, jnp.repeat(ws[:, 0, 0], CM_GROUP_DIM)[None, :], jnp.repeat(wb[:, 0], CM_GROUP_DIM)[None, :],
      v_gain[None, :], mix)


def kernel(x_prompt, x_sample, state_ssd_conv, state_ssd, state_gdn_conv, state_gdn, state_mlstm_c,
           state_mlstm_n, state_mlstm_m, even_norm, even_w_in, ssd_conv_w, ssd_conv_b, ssd_dt_bias, ssd_a_log,
           ssd_d, ssd_norm, gdn_conv_w, gdn_dt_bias, gdn_a_log, gdn_norm, even_w_out, odd_norm, odd_w_in,
           mlstm_i_bias, mlstm_f_bias, mlstm_norm, cmlp_v_norm, cmlp_ws, cmlp_b, odd_w_out, final_norm):
    bp, seq, d = x_prompt.shape
    bs = x_sample.shape[0]
    mp = bp * seq
    mt = mp + bs
    x = jnp.concatenate([x_prompt.reshape(mp, d), x_sample.reshape(bs, d)], axis=0)

    keys = ("sc", "ss", "gc", "gs", "mc", "mn", "mm", "cv")
    outs_p = {k: [] for k in keys}
    outs_s = {k: [] for k in keys}
    ss_all = gs_all = mc_all = None
    even_wt_in = jnp.swapaxes(even_w_in, 1, 2)
    odd_wt_in = jnp.swapaxes(odd_w_in, 1, 2)
    for layer in range(DEPTH):
        i = layer // 2
        if layer % 2 == 0:
            w_main, w_small = _prep_even_w_in(even_wt_in, i)
            zm, zs = _inproj(x, even_norm[i], w_main, w_small, tn=1024)
            zs_t = zs[:mp].T
            mix_a, cx, cbc, st = _ssd_prompt(zm, zs, zs_t, mt, bp, seq, ssd_conv_w[i], ssd_conv_b[i],
                                             ssd_dt_bias[i], ssd_a_log[i], ssd_d[i], ssd_norm[i])
            mix_b, cq, ck, cv, gst = _gdn_prompt(zm, zs, zs_t, mt, bp, seq, gdn_conv_w[i], gdn_dt_bias[i],
                                                 gdn_a_log[i], gdn_norm[i])
            hpg = SSD_HEADS // SSD_GROUPS
            outs_p["sc"].append(jnp.concatenate([cx, cbc], axis=-1))
            outs_p["ss"].append(st.reshape(bp, SSD_GROUPS, SSD_STATE, hpg, SSD_HEAD_DIM).transpose(0, 1, 3, 4, 2)
                                .reshape(bp, SSD_HEADS, SSD_HEAD_DIM, SSD_STATE))
            outs_p["gc"].append(jnp.concatenate([cq, ck, cv], axis=-1))
            outs_p["gs"].append(gst)
            mix_a, ncx, ncb, ncc, ss_all = _ssd_decode(
                zm, zs, mix_a, jnp.swapaxes(state_ssd_conv[i], 0, 1), state_ssd, i, ss_all, mp, ssd_conv_w[i],
                ssd_conv_b[i], ssd_dt_bias[i], ssd_a_log[i], ssd_d[i], ssd_norm[i])
            mix_b, ncq, nck, ncv, gs_all = _gdn_decode(
                zm, zs, mix_b, jnp.swapaxes(state_gdn_conv[i], 0, 1), state_gdn, i, gs_all, mp, gdn_conv_w[i],
                gdn_dt_bias[i], gdn_a_log[i], gdn_norm[i])
            outs_s["sc"].append(jnp.swapaxes(jnp.concatenate([ncx, ncb, ncc], axis=-1), 0, 1))
            outs_s["gc"].append(jnp.swapaxes(jnp.concatenate([ncq, nck, ncv], axis=-1), 0, 1))
            w_out = even_w_out[i].astype(BF16)
        else:
            w_main, w_small = _prep_odd_w_in(odd_wt_in, i)
            zm, zs = _inproj(x, odd_norm[i], w_main, w_small, tn=1024)
            zs_t = zs[:mp].T
            mix_a, c_p, n_p, m_p = _mlstm_prompt(zm, zs, zs_t, mt, bp, seq, mlstm_i_bias[i], mlstm_f_bias[i],
                                                 mlstm_norm[i])
            mix_b, v_rows = _cmlp_prompt(zm, mt, bp, seq, cmlp_v_norm[i], cmlp_ws[i], cmlp_b[i])
            outs_p["mc"].append(c_p); outs_p["mn"].append(n_p); outs_p["mm"].append(m_p[:, :, 0])
            outs_p["cv"].append(v_rows)
            mix_a, n_s, m_s, mc_all = _mlstm_decode(zm, zs, mix_a, state_mlstm_c, i, mc_all, state_mlstm_n[i],
                                                    state_mlstm_m[i], mp, mlstm_i_bias[i], mlstm_f_bias[i],
                                                    mlstm_norm[i])
            mix_b, v_row_s = _cmlp_decode(zm, mix_b, mp, bs, cmlp_v_norm[i], cmlp_ws[i], cmlp_b[i])
            outs_s["mn"].append(n_s.reshape(bs, ML_HEADS, ML_DK))
            outs_s["mm"].append(m_s[:, :ML_HEADS])
            outs_s["cv"].append(v_row_s.reshape(bs, 1, CM_WIDTH))
            w_out = odd_w_out[i].astype(BF16)
        x = _outproj(x, mix_a, mix_b, w_out)

    y_p = _final_norm(x, final_norm, 0, mp, 512).reshape(bp, seq, d)
    y_s = _final_norm(x, final_norm, mp, bs, bs).reshape(bs, 1, d)
    st = lambda o, k: jnp.stack(o[k])
    return (y_p, y_s, st(outs_p, "sc"), st(outs_s, "sc"), st(outs_p, "ss"), ss_all,
            st(outs_p, "gc"), st(outs_s, "gc"), st(outs_p, "gs"), gs_all,
            st(outs_p, "mc"), mc_all, st(outs_p, "mn"), st(outs_s, "mn"),
            st(outs_p, "mm"), st(outs_s, "mm"), st(outs_p, "cv"), st(outs_s, "cv"))
```

```python
import jax
import jax.numpy as jnp
import numpy as np
from jax import lax
from jax.experimental import pallas as pl
from jax.experimental.pallas import tpu as pltpu

F32 = jnp.float32
BF16 = jnp.bfloat16
HI = lax.Precision.HIGHEST

D_MODEL = 2048
DEPTH = 4
CHUNK = 128
CONV_K = 4
EPS = 1e-6
NEG = -1e30

SSD_WIDTH = D_MODEL
SSD_HEAD_DIM = 64
SSD_HEADS = SSD_WIDTH // SSD_HEAD_DIM
SSD_STATE = 128
SSD_GROUPS = 4
SSD_GW = SSD_WIDTH // SSD_GROUPS
SSD_BC = 2 * SSD_GROUPS * SSD_STATE
SSD_CONV_DIM = SSD_WIDTH + SSD_BC
GDN_HEADS = 16
GDN_DK = 128
GDN_DV = 128
GDN_QK = GDN_HEADS * GDN_DK
GDN_V = GDN_HEADS * GDN_DV
GDN_CONV_DIM = 2 * GDN_QK + GDN_V
GDN_HG = 8
PROMPT_CPS = 2
ML_HEADS = 8
ML_DK = 128
ML_DV = 256
ML_QK = ML_HEADS * ML_DK
ML_V = ML_HEADS * ML_DV
CM_WIDTH = D_MODEL // 2
CM_GROUPS = 8
CM_GROUP_DIM = CM_WIDTH // CM_GROUPS
CM_CHUNK = 128
CM_CPS = 4

LANES = 128
SUBLANES = 8

EVEN_MAIN = 6 * D_MODEL + SSD_BC
EVEN_SMALL = 3 * LANES
ODD_MAIN = 2 * ML_QK + 3 * ML_V + 3 * CM_WIDTH
ODD_SMALL = LANES

VMEM_LIMIT = 56 * 1024 * 1024
ROW_TILE_CAP = 1040
BF16_SUBLANES = 16


def _row_tile(m):
    return max(t for t in range(BF16_SUBLANES, ROW_TILE_CAP + 1, BF16_SUBLANES) if m % t == 0)


def _silu(x):
    return x * jax.nn.sigmoid(x)


def _softplus(x):
    return jnp.maximum(x, 0.0) + jnp.log1p(jnp.exp(-jnp.abs(x)))


def _gelu(x):
    return 0.5 * x * (1.0 + jnp.tanh(np.sqrt(2.0 / np.pi).astype(np.float32) * (x + 0.044715 * (x * x * x))))


def _mm(a, b):
    return jnp.dot(a.astype(BF16), b.astype(BF16), preferred_element_type=F32)


def _mm_nt(a, b):
    return lax.dot_general(a.astype(BF16), b.astype(BF16), (((1,), (1,)), ((), ())), preferred_element_type=F32)


def _mm_tn(a, b):
    return lax.dot_general(a.astype(BF16), b.astype(BF16), (((0,), (0,)), ((), ())), preferred_element_type=F32)


def _mm_hi(a, b):
    return jnp.dot(a, b, precision=HI, preferred_element_type=F32)


def _split3(x):
    hi = x.astype(BF16).astype(F32)
    r1 = x - hi
    mid = r1.astype(BF16).astype(F32)
    lo = (r1 - mid).astype(BF16).astype(F32)
    return hi, mid, lo


def _mm_sel(a, sel):
    hi, mid, lo = _split3(a)
    return (_mm(hi, sel) + _mm(mid, sel)) + _mm(lo, sel)


def _iota(shape, axis):
    return lax.broadcasted_iota(jnp.int32, shape, axis)


def _params(*sem):
    return pltpu.CompilerParams(dimension_semantics=sem, vmem_limit_bytes=VMEM_LIMIT)


def _inproj_kernel(x_ref, g_ref, w_ref, ws_ref, z_ref, zs_ref, xn_ref):
    @pl.when(pl.program_id(1) == 0)
    def _():
        x = x_ref[...]
        y = x * lax.rsqrt(jnp.mean(x * x, axis=-1, keepdims=True) + EPS)
        xn = (y * g_ref[...]).astype(BF16)
        xn_ref[...] = xn
        zs_ref[...] = _mm_nt(xn, ws_ref[...])

    z_ref[...] = _mm_nt(xn_ref[...], w_ref[...])


def _inproj(x, g, w_main, w_small, tn):
    m, d = x.shape
    n = w_main.shape[0]
    ns = w_small.shape[0]
    tm = _row_tile(m)
    return pl.pallas_call(
        _inproj_kernel,
        grid=(m // tm, n // tn),
        in_specs=[
            pl.BlockSpec((tm, d), lambda i, j: (i, 0)),
            pl.BlockSpec((1, d), lambda i, j: (0, 0)),
            pl.BlockSpec((tn, d), lambda i, j: (j, 0)),
            pl.BlockSpec((ns, d), lambda i, j: (0, 0)),
        ],
        out_specs=[
            pl.BlockSpec((tm, tn), lambda i, j: (i, j)),
            pl.BlockSpec((tm, ns), lambda i, j: (i, 0)),
        ],
        out_shape=[jax.ShapeDtypeStruct((m, n), F32), jax.ShapeDtypeStruct((m, ns), F32)],
        scratch_shapes=[pltpu.VMEM((tm, d), BF16)],
        compiler_params=_params("parallel", "arbitrary"),
        name="inproj",
    )(x, g.reshape(1, d), w_main, w_small)


def _outproj_kernel(x_ref, ma_ref, mb_ref, wa_ref, wb_ref, o_ref):
    o_ref[...] = (x_ref[...] + jnp.dot(ma_ref[...], wa_ref[...], preferred_element_type=F32)
                  + jnp.dot(mb_ref[...], wb_ref[...], preferred_element_type=F32))


def _outproj(x, mix_a, mix_b, w):
    m, d = x.shape
    ka, kb = mix_a.shape[1], mix_b.shape[1]
    tm, tn = _row_tile(m), 512
    return pl.pallas_call(
        _outproj_kernel,
        grid=(m // tm, d // tn),
        in_specs=[
            pl.BlockSpec((tm, tn), lambda i, j: (i, j)),
            pl.BlockSpec((tm, ka), lambda i, j: (i, 0)),
            pl.BlockSpec((tm, kb), lambda i, j: (i, 0)),
            pl.BlockSpec((ka, tn), lambda i, j: (0, j)),
            pl.BlockSpec((kb, tn), lambda i, j: (ka // kb, j)),
        ],
        out_specs=pl.BlockSpec((tm, tn), lambda i, j: (i, j)),
        out_shape=jax.ShapeDtypeStruct((m, d), F32),
        compiler_params=_params("parallel", "arbitrary"),
        name="outproj",
    )(x, mix_a, mix_b, w, w)


def _final_norm_kernel(x_ref, g_ref, o_ref):
    x = x_ref[...]
    o_ref[...] = x * lax.rsqrt(jnp.mean(x * x, axis=-1, keepdims=True) + EPS) * g_ref[...]


def _final_norm(x, g, row0, rows, tm):
    d = x.shape[1]
    return pl.pallas_call(
        _final_norm_kernel,
        grid=(rows // tm,),
        in_specs=[pl.BlockSpec((tm, d), lambda i: (row0 // tm + i, 0)), pl.BlockSpec((1, d), lambda i: (0, 0))],
        out_specs=pl.BlockSpec((tm, d), lambda i: (i, 0)),
        out_shape=jax.ShapeDtypeStruct((rows, d), F32),
        compiler_params=_params("parallel"),
        name="final_norm",
    )(x, g.reshape(1, d))


REPACK_TN = 512


def _repack_kernel(a_ref, b_ref, o_ref, *, shift, lo, hi):
    j = pl.program_id(0)
    shifted = (j >= lo) & (j < hi)

    @pl.when(shifted)
    def _():
        o_ref[...] = jnp.concatenate([a_ref[shift:, :], b_ref[...]], axis=0).astype(BF16)

    @pl.when(jnp.logical_not(shifted))
    def _():
        o_ref[...] = a_ref[...].astype(BF16)


def _repack(wt_all, layer, n_out, a_idx, shift, lo, hi):
    d = wt_all.shape[2]
    per = REPACK_TN // shift
    kern = lambda a, b, o: _repack_kernel(a, b, o, shift=shift, lo=lo, hi=hi)
    return pl.pallas_call(
        kern,
        grid=(n_out // REPACK_TN,),
        in_specs=[pl.BlockSpec((None, REPACK_TN, d), lambda j: (layer, a_idx(j), 0)),
                  pl.BlockSpec((None, shift, d), lambda j: (layer, (a_idx(j) + 1) * per, 0))],
        out_specs=pl.BlockSpec((REPACK_TN, d), lambda j: (j, 0)),
        out_shape=jax.ShapeDtypeStruct((n_out, d), BF16),
        compiler_params=_params("parallel"),
        name="repack",
    )(wt_all, wt_all)


def _small_pack_kernel(*refs, layout):
    pieces, o_ref = refs[:-1], refs[-1]
    d = o_ref.shape[1]
    for blk, idxs in enumerate(layout):
        rows = [pieces[i][...] for i in idxs] + [jnp.zeros((LANES - SUBLANES * len(idxs), d), F32)]
        o_ref[blk * LANES:(blk + 1) * LANES, :] = jnp.concatenate(rows, axis=0).astype(BF16)


def _small_pack(wt_all, layer, src_rows, layout):
    d = wt_all.shape[2]
    kern = lambda *refs: _small_pack_kernel(*refs, layout=layout)
    spec = lambda r: pl.BlockSpec((None, SUBLANES, d), lambda i: (layer, r // SUBLANES, 0))
    return pl.pallas_call(
        kern,
        grid=(1,),
        in_specs=[spec(r) for r in src_rows],
        out_specs=pl.BlockSpec((len(layout) * LANES, d), lambda i: (0, 0)),
        out_shape=jax.ShapeDtypeStruct((len(layout) * LANES, d), BF16),
        compiler_params=_params("arbitrary"),
        name="small_pack",
    )(*([wt_all] * len(src_rows)))


def _prep_even_w_in(wt_all, layer):
    o_bc = 2 * SSD_WIDTH
    o_dt = o_bc + SSD_BC
    o_q = o_dt + SSD_HEADS
    o_beta = o_q + GDN_CONV_DIM + GDN_V
    o_g = o_beta + GDN_HEADS
    n1, n2 = o_bc // REPACK_TN, (o_bc + o_beta - o_q) // REPACK_TN
    src2, src3 = o_dt // REPACK_TN, o_bc // REPACK_TN
    a_idx = lambda j: jnp.where(j < n1, j, jnp.where(j < n2, j - n1 + src2, j - n2 + src3))
    main = _repack(wt_all, layer, EVEN_MAIN, a_idx, o_q - o_dt, n1, n2)
    n_dt = SSD_HEADS // SUBLANES
    src = [o_dt + SUBLANES * i for i in range(n_dt)]
    layout = [tuple(range(n_dt))]
    for hg in range(GDN_HEADS // GDN_HG):
        src += [o_beta + hg * GDN_HG, o_g + hg * GDN_HG]
        layout.append((len(src) - 2, len(src) - 1))
    return main, _small_pack(wt_all, layer, src, layout)


def _prep_odd_w_in(wt_all, layer):
    o1 = 2 * ML_QK + 3 * ML_V
    o2 = o1 + 2 * ML_HEADS
    main = _repack(wt_all, layer, ODD_MAIN, lambda j: j, o2 - o1, o1 // REPACK_TN, ODD_MAIN // REPACK_TN)
    small = _small_pack(wt_all, layer, [o1, o1 + ML_HEADS], [(0, 1)])
    return main, small


def _conv_chunk(ext_ref, u, w_ref, first):
    t = u.shape[0]

    @pl.when(first)
    def _():
        ext_ref[0:SUBLANES, :] = jnp.zeros((SUBLANES, ext_ref.shape[1]), F32)

    prev = ext_ref[0:SUBLANES, :]
    rid = _iota((SUBLANES, u.shape[1]), 0)
    out = w_ref[CONV_K - 1:CONV_K, :] * u
    for k in range(1, CONV_K):
        rolled = pltpu.roll(u, k, 0)
        head = jnp.where(rid < k, pltpu.roll(prev, k, 0), rolled[0:SUBLANES, :])
        shifted = jnp.concatenate([head, rolled[SUBLANES:, :]], axis=0)
        out = out + w_ref[CONV_K - 1 - k:CONV_K - k, :] * shifted
    ext_ref[0:SUBLANES, :] = u[t - SUBLANES:, :]
    return out


def _conv_tail(ext_ref):
    return ext_ref[SUBLANES - (CONV_K - 1):SUBLANES, :]


def _causal_masks():
    r = _iota((CHUNK, CHUNK), 0)
    c = _iota((CHUNK, CHUNK), 1)
    return r, c


def _ssd_prompt_kernel(za_ref, xs_ref, bc_ref, zc_ref, zr_ref, wx_ref, bx_ref, wbc_ref, bbc_ref, pc_ref, pr_ref,
                       d_ref, nrm_ref, mix_ref, cx_ref, cbc_ref, st_ref, extx, extbc, s_ref):
    c_id = pl.program_id(1)
    first = c_id == 0
    last = c_id == pl.num_programs(1) - 1

    @pl.when(first)
    def _():
        s_ref[...] = jnp.zeros(s_ref.shape, F32)

    xs = _silu(_conv_chunk(extx, xs_ref[...], wx_ref, first) + bx_ref[...])
    bc = _silu(_conv_chunk(extbc, bc_ref[...], wbc_ref, first) + bbc_ref[...])

    r, c = _causal_masks()
    causal = r >= c
    tril = jnp.where(causal, 1.0, 0.0)
    triu = jnp.where(r <= c, 1.0, 0.0)
    dt = _softplus(zc_ref[...] + pc_ref[0:1, :])
    la = _mm_hi(tril, dt * (-jnp.exp(pc_ref[1:2, :])))
    dtr = _softplus(zr_ref[...] + pr_ref[0])
    lar = _mm_hi(dtr * (-jnp.exp(pr_ref[1])), triu)
    la_last = la[CHUNK - 1:CHUNK, :]
    e_mat = jnp.where((_iota((LANES, SSD_WIDTH), 1) >> 6) == _iota((LANES, SSD_WIDTH), 0), 1.0, 0.0)
    ela_x = _mm_sel(jnp.exp(la), e_mat)
    wsx = _mm_sel(jnp.exp(la_last - la) * dt, e_mat)
    dec_x = _mm_sel(jnp.broadcast_to(jnp.exp(la_last), (SUBLANES, LANES)), e_mat)[0:1, :]
    lane_lo = _iota((CHUNK, LANES), 1) < SSD_HEAD_DIM

    hpg = SSD_HEADS // SSD_GROUPS
    for g in range(SSD_GROUPS):
        gs = slice(g * SSD_GW, (g + 1) * SSD_GW)
        bg = bc[:, g * SSD_STATE:(g + 1) * SSD_STATE]
        cg = bc[:, SSD_GROUPS * SSD_STATE + g * SSD_STATE:SSD_GROUPS * SSD_STATE + (g + 1) * SSD_STATE]
        cb = _mm_nt(cg, bg)
        ys = []
        for pair in range(hpg // 2):
            h0 = g * hpg + 2 * pair
            xpair = xs[:, h0 * SSD_HEAD_DIM:(h0 + 2) * SSD_HEAD_DIM]
            halves = []
            for hh in (h0, h0 + 1):
                seg = jnp.where(causal, la[:, hh:hh + 1] - lar[hh:hh + 1, :], NEG)
                lmat = jnp.exp(seg) * cb * dtr[hh:hh + 1, :]
                halves.append(_mm(lmat, xpair))
            ys.append(jnp.where(lane_lo, halves[0], halves[1]))
        y = jnp.concatenate(ys, axis=1)
        s_prev = s_ref[g]
        y = y + _mm(cg, s_prev) * ela_x[:, gs] + d_ref[:, gs] * xs[:, gs]
        y = y * _silu(za_ref[:, gs])
        y = y * lax.rsqrt(jnp.mean(y * y, axis=-1, keepdims=True) + EPS) * nrm_ref[:, gs]
        mix_ref[:, gs] = y.astype(BF16)
        s_ref[g] = s_prev * dec_x[:, gs] + _mm_tn(bg, xs[:, gs] * wsx[:, gs])

    @pl.when(last)
    def _():
        st_ref[0] = s_ref[...]
        cx_ref[0] = _conv_tail(extx)
        cbc_ref[0] = _conv_tail(extbc)


def _ssd_prompt(zm, zs, zs_t, m_total, bsz, seq, conv_w, conv_b, dt_bias, a_log, d_skip, norm):
    nc = seq // CHUNK
    rb = lambda b, c: b * nc + c
    pc = jnp.zeros((SUBLANES, LANES), F32).at[0, :SSD_HEADS].set(dt_bias).at[1, :SSD_HEADS].set(a_log)
    pr = jnp.stack([jnp.broadcast_to(dt_bias[:, None], (SSD_HEADS, CHUNK)),
                    jnp.broadcast_to(a_log[:, None], (SSD_HEADS, CHUNK))])
    full = lambda *shape: pl.BlockSpec(shape, lambda b, c: (0,) * len(shape))
    return pl.pallas_call(
        _ssd_prompt_kernel,
        grid=(bsz, nc),
        in_specs=[
            pl.BlockSpec((CHUNK, SSD_WIDTH), lambda b, c: (rb(b, c), 0)),
            pl.BlockSpec((CHUNK, SSD_WIDTH), lambda b, c: (rb(b, c), 1)),
            pl.BlockSpec((CHUNK, SSD_BC), lambda b, c: (rb(b, c), 6 * D_MODEL // SSD_BC)),
            pl.BlockSpec((CHUNK, LANES), lambda b, c: (rb(b, c), 0)),
            pl.BlockSpec((SSD_HEADS, CHUNK), lambda b, c: (0, rb(b, c))),
            full(CONV_K, SSD_WIDTH), full(1, SSD_WIDTH), full(CONV_K, SSD_BC), full(1, SSD_BC),
            full(SUBLANES, LANES), full(2, SSD_HEADS, CHUNK), full(1, SSD_WIDTH), full(1, SSD_WIDTH),
        ],
        out_specs=[
            pl.BlockSpec((CHUNK, SSD_WIDTH), lambda b, c: (rb(b, c), 0)),
            pl.BlockSpec((1, CONV_K - 1, SSD_WIDTH), lambda b, c: (b, 0, 0)),
            pl.BlockSpec((1, CONV_K - 1, SSD_BC), lambda b, c: (b, 0, 0)),
            pl.BlockSpec((1, SSD_GROUPS, SSD_STATE, SSD_GW), lambda b, c: (b, 0, 0, 0)),
        ],
        out_shape=[
            jax.ShapeDtypeStruct((m_total, SSD_WIDTH), BF16),
            jax.ShapeDtypeStruct((bsz, CONV_K - 1, SSD_WIDTH), F32),
            jax.ShapeDtypeStruct((bsz, CONV_K - 1, SSD_BC), F32),
            jax.ShapeDtypeStruct((bsz, SSD_GROUPS, SSD_STATE, SSD_GW), F32),
        ],
        scratch_shapes=[pltpu.VMEM((SUBLANES, SSD_WIDTH), F32), pltpu.VMEM((SUBLANES, SSD_BC), F32),
                        pltpu.VMEM((SSD_GROUPS, SSD_STATE, SSD_GW), F32)],
        compiler_params=_params("parallel", "arbitrary"),
        name="ssd_prompt",
    )(zm, zm, zm, zs, zs_t, conv_w[:, :SSD_WIDTH], conv_b[None, :SSD_WIDTH], conv_w[:, SSD_WIDTH:],
      conv_b[None, SSD_WIDTH:], pc, pr, jnp.repeat(d_skip, SSD_HEAD_DIM)[None, :], norm[None, :])


def _tri_inverse(mats, r, c):
    def corner(level):
        return ((r >> (level + 1)) == (c >> (level + 1))) & (((r >> level) & 1) == 1) & (((c >> level) & 1) == 0)

    eye = jnp.where(r == c, 1.0, 0.0)
    ts = [eye - jnp.where(corner(0), a, 0.0) for a in mats]
    for level in range(1, 7):
        cm = corner(level)
        xs = [_mm(t, jnp.where(cm, a, 0.0)) for t, a in zip(ts, mats)]
        ts = [t - _mm(x, t) for t, x in zip(ts, xs)]
    return ts


def _gdn_prompt_kernel(q_ref, k_ref, v_ref, zb_ref, zc_ref, zr_ref, wq_ref, wk_ref, wv_ref, pc_ref, pr_ref, nrm_ref,
                       mix_ref, cq_ref, ck_ref, cv_ref, st_ref, extq, extk, extv, s_ref):
    c_id = pl.program_id(2)
    first = c_id == 0
    last = c_id == pl.num_programs(2) - 1

    @pl.when(first)
    def _():
        s_ref[...] = jnp.zeros(s_ref.shape, F32)

    q_all = _silu(_conv_chunk(extq, q_ref[...], wq_ref, first))
    k_all = _silu(_conv_chunk(extk, k_ref[...], wk_ref, first))
    v_all = _silu(_conv_chunk(extv, v_ref[...], wv_ref, first))

    r, c = _causal_masks()
    causal = r >= c
    strict = r > c
    tril = jnp.where(causal, 1.0, 0.0)
    triu = jnp.where(r <= c, 1.0, 0.0)
    chunks = range(PROMPT_CPS)
    rows = [slice(ch * CHUNK, (ch + 1) * CHUNK) for ch in chunks]
    zc = zc_ref[...]
    beta_c = jax.nn.sigmoid(zc)
    g_c = -jnp.exp(pc_ref[0:1, :]) * _softplus(zc + pc_ref[1:2, :])
    gc_c = [_mm_hi(tril, g_c[rw, :]) for rw in rows]
    g_r = -jnp.exp(pr_ref[0]) * _softplus(zr_ref[GDN_HG:2 * GDN_HG, :] + pr_ref[1])
    gc_r = [_mm_hi(g_r[:, rw], triu) for rw in rows]

    heads = range(GDN_HG)
    hs = [slice(j * GDN_DK, (j + 1) * GDN_DK) for j in heads]
    units = [(ch, j) for ch in chunks for j in heads]
    qh = [q_all[rows[ch], hs[j]] for ch, j in units]
    kh = [k_all[rows[ch], hs[j]] for ch, j in units]
    qh = [x * (lax.rsqrt(jnp.sum(x * x, axis=-1, keepdims=True) + EPS) * (GDN_DK ** -0.5)) for x in qh]
    kh = [x * lax.rsqrt(jnp.sum(x * x, axis=-1, keepdims=True) + EPS) for x in kh]
    gcc = [gc_c[ch][:, GDN_HG + j:GDN_HG + j + 1] for ch, j in units]
    beta = [beta_c[rows[ch], j:j + 1] for ch, j in units]
    gam = [jnp.exp(jnp.where(causal, gcc[u] - gc_r[ch][j:j + 1, :], NEG)) for u, (ch, j) in enumerate(units)]
    qkk = [_mm_nt(jnp.concatenate([qh[u], kh[u]], axis=0), kh[u]) for u in range(len(units))]
    aqk = [qkk[u][:CHUNK] * gam[u] for u in range(len(units))]
    tinv = _tri_inverse([jnp.where(strict, beta[u] * qkk[u][CHUNK:] * gam[u], 0.0) for u in range(len(units))], r, c)
    egc = [jnp.exp(g) for g in gcc]
    uw = [_mm(tinv[u], jnp.concatenate([beta[u] * v_all[rows[ch], hs[j]], (beta[u] * egc[u]) * kh[u]], axis=1))
          for u, (ch, j) in enumerate(units)]
    s_cur = [s_ref[j] for j in heads]
    for ch in chunks:
        us = [ch * GDN_HG + j for j in heads]
        ws_qs = [_mm(jnp.concatenate([uw[u][:, GDN_DV:], qh[u] * egc[u]], axis=0), s_cur[j]) for j, u in enumerate(us)]
        vn = [uw[u][:, :GDN_DV] - ws_qs[j][:CHUNK] for j, u in enumerate(us)]
        o = [ws_qs[j][CHUNK:] + _mm(aqk[u], vn[j]) for j, u in enumerate(us)]
        nxt = []
        for j, u in enumerate(us):
            gc_last = gcc[u][CHUNK - 1:CHUNK, :]
            nxt.append(s_cur[j] * jnp.exp(gc_last) + _mm_tn(kh[u] * jnp.exp(gc_last - gcc[u]), vn[j]))
        s_cur = nxt
        for j in heads:
            on = o[j] * lax.rsqrt(jnp.mean(o[j] * o[j], axis=-1, keepdims=True) + EPS) * nrm_ref[...]
            mix_ref[rows[ch], hs[j]] = (on * _silu(zb_ref[rows[ch], hs[j]])).astype(BF16)
    for j in heads:
        s_ref[j] = s_cur[j]

    @pl.when(last)
    def _():
        st_ref[0] = s_ref[...]
        cq_ref[0] = _conv_tail(extq)
        ck_ref[0] = _conv_tail(extk)
        cv_ref[0] = _conv_tail(extv)


def _gdn_prompt(zm, zs, zs_t, m_total, bsz, seq, conv_w, dt_bias, a_log, norm):
    tt = PROMPT_CPS * CHUNK
    nc = seq // tt
    nhg = GDN_HEADS // GDN_HG
    w = GDN_HG * GDN_DK
    rb = lambda b, c: b * nc + c
    col0 = 2 * D_MODEL // w
    pc = jnp.zeros((nhg, SUBLANES, LANES), F32)
    pc = pc.at[:, 0, GDN_HG:2 * GDN_HG].set(a_log.reshape(nhg, GDN_HG))
    pc = pc.at[:, 1, GDN_HG:2 * GDN_HG].set(dt_bias.reshape(nhg, GDN_HG))
    pr = jnp.stack([jnp.broadcast_to(a_log.reshape(nhg, GDN_HG, 1), (nhg, GDN_HG, tt)),
                    jnp.broadcast_to(dt_bias.reshape(nhg, GDN_HG, 1), (nhg, GDN_HG, tt))], axis=1)
    seg = lambda s: pl.BlockSpec((tt, w), lambda b, h, c: (rb(b, c), col0 + s * nhg + h))
    wseg = lambda s: pl.BlockSpec((CONV_K, w), lambda b, h, c: (0, s * nhg + h))
    cout = pl.BlockSpec((1, CONV_K - 1, w), lambda b, h, c: (b, 0, h))
    return pl.pallas_call(
        _gdn_prompt_kernel,
        grid=(bsz, nhg, nc),
        in_specs=[
            seg(0), seg(1), seg(2), seg(3),
            pl.BlockSpec((tt, LANES), lambda b, h, c: (rb(b, c), 1 + h)),
            pl.BlockSpec((2 * GDN_HG, tt), lambda b, h, c: ((1 + h) * LANES // (2 * GDN_HG), rb(b, c))),
            wseg(0), wseg(1), wseg(2),
            pl.BlockSpec((None, SUBLANES, LANES), lambda b, h, c: (h, 0, 0)),
            pl.BlockSpec((None, 2, GDN_HG, tt), lambda b, h, c: (h, 0, 0, 0)),
            pl.BlockSpec((1, GDN_DV), lambda b, h, c: (0, 0)),
        ],
        out_specs=[
            pl.BlockSpec((tt, w), lambda b, h, c: (rb(b, c), h)),
            cout, cout, cout,
            pl.BlockSpec((1, GDN_HG, GDN_DK, GDN_DV), lambda b, h, c: (b, h, 0, 0)),
        ],
        out_shape=[
            jax.ShapeDtypeStruct((m_total, GDN_V), BF16),
            jax.ShapeDtypeStruct((bsz, CONV_K - 1, GDN_QK), F32),
            jax.ShapeDtypeStruct((bsz, CONV_K - 1, GDN_QK), F32),
            jax.ShapeDtypeStruct((bsz, CONV_K - 1, GDN_V), F32),
            jax.ShapeDtypeStruct((bsz, GDN_HEADS, GDN_DK, GDN_DV), F32),
        ],
        scratch_shapes=[pltpu.VMEM((SUBLANES, w), F32)] * 3 + [pltpu.VMEM((GDN_HG, GDN_DK, GDN_DV), F32)],
        compiler_params=_params("parallel", "parallel", "arbitrary"),
        name="gdn_prompt",
    )(zm, zm, zm, zm, zs, zs_t, conv_w, conv_w, conv_w, pc, pr, norm[None, :])


def _mlstm_prompt_kernel(q_ref, k_ref, v_ref, o_ref, zc_ref, gc_ref, gr_ref, pc_ref, pr_ref, nrm_ref,
                         mix_ref, c_out, n_out, m_out, c_ref, n_ref, m_ref):
    c_id = pl.program_id(1)
    first = c_id == 0
    last = c_id == pl.num_programs(1) - 1

    @pl.when(first)
    def _():
        c_ref[...] = jnp.zeros(c_ref.shape, F32)
        n_ref[...] = jnp.zeros(n_ref.shape, F32)
        m_ref[...] = jnp.zeros(m_ref.shape, F32)

    r, c = _causal_masks()
    causal = r >= c
    tril = jnp.where(causal, 1.0, 0.0)
    triu = jnp.where(r <= c, 1.0, 0.0)
    chunks = range(PROMPT_CPS)
    rows = [slice(ch * CHUNK, (ch + 1) * CHUNK) for ch in chunks]
    gc = gc_ref[...]
    logi_c = gc + pc_ref[0:1, :]
    logf_c = -_softplus(-(gc + pc_ref[1:2, :]))
    b_c = [_mm_hi(tril, logf_c[rw, :]) for rw in rows]
    logi_r = gr_ref[0:ML_HEADS, :] + pr_ref[0]
    logf_r = -_softplus(-(gr_ref[ML_HEADS:2 * ML_HEADS, :] + pr_ref[1]))
    b_r = [_mm_hi(logf_r[:, rw], triu) for rw in rows]

    heads = range(ML_HEADS)
    ks = [slice(j * ML_DK, (j + 1) * ML_DK) for j in heads]
    vs = [slice(j * ML_DV, (j + 1) * ML_DV) for j in heads]
    units = [(ch, j) for ch in chunks for j in heads]
    nu = range(len(units))
    q = [q_ref[rows[ch], ks[j]] for ch, j in units]
    k = [k_ref[rows[ch], ks[j]] * (ML_DK ** -0.5) for ch, j in units]
    v = [v_ref[rows[ch], vs[j]] for ch, j in units]
    bc = [b_c[ch][:, ML_HEADS + j:ML_HEADS + j + 1] for ch, j in units]
    dmat = [jnp.where(causal, bc[u] - b_r[ch][j:j + 1, :] + logi_r[j:j + 1, rows[ch]], NEG)
            for u, (ch, j) in enumerate(units)]
    m_intra = [jnp.max(x, axis=-1, keepdims=True) for x in dmat]
    p = [_mm_nt(q[u], k[u]) * jnp.exp(dmat[u] - m_intra[u]) for u in nu]
    h_intra = [_mm(p[u], v[u]) for u in nu]
    n_intra = [jnp.sum(p[u], axis=-1, keepdims=True) for u in nu]
    b_last = [x[CHUNK - 1:CHUNK, :] for x in bc]
    gk = [b_last[u] - bc[u] + logi_c[rows[ch], j:j + 1] for u, (ch, j) in enumerate(units)]
    m_k = [jnp.max(x, axis=0, keepdims=True) for x in gk]
    kw = [k[u] * jnp.exp(gk[u] - m_k[u]) for u in nu]
    c_loc = [_mm_tn(kw[u], v[u]) for u in nu]
    n_loc = [jnp.sum(kw[u], axis=0, keepdims=True) for u in nu]
    c_cur = [c_ref[j] for j in heads]
    n_cur = [n_ref[j:j + 1, :] for j in heads]
    m_cur = [m_ref[j:j + 1, 0:1] for j in heads]
    for ch in chunks:
        us = [ch * ML_HEADS + j for j in heads]
        qc = [_mm(q[u], c_cur[j]) for j, u in enumerate(us)]
        for j, u in enumerate(us):
            mb = bc[u] + m_cur[j]
            m_t = jnp.maximum(mb, m_intra[u])
            s_inter = jnp.exp(mb - m_t)
            s_intra = jnp.exp(m_intra[u] - m_t)
            num = s_inter * qc[j] + s_intra * h_intra[u]
            den = s_inter * jnp.sum(q[u] * n_cur[j], axis=-1, keepdims=True) + s_intra * n_intra[u]
            h = num / jnp.maximum(jnp.abs(den), jnp.exp(-m_t))
            m_new = jnp.maximum(b_last[u] + m_cur[j], m_k[u])
            sa = jnp.exp(b_last[u] + m_cur[j] - m_new)
            sb = jnp.exp(m_k[u] - m_new)
            c_cur[j] = c_cur[j] * sa + c_loc[u] * sb
            n_cur[j] = n_cur[j] * sa + n_loc[u] * sb
            m_cur[j] = m_new
            h = h * lax.rsqrt(jnp.mean(h * h, axis=-1, keepdims=True) + EPS) * nrm_ref[...]
            mix_ref[rows[ch], vs[j]] = (h * jax.nn.sigmoid(o_ref[rows[ch], vs[j]])
                                        * _silu(zc_ref[rows[ch], vs[j]])).astype(BF16)
    for j in heads:
        c_ref[j] = c_cur[j]
        n_ref[j:j + 1, :] = n_cur[j]
        m_ref[j:j + 1, :] = jnp.broadcast_to(m_cur[j], (1, LANES))

    @pl.when(last)
    def _():
        c_out[0] = c_ref[...]
        n_out[0] = n_ref[...]
        m_out[0] = m_ref[...]


def _mlstm_prompt(zm, zs, zs_t, m_total, bsz, seq, i_bias, f_bias, norm):
    tt = PROMPT_CPS * CHUNK
    nc = seq // tt
    rb = lambda b, c: b * nc + c
    pc = jnp.zeros((SUBLANES, LANES), F32).at[0, :ML_HEADS].set(i_bias).at[1, ML_HEADS:2 * ML_HEADS].set(f_bias)
    pr = jnp.stack([jnp.broadcast_to(i_bias[:, None], (ML_HEADS, tt)),
                    jnp.broadcast_to(f_bias[:, None], (ML_HEADS, tt))])
    full = lambda *shape: pl.BlockSpec(shape, lambda b, c: (0,) * len(shape))
    return pl.pallas_call(
        _mlstm_prompt_kernel,
        grid=(bsz, nc),
        in_specs=[
            pl.BlockSpec((tt, ML_QK), lambda b, c: (rb(b, c), 0)),
            pl.BlockSpec((tt, ML_QK), lambda b, c: (rb(b, c), 1)),
            pl.BlockSpec((tt, ML_V), lambda b, c: (rb(b, c), 1)),
            pl.BlockSpec((tt, ML_V), lambda b, c: (rb(b, c), 2)),
            pl.BlockSpec((tt, ML_V), lambda b, c: (rb(b, c), 3)),
            pl.BlockSpec((tt, LANES), lambda b, c: (rb(b, c), 0)),
            pl.BlockSpec((2 * ML_HEADS, tt), lambda b, c: (0, rb(b, c))),
            full(SUBLANES, LANES), full(2, ML_HEADS, tt), full(1, ML_DV),
        ],
        out_specs=[
            pl.BlockSpec((tt, ML_V), lambda b, c: (rb(b, c), 0)),
            pl.BlockSpec((1, ML_HEADS, ML_DK, ML_DV), lambda b, c: (b, 0, 0, 0)),
            pl.BlockSpec((1, ML_HEADS, ML_DK), lambda b, c: (b, 0, 0)),
            pl.BlockSpec((1, ML_HEADS, LANES), lambda b, c: (b, 0, 0)),
        ],
        out_shape=[
            jax.ShapeDtypeStruct((m_total, ML_V), BF16),
            jax.ShapeDtypeStruct((bsz, ML_HEADS, ML_DK, ML_DV), F32),
            jax.ShapeDtypeStruct((bsz, ML_HEADS, ML_DK), F32),
            jax.ShapeDtypeStruct((bsz, ML_HEADS, LANES), F32),
        ],
        scratch_shapes=[pltpu.VMEM((ML_HEADS, ML_DK, ML_DV), F32), pltpu.VMEM((ML_HEADS, ML_DK), F32),
                        pltpu.VMEM((ML_HEADS, LANES), F32)],
        compiler_params=_params("parallel", "arbitrary"),
        name="mlstm_prompt",
    )(zm, zm, zm, zm, zm, zs, zs_t, pc, pr, norm[None, :])


def _cmlp_prompt_kernel(u_ref, v_ref, z_ref, ws_ref, wb_ref, gain_ref, mix_ref, vrows_ref):
    r, c = _causal_masks()
    causal = r >= c
    for g in range(CM_GROUPS):
        gs = slice(g * CM_GROUP_DIM, (g + 1) * CM_GROUP_DIM)
        w_causal = jnp.where(causal, ws_ref[g], 0.0).astype(BF16)
        for ch in range(CM_CPS):
            rw = slice(ch * CM_CHUNK, (ch + 1) * CM_CHUNK)
            v = _gelu(v_ref[rw, gs])
            v = v * lax.rsqrt(jnp.mean(v * v, axis=-1, keepdims=True) + EPS) * gain_ref[...]
            s = _mm(w_causal, v) + wb_ref[:, g:g + 1]
            mix_ref[rw, gs] = (_gelu(u_ref[rw, gs]) * s * _silu(z_ref[rw, gs])).astype(BF16)
            if ch == CM_CPS - 1:
                vrows_ref[0, :, gs] = v


def _cmlp_prompt(zm, m_total, bsz, seq, v_gain, ws, wb):
    tt = CM_CPS * CM_CHUNK
    nc = seq // tt
    rb = lambda b, c: b * nc + c
    col0 = (2 * ML_QK + 3 * ML_V) // CM_WIDTH
    return pl.pallas_call(
        _cmlp_prompt_kernel,
        grid=(bsz, nc),
        in_specs=[
            pl.BlockSpec((tt, CM_WIDTH), lambda b, c: (rb(b, c), col0)),
            pl.BlockSpec((tt, CM_WIDTH), lambda b, c: (rb(b, c), col0 + 1)),
            pl.BlockSpec((tt, CM_WIDTH), lambda b, c: (rb(b, c), col0 + 2)),
            pl.BlockSpec((CM_GROUPS, CM_CHUNK, CM_CHUNK), lambda b, c: (0, 0, 0)),
            pl.BlockSpec((CM_CHUNK, CM_GROUPS), lambda b, c: (0, 0)),
            pl.BlockSpec((1, CM_GROUP_DIM), lambda b, c: (0, 0)),
        ],
        out_specs=[
            pl.BlockSpec((tt, CM_WIDTH), lambda b, c: (rb(b, c), 0)),
            pl.BlockSpec((1, CM_CHUNK, CM_WIDTH), lambda b, c: (b, 0, 0)),
        ],
        out_shape=[jax.ShapeDtypeStruct((m_total, CM_WIDTH), BF16),
                   jax.ShapeDtypeStruct((bsz, CM_CHUNK, CM_WIDTH), F32)],
        compiler_params=_params("parallel", "arbitrary"),
        name="cmlp_prompt",
    )(zm, zm, zm, ws, wb.T, v_gain[None, :])


SEQ_BLOCK = SUBLANES
DEC_UNROLL = 2
SSD_DG = 2


def _conv_step(raw_ref, cin_ref, w_ref, cout_ref):
    u = raw_ref[...]
    out = w_ref[CONV_K - 1:CONV_K, :] * u
    for k in range(CONV_K - 1):
        out = out + w_ref[k:k + 1, :] * cin_ref[k]
    for k in range(CONV_K - 2):
        cout_ref[k] = cin_ref[k + 1]
    cout_ref[CONV_K - 2] = u
    return out


def _row0(row, fill=0.0):
    return jnp.where(_iota((SUBLANES, row.shape[1]), 0) == 0, row, fill)


def _stash_rows(dst_ref, val):
    for i in range(SEQ_BLOCK):
        dst_ref[i] = jnp.broadcast_to(val[i:i + 1, :], (SUBLANES, val.shape[1]))


def _gather_rows(src_ref):
    rid = _iota(src_ref.shape[1:], 0)
    acc = src_ref[0]
    for i in range(1, SEQ_BLOCK):
        acc = jnp.where(rid == i, src_ref[i], acc)
    return acc


def _ssd_decode_kernel(za_ref, xs_ref, b_ref, c_ref, zc_ref, cx_ref, cb_ref, cc_ref, wx_ref, bx_ref, wb_ref, bb_ref,
                       wc_ref, bc_ref, pc_ref, e_ref, a_ref, d_ref, nrm_ref, s_ref, *rest):
    mix_ref, ncx_ref, ncb_ref, ncc_ref, so_ref = rest[-5:]
    xs = _silu(_conv_step(xs_ref, cx_ref, wx_ref, ncx_ref) + bx_ref[...])
    bm = _silu(_conv_step(b_ref, cb_ref, wb_ref, ncb_ref) + bb_ref[...])
    cm = _silu(_conv_step(c_ref, cc_ref, wc_ref, ncc_ref) + bc_ref[...])
    dtx = _mm_hi(_softplus(zc_ref[...] + pc_ref[0:1, :]), e_ref[...])
    hi, mid, lo = _split3(jnp.exp(dtx * a_ref[...]))
    dx = dtx * xs
    rid = _iota((SUBLANES, SSD_GW), 0)
    ones_rows = jnp.where((_iota((SUBLANES, SSD_STATE), 0) >= 1) & (_iota((SUBLANES, SSD_STATE), 0) <= 3), 1.0, 0.0)
    hpg = SSD_HEADS // SSD_GROUPS
    units = [(g, i) for g in range(SSD_DG) for i in range(SEQ_BLOCK)]
    gs = [slice(g * SSD_GW, (g + 1) * SSD_GW) for g in range(SSD_DG)]
    ns = [slice(g * SSD_STATE, (g + 1) * SSD_STATE) for g in range(SSD_DG)]
    lmat = [jnp.where(rid == 0, dx[i:i + 1, gs[g]], jnp.where(rid == 1, hi[i:i + 1, gs[g]],
                      jnp.where(rid == 2, mid[i:i + 1, gs[g]], jnp.where(rid == 3, lo[i:i + 1, gs[g]], 0.0))))
            for g, i in units]
    upd = [_mm_tn(lmat[n], jnp.concatenate([_row0(bm[i:i + 1, ns[g]]), ones_rows], axis=1))
           for n, (g, i) in enumerate(units)]
    s_new = [s_ref[i, g * hpg:(g + 1) * hpg].reshape(SSD_GW, SSD_STATE) * upd[n][:, SSD_STATE:]
             + upd[n][:, :SSD_STATE] for n, (g, i) in enumerate(units)]
    for n, (g, i) in enumerate(units):
        so_ref[i, g * hpg:(g + 1) * hpg] = s_new[n].reshape(hpg, SSD_HEAD_DIM, SSD_STATE)
    ys = [_mm_nt(jnp.broadcast_to(cm[i:i + 1, ns[g]], (SUBLANES, SSD_STATE)), s_new[n])
          for n, (g, i) in enumerate(units)]
    for g in range(SSD_DG):
        y = ys[g * SEQ_BLOCK]
        for i in range(1, SEQ_BLOCK):
            y = jnp.where(rid == i, ys[g * SEQ_BLOCK + i], y)
        y = y + d_ref[:, gs[g]] * xs[:, gs[g]]
        y = y * _silu(za_ref[:, gs[g]])
        y = y * lax.rsqrt(jnp.mean(y * y, axis=-1, keepdims=True) + EPS) * nrm_ref[:, gs[g]]
        mix_ref[:, gs[g]] = y.astype(BF16)


def _stacked_state_io(states, layer, prev, block, index):
    spec = pl.BlockSpec((None,) + block, lambda *g: (layer,) + index(*g))
    extra_in = [] if prev is None else [prev]
    extra_specs = [] if prev is None else [pl.BlockSpec(memory_space=pl.ANY)]
    return spec, jax.ShapeDtypeStruct(states.shape, F32), extra_in, extra_specs


def _ssd_decode(zm, zs, mix, conv_t, states, layer, prev, mp, conv_w, conv_b, dt_bias, a_log, d_skip, norm):
    bsz = states.shape[1]
    hpg = SSD_HEADS // SSD_GROUPS
    gw, nw, ng = SSD_DG * SSD_GW, SSD_DG * SSD_STATE, SSD_GROUPS // SSD_DG
    st_spec, st_shape, extra_in, extra_specs = _stacked_state_io(
        states, layer, prev, (SEQ_BLOCK, SSD_DG * hpg, SSD_HEAD_DIM, SSD_STATE), lambda s, g: (s, g, 0, 0))
    r0 = mp // SEQ_BLOCK
    pc = jnp.zeros((SUBLANES, LANES), F32).at[0, :SSD_HEADS].set(dt_bias)
    e_mat = (jnp.arange(SSD_WIDTH)[None, :] // SSD_HEAD_DIM == jnp.arange(LANES)[:, None]).astype(F32)
    a_x = jnp.repeat(-jnp.exp(a_log), SSD_HEAD_DIM)[None, :]
    nb = SSD_WIDTH // nw
    zrow = lambda w, col: pl.BlockSpec((SEQ_BLOCK, w), lambda s, g: (r0 + s, col(g)))
    cst = lambda w, col: pl.BlockSpec((CONV_K - 1, SEQ_BLOCK, w), lambda s, g: (0, s, col(g)))
    par = lambda rows, w, col: pl.BlockSpec((rows, w), lambda s, g: (0, col(g)))
    mainb = 6 * D_MODEL // nw
    return pl.pallas_call(
        _ssd_decode_kernel,
        grid=(bsz // SEQ_BLOCK, ng),
        in_specs=[
            zrow(gw, lambda g: g), zrow(gw, lambda g: ng + g),
            zrow(nw, lambda g: mainb + g), zrow(nw, lambda g: mainb + ng + g),
            zrow(LANES, lambda g: 0),
            cst(gw, lambda g: g), cst(nw, lambda g: nb + g), cst(nw, lambda g: nb + ng + g),
            par(CONV_K, gw, lambda g: g), par(1, gw, lambda g: g),
            par(CONV_K, nw, lambda g: nb + g), par(1, nw, lambda g: nb + g),
            par(CONV_K, nw, lambda g: nb + ng + g), par(1, nw, lambda g: nb + ng + g),
            par(SUBLANES, LANES, lambda g: 0), par(LANES, gw, lambda g: g),
            par(1, gw, lambda g: g), par(1, gw, lambda g: g), par(1, gw, lambda g: g),
            st_spec,
            pl.BlockSpec(memory_space=pl.ANY),
        ] + extra_specs,
        out_specs=[
            pl.BlockSpec((SEQ_BLOCK, gw), lambda s, g: (r0 + s, g)),
            cst(gw, lambda g: g), cst(nw, lambda g: g), cst(nw, lambda g: g),
            st_spec,
        ],
        out_shape=[
            jax.ShapeDtypeStruct(mix.shape, mix.dtype),
            jax.ShapeDtypeStruct((CONV_K - 1, bsz, SSD_WIDTH), F32),
            jax.ShapeDtypeStruct((CONV_K - 1, bsz, SSD_GROUPS * SSD_STATE), F32),
            jax.ShapeDtypeStruct((CONV_K - 1, bsz, SSD_GROUPS * SSD_STATE), F32),
            st_shape,
        ],
        input_output_aliases={20: 0} if prev is None else {20: 0, 21: 4},
        compiler_params=_params("parallel", "arbitrary"),
        name="ssd_decode",
    )(zm, zm, zm, zm, zs, conv_t, conv_t, conv_t, conv_w, conv_b[None, :], conv_w, conv_b[None, :], conv_w,
      conv_b[None, :], pc, e_mat, a_x, jnp.repeat(d_skip, SSD_HEAD_DIM)[None, :], norm[None, :], states, mix,
      *extra_in)


def _gdn_decode_kernel(q_ref, k_ref, v_ref, zb_ref, zc_ref, cq_ref, ck_ref, cv_ref, wq_ref, wk_ref, wv_ref, pc_ref,
                       nrm_ref, s_ref, *rest):
    mix_ref, ncq_ref, nck_ref, ncv_ref, so_ref, q_s, k_s, v_s, beta_s, eg_s, qk_s, o_s = rest[-12:]
    q_all = _silu(_conv_step(q_ref, cq_ref, wq_ref, ncq_ref))
    k_all = _silu(_conv_step(k_ref, ck_ref, wk_ref, nck_ref))
    _stash_rows(v_s, _silu(_conv_step(v_ref, cv_ref, wv_ref, ncv_ref)))
    zc = zc_ref[...]
    beta = jax.nn.sigmoid(zc)
    eg = jnp.exp(-jnp.exp(pc_ref[0:1, :]) * _softplus(zc + pc_ref[1:2, :]))
    heads = range(GDN_HG)
    hs = [slice(j * GDN_DK, (j + 1) * GDN_DK) for j in heads]
    qn, kn, beta_x, eg_x, qk_x = [], [], [], [], []
    shape = (SEQ_BLOCK, GDN_DK)
    for j in heads:
        qh, kh = q_all[:, hs[j]], k_all[:, hs[j]]
        qh = qh * (lax.rsqrt(jnp.sum(qh * qh, axis=-1, keepdims=True) + EPS) * (GDN_DK ** -0.5))
        kh = kh * lax.rsqrt(jnp.sum(kh * kh, axis=-1, keepdims=True) + EPS)
        qn.append(qh)
        kn.append(kh)
        beta_x.append(jnp.broadcast_to(beta[:, j:j + 1], shape))
        eg_x.append(jnp.broadcast_to(eg[:, GDN_HG + j:GDN_HG + j + 1], shape))
        qk_x.append(jnp.broadcast_to(jnp.sum(qh * kh, axis=-1, keepdims=True), shape))
    for ref, parts in ((q_s, qn), (k_s, kn), (beta_s, beta_x), (eg_s, eg_x), (qk_s, qk_x)):
        _stash_rows(ref, jnp.concatenate(parts, axis=1))

    rid = _iota((SUBLANES, GDN_DK), 0)

    def body(it, carry):
        seqs = [it * DEC_UNROLL + u for u in range(DEC_UNROLL)]
        k_b, q_b, v_b = [k_s[i] for i in seqs], [q_s[i] for i in seqs], [v_s[i] for i in seqs]
        beta_b, eg_b, qk_b = [beta_s[i] for i in seqs], [eg_s[i] for i in seqs], [qk_s[i] for i in seqs]
        units = [(u, j) for u in range(DEC_UNROLL) for j in heads]
        s_prev = [s_ref[seqs[u], j] for u, j in units]
        ks_qs = [_mm(jnp.where(rid == 0, k_b[u][:, hs[j]], q_b[u][:, hs[j]]), s_prev[n])
                 for n, (u, j) in enumerate(units)]
        vn = [beta_b[u][0:1, hs[j]] * (v_b[u][0:1, hs[j]] - eg_b[u][0:1, hs[j]] * ks_qs[n][0:1, :])
              for n, (u, j) in enumerate(units)]
        outer = [_mm_tn(_row0(k_b[u][:, hs[j]]), _row0(vn[n])) for n, (u, j) in enumerate(units)]
        for n, (u, j) in enumerate(units):
            so_ref[seqs[u], j] = s_prev[n] * eg_b[u][0:1, j * GDN_DK:j * GDN_DK + 1] + outer[n]
            o_row = eg_b[u][0:1, hs[j]] * ks_qs[n][1:2, :] + qk_b[u][0:1, hs[j]] * vn[n]
            o_s[seqs[u], :, hs[j]] = jnp.broadcast_to(o_row, (SUBLANES, GDN_DV))
        return carry

    lax.fori_loop(0, SEQ_BLOCK // DEC_UNROLL, body, 0)
    o_all = _gather_rows(o_s)
    for j in heads:
        o = o_all[:, hs[j]]
        o = o * lax.rsqrt(jnp.mean(o * o, axis=-1, keepdims=True) + EPS) * nrm_ref[...]
        mix_ref[:, hs[j]] = (o * _silu(zb_ref[:, hs[j]])).astype(BF16)


def _gdn_decode(zm, zs, mix, conv_t, states, layer, prev, mp, conv_w, dt_bias, a_log, norm):
    bsz = states.shape[1]
    st_spec, st_shape, extra_in, extra_specs = _stacked_state_io(
        states, layer, prev, (SEQ_BLOCK, GDN_HG, GDN_DK, GDN_DV), lambda s, h: (s, h, 0, 0))
    nhg = GDN_HEADS // GDN_HG
    w = GDN_HG * GDN_DK
    r0 = mp // SEQ_BLOCK
    col0 = 2 * D_MODEL // w
    pc = jnp.zeros((nhg, SUBLANES, LANES), F32)
    pc = pc.at[:, 0, GDN_HG:2 * GDN_HG].set(a_log.reshape(nhg, GDN_HG))
    pc = pc.at[:, 1, GDN_HG:2 * GDN_HG].set(dt_bias.reshape(nhg, GDN_HG))
    seg = lambda p: pl.BlockSpec((SEQ_BLOCK, w), lambda s, h: (r0 + s, col0 + p * nhg + h))
    cst = lambda p: pl.BlockSpec((CONV_K - 1, SEQ_BLOCK, w), lambda s, h: (0, s, p * nhg + h))
    wseg = lambda p: pl.BlockSpec((CONV_K, w), lambda s, h: (0, p * nhg + h))
    cout = pl.BlockSpec((CONV_K - 1, SEQ_BLOCK, w), lambda s, h: (0, s, h))
    row_scratch = pltpu.VMEM((SEQ_BLOCK, SUBLANES, w), F32)
    return pl.pallas_call(
        _gdn_decode_kernel,
        grid=(bsz // SEQ_BLOCK, nhg),
        in_specs=[
            seg(0), seg(1), seg(2), seg(3),
            pl.BlockSpec((SEQ_BLOCK, LANES), lambda s, h: (r0 + s, 1 + h)),
            cst(0), cst(1), cst(2), wseg(0), wseg(1), wseg(2),
            pl.BlockSpec((None, SUBLANES, LANES), lambda s, h: (h, 0, 0)),
            pl.BlockSpec((1, GDN_DV), lambda s, h: (0, 0)),
            st_spec,
            pl.BlockSpec(memory_space=pl.ANY),
        ] + extra_specs,
        out_specs=[
            pl.BlockSpec((SEQ_BLOCK, w), lambda s, h: (r0 + s, h)),
            cout, cout, cout,
            st_spec,
        ],
        out_shape=[
            jax.ShapeDtypeStruct(mix.shape, mix.dtype),
            jax.ShapeDtypeStruct((CONV_K - 1, bsz, GDN_QK), F32),
            jax.ShapeDtypeStruct((CONV_K - 1, bsz, GDN_QK), F32),
            jax.ShapeDtypeStruct((CONV_K - 1, bsz, GDN_V), F32),
            st_shape,
        ],
        scratch_shapes=[row_scratch] * 7,
        input_output_aliases={14: 0} if prev is None else {14: 0, 15: 4},
        compiler_params=_params("parallel", "arbitrary"),
        name="gdn_decode",
    )(zm, zm, zm, zm, zs, conv_t, conv_t, conv_t, conv_w, conv_w, conv_w, pc, norm[None, :], states, mix, *extra_in)


ML_HG = 4


def _mlstm_decode_kernel(q_ref, k_ref, v_ref, o_ref, zc_ref, g_ref, m_ref, n_ref, pc_ref, e_ref, nrm_ref, c_ref,
                         *rest):
    mix_ref, n_out, m_out, c_out, q_s, kb_s, v_s, sa_s, sbqk_s, den_s, h_s = rest[-11:]
    g = g_ref[...]
    logi = g + pc_ref[0:1, :]
    logf = -_softplus(-pltpu.roll(g + pc_ref[1:2, :], LANES - ML_HEADS, 1))
    m_prev = m_ref[...]
    m_new = jnp.maximum(logf + m_prev, logi)
    m_out[...] = m_new
    e_mat = e_ref[...]
    sa_x = _mm_hi(jnp.exp(logf + m_prev - m_new), e_mat)
    sb_x = _mm_hi(jnp.exp(logi - m_new), e_mat)
    em_x = _mm_hi(jnp.exp(-m_new), e_mat)
    q = q_ref[...]
    k = k_ref[...] * (ML_DK ** -0.5)
    n_prev = n_ref[...]
    n_out[...] = n_prev * sa_x + sb_x * k
    _stash_rows(q_s, q)
    _stash_rows(kb_s, sb_x * k)
    _stash_rows(v_s, v_ref[...])
    _stash_rows(sa_s, sa_x)
    heads = range(ML_HG)
    ks = [slice(j * ML_DK, (j + 1) * ML_DK) for j in heads]
    vs = [slice(j * ML_DV, (j + 1) * ML_DV) for j in heads]
    sbqk_x, den_x = [], []
    for j in heads:
        shape = (SEQ_BLOCK, ML_DK)
        qk = jnp.sum(q[:, ks[j]] * k[:, ks[j]], axis=-1, keepdims=True)
        qn = jnp.sum(q[:, ks[j]] * n_prev[:, ks[j]], axis=-1, keepdims=True)
        sbqk = sb_x[:, ks[j]] * jnp.broadcast_to(qk, shape)
        sbqk_x.append(sbqk)
        den_x.append(jnp.maximum(jnp.abs(sa_x[:, ks[j]] * jnp.broadcast_to(qn, shape) + sbqk), em_x[:, ks[j]]))
    _stash_rows(sbqk_s, jnp.concatenate(sbqk_x, axis=1))
    _stash_rows(den_s, jnp.concatenate(den_x, axis=1))

    def body(it, carry):
        seqs = [it * DEC_UNROLL + u for u in range(DEC_UNROLL)]
        q_b, kb_b, v_b = [q_s[i] for i in seqs], [kb_s[i] for i in seqs], [v_s[i] for i in seqs]
        sa_b, sbqk_b, den_b = [sa_s[i] for i in seqs], [sbqk_s[i] for i in seqs], [den_s[i] for i in seqs]
        units = [(u, j) for u in range(DEC_UNROLL) for j in heads]
        c_prev = [c_ref[seqs[u], j] for u, j in units]
        qc = [_mm(q_b[u][:, ks[j]], c_prev[n]) for n, (u, j) in enumerate(units)]
        outer = [_mm_tn(_row0(kb_b[u][:, ks[j]]), _row0(v_b[u][:, vs[j]])) for u, j in units]
        for n, (u, j) in enumerate(units):
            lane0 = slice(j * ML_DK, j * ML_DK + 1)
            c_out[seqs[u], j] = c_prev[n] * sa_b[u][0:1, lane0] + outer[n]
            num = sa_b[u][:, lane0] * qc[n] + sbqk_b[u][:, lane0] * v_b[u][:, vs[j]]
            h_s[seqs[u], :, vs[j]] = num / den_b[u][:, lane0]
        return carry

    lax.fori_loop(0, SEQ_BLOCK // DEC_UNROLL, body, 0)
    h_all = _gather_rows(h_s)
    for j in heads:
        h = h_all[:, vs[j]]
        h = h * lax.rsqrt(jnp.mean(h * h, axis=-1, keepdims=True) + EPS) * nrm_ref[...]
        mix_ref[:, vs[j]] = (h * jax.nn.sigmoid(o_ref[:, vs[j]]) * _silu(zc_ref[:, vs[j]])).astype(BF16)


def _mlstm_decode(zm, zs, mix, c_states, layer, prev, n0, m0, mp, i_bias, f_bias, norm):
    bsz = c_states.shape[1]
    st_spec, st_shape, extra_in, extra_specs = _stacked_state_io(
        c_states, layer, prev, (SEQ_BLOCK, ML_HG, ML_DK, ML_DV), lambda s, h: (s, h, 0, 0))
    nhg = ML_HEADS // ML_HG
    wk, wv = ML_HG * ML_DK, ML_HG * ML_DV
    r0 = mp // SEQ_BLOCK
    pc = jnp.zeros((SUBLANES, LANES), F32).at[0, :ML_HEADS].set(i_bias).at[1, ML_HEADS:2 * ML_HEADS].set(f_bias)
    e_mat = (jnp.arange(ML_QK)[None, :] // ML_DK == jnp.arange(LANES)[:, None]).astype(F32)
    zrow = lambda w, col: pl.BlockSpec((SEQ_BLOCK, w), lambda s, h: (r0 + s, col(h)))
    vcol = 2 * ML_QK // wv
    return pl.pallas_call(
        _mlstm_decode_kernel,
        grid=(bsz // SEQ_BLOCK, nhg),
        in_specs=[
            zrow(wk, lambda h: h), zrow(wk, lambda h: nhg + h),
            zrow(wv, lambda h: vcol + h), zrow(wv, lambda h: vcol + nhg + h), zrow(wv, lambda h: vcol + 2 * nhg + h),
            zrow(LANES, lambda h: 0),
            pl.BlockSpec((SEQ_BLOCK, LANES), lambda s, h: (s, 0)),
            pl.BlockSpec((SEQ_BLOCK, wk), lambda s, h: (s, h)),
            pl.BlockSpec((SUBLANES, LANES), lambda s, h: (0, 0)),
            pl.BlockSpec((LANES, wk), lambda s, h: (0, h)),
            pl.BlockSpec((1, ML_DV), lambda s, h: (0, 0)),
            st_spec,
            pl.BlockSpec(memory_space=pl.ANY),
        ] + extra_specs,
        out_specs=[
            pl.BlockSpec((SEQ_BLOCK, wv), lambda s, h: (r0 + s, h)),
            pl.BlockSpec((SEQ_BLOCK, wk), lambda s, h: (s, h)),
            pl.BlockSpec((SEQ_BLOCK, LANES), lambda s, h: (s, 0)),
            st_spec,
        ],
        out_shape=[
            jax.ShapeDtypeStruct(mix.shape, mix.dtype),
            jax.ShapeDtypeStruct((bsz, ML_QK), F32),
            jax.ShapeDtypeStruct((bsz, LANES), F32),
            st_shape,
        ],
        scratch_shapes=[pltpu.VMEM((SEQ_BLOCK, SUBLANES, w), F32) for w in (wk, wk, wv, wk, wk, wk, wv)],
        input_output_aliases={12: 0} if prev is None else {12: 0, 13: 3},
        compiler_params=_params("parallel", "arbitrary"),
        name="mlstm_decode",
    )(zm, zm, zm, zm, zm, zs, jnp.pad(m0, ((0, 0), (0, LANES - ML_HEADS))), n0.reshape(bsz, ML_QK), pc, e_mat,
      norm[None, :], c_states, mix, *extra_in)


def _cmlp_decode_kernel(u_ref, v_ref, z_ref, ws_ref, wb_ref, gain_ref, mixin_ref, mix_ref, vrows_ref):
    del mixin_ref
    for g in range(CM_GROUPS):
        gs = slice(g * CM_GROUP_DIM, (g + 1) * CM_GROUP_DIM)
        v = _gelu(v_ref[:, gs])
        v = v * lax.rsqrt(jnp.mean(v * v, axis=-1, keepdims=True) + EPS) * gain_ref[...]
        s = ws_ref[:, gs] * v + wb_ref[:, gs]
        mix_ref[:, gs] = (_gelu(u_ref[:, gs]) * s * _silu(z_ref[:, gs])).astype(BF16)
        vrows_ref[:, gs] = v


def _cmlp_decode(zm, mix, mp, bsz, v_gain, ws, wb):
    col0 = (2 * ML_QK + 3 * ML_V) // CM_WIDTH
    r0 = mp // bsz
    zrow = lambda col: pl.BlockSpec((bsz, CM_WIDTH), lambda i: (r0, col))
    par = pl.BlockSpec((1, CM_WIDTH), lambda i: (0, 0))
    return pl.pallas_call(
        _cmlp_decode_kernel,
        grid=(1,),
        in_specs=[zrow(col0), zrow(col0 + 1), zrow(col0 + 2), par, par,
                  pl.BlockSpec((1, CM_GROUP_DIM), lambda i: (0, 0)), pl.BlockSpec(memory_space=pl.ANY)],
        out_specs=[pl.BlockSpec((bsz, CM_WIDTH), lambda i: (r0, 0)), pl.BlockSpec((bsz, CM_WIDTH), lambda i: (0, 0))],
        out_shape=[jax.ShapeDtypeStruct(mix.shape, mix.dtype), jax.ShapeDtypeStruct((bsz, CM_WIDTH), F32)],
        input_output_aliases={6: 0},
        compiler_params=_params("arbitrary"),
        name="cmlp_decode",
    )(zm, zm, zm, jnp.repeat(ws[:, 0, 0], CM_GROUP_DIM)[None, :], jnp.repeat(wb[:, 0], CM_GROUP_DIM)[None, :],
      v_gain[None, :], mix)


def kernel(x_prompt, x_sample, state_ssd_conv, state_ssd, state_gdn_conv, state_gdn, state_mlstm_c,
           state_mlstm_n, state_mlstm_m, even_norm, even_w_in, ssd_conv_w, ssd_conv_b, ssd_dt_bias, ssd_a_log,
           ssd_d, ssd_norm, gdn_conv_w, gdn_dt_bias, gdn_a_log, gdn_norm, even_w_out, odd_norm, odd_w_in,
           mlstm_i_bias, mlstm_f_bias, mlstm_norm, cmlp_v_norm, cmlp_ws, cmlp_b, odd_w_out, final_norm):
    bp, seq, d = x_prompt.shape
    bs = x_sample.shape[0]
    mp = bp * seq
    mt = mp + bs
    x = jnp.concatenate([x_prompt.reshape(mp, d), x_sample.reshape(bs, d)], axis=0)

    keys = ("sc", "ss", "gc", "gs", "mc", "mn", "mm", "cv")
    outs_p = {k: [] for k in keys}
    outs_s = {k: [] for k in keys}
    ss_all = gs_all = mc_all = None
    even_wt_in = jnp.swapaxes(even_w_in, 1, 2)
    odd_wt_in = jnp.swapaxes(odd_w_in, 1, 2)
    for layer in range(DEPTH):
        i = layer // 2
        if layer % 2 == 0:
            w_main, w_small = _prep_even_w_in(even_wt_in, i)
            zm, zs = _inproj(x, even_norm[i], w_main, w_small, tn=1024)
            zs_t = zs[:mp].T
            mix_a, cx, cbc, st = _ssd_prompt(zm, zs, zs_t, mt, bp, seq, ssd_conv_w[i], ssd_conv_b[i],
                                             ssd_dt_bias[i], ssd_a_log[i], ssd_d[i], ssd_norm[i])
            mix_b, cq, ck, cv, gst = _gdn_prompt(zm, zs, zs_t, mt, bp, seq, gdn_conv_w[i], gdn_dt_bias[i],
                                                 gdn_a_log[i], gdn_norm[i])
            hpg = SSD_HEADS // SSD_GROUPS
            outs_p["sc"].append(jnp.concatenate([cx, cbc], axis=-1))
            outs_p["ss"].append(st.reshape(bp, SSD_GROUPS, SSD_STATE, hpg, SSD_HEAD_DIM).transpose(0, 1, 3, 4, 2)
                                .reshape(bp, SSD_HEADS, SSD_HEAD_DIM, SSD_STATE))
            outs_p["gc"].append(jnp.concatenate([cq, ck, cv], axis=-1))
            outs_p["gs"].append(gst)
            mix_a, ncx, ncb, ncc, ss_all = _ssd_decode(
                zm, zs, mix_a, jnp.swapaxes(state_ssd_conv[i], 0, 1), state_ssd, i, ss_all, mp, ssd_conv_w[i],
                ssd_conv_b[i], ssd_dt_bias[i], ssd_a_log[i], ssd_d[i], ssd_norm[i])
            mix_b, ncq, nck, ncv, gs_all = _gdn_decode(
                zm, zs, mix_b, jnp.swapaxes(state_gdn_conv[i], 0, 1), state_gdn, i, gs_all, mp, gdn_conv_w[i],
                gdn_dt_bias[i], gdn_a_log[i], gdn_norm[i])
            outs_s["sc"].append(jnp.swapaxes(jnp.concatenate([ncx, ncb, ncc], axis=-1), 0, 1))
            outs_s["gc"].append(jnp.swapaxes(jnp.concatenate([ncq, nck, ncv], axis=-1), 0, 1))
            w_out = even_w_out[i].astype(BF16)
        else:
            w_main, w_small = _prep_odd_w_in(odd_wt_in, i)
            zm, zs = _inproj(x, odd_norm[i], w_main, w_small, tn=1024)
            zs_t = zs[:mp].T
            mix_a, c_p, n_p, m_p = _mlstm_prompt(zm, zs, zs_t, mt, bp, seq, mlstm_i_bias[i], mlstm_f_bias[i],
                                                 mlstm_norm[i])
            mix_b, v_rows = _cmlp_prompt(zm, mt, bp, seq, cmlp_v_norm[i], cmlp_ws[i], cmlp_b[i])
            outs_p["mc"].append(c_p); outs_p["mn"].append(n_p); outs_p["mm"].append(m_p[:, :, 0])
            outs_p["cv"].append(v_rows)
            mix_a, n_s, m_s, mc_all = _mlstm_decode(zm, zs, mix_a, state_mlstm_c, i, mc_all, state_mlstm_n[i],
                                                    state_mlstm_m[i], mp, mlstm_i_bias[i], mlstm_f_bias[i],
                                                    mlstm_norm[i])
            mix_b, v_row_s = _cmlp_decode(zm, mix_b, mp, bs, cmlp_v_norm[i], cmlp_ws[i], cmlp_b[i])
            outs_s["mn"].append(n_s.reshape(bs, ML_HEADS, ML_DK))
            outs_s["mm"].append(m_s[:, :ML_HEADS])
            outs_s["cv"].append(v_row_s.reshape(bs, 1, CM_WIDTH))
            w_out = odd_w_out[i].astype(BF16)
        x = _outproj(x, mix_a, mix_b, w_out)

    y_p = _final_norm(x, final_norm, 0, mp, 512).reshape(bp, seq, d)
    y_s = _final_norm(x, final_norm, mp, bs, bs).reshape(bs, 1, d)
    st = lambda o, k: jnp.stack(o[k])
    return (y_p, y_s, st(outs_p, "sc"), st(outs_s, "sc"), st(outs_p, "ss"), ss_all,
            st(outs_p, "gc"), st(outs_s, "gc"), st(outs_p, "gs"), gs_all,
            st(outs_p, "mc"), mc_all, st(outs_p, "mn"), st(outs_s, "mn"),
            st(outs_p, "mm"), st(outs_s, "mm"), st(outs_p, "cv"), st(outs_s, "cv"))
```

```python
import jax
import jax.numpy as jnp
import numpy as np
from jax import lax
from jax.experimental import pallas as pl
from jax.experimental.pallas import tpu as pltpu

F32 = jnp.float32
BF16 = jnp.bfloat16
HI = lax.Precision.HIGHEST

D_MODEL = 2048
DEPTH = 4
CHUNK = 128
CONV_K = 4
EPS = 1e-6
NEG = -1e30

SSD_WIDTH = D_MODEL
SSD_HEAD_DIM = 64
SSD_HEADS = SSD_WIDTH // SSD_HEAD_DIM
SSD_STATE = 128
SSD_GROUPS = 4
SSD_GW = SSD_WIDTH // SSD_GROUPS
SSD_BC = 2 * SSD_GROUPS * SSD_STATE
SSD_CONV_DIM = SSD_WIDTH + SSD_BC
GDN_HEADS = 16
GDN_DK = 128
GDN_DV = 128
GDN_QK = GDN_HEADS * GDN_DK
GDN_V = GDN_HEADS * GDN_DV
GDN_CONV_DIM = 2 * GDN_QK + GDN_V
GDN_HG = 8
PROMPT_CPS = 2
ML_CPS = 4
ML_HEADS = 8
ML_DK = 128
ML_DV = 256
ML_QK = ML_HEADS * ML_DK
ML_V = ML_HEADS * ML_DV
CM_WIDTH = D_MODEL // 2
CM_GROUPS = 8
CM_GROUP_DIM = CM_WIDTH // CM_GROUPS
CM_CHUNK = 128
CM_CPS = 4

LANES = 128
SUBLANES = 8

EVEN_MAIN = 6 * D_MODEL + SSD_BC
EVEN_SMALL = 3 * LANES
ODD_MAIN = 2 * ML_QK + 3 * ML_V + 3 * CM_WIDTH
ODD_SMALL = LANES

VMEM_LIMIT = 56 * 1024 * 1024
ROW_TILE_CAP = 1040
BF16_SUBLANES = 16


def _row_tile(m):
    return max(t for t in range(BF16_SUBLANES, ROW_TILE_CAP + 1, BF16_SUBLANES) if m % t == 0)


def _silu(x):
    return x * jax.nn.sigmoid(x)


def _softplus(x):
    return jnp.maximum(x, 0.0) + jnp.log1p(jnp.exp(-jnp.abs(x)))


def _gelu(x):
    return 0.5 * x * (1.0 + jnp.tanh(np.sqrt(2.0 / np.pi).astype(np.float32) * (x + 0.044715 * (x * x * x))))


def _mm(a, b):
    return jnp.dot(a.astype(BF16), b.astype(BF16), preferred_element_type=F32)


def _mm_nt(a, b):
    return lax.dot_general(a.astype(BF16), b.astype(BF16), (((1,), (1,)), ((), ())), preferred_element_type=F32)


def _mm_tn(a, b):
    return lax.dot_general(a.astype(BF16), b.astype(BF16), (((0,), (0,)), ((), ())), preferred_element_type=F32)


def _mm_hi(a, b):
    return jnp.dot(a, b, precision=HI, preferred_element_type=F32)


def _split3(x):
    hi = x.astype(BF16).astype(F32)
    r1 = x - hi
    mid = r1.astype(BF16).astype(F32)
    lo = (r1 - mid).astype(BF16).astype(F32)
    return hi, mid, lo


def _mm_sel(a, sel):
    hi, mid, lo = _split3(a)
    return (_mm(hi, sel) + _mm(mid, sel)) + _mm(lo, sel)


def _iota(shape, axis):
    return lax.broadcasted_iota(jnp.int32, shape, axis)


def _params(*sem):
    return pltpu.CompilerParams(dimension_semantics=sem, vmem_limit_bytes=VMEM_LIMIT)


def _inproj_kernel(x_ref, g_ref, w_ref, ws_ref, z_ref, zs_ref, xn_ref):
    @pl.when(pl.program_id(1) == 0)
    def _():
        x = x_ref[...]
        y = x * lax.rsqrt(jnp.mean(x * x, axis=-1, keepdims=True) + EPS)
        xn = (y * g_ref[...]).astype(BF16)
        xn_ref[...] = xn
        zs_ref[...] = _mm_nt(xn, ws_ref[...])

    z_ref[...] = _mm_nt(xn_ref[...], w_ref[...])


def _inproj(x, g, w_main, w_small, tn):
    m, d = x.shape
    n = w_main.shape[0]
    ns = w_small.shape[0]
    tm = _row_tile(m)
    return pl.pallas_call(
        _inproj_kernel,
        grid=(m // tm, n // tn),
        in_specs=[
            pl.BlockSpec((tm, d), lambda i, j: (i, 0)),
            pl.BlockSpec((1, d), lambda i, j: (0, 0)),
            pl.BlockSpec((tn, d), lambda i, j: (j, 0)),
            pl.BlockSpec((ns, d), lambda i, j: (0, 0)),
        ],
        out_specs=[
            pl.BlockSpec((tm, tn), lambda i, j: (i, j)),
            pl.BlockSpec((tm, ns), lambda i, j: (i, 0)),
        ],
        out_shape=[jax.ShapeDtypeStruct((m, n), F32), jax.ShapeDtypeStruct((m, ns), F32)],
        scratch_shapes=[pltpu.VMEM((tm, d), BF16)],
        compiler_params=_params("parallel", "arbitrary"),
        name="inproj",
    )(x, g.reshape(1, d), w_main, w_small)


def _outproj_kernel(x_ref, ma_ref, mb_ref, wa_ref, wb_ref, o_ref):
    o_ref[...] = (x_ref[...] + jnp.dot(ma_ref[...], wa_ref[...], preferred_element_type=F32)
                  + jnp.dot(mb_ref[...], wb_ref[...], preferred_element_type=F32))


def _outproj(x, mix_a, mix_b, w):
    m, d = x.shape
    ka, kb = mix_a.shape[1], mix_b.shape[1]
    tm, tn = _row_tile(m), 512
    return pl.pallas_call(
        _outproj_kernel,
        grid=(m // tm, d // tn),
        in_specs=[
            pl.BlockSpec((tm, tn), lambda i, j: (i, j)),
            pl.BlockSpec((tm, ka), lambda i, j: (i, 0)),
            pl.BlockSpec((tm, kb), lambda i, j: (i, 0)),
            pl.BlockSpec((ka, tn), lambda i, j: (0, j)),
            pl.BlockSpec((kb, tn), lambda i, j: (ka // kb, j)),
        ],
        out_specs=pl.BlockSpec((tm, tn), lambda i, j: (i, j)),
        out_shape=jax.ShapeDtypeStruct((m, d), F32),
        compiler_params=_params("parallel", "arbitrary"),
        name="outproj",
    )(x, mix_a, mix_b, w, w)


def _final_norm_kernel(x_ref, g_ref, o_ref):
    x = x_ref[...]
    o_ref[...] = x * lax.rsqrt(jnp.mean(x * x, axis=-1, keepdims=True) + EPS) * g_ref[...]


def _final_norm(x, g, row0, rows, tm):
    d = x.shape[1]
    return pl.pallas_call(
        _final_norm_kernel,
        grid=(rows // tm,),
        in_specs=[pl.BlockSpec((tm, d), lambda i: (row0 // tm + i, 0)), pl.BlockSpec((1, d), lambda i: (0, 0))],
        out_specs=pl.BlockSpec((tm, d), lambda i: (i, 0)),
        out_shape=jax.ShapeDtypeStruct((rows, d), F32),
        compiler_params=_params("parallel"),
        name="final_norm",
    )(x, g.reshape(1, d))


REPACK_TN = 512


def _repack_kernel(a_ref, b_ref, o_ref, *, shift, lo, hi):
    j = pl.program_id(0)
    shifted = (j >= lo) & (j < hi)

    @pl.when(shifted)
    def _():
        o_ref[...] = jnp.concatenate([a_ref[shift:, :], b_ref[...]], axis=0).astype(BF16)

    @pl.when(jnp.logical_not(shifted))
    def _():
        o_ref[...] = a_ref[...].astype(BF16)


def _repack(wt_all, layer, n_out, a_idx, shift, lo, hi):
    d = wt_all.shape[2]
    per = REPACK_TN // shift
    kern = lambda a, b, o: _repack_kernel(a, b, o, shift=shift, lo=lo, hi=hi)
    return pl.pallas_call(
        kern,
        grid=(n_out // REPACK_TN,),
        in_specs=[pl.BlockSpec((None, REPACK_TN, d), lambda j: (layer, a_idx(j), 0)),
                  pl.BlockSpec((None, shift, d), lambda j: (layer, (a_idx(j) + 1) * per, 0))],
        out_specs=pl.BlockSpec((REPACK_TN, d), lambda j: (j, 0)),
        out_shape=jax.ShapeDtypeStruct((n_out, d), BF16),
        compiler_params=_params("parallel"),
        name="repack",
    )(wt_all, wt_all)


def _small_pack_kernel(*refs, layout):
    pieces, o_ref = refs[:-1], refs[-1]
    d = o_ref.shape[1]
    for blk, idxs in enumerate(layout):
        rows = [pieces[i][...] for i in idxs] + [jnp.zeros((LANES - SUBLANES * len(idxs), d), F32)]
        o_ref[blk * LANES:(blk + 1) * LANES, :] = jnp.concatenate(rows, axis=0).astype(BF16)


def _small_pack(wt_all, layer, src_rows, layout):
    d = wt_all.shape[2]
    kern = lambda *refs: _small_pack_kernel(*refs, layout=layout)
    spec = lambda r: pl.BlockSpec((None, SUBLANES, d), lambda i: (layer, r // SUBLANES, 0))
    return pl.pallas_call(
        kern,
        grid=(1,),
        in_specs=[spec(r) for r in src_rows],
        out_specs=pl.BlockSpec((len(layout) * LANES, d), lambda i: (0, 0)),
        out_shape=jax.ShapeDtypeStruct((len(layout) * LANES, d), BF16),
        compiler_params=_params("arbitrary"),
        name="small_pack",
    )(*([wt_all] * len(src_rows)))


def _prep_even_w_in(wt_all, layer):
    o_bc = 2 * SSD_WIDTH
    o_dt = o_bc + SSD_BC
    o_q = o_dt + SSD_HEADS
    o_beta = o_q + GDN_CONV_DIM + GDN_V
    o_g = o_beta + GDN_HEADS
    n1, n2 = o_bc // REPACK_TN, (o_bc + o_beta - o_q) // REPACK_TN
    src2, src3 = o_dt // REPACK_TN, o_bc // REPACK_TN
    a_idx = lambda j: jnp.where(j < n1, j, jnp.where(j < n2, j - n1 + src2, j - n2 + src3))
    main = _repack(wt_all, layer, EVEN_MAIN, a_idx, o_q - o_dt, n1, n2)
    n_dt = SSD_HEADS // SUBLANES
    src = [o_dt + SUBLANES * i for i in range(n_dt)]
    layout = [tuple(range(n_dt))]
    for hg in range(GDN_HEADS // GDN_HG):
        src += [o_beta + hg * GDN_HG, o_g + hg * GDN_HG]
        layout.append((len(src) - 2, len(src) - 1))
    return main, _small_pack(wt_all, layer, src, layout)


def _prep_odd_w_in(wt_all, layer):
    o1 = 2 * ML_QK + 3 * ML_V
    o2 = o1 + 2 * ML_HEADS
    main = _repack(wt_all, layer, ODD_MAIN, lambda j: j, o2 - o1, o1 // REPACK_TN, ODD_MAIN // REPACK_TN)
    small = _small_pack(wt_all, layer, [o1, o1 + ML_HEADS], [(0, 1)])
    return main, small


def _conv_chunk(ext_ref, u, w_ref, first):
    t = u.shape[0]

    @pl.when(first)
    def _():
        ext_ref[0:SUBLANES, :] = jnp.zeros((SUBLANES, ext_ref.shape[1]), F32)

    prev = ext_ref[0:SUBLANES, :]
    rid = _iota((SUBLANES, u.shape[1]), 0)
    out = w_ref[CONV_K - 1:CONV_K, :] * u
    for k in range(1, CONV_K):
        rolled = pltpu.roll(u, k, 0)
        head = jnp.where(rid < k, pltpu.roll(prev, k, 0), rolled[0:SUBLANES, :])
        shifted = jnp.concatenate([head, rolled[SUBLANES:, :]], axis=0)
        out = out + w_ref[CONV_K - 1 - k:CONV_K - k, :] * shifted
    ext_ref[0:SUBLANES, :] = u[t - SUBLANES:, :]
    return out


def _conv_tail(ext_ref):
    return ext_ref[SUBLANES - (CONV_K - 1):SUBLANES, :]


def _causal_masks():
    r = _iota((CHUNK, CHUNK), 0)
    c = _iota((CHUNK, CHUNK), 1)
    return r, c


def _ssd_prompt_kernel(za_ref, xs_ref, bc_ref, zc_ref, zr_ref, wx_ref, bx_ref, wbc_ref, bbc_ref, pc_ref, pr_ref,
                       d_ref, nrm_ref, mix_ref, cx_ref, cbc_ref, st_ref, extx, extbc, s_ref):
    c_id = pl.program_id(1)
    first = c_id == 0
    last = c_id == pl.num_programs(1) - 1

    @pl.when(first)
    def _():
        s_ref[...] = jnp.zeros(s_ref.shape, F32)

    xs = _silu(_conv_chunk(extx, xs_ref[...], wx_ref, first) + bx_ref[...])
    bc = _silu(_conv_chunk(extbc, bc_ref[...], wbc_ref, first) + bbc_ref[...])

    r, c = _causal_masks()
    causal = r >= c
    tril = jnp.where(causal, 1.0, 0.0)
    triu = jnp.where(r <= c, 1.0, 0.0)
    dt = _softplus(zc_ref[...] + pc_ref[0:1, :])
    la = _mm_hi(tril, dt * (-jnp.exp(pc_ref[1:2, :])))
    dtr = _softplus(zr_ref[...] + pr_ref[0])
    lar = _mm_hi(dtr * (-jnp.exp(pr_ref[1])), triu)
    la_last = la[CHUNK - 1:CHUNK, :]
    e_mat = jnp.where((_iota((LANES, SSD_WIDTH), 1) >> 6) == _iota((LANES, SSD_WIDTH), 0), 1.0, 0.0)
    ela_x = _mm_sel(jnp.exp(la), e_mat)
    wsx = _mm_sel(jnp.exp(la_last - la) * dt, e_mat)
    dec_x = _mm_sel(jnp.broadcast_to(jnp.exp(la_last), (SUBLANES, LANES)), e_mat)[0:1, :]
    lane_lo = _iota((CHUNK, LANES), 1) < SSD_HEAD_DIM

    hpg = SSD_HEADS // SSD_GROUPS
    for g in range(SSD_GROUPS):
        gs = slice(g * SSD_GW, (g + 1) * SSD_GW)
        bg = bc[:, g * SSD_STATE:(g + 1) * SSD_STATE]
        cg = bc[:, SSD_GROUPS * SSD_STATE + g * SSD_STATE:SSD_GROUPS * SSD_STATE + (g + 1) * SSD_STATE]
        cb = _mm_nt(cg, bg)
        ys = []
        for pair in range(hpg // 2):
            h0 = g * hpg + 2 * pair
            xpair = xs[:, h0 * SSD_HEAD_DIM:(h0 + 2) * SSD_HEAD_DIM]
            halves = []
            for hh in (h0, h0 + 1):
                seg = jnp.where(causal, la[:, hh:hh + 1] - lar[hh:hh + 1, :], NEG)
                lmat = jnp.exp(seg) * cb * dtr[hh:hh + 1, :]
                halves.append(_mm(lmat, xpair))
            ys.append(jnp.where(lane_lo, halves[0], halves[1]))
        y = jnp.concatenate(ys, axis=1)
        s_prev = s_ref[g]
        y = y + _mm(cg, s_prev) * ela_x[:, gs] + d_ref[:, gs] * xs[:, gs]
        y = y * _silu(za_ref[:, gs])
        y = y * lax.rsqrt(jnp.mean(y * y, axis=-1, keepdims=True) + EPS) * nrm_ref[:, gs]
        mix_ref[:, gs] = y.astype(BF16)
        s_ref[g] = s_prev * dec_x[:, gs] + _mm_tn(bg, xs[:, gs] * wsx[:, gs])

    @pl.when(last)
    def _():
        st_ref[0] = s_ref[...]
        cx_ref[0] = _conv_tail(extx)
        cbc_ref[0] = _conv_tail(extbc)


def _ssd_prompt(zm, zs, zs_t, m_total, bsz, seq, conv_w, conv_b, dt_bias, a_log, d_skip, norm):
    nc = seq // CHUNK
    rb = lambda b, c: b * nc + c
    pc = jnp.zeros((SUBLANES, LANES), F32).at[0, :SSD_HEADS].set(dt_bias).at[1, :SSD_HEADS].set(a_log)
    pr = jnp.stack([jnp.broadcast_to(dt_bias[:, None], (SSD_HEADS, CHUNK)),
                    jnp.broadcast_to(a_log[:, None], (SSD_HEADS, CHUNK))])
    full = lambda *shape: pl.BlockSpec(shape, lambda b, c: (0,) * len(shape))
    return pl.pallas_call(
        _ssd_prompt_kernel,
        grid=(bsz, nc),
        in_specs=[
            pl.BlockSpec((CHUNK, SSD_WIDTH), lambda b, c: (rb(b, c), 0)),
            pl.BlockSpec((CHUNK, SSD_WIDTH), lambda b, c: (rb(b, c), 1)),
            pl.BlockSpec((CHUNK, SSD_BC), lambda b, c: (rb(b, c), 6 * D_MODEL // SSD_BC)),
            pl.BlockSpec((CHUNK, LANES), lambda b, c: (rb(b, c), 0)),
            pl.BlockSpec((SSD_HEADS, CHUNK), lambda b, c: (0, rb(b, c))),
            full(CONV_K, SSD_WIDTH), full(1, SSD_WIDTH), full(CONV_K, SSD_BC), full(1, SSD_BC),
            full(SUBLANES, LANES), full(2, SSD_HEADS, CHUNK), full(1, SSD_WIDTH), full(1, SSD_WIDTH),
        ],
        out_specs=[
            pl.BlockSpec((CHUNK, SSD_WIDTH), lambda b, c: (rb(b, c), 0)),
            pl.BlockSpec((1, CONV_K - 1, SSD_WIDTH), lambda b, c: (b, 0, 0)),
            pl.BlockSpec((1, CONV_K - 1, SSD_BC), lambda b, c: (b, 0, 0)),
            pl.BlockSpec((1, SSD_GROUPS, SSD_STATE, SSD_GW), lambda b, c: (b, 0, 0, 0)),
        ],
        out_shape=[
            jax.ShapeDtypeStruct((m_total, SSD_WIDTH), BF16),
            jax.ShapeDtypeStruct((bsz, CONV_K - 1, SSD_WIDTH), F32),
            jax.ShapeDtypeStruct((bsz, CONV_K - 1, SSD_BC), F32),
            jax.ShapeDtypeStruct((bsz, SSD_GROUPS, SSD_STATE, SSD_GW), F32),
        ],
        scratch_shapes=[pltpu.VMEM((SUBLANES, SSD_WIDTH), F32), pltpu.VMEM((SUBLANES, SSD_BC), F32),
                        pltpu.VMEM((SSD_GROUPS, SSD_STATE, SSD_GW), F32)],
        compiler_params=_params("parallel", "arbitrary"),
        name="ssd_prompt",
    )(zm, zm, zm, zs, zs_t, conv_w[:, :SSD_WIDTH], conv_b[None, :SSD_WIDTH], conv_w[:, SSD_WIDTH:],
      conv_b[None, SSD_WIDTH:], pc, pr, jnp.repeat(d_skip, SSD_HEAD_DIM)[None, :], norm[None, :])


def _tri_inverse(mats, r, c):
    def corner(level):
        return ((r >> (level + 1)) == (c >> (level + 1))) & (((r >> level) & 1) == 1) & (((c >> level) & 1) == 0)

    eye = jnp.where(r == c, 1.0, 0.0)
    ts = [eye - jnp.where(corner(0), a, 0.0) for a in mats]
    for level in range(1, 7):
        cm = corner(level)
        xs = [_mm(t, jnp.where(cm, a, 0.0)) for t, a in zip(ts, mats)]
        ts = [t - _mm(x, t) for t, x in zip(ts, xs)]
    return ts


def _gdn_prompt_kernel(q_ref, k_ref, v_ref, zb_ref, zc_ref, zr_ref, wq_ref, wk_ref, wv_ref, pc_ref, pr_ref, nrm_ref,
                       mix_ref, cq_ref, ck_ref, cv_ref, st_ref, extq, extk, extv, s_ref):
    c_id = pl.program_id(2)
    first = c_id == 0
    last = c_id == pl.num_programs(2) - 1

    @pl.when(first)
    def _():
        s_ref[...] = jnp.zeros(s_ref.shape, F32)

    q_all = _silu(_conv_chunk(extq, q_ref[...], wq_ref, first))
    k_all = _silu(_conv_chunk(extk, k_ref[...], wk_ref, first))
    v_all = _silu(_conv_chunk(extv, v_ref[...], wv_ref, first))

    r, c = _causal_masks()
    causal = r >= c
    strict = r > c
    tril = jnp.where(causal, 1.0, 0.0)
    triu = jnp.where(r <= c, 1.0, 0.0)
    chunks = range(PROMPT_CPS)
    rows = [slice(ch * CHUNK, (ch + 1) * CHUNK) for ch in chunks]
    zc = zc_ref[...]
    beta_c = jax.nn.sigmoid(zc)
    g_c = -jnp.exp(pc_ref[0:1, :]) * _softplus(zc + pc_ref[1:2, :])
    gc_c = [_mm_hi(tril, g_c[rw, :]) for rw in rows]
    g_r = -jnp.exp(pr_ref[0]) * _softplus(zr_ref[GDN_HG:2 * GDN_HG, :] + pr_ref[1])
    gc_r = [_mm_hi(g_r[:, rw], triu) for rw in rows]

    heads = range(GDN_HG)
    hs = [slice(j * GDN_DK, (j + 1) * GDN_DK) for j in heads]
    units = [(ch, j) for ch in chunks for j in heads]
    qh = [q_all[rows[ch], hs[j]] for ch, j in units]
    kh = [k_all[rows[ch], hs[j]] for ch, j in units]
    qh = [x * (lax.rsqrt(jnp.sum(x * x, axis=-1, keepdims=True) + EPS) * (GDN_DK ** -0.5)) for x in qh]
    kh = [x * lax.rsqrt(jnp.sum(x * x, axis=-1, keepdims=True) + EPS) for x in kh]
    gcc = [gc_c[ch][:, GDN_HG + j:GDN_HG + j + 1] for ch, j in units]
    beta = [beta_c[rows[ch], j:j + 1] for ch, j in units]
    gam = [jnp.exp(jnp.where(causal, gcc[u] - gc_r[ch][j:j + 1, :], NEG)) for u, (ch, j) in enumerate(units)]
    qkk = [_mm_nt(jnp.concatenate([qh[u], kh[u]], axis=0), kh[u]) for u in range(len(units))]
    aqk = [qkk[u][:CHUNK] * gam[u] for u in range(len(units))]
    tinv = _tri_inverse([jnp.where(strict, beta[u] * qkk[u][CHUNK:] * gam[u], 0.0) for u in range(len(units))], r, c)
    egc = [jnp.exp(g) for g in gcc]
    uw = [_mm(tinv[u], jnp.concatenate([beta[u] * v_all[rows[ch], hs[j]], (beta[u] * egc[u]) * kh[u]], axis=1))
          for u, (ch, j) in enumerate(units)]
    s_cur = [s_ref[j] for j in heads]
    for ch in chunks:
        us = [ch * GDN_HG + j for j in heads]
        ws_qs = [_mm(jnp.concatenate([uw[u][:, GDN_DV:], qh[u] * egc[u]], axis=0), s_cur[j]) for j, u in enumerate(us)]
        vn = [uw[u][:, :GDN_DV] - ws_qs[j][:CHUNK] for j, u in enumerate(us)]
        o = [ws_qs[j][CHUNK:] + _mm(aqk[u], vn[j]) for j, u in enumerate(us)]
        nxt = []
        for j, u in enumerate(us):
            gc_last = gcc[u][CHUNK - 1:CHUNK, :]
            nxt.append(s_cur[j] * jnp.exp(gc_last) + _mm_tn(kh[u] * jnp.exp(gc_last - gcc[u]), vn[j]))
        s_cur = nxt
        for j in heads:
            on = o[j] * lax.rsqrt(jnp.mean(o[j] * o[j], axis=-1, keepdims=True) + EPS) * nrm_ref[...]
            mix_ref[rows[ch], hs[j]] = (on * _silu(zb_ref[rows[ch], hs[j]])).astype(BF16)
    for j in heads:
        s_ref[j] = s_cur[j]

    @pl.when(last)
    def _():
        st_ref[0] = s_ref[...]
        cq_ref[0] = _conv_tail(extq)
        ck_ref[0] = _conv_tail(extk)
        cv_ref[0] = _conv_tail(extv)


def _gdn_prompt(zm, zs, zs_t, m_total, bsz, seq, conv_w, dt_bias, a_log, norm):
    tt = PROMPT_CPS * CHUNK
    nc = seq // tt
    nhg = GDN_HEADS // GDN_HG
    w = GDN_HG * GDN_DK
    rb = lambda b, c: b * nc + c
    col0 = 2 * D_MODEL // w
    pc = jnp.zeros((nhg, SUBLANES, LANES), F32)
    pc = pc.at[:, 0, GDN_HG:2 * GDN_HG].set(a_log.reshape(nhg, GDN_HG))
    pc = pc.at[:, 1, GDN_HG:2 * GDN_HG].set(dt_bias.reshape(nhg, GDN_HG))
    pr = jnp.stack([jnp.broadcast_to(a_log.reshape(nhg, GDN_HG, 1), (nhg, GDN_HG, tt)),
                    jnp.broadcast_to(dt_bias.reshape(nhg, GDN_HG, 1), (nhg, GDN_HG, tt))], axis=1)
    seg = lambda s: pl.BlockSpec((tt, w), lambda b, h, c: (rb(b, c), col0 + s * nhg + h))
    wseg = lambda s: pl.BlockSpec((CONV_K, w), lambda b, h, c: (0, s * nhg + h))
    cout = pl.BlockSpec((1, CONV_K - 1, w), lambda b, h, c: (b, 0, h))
    return pl.pallas_call(
        _gdn_prompt_kernel,
        grid=(bsz, nhg, nc),
        in_specs=[
            seg(0), seg(1), seg(2), seg(3),
            pl.BlockSpec((tt, LANES), lambda b, h, c: (rb(b, c), 1 + h)),
            pl.BlockSpec((2 * GDN_HG, tt), lambda b, h, c: ((1 + h) * LANES // (2 * GDN_HG), rb(b, c))),
            wseg(0), wseg(1), wseg(2),
            pl.BlockSpec((None, SUBLANES, LANES), lambda b, h, c: (h, 0, 0)),
            pl.BlockSpec((None, 2, GDN_HG, tt), lambda b, h, c: (h, 0, 0, 0)),
            pl.BlockSpec((1, GDN_DV), lambda b, h, c: (0, 0)),
        ],
        out_specs=[
            pl.BlockSpec((tt, w), lambda b, h, c: (rb(b, c), h)),
            cout, cout, cout,
            pl.BlockSpec((1, GDN_HG, GDN_DK, GDN_DV), lambda b, h, c: (b, h, 0, 0)),
        ],
        out_shape=[
            jax.ShapeDtypeStruct((m_total, GDN_V), BF16),
            jax.ShapeDtypeStruct((bsz, CONV_K - 1, GDN_QK), F32),
            jax.ShapeDtypeStruct((bsz, CONV_K - 1, GDN_QK), F32),
            jax.ShapeDtypeStruct((bsz, CONV_K - 1, GDN_V), F32),
            jax.ShapeDtypeStruct((bsz, GDN_HEADS, GDN_DK, GDN_DV), F32),
        ],
        scratch_shapes=[pltpu.VMEM((SUBLANES, w), F32)] * 3 + [pltpu.VMEM((GDN_HG, GDN_DK, GDN_DV), F32)],
        compiler_params=_params("parallel", "parallel", "arbitrary"),
        name="gdn_prompt",
    )(zm, zm, zm, zm, zs, zs_t, conv_w, conv_w, conv_w, pc, pr, norm[None, :])


def _mlstm_prompt_kernel(q_ref, k_ref, v_ref, o_ref, zc_ref, gc_ref, gr_ref, pc_ref, pr_ref, nrm_ref,
                         mix_ref, c_out, n_out, m_out, c_ref, n_ref, m_ref):
    c_id = pl.program_id(1)
    first = c_id == 0
    last = c_id == pl.num_programs(1) - 1

    @pl.when(first)
    def _():
        c_ref[...] = jnp.zeros(c_ref.shape, F32)
        n_ref[...] = jnp.zeros(n_ref.shape, F32)
        m_ref[...] = jnp.zeros(m_ref.shape, F32)

    r, c = _causal_masks()
    causal = r >= c
    tril = jnp.where(causal, 1.0, 0.0)
    triu = jnp.where(r <= c, 1.0, 0.0)
    chunks = range(ML_CPS)
    rows = [slice(ch * CHUNK, (ch + 1) * CHUNK) for ch in chunks]
    gc = gc_ref[...]
    logi_c = gc + pc_ref[0:1, :]
    logf_c = -_softplus(-(gc + pc_ref[1:2, :]))
    b_c = [_mm_hi(tril, logf_c[rw, :]) for rw in rows]
    logi_r = gr_ref[0:ML_HEADS, :] + pr_ref[0]
    logf_r = -_softplus(-(gr_ref[ML_HEADS:2 * ML_HEADS, :] + pr_ref[1]))
    b_r = [_mm_hi(logf_r[:, rw], triu) for rw in rows]

    heads = range(ML_HEADS)
    ks = [slice(j * ML_DK, (j + 1) * ML_DK) for j in heads]
    vs = [slice(j * ML_DV, (j + 1) * ML_DV) for j in heads]
    units = [(ch, j) for ch in chunks for j in heads]
    nu = range(len(units))
    q = [q_ref[rows[ch], ks[j]] for ch, j in units]
    k = [k_ref[rows[ch], ks[j]] * (ML_DK ** -0.5) for ch, j in units]
    v = [v_ref[rows[ch], vs[j]] for ch, j in units]
    bc = [b_c[ch][:, ML_HEADS + j:ML_HEADS + j + 1] for ch, j in units]
    dmat = [jnp.where(causal, bc[u] - b_r[ch][j:j + 1, :] + logi_r[j:j + 1, rows[ch]], NEG)
            for u, (ch, j) in enumerate(units)]
    m_intra = [jnp.max(x, axis=-1, keepdims=True) for x in dmat]
    p = [_mm_nt(q[u], k[u]) * jnp.exp(dmat[u] - m_intra[u]) for u in nu]
    h_intra = [_mm(p[u], v[u]) for u in nu]
    n_intra = [jnp.sum(p[u], axis=-1, keepdims=True) for u in nu]
    b_last = [x[CHUNK - 1:CHUNK, :] for x in bc]
    gk = [b_last[u] - bc[u] + logi_c[rows[ch], j:j + 1] for u, (ch, j) in enumerate(units)]
    m_k = [jnp.max(x, axis=0, keepdims=True) for x in gk]
    kw = [k[u] * jnp.exp(gk[u] - m_k[u]) for u in nu]
    c_loc = [_mm_tn(kw[u], v[u]) for u in nu]
    n_loc = [jnp.sum(kw[u], axis=0, keepdims=True) for u in nu]
    c_cur = [c_ref[j] for j in heads]
    n_cur = [n_ref[j:j + 1, :] for j in heads]
    m_cur = [m_ref[j:j + 1, 0:1] for j in heads]
    for ch in chunks:
        us = [ch * ML_HEADS + j for j in heads]
        qc = [_mm(q[u], c_cur[j]) for j, u in enumerate(us)]
        for j, u in enumerate(us):
            mb = bc[u] + m_cur[j]
            m_t = jnp.maximum(mb, m_intra[u])
            s_inter = jnp.exp(mb - m_t)
            s_intra = jnp.exp(m_intra[u] - m_t)
            num = s_inter * qc[j] + s_intra * h_intra[u]
            den = s_inter * jnp.sum(q[u] * n_cur[j], axis=-1, keepdims=True) + s_intra * n_intra[u]
            h = num / jnp.maximum(jnp.abs(den), jnp.exp(-m_t))
            m_new = jnp.maximum(b_last[u] + m_cur[j], m_k[u])
            sa = jnp.exp(b_last[u] + m_cur[j] - m_new)
            sb = jnp.exp(m_k[u] - m_new)
            c_cur[j] = c_cur[j] * sa + c_loc[u] * sb
            n_cur[j] = n_cur[j] * sa + n_loc[u] * sb
            m_cur[j] = m_new
            h = h * lax.rsqrt(jnp.mean(h * h, axis=-1, keepdims=True) + EPS) * nrm_ref[...]
            mix_ref[rows[ch], vs[j]] = (h * jax.nn.sigmoid(o_ref[rows[ch], vs[j]])
                                        * _silu(zc_ref[rows[ch], vs[j]])).astype(BF16)
    for j in heads:
        c_ref[j] = c_cur[j]
        n_ref[j:j + 1, :] = n_cur[j]
        m_ref[j:j + 1, :] = jnp.broadcast_to(m_cur[j], (1, LANES))

    @pl.when(last)
    def _():
        c_out[0] = c_ref[...]
        n_out[0] = n_ref[...]
        m_out[0] = m_ref[...]


def _mlstm_prompt(zm, zs, zs_t, m_total, bsz, seq, i_bias, f_bias, norm):
    tt = ML_CPS * CHUNK
    nc = seq // tt
    rb = lambda b, c: b * nc + c
    pc = jnp.zeros((SUBLANES, LANES), F32).at[0, :ML_HEADS].set(i_bias).at[1, ML_HEADS:2 * ML_HEADS].set(f_bias)
    pr = jnp.stack([jnp.broadcast_to(i_bias[:, None], (ML_HEADS, tt)),
                    jnp.broadcast_to(f_bias[:, None], (ML_HEADS, tt))])
    full = lambda *shape: pl.BlockSpec(shape, lambda b, c: (0,) * len(shape))
    return pl.pallas_call(
        _mlstm_prompt_kernel,
        grid=(bsz, nc),
        in_specs=[
            pl.BlockSpec((tt, ML_QK), lambda b, c: (rb(b, c), 0)),
            pl.BlockSpec((tt, ML_QK), lambda b, c: (rb(b, c), 1)),
            pl.BlockSpec((tt, ML_V), lambda b, c: (rb(b, c), 1)),
            pl.BlockSpec((tt, ML_V), lambda b, c: (rb(b, c), 2)),
            pl.BlockSpec((tt, ML_V), lambda b, c: (rb(b, c), 3)),
            pl.BlockSpec((tt, LANES), lambda b, c: (rb(b, c), 0)),
            pl.BlockSpec((2 * ML_HEADS, tt), lambda b, c: (0, rb(b, c))),
            full(SUBLANES, LANES), full(2, ML_HEADS, tt), full(1, ML_DV),
        ],
        out_specs=[
            pl.BlockSpec((tt, ML_V), lambda b, c: (rb(b, c), 0)),
            pl.BlockSpec((1, ML_HEADS, ML_DK, ML_DV), lambda b, c: (b, 0, 0, 0)),
            pl.BlockSpec((1, ML_HEADS, ML_DK), lambda b, c: (b, 0, 0)),
            pl.BlockSpec((1, ML_HEADS, LANES), lambda b, c: (b, 0, 0)),
        ],
        out_shape=[
            jax.ShapeDtypeStruct((m_total, ML_V), BF16),
            jax.ShapeDtypeStruct((bsz, ML_HEADS, ML_DK, ML_DV), F32),
            jax.ShapeDtypeStruct((bsz, ML_HEADS, ML_DK), F32),
            jax.ShapeDtypeStruct((bsz, ML_HEADS, LANES), F32),
        ],
        scratch_shapes=[pltpu.VMEM((ML_HEADS, ML_DK, ML_DV), F32), pltpu.VMEM((ML_HEADS, ML_DK), F32),
                        pltpu.VMEM((ML_HEADS, LANES), F32)],
        compiler_params=_params("parallel", "arbitrary"),
        name="mlstm_prompt",
    )(zm, zm, zm, zm, zm, zs, zs_t, pc, pr, norm[None, :])


def _cmlp_prompt_kernel(u_ref, v_ref, z_ref, ws_ref, wb_ref, gain_ref, mix_ref, vrows_ref):
    r, c = _causal_masks()
    causal = r >= c
    for g in range(CM_GROUPS):
        gs = slice(g * CM_GROUP_DIM, (g + 1) * CM_GROUP_DIM)
        w_causal = jnp.where(causal, ws_ref[g], 0.0).astype(BF16)
        for ch in range(CM_CPS):
            rw = slice(ch * CM_CHUNK, (ch + 1) * CM_CHUNK)
            v = _gelu(v_ref[rw, gs])
            v = v * lax.rsqrt(jnp.mean(v * v, axis=-1, keepdims=True) + EPS) * gain_ref[...]
            s = _mm(w_causal, v) + wb_ref[:, g:g + 1]
            mix_ref[rw, gs] = (_gelu(u_ref[rw, gs]) * s * _silu(z_ref[rw, gs])).astype(BF16)
            if ch == CM_CPS - 1:
                vrows_ref[0, :, gs] = v


def _cmlp_prompt(zm, m_total, bsz, seq, v_gain, ws, wb):
    tt = CM_CPS * CM_CHUNK
    nc = seq // tt
    rb = lambda b, c: b * nc + c
    col0 = (2 * ML_QK + 3 * ML_V) // CM_WIDTH
    return pl.pallas_call(
        _cmlp_prompt_kernel,
        grid=(bsz, nc),
        in_specs=[
            pl.BlockSpec((tt, CM_WIDTH), lambda b, c: (rb(b, c), col0)),
            pl.BlockSpec((tt, CM_WIDTH), lambda b, c: (rb(b, c), col0 + 1)),
            pl.BlockSpec((tt, CM_WIDTH), lambda b, c: (rb(b, c), col0 + 2)),
            pl.BlockSpec((CM_GROUPS, CM_CHUNK, CM_CHUNK), lambda b, c: (0, 0, 0)),
            pl.BlockSpec((CM_CHUNK, CM_GROUPS), lambda b, c: (0, 0)),
            pl.BlockSpec((1, CM_GROUP_DIM), lambda b, c: (0, 0)),
        ],
        out_specs=[
            pl.BlockSpec((tt, CM_WIDTH), lambda b, c: (rb(b, c), 0)),
            pl.BlockSpec((1, CM_CHUNK, CM_WIDTH), lambda b, c: (b, 0, 0)),
        ],
        out_shape=[jax.ShapeDtypeStruct((m_total, CM_WIDTH), BF16),
                   jax.ShapeDtypeStruct((bsz, CM_CHUNK, CM_WIDTH), F32)],
        compiler_params=_params("parallel", "arbitrary"),
        name="cmlp_prompt",
    )(zm, zm, zm, ws, wb.T, v_gain[None, :])


SEQ_BLOCK = SUBLANES
DEC_UNROLL = 2
SSD_DG = 2


def _conv_step(raw_ref, cin_ref, w_ref, cout_ref):
    u = raw_ref[...]
    out = w_ref[CONV_K - 1:CONV_K, :] * u
    for k in range(CONV_K - 1):
        out = out + w_ref[k:k + 1, :] * cin_ref[k]
    for k in range(CONV_K - 2):
        cout_ref[k] = cin_ref[k + 1]
    cout_ref[CONV_K - 2] = u
    return out


def _row0(row, fill=0.0):
    return jnp.where(_iota((SUBLANES, row.shape[1]), 0) == 0, row, fill)


def _stash_rows(dst_ref, val):
    for i in range(SEQ_BLOCK):
        dst_ref[i] = jnp.broadcast_to(val[i:i + 1, :], (SUBLANES, val.shape[1]))


def _gather_rows(src_ref):
    rid = _iota(src_ref.shape[1:], 0)
    acc = src_ref[0]
    for i in range(1, SEQ_BLOCK):
        acc = jnp.where(rid == i, src_ref[i], acc)
    return acc


def _ssd_decode_kernel(za_ref, xs_ref, b_ref, c_ref, zc_ref, cx_ref, cb_ref, cc_ref, wx_ref, bx_ref, wb_ref, bb_ref,
                       wc_ref, bc_ref, pc_ref, e_ref, a_ref, d_ref, nrm_ref, s_ref, *rest):
    mix_ref, ncx_ref, ncb_ref, ncc_ref, so_ref = rest[-5:]
    xs = _silu(_conv_step(xs_ref, cx_ref, wx_ref, ncx_ref) + bx_ref[...])
    bm = _silu(_conv_step(b_ref, cb_ref, wb_ref, ncb_ref) + bb_ref[...])
    cm = _silu(_conv_step(c_ref, cc_ref, wc_ref, ncc_ref) + bc_ref[...])
    dtx = _mm_hi(_softplus(zc_ref[...] + pc_ref[0:1, :]), e_ref[...])
    hi, mid, lo = _split3(jnp.exp(dtx * a_ref[...]))
    dx = dtx * xs
    rid = _iota((SUBLANES, SSD_GW), 0)
    ones_rows = jnp.where((_iota((SUBLANES, SSD_STATE), 0) >= 1) & (_iota((SUBLANES, SSD_STATE), 0) <= 3), 1.0, 0.0)
    hpg = SSD_HEADS // SSD_GROUPS
    units = [(g, i) for g in range(SSD_DG) for i in range(SEQ_BLOCK)]
    gs = [slice(g * SSD_GW, (g + 1) * SSD_GW) for g in range(SSD_DG)]
    ns = [slice(g * SSD_STATE, (g + 1) * SSD_STATE) for g in range(SSD_DG)]
    lmat = [jnp.where(rid == 0, dx[i:i + 1, gs[g]], jnp.where(rid == 1, hi[i:i + 1, gs[g]],
                      jnp.where(rid == 2, mid[i:i + 1, gs[g]], jnp.where(rid == 3, lo[i:i + 1, gs[g]], 0.0))))
            for g, i in units]
    upd = [_mm_tn(lmat[n], jnp.concatenate([_row0(bm[i:i + 1, ns[g]]), ones_rows], axis=1))
           for n, (g, i) in enumerate(units)]
    s_new = [s_ref[i, g * hpg:(g + 1) * hpg].reshape(SSD_GW, SSD_STATE) * upd[n][:, SSD_STATE:]
             + upd[n][:, :SSD_STATE] for n, (g, i) in enumerate(units)]
    for n, (g, i) in enumerate(units):
        so_ref[i, g * hpg:(g + 1) * hpg] = s_new[n].reshape(hpg, SSD_HEAD_DIM, SSD_STATE)
    ys = [_mm_nt(jnp.broadcast_to(cm[i:i + 1, ns[g]], (SUBLANES, SSD_STATE)), s_new[n])
          for n, (g, i) in enumerate(units)]
    for g in range(SSD_DG):
        y = ys[g * SEQ_BLOCK]
        for i in range(1, SEQ_BLOCK):
            y = jnp.where(rid == i, ys[g * SEQ_BLOCK + i], y)
        y = y + d_ref[:, gs[g]] * xs[:, gs[g]]
        y = y * _silu(za_ref[:, gs[g]])
        y = y * lax.rsqrt(jnp.mean(y * y, axis=-1, keepdims=True) + EPS) * nrm_ref[:, gs[g]]
        mix_ref[:, gs[g]] = y.astype(BF16)


def _stacked_state_io(states, layer, prev, block, index):
    spec = pl.BlockSpec((None,) + block, lambda *g: (layer,) + index(*g))
    extra_in = [] if prev is None else [prev]
    extra_specs = [] if prev is None else [pl.BlockSpec(memory_space=pl.ANY)]
    return spec, jax.ShapeDtypeStruct(states.shape, F32), extra_in, extra_specs


def _ssd_decode(zm, zs, mix, conv_t, states, layer, prev, mp, conv_w, conv_b, dt_bias, a_log, d_skip, norm):
    bsz = states.shape[1]
    hpg = SSD_HEADS // SSD_GROUPS
    gw, nw, ng = SSD_DG * SSD_GW, SSD_DG * SSD_STATE, SSD_GROUPS // SSD_DG
    st_spec, st_shape, extra_in, extra_specs = _stacked_state_io(
        states, layer, prev, (SEQ_BLOCK, SSD_DG * hpg, SSD_HEAD_DIM, SSD_STATE), lambda s, g: (s, g, 0, 0))
    r0 = mp // SEQ_BLOCK
    pc = jnp.zeros((SUBLANES, LANES), F32).at[0, :SSD_HEADS].set(dt_bias)
    e_mat = (jnp.arange(SSD_WIDTH)[None, :] // SSD_HEAD_DIM == jnp.arange(LANES)[:, None]).astype(F32)
    a_x = jnp.repeat(-jnp.exp(a_log), SSD_HEAD_DIM)[None, :]
    nb = SSD_WIDTH // nw
    zrow = lambda w, col: pl.BlockSpec((SEQ_BLOCK, w), lambda s, g: (r0 + s, col(g)))
    cst = lambda w, col: pl.BlockSpec((CONV_K - 1, SEQ_BLOCK, w), lambda s, g: (0, s, col(g)))
    par = lambda rows, w, col: pl.BlockSpec((rows, w), lambda s, g: (0, col(g)))
    mainb = 6 * D_MODEL // nw
    return pl.pallas_call(
        _ssd_decode_kernel,
        grid=(bsz // SEQ_BLOCK, ng),
        in_specs=[
            zrow(gw, lambda g: g), zrow(gw, lambda g: ng + g),
            zrow(nw, lambda g: mainb + g), zrow(nw, lambda g: mainb + ng + g),
            zrow(LANES, lambda g: 0),
            cst(gw, lambda g: g), cst(nw, lambda g: nb + g), cst(nw, lambda g: nb + ng + g),
            par(CONV_K, gw, lambda g: g), par(1, gw, lambda g: g),
            par(CONV_K, nw, lambda g: nb + g), par(1, nw, lambda g: nb + g),
            par(CONV_K, nw, lambda g: nb + ng + g), par(1, nw, lambda g: nb + ng + g),
            par(SUBLANES, LANES, lambda g: 0), par(LANES, gw, lambda g: g),
            par(1, gw, lambda g: g), par(1, gw, lambda g: g), par(1, gw, lambda g: g),
            st_spec,
            pl.BlockSpec(memory_space=pl.ANY),
        ] + extra_specs,
        out_specs=[
            pl.BlockSpec((SEQ_BLOCK, gw), lambda s, g: (r0 + s, g)),
            cst(gw, lambda g: g), cst(nw, lambda g: g), cst(nw, lambda g: g),
            st_spec,
        ],
        out_shape=[
            jax.ShapeDtypeStruct(mix.shape, mix.dtype),
            jax.ShapeDtypeStruct((CONV_K - 1, bsz, SSD_WIDTH), F32),
            jax.ShapeDtypeStruct((CONV_K - 1, bsz, SSD_GROUPS * SSD_STATE), F32),
            jax.ShapeDtypeStruct((CONV_K - 1, bsz, SSD_GROUPS * SSD_STATE), F32),
            st_shape,
        ],
        input_output_aliases={20: 0} if prev is None else {20: 0, 21: 4},
        compiler_params=_params("parallel", "arbitrary"),
        name="ssd_decode",
    )(zm, zm, zm, zm, zs, conv_t, conv_t, conv_t, conv_w, conv_b[None, :], conv_w, conv_b[None, :], conv_w,
      conv_b[None, :], pc, e_mat, a_x, jnp.repeat(d_skip, SSD_HEAD_DIM)[None, :], norm[None, :], states, mix,
      *extra_in)


def _gdn_decode_kernel(q_ref, k_ref, v_ref, zb_ref, zc_ref, cq_ref, ck_ref, cv_ref, wq_ref, wk_ref, wv_ref, pc_ref,
                       nrm_ref, s_ref, *rest):
    mix_ref, ncq_ref, nck_ref, ncv_ref, so_ref, q_s, k_s, v_s, beta_s, eg_s, qk_s, o_s = rest[-12:]
    q_all = _silu(_conv_step(q_ref, cq_ref, wq_ref, ncq_ref))
    k_all = _silu(_conv_step(k_ref, ck_ref, wk_ref, nck_ref))
    _stash_rows(v_s, _silu(_conv_step(v_ref, cv_ref, wv_ref, ncv_ref)))
    zc = zc_ref[...]
    beta = jax.nn.sigmoid(zc)
    eg = jnp.exp(-jnp.exp(pc_ref[0:1, :]) * _softplus(zc + pc_ref[1:2, :]))
    heads = range(GDN_HG)
    hs = [slice(j * GDN_DK, (j + 1) * GDN_DK) for j in heads]
    qn, kn, beta_x, eg_x, qk_x = [], [], [], [], []
    shape = (SEQ_BLOCK, GDN_DK)
    for j in heads:
        qh, kh = q_all[:, hs[j]], k_all[:, hs[j]]
        qh = qh * (lax.rsqrt(jnp.sum(qh * qh, axis=-1, keepdims=True) + EPS) * (GDN_DK ** -0.5))
        kh = kh * lax.rsqrt(jnp.sum(kh * kh, axis=-1, keepdims=True) + EPS)
        qn.append(qh)
        kn.append(kh)
        beta_x.append(jnp.broadcast_to(beta[:, j:j + 1], shape))
        eg_x.append(jnp.broadcast_to(eg[:, GDN_HG + j:GDN_HG + j + 1], shape))
        qk_x.append(jnp.broadcast_to(jnp.sum(qh * kh, axis=-1, keepdims=True), shape))
    for ref, parts in ((q_s, qn), (k_s, kn), (beta_s, beta_x), (eg_s, eg_x), (qk_s, qk_x)):
        _stash_rows(ref, jnp.concatenate(parts, axis=1))

    rid = _iota((SUBLANES, GDN_DK), 0)

    def body(it, carry):
        seqs = [it * DEC_UNROLL + u for u in range(DEC_UNROLL)]
        k_b, q_b, v_b = [k_s[i] for i in seqs], [q_s[i] for i in seqs], [v_s[i] for i in seqs]
        beta_b, eg_b, qk_b = [beta_s[i] for i in seqs], [eg_s[i] for i in seqs], [qk_s[i] for i in seqs]
        units = [(u, j) for u in range(DEC_UNROLL) for j in heads]
        s_prev = [s_ref[seqs[u], j] for u, j in units]
        ks_qs = [_mm(jnp.where(rid == 0, k_b[u][:, hs[j]], q_b[u][:, hs[j]]), s_prev[n])
                 for n, (u, j) in enumerate(units)]
        vn = [beta_b[u][0:1, hs[j]] * (v_b[u][0:1, hs[j]] - eg_b[u][0:1, hs[j]] * ks_qs[n][0:1, :])
              for n, (u, j) in enumerate(units)]
        outer = [_mm_tn(_row0(k_b[u][:, hs[j]]), _row0(vn[n])) for n, (u, j) in enumerate(units)]
        for n, (u, j) in enumerate(units):
            so_ref[seqs[u], j] = s_prev[n] * eg_b[u][0:1, j * GDN_DK:j * GDN_DK + 1] + outer[n]
            o_row = eg_b[u][0:1, hs[j]] * ks_qs[n][1:2, :] + qk_b[u][0:1, hs[j]] * vn[n]
            o_s[seqs[u], :, hs[j]] = jnp.broadcast_to(o_row, (SUBLANES, GDN_DV))
        return carry

    lax.fori_loop(0, SEQ_BLOCK // DEC_UNROLL, body, 0)
    o_all = _gather_rows(o_s)
    for j in heads:
        o = o_all[:, hs[j]]
        o = o * lax.rsqrt(jnp.mean(o * o, axis=-1, keepdims=True) + EPS) * nrm_ref[...]
        mix_ref[:, hs[j]] = (o * _silu(zb_ref[:, hs[j]])).astype(BF16)


def _gdn_decode(zm, zs, mix, conv_t, states, layer, prev, mp, conv_w, dt_bias, a_log, norm):
    bsz = states.shape[1]
    st_spec, st_shape, extra_in, extra_specs = _stacked_state_io(
        states, layer, prev, (SEQ_BLOCK, GDN_HG, GDN_DK, GDN_DV), lambda s, h: (s, h, 0, 0))
    nhg = GDN_HEADS // GDN_HG
    w = GDN_HG * GDN_DK
    r0 = mp // SEQ_BLOCK
    col0 = 2 * D_MODEL // w
    pc = jnp.zeros((nhg, SUBLANES, LANES), F32)
    pc = pc.at[:, 0, GDN_HG:2 * GDN_HG].set(a_log.reshape(nhg, GDN_HG))
    pc = pc.at[:, 1, GDN_HG:2 * GDN_HG].set(dt_bias.reshape(nhg, GDN_HG))
    seg = lambda p: pl.BlockSpec((SEQ_BLOCK, w), lambda s, h: (r0 + s, col0 + p * nhg + h))
    cst = lambda p: pl.BlockSpec((CONV_K - 1, SEQ_BLOCK, w), lambda s, h: (0, s, p * nhg + h))
    wseg = lambda p: pl.BlockSpec((CONV_K, w), lambda s, h: (0, p * nhg + h))
    cout = pl.BlockSpec((CONV_K - 1, SEQ_BLOCK, w), lambda s, h: (0, s, h))
    row_scratch = pltpu.VMEM((SEQ_BLOCK, SUBLANES, w), F32)
    return pl.pallas_call(
        _gdn_decode_kernel,
        grid=(bsz // SEQ_BLOCK, nhg),
        in_specs=[
            seg(0), seg(1), seg(2), seg(3),
            pl.BlockSpec((SEQ_BLOCK, LANES), lambda s, h: (r0 + s, 1 + h)),
            cst(0), cst(1), cst(2), wseg(0), wseg(1), wseg(2),
            pl.BlockSpec((None, SUBLANES, LANES), lambda s, h: (h, 0, 0)),
            pl.BlockSpec((1, GDN_DV), lambda s, h: (0, 0)),
            st_spec,
            pl.BlockSpec(memory_space=pl.ANY),
        ] + extra_specs,
        out_specs=[
            pl.BlockSpec((SEQ_BLOCK, w), lambda s, h: (r0 + s, h)),
            cout, cout, cout,
            st_spec,
        ],
        out_shape=[
            jax.ShapeDtypeStruct(mix.shape, mix.dtype),
            jax.ShapeDtypeStruct((CONV_K - 1, bsz, GDN_QK), F32),
            jax.ShapeDtypeStruct((CONV_K - 1, bsz, GDN_QK), F32),
            jax.ShapeDtypeStruct((CONV_K - 1, bsz, GDN_V), F32),
            st_shape,
        ],
        scratch_shapes=[row_scratch] * 7,
        input_output_aliases={14: 0} if prev is None else {14: 0, 15: 4},
        compiler_params=_params("parallel", "arbitrary"),
        name="gdn_decode",
    )(zm, zm, zm, zm, zs, conv_t, conv_t, conv_t, conv_w, conv_w, conv_w, pc, norm[None, :], states, mix, *extra_in)


ML_HG = 4


def _mlstm_decode_kernel(q_ref, k_ref, v_ref, o_ref, zc_ref, g_ref, m_ref, n_ref, pc_ref, e_ref, nrm_ref, c_ref,
                         *rest):
    mix_ref, n_out, m_out, c_out, q_s, kb_s, v_s, sa_s, sbqk_s, den_s, h_s = rest[-11:]
    g = g_ref[...]
    logi = g + pc_ref[0:1, :]
    logf = -_softplus(-pltpu.roll(g + pc_ref[1:2, :], LANES - ML_HEADS, 1))
    m_prev = m_ref[...]
    m_new = jnp.maximum(logf + m_prev, logi)
    m_out[...] = m_new
    e_mat = e_ref[...]
    sa_x = _mm_hi(jnp.exp(logf + m_prev - m_new), e_mat)
    sb_x = _mm_hi(jnp.exp(logi - m_new), e_mat)
    em_x = _mm_hi(jnp.exp(-m_new), e_mat)
    q = q_ref[...]
    k = k_ref[...] * (ML_DK ** -0.5)
    n_prev = n_ref[...]
    n_out[...] = n_prev * sa_x + sb_x * k
    _stash_rows(q_s, q)
    _stash_rows(kb_s, sb_x * k)
    _stash_rows(v_s, v_ref[...])
    _stash_rows(sa_s, sa_x)
    heads = range(ML_HG)
    ks = [slice(j * ML_DK, (j + 1) * ML_DK) for j in heads]
    vs = [slice(j * ML_DV, (j + 1) * ML_DV) for j in heads]
    sbqk_x, den_x = [], []
    for j in heads:
        shape = (SEQ_BLOCK, ML_DK)
        qk = jnp.sum(q[:, ks[j]] * k[:, ks[j]], axis=-1, keepdims=True)
        qn = jnp.sum(q[:, ks[j]] * n_prev[:, ks[j]], axis=-1, keepdims=True)
        sbqk = sb_x[:, ks[j]] * jnp.broadcast_to(qk, shape)
        sbqk_x.append(sbqk)
        den_x.append(jnp.maximum(jnp.abs(sa_x[:, ks[j]] * jnp.broadcast_to(qn, shape) + sbqk), em_x[:, ks[j]]))
    _stash_rows(sbqk_s, jnp.concatenate(sbqk_x, axis=1))
    _stash_rows(den_s, jnp.concatenate(den_x, axis=1))

    def body(it, carry):
        seqs = [it * DEC_UNROLL + u for u in range(DEC_UNROLL)]
        q_b, kb_b, v_b = [q_s[i] for i in seqs], [kb_s[i] for i in seqs], [v_s[i] for i in seqs]
        sa_b, sbqk_b, den_b = [sa_s[i] for i in seqs], [sbqk_s[i] for i in seqs], [den_s[i] for i in seqs]
        units = [(u, j) for u in range(DEC_UNROLL) for j in heads]
        c_prev = [c_ref[seqs[u], j] for u, j in units]
        qc = [_mm(q_b[u][:, ks[j]], c_prev[n]) for n, (u, j) in enumerate(units)]
        outer = [_mm_tn(_row0(kb_b[u][:, ks[j]]), _row0(v_b[u][:, vs[j]])) for u, j in units]
        for n, (u, j) in enumerate(units):
            lane0 = slice(j * ML_DK, j * ML_DK + 1)
            c_out[seqs[u], j] = c_prev[n] * sa_b[u][0:1, lane0] + outer[n]
            num = sa_b[u][:, lane0] * qc[n] + sbqk_b[u][:, lane0] * v_b[u][:, vs[j]]
            h_s[seqs[u], :, vs[j]] = num / den_b[u][:, lane0]
        return carry

    lax.fori_loop(0, SEQ_BLOCK // DEC_UNROLL, body, 0)
    h_all = _gather_rows(h_s)
    for j in heads:
        h = h_all[:, vs[j]]
        h = h * lax.rsqrt(jnp.mean(h * h, axis=-1, keepdims=True) + EPS) * nrm_ref[...]
        mix_ref[:, vs[j]] = (h * jax.nn.sigmoid(o_ref[:, vs[j]]) * _silu(zc_ref[:, vs[j]])).astype(BF16)


def _mlstm_decode(zm, zs, mix, c_states, layer, prev, n0, m0, mp, i_bias, f_bias, norm):
    bsz = c_states.shape[1]
    st_spec, st_shape, extra_in, extra_specs = _stacked_state_io(
        c_states, layer, prev, (SEQ_BLOCK, ML_HG, ML_DK, ML_DV), lambda s, h: (s, h, 0, 0))
    nhg = ML_HEADS // ML_HG
    wk, wv = ML_HG * ML_DK, ML_HG * ML_DV
    r0 = mp // SEQ_BLOCK
    pc = jnp.zeros((SUBLANES, LANES), F32).at[0, :ML_HEADS].set(i_bias).at[1, ML_HEADS:2 * ML_HEADS].set(f_bias)
    e_mat = (jnp.arange(ML_QK)[None, :] // ML_DK == jnp.arange(LANES)[:, None]).astype(F32)
    zrow = lambda w, col: pl.BlockSpec((SEQ_BLOCK, w), lambda s, h: (r0 + s, col(h)))
    vcol = 2 * ML_QK // wv
    return pl.pallas_call(
        _mlstm_decode_kernel,
        grid=(bsz // SEQ_BLOCK, nhg),
        in_specs=[
            zrow(wk, lambda h: h), zrow(wk, lambda h: nhg + h),
            zrow(wv, lambda h: vcol + h), zrow(wv, lambda h: vcol + nhg + h), zrow(wv, lambda h: vcol + 2 * nhg + h),
            zrow(LANES, lambda h: 0),
            pl.BlockSpec((SEQ_BLOCK, LANES), lambda s, h: (s, 0)),
            pl.BlockSpec((SEQ_BLOCK, wk), lambda s, h: (s, h)),
            pl.BlockSpec((SUBLANES, LANES), lambda s, h: (0, 0)),
            pl.BlockSpec((LANES, wk), lambda s, h: (0, h)),
            pl.BlockSpec((1, ML_DV), lambda s, h: (0, 0)),
            st_spec,
            pl.BlockSpec(memory_space=pl.ANY),
        ] + extra_specs,
        out_specs=[
            pl.BlockSpec((SEQ_BLOCK, wv), lambda s, h: (r0 + s, h)),
            pl.BlockSpec((SEQ_BLOCK, wk), lambda s, h: (s, h)),
            pl.BlockSpec((SEQ_BLOCK, LANES), lambda s, h: (s, 0)),
            st_spec,
        ],
        out_shape=[
            jax.ShapeDtypeStruct(mix.shape, mix.dtype),
            jax.ShapeDtypeStruct((bsz, ML_QK), F32),
            jax.ShapeDtypeStruct((bsz, LANES), F32),
            st_shape,
        ],
        scratch_shapes=[pltpu.VMEM((SEQ_BLOCK, SUBLANES, w), F32) for w in (wk, wk, wv, wk, wk, wk, wv)],
        input_output_aliases={12: 0} if prev is None else {12: 0, 13: 3},
        compiler_params=_params("parallel", "arbitrary"),
        name="mlstm_decode",
    )(zm, zm, zm, zm, zm, zs, jnp.pad(m0, ((0, 0), (0, LANES - ML_HEADS))), n0.reshape(bsz, ML_QK), pc, e_mat,
      norm[None, :], c_states, mix, *extra_in)


def _cmlp_decode_kernel(u_ref, v_ref, z_ref, ws_ref, wb_ref, gain_ref, mixin_ref, mix_ref, vrows_ref):
    del mixin_ref
    for g in range(CM_GROUPS):
        gs = slice(g * CM_GROUP_DIM, (g + 1) * CM_GROUP_DIM)
        v = _gelu(v_ref[:, gs])
        v = v * lax.rsqrt(jnp.mean(v * v, axis=-1, keepdims=True) + EPS) * gain_ref[...]
        s = ws_ref[:, gs] * v + wb_ref[:, gs]
        mix_ref[:, gs] = (_gelu(u_ref[:, gs]) * s * _silu(z_ref[:, gs])).astype(BF16)
        vrows_ref[:, gs] = v


def _cmlp_decode(zm, mix, mp, bsz, v_gain, ws, wb):
    col0 = (2 * ML_QK + 3 * ML_V) // CM_WIDTH
    r0 = mp // bsz
    zrow = lambda col: pl.BlockSpec((bsz, CM_WIDTH), lambda i: (r0, col))
    par = pl.BlockSpec((1, CM_WIDTH), lambda i: (0, 0))
    return pl.pallas_call(
        _cmlp_decode_kernel,
        grid=(1,),
        in_specs=[zrow(col0), zrow(col0 + 1), zrow(col0 + 2), par, par,
                  pl.BlockSpec((1, CM_GROUP_DIM), lambda i: (0, 0)), pl.BlockSpec(memory_space=pl.ANY)],
        out_specs=[pl.BlockSpec((bsz, CM_WIDTH), lambda i: (r0, 0)), pl.BlockSpec((bsz, CM_WIDTH), lambda i: (0, 0))],
        out_shape=[jax.ShapeDtypeStruct(mix.shape, mix.dtype), jax.ShapeDtypeStruct((bsz, CM_WIDTH), F32)],
        input_output_aliases={6: 0},
        compiler_params=_params("arbitrary"),
        name="cmlp_decode",
    )(zm, zm, zm, jnp.repeat(ws[:, 0, 0], CM_GROUP_DIM)[None, :], jnp.repeat(wb[:, 0], CM_GROUP_DIM)[None, :],
      v_gain[None, :], mix)


def kernel(x_prompt, x_sample, state_ssd_conv, state_ssd, state_gdn_conv, state_gdn, state_mlstm_c,
           state_mlstm_n, state_mlstm_m, even_norm, even_w_in, ssd_conv_w, ssd_conv_b, ssd_dt_bias, ssd_a_log,
           ssd_d, ssd_norm, gdn_conv_w, gdn_dt_bias, gdn_a_log, gdn_norm, even_w_out, odd_norm, odd_w_in,
           mlstm_i_bias, mlstm_f_bias, mlstm_norm, cmlp_v_norm, cmlp_ws, cmlp_b, odd_w_out, final_norm):
    bp, seq, d = x_prompt.shape
    bs = x_sample.shape[0]
    mp = bp * seq
    mt = mp + bs
    x = jnp.concatenate([x_prompt.reshape(mp, d), x_sample.reshape(bs, d)], axis=0)

    keys = ("sc", "ss", "gc", "gs", "mc", "mn", "mm", "cv")
    outs_p = {k: [] for k in keys}
    outs_s = {k: [] for k in keys}
    ss_all = gs_all = mc_all = None
    even_wt_in = jnp.swapaxes(even_w_in, 1, 2)
    odd_wt_in = jnp.swapaxes(odd_w_in, 1, 2)
    for layer in range(DEPTH):
        i = layer // 2
        if layer % 2 == 0:
            w_main, w_small = _prep_even_w_in(even_wt_in, i)
            zm, zs = _inproj(x, even_norm[i], w_main, w_small, tn=1024)
            zs_t = zs[:mp].T
            mix_a, cx, cbc, st = _ssd_prompt(zm, zs, zs_t, mt, bp, seq, ssd_conv_w[i], ssd_conv_b[i],
                                             ssd_dt_bias[i], ssd_a_log[i], ssd_d[i], ssd_norm[i])
            mix_b, cq, ck, cv, gst = _gdn_prompt(zm, zs, zs_t, mt, bp, seq, gdn_conv_w[i], gdn_dt_bias[i],
                                                 gdn_a_log[i], gdn_norm[i])
            hpg = SSD_HEADS // SSD_GROUPS
            outs_p["sc"].append(jnp.concatenate([cx, cbc], axis=-1))
            outs_p["ss"].append(st.reshape(bp, SSD_GROUPS, SSD_STATE, hpg, SSD_HEAD_DIM).transpose(0, 1, 3, 4, 2)
                                .reshape(bp, SSD_HEADS, SSD_HEAD_DIM, SSD_STATE))
            outs_p["gc"].append(jnp.concatenate([cq, ck, cv], axis=-1))
            outs_p["gs"].append(gst)
            mix_a, ncx, ncb, ncc, ss_all = _ssd_decode(
                zm, zs, mix_a, jnp.swapaxes(state_ssd_conv[i], 0, 1), state_ssd, i, ss_all, mp, ssd_conv_w[i],
                ssd_conv_b[i], ssd_dt_bias[i], ssd_a_log[i], ssd_d[i], ssd_norm[i])
            mix_b, ncq, nck, ncv, gs_all = _gdn_decode(
                zm, zs, mix_b, jnp.swapaxes(state_gdn_conv[i], 0, 1), state_gdn, i, gs_all, mp, gdn_conv_w[i],
                gdn_dt_bias[i], gdn_a_log[i], gdn_norm[i])
            outs_s["sc"].append(jnp.swapaxes(jnp.concatenate([ncx, ncb, ncc], axis=-1), 0, 1))
            outs_s["gc"].append(jnp.swapaxes(jnp.concatenate([ncq, nck, ncv], axis=-1), 0, 1))
            w_out = even_w_out[i].astype(BF16)
        else:
            w_main, w_small = _prep_odd_w_in(odd_wt_in, i)
            zm, zs = _inproj(x, odd_norm[i], w_main, w_small, tn=1024)
            zs_t = zs[:mp].T
            mix_a, c_p, n_p, m_p = _mlstm_prompt(zm, zs, zs_t, mt, bp, seq, mlstm_i_bias[i], mlstm_f_bias[i],
                                                 mlstm_norm[i])
            mix_b, v_rows = _cmlp_prompt(zm, mt, bp, seq, cmlp_v_norm[i], cmlp_ws[i], cmlp_b[i])
            outs_p["mc"].append(c_p); outs_p["mn"].append(n_p); outs_p["mm"].append(m_p[:, :, 0])
            outs_p["cv"].append(v_rows)
            mix_a, n_s, m_s, mc_all = _mlstm_decode(zm, zs, mix_a, state_mlstm_c, i, mc_all, state_mlstm_n[i],
                                                    state_mlstm_m[i], mp, mlstm_i_bias[i], mlstm_f_bias[i],
                                                    mlstm_norm[i])
            mix_b, v_row_s = _cmlp_decode(zm, mix_b, mp, bs, cmlp_v_norm[i], cmlp_ws[i], cmlp_b[i])
            outs_s["mn"].append(n_s.reshape(bs, ML_HEADS, ML_DK))
            outs_s["mm"].append(m_s[:, :ML_HEADS])
            outs_s["cv"].append(v_row_s.reshape(bs, 1, CM_WIDTH))
            w_out = odd_w_out[i].astype(BF16)
        x = _outproj(x, mix_a, mix_b, w_out)

    y_p = _final_norm(x, final_norm, 0, mp, 512).reshape(bp, seq, d)
    y_s = _final_norm(x, final_norm, mp, bs, bs).reshape(bs, 1, d)
    st = lambda o, k: jnp.stack(o[k])
    return (y_p, y_s, st(outs_p, "sc"), st(outs_s, "sc"), st(outs_p, "ss"), ss_all,
            st(outs_p, "gc"), st(outs_s, "gc"), st(outs_p, "gs"), gs_all,
            st(outs_p, "mc"), mc_all, st(outs_p, "mn"), st(outs_s, "mn"),
            st(outs_p, "mm"), st(outs_s, "mm"), st(outs_p, "cv"), st(outs_s, "cv"))
```
